```python
import math
import jax, jax.numpy as jnp
from jax import lax
import numpy as np

D_MODEL = 2048
BATCH = 2
SEQ = 8192
DEPTH = 4

GRID_W = 64
CTX_LEN = 256
HEAD_DIM = 64
GROUP_W = D_MODEL // 4
N_MOD = 9
D_FF = 5632
NORM_EPS = 1e-6
ROPE_BASE = 10000.0
NEG_INF = -1e30
F32 = jnp.float32

NA_HEADS = GROUP_W // HEAD_DIM
NA_WIN_R = 8
NA_WIN_C = 16
NA_QBLOCK = 128
RW_HEADS = GROUP_W // HEAD_DIM
RW_N = HEAD_DIM
RW_DECAY_RANK = 32
RW_ICLR_RANK = 32
RW_GATE_RANK = 96
RW_GN_EPS = 64e-5
WA_HEADS = GROUP_W // HEAD_DIM
WA_KV_HEADS = 2
WA_GROUP = WA_HEADS // WA_KV_HEADS
WA_WINDOW = 128
WA_BLOCK = 128
S5_P = 16
S5_GROUPS = GROUP_W // S5_P
S5_N = 64

IN_SPLITS = (GROUP_W, GROUP_W, GROUP_W,
             GROUP_W, GROUP_W, GROUP_W, RW_DECAY_RANK, RW_ICLR_RANK, RW_GATE_RANK,
             WA_HEADS * HEAD_DIM, WA_KV_HEADS * HEAD_DIM, WA_KV_HEADS * HEAD_DIM,
             GROUP_W)
D_IN = sum(IN_SPLITS)

kernel_name = "hybrid_parallel_group_diffusion_trunk"


def rmsnorm(x, g, eps=NORM_EPS):
    xf = x.astype(F32)
    y = xf * lax.rsqrt(jnp.mean(xf * xf, axis=-1, keepdims=True) + eps)
    return (y * g.astype(F32)).astype(x.dtype)


def modulate(x, g, shift, scale):
    return rmsnorm(x, g) * (1 + scale) + shift


def swiglu(h, wg, wu, wd):
    return (jax.nn.silu(h @ wg) * (h @ wu)) @ wd


def centred_conv3(x, w):
    xp = jnp.pad(x, ((0, 0), (1, 1), (0, 0)))
    return xp[:, :-2] * w[0] + xp[:, 1:-1] * w[1] + xp[:, 2:] * w[2]


def axial_rope_tables(n_tokens):
    t = jnp.arange(n_tokens)
    nf = HEAD_DIM // 4
    inv = 1.0 / (ROPE_BASE ** (jnp.arange(nf, dtype=F32) / nf))

    def ang(p):
        a = p.astype(F32)[:, None] * inv[None, :]
        return jnp.concatenate([a, a], -1)

    a = jnp.concatenate([ang(t // GRID_W), ang(t % GRID_W)], -1)
    return jnp.cos(a), jnp.sin(a)


def _rot_half(t):
    h = t.shape[-1] // 2
    return jnp.concatenate([-t[..., h:], t[..., :h]], -1)


def apply_axial_rope(x, cos, sin):
    half = HEAD_DIM // 2
    cos, sin = cos.astype(x.dtype), sin.astype(x.dtype)
    xr, xc = x[..., :half], x[..., half:]
    return jnp.concatenate([xr * cos[..., :half] + _rot_half(xr) * sin[..., :half],
                            xc * cos[..., half:] + _rot_half(xc) * sin[..., half:]], -1)


def context_attention(q, k, v, sink=None):
    s = jnp.einsum('bqkgd,bckd->bkgqc', q, k).astype(F32) * q.shape[-1] ** -0.5
    if sink is not None:
        s = jnp.concatenate([s, jnp.broadcast_to(sink.astype(F32)[None, :, :, None, None], s.shape[:-1] + (1,))], -1)
    p = jax.nn.softmax(s, axis=-1)[..., :k.shape[1]].astype(v.dtype)
    return jnp.einsum('bkgqc,bckd->bqkgd', p, v)


def neighbourhood_indices(n_tokens):
    rows = n_tokens // GRID_W
    kr = min(NA_WIN_R, rows)
    t = jnp.arange(n_tokens)
    r, c = t // GRID_W, t % GRID_W
    rs = jnp.clip(r - kr // 2, 0, rows - kr)
    cs = jnp.clip(c - NA_WIN_C // 2, 0, GRID_W - NA_WIN_C)
    key_r = rs[:, None, None] + jnp.arange(kr)[None, :, None]
    key_c = cs[:, None, None] + jnp.arange(NA_WIN_C)[None, None, :]
    idx = (key_r * GRID_W + key_c).reshape(n_tokens, kr * NA_WIN_C)
    off_r = key_r - r[:, None, None] + (NA_WIN_R - 1)
    off_c = key_c - c[:, None, None] + (NA_WIN_C - 1)
    bidx = (off_r * (2 * NA_WIN_C - 1) + off_c).reshape(n_tokens, kr * NA_WIN_C)
    return idx, bidx


def neighbourhood_attention(q, k, v, kc, vc, rpb):
    bsz, n, h, dh = q.shape
    idx, bidx = neighbourhood_indices(n)
    nk = idx.shape[1]
    nb = n // NA_QBLOCK
    qb = jnp.moveaxis(q.reshape(bsz, nb, NA_QBLOCK, h, dh), 1, 0)
    idxb = idx.reshape(nb, NA_QBLOCK, nk)
    bidxb = bidx.reshape(nb, NA_QBLOCK, nk)
    scale = dh ** -0.5
    rpb32 = rpb.astype(F32)

    def block(args):
        qq, ii, bi = args
        kk, vv = k[:, ii], v[:, ii]
        s_n = jnp.einsum('bqhd,bqkhd->bhqk', qq, kk).astype(F32) * scale + rpb32[:, bi]
        s_c = jnp.einsum('bqhd,bchd->bhqc', qq, kc).astype(F32) * scale
        p = jax.nn.softmax(jnp.concatenate([s_n, s_c], -1), axis=-1).astype(v.dtype)
        return (jnp.einsum('bhqk,bqkhd->bqhd', p[..., :nk], vv)
                + jnp.einsum('bhqc,bchd->bqhd', p[..., nk:], vc))

    out = lax.map(block, (qb, idxb, bidxb))
    return jnp.moveaxis(out, 0, 1).reshape(bsz, n, h, dh)


def mixer_neighbourhood(pa, pac, q_g, k_g, rpb, want_ctx):
    def qkv(p):
        q, k, v = (t.reshape(*t.shape[:-1], NA_HEADS, HEAD_DIM) for t in p)
        return rmsnorm(q, q_g), rmsnorm(k, k_g), v

    q, k, v = qkv(pa)
    qc, kc, vc = qkv(pac)
    o = neighbourhood_attention(q, k, v, kc, vc, rpb)
    o = o.reshape(*o.shape[:2], GROUP_W)
    oc = context_attention(qc[:, :, :, None], kc, vc).reshape(*qc.shape[:2], GROUP_W) if want_ctx else None
    return o, oc


def _heads(t):
    return t.astype(F32).reshape(*t.shape[:-1], RW_HEADS, RW_N)


def rwkv7_prepare(p, conv_w, w0, w_up, a0, a_up, g_up, k_k, k_a):
    r, k, v, dw, da, dg = p
    r, k, v = jnp.split(centred_conv3(jnp.concatenate([r, k, v], -1), conv_w), 3, axis=-1)
    dwt = jnp.tanh(dw.astype(F32))
    daf = da.astype(F32)
    g = jax.nn.sigmoid(dg.astype(F32)) @ g_up.astype(F32)
    kf = _heads(k)
    kk = kf * _heads(k_k)
    kk = kk / jnp.maximum(jnp.sqrt(jnp.sum(kk * kk, -1, keepdims=True)), 1e-12)
    k_a_h = _heads(k_a)
    dirs = []
    for d in range(2):
        wlog = -jax.nn.softplus(-(w0[d].astype(F32) + dwt @ w_up[d].astype(F32))) - 0.5
        decay = _heads(jnp.exp(-jnp.exp(wlog)))
        a = _heads(jax.nn.sigmoid(a0[d].astype(F32) + daf @ a_up[d].astype(F32)))
        dirs.append((decay, kf * (1 + (a - 1) * k_a_h), a))
    return _heads(r), _heads(v), kk, g, dirs


def rwkv7_scan(r, w, k, v, kk, a, s0, reverse, want_y):
    xs = tuple(jnp.moveaxis(t, 1, 0) for t in (r, w, k, v, kk, a))

    def step(S, inp):
        rt, wt, kt, vt, kkt, at = inp
        sa = jnp.einsum('bhvk,bhk->bhv', S, -kkt)
        S = S * wt[:, :, None, :] + sa[..., None] * (kkt * at)[:, :, None, :] + vt[..., None] * kt[:, :, None, :]
        return S, (jnp.einsum('bhvk,bhk->bhv', S, rt) if want_y else None)

    S, ys = lax.scan(step, s0, xs, reverse=reverse)
    return S, (jnp.moveaxis(ys, 0, 1) if want_y else None)


def rwkv7_readout(r, v, g, k_sum, y, r_k, gn_w, gn_b, dtype):
    y = y + jnp.sum(r * k_sum * r_k.astype(F32), -1, keepdims=True) * v
    mu = jnp.mean(y, -1, keepdims=True)
    var = jnp.mean(jnp.square(y - mu), -1, keepdims=True)
    yn = ((y - mu) * lax.rsqrt(var + RW_GN_EPS)).reshape(*y.shape[:2], GROUP_W)
    return ((yn * gn_w.astype(F32) + gn_b.astype(F32)) * g).astype(dtype)


def mixer_rwkv7(pb, pbc, conv_w, w0, w_up, a0, a_up, g_up, k_k, k_a, r_k, gn_w, gn_b, want_ctx):
    prm = (conv_w, w0, w_up, a0, a_up, g_up, k_k, k_a)
    r, v, kk, g, dirs = rwkv7_prepare(pb, *prm)
    rc, vc, kkc, gc, dirs_c = rwkv7_prepare(pbc, *prm)
    zero = jnp.zeros((r.shape[0], RW_HEADS, RW_N, RW_N), F32)
    ys, ycs = [], []
    for d in range(2):
        wc_, kc_, ac_ = dirs_c[d]
        s_ctx, yc = rwkv7_scan(rc, wc_, kc_, vc, kkc, ac_, zero, d == 1, want_ctx)
        w_, k_, a_ = dirs[d]
        _, y = rwkv7_scan(r, w_, k_, v, kk, a_, s_ctx, d == 1, True)
        ys.append(y)
        ycs.append(yc)
    dtype = pb[0].dtype
    o = rwkv7_readout(r, v, g, dirs[0][1] + dirs[1][1], ys[0] + ys[1], r_k, gn_w, gn_b, dtype)
    oc = (rwkv7_readout(rc, vc, gc, dirs_c[0][1] + dirs_c[1][1], ycs[0] + ycs[1], r_k, gn_w, gn_b, dtype)
          if want_ctx else None)
    return o, oc


def window_attention(q, k, v, kc, vc, sink):
    bsz, n, hk, g, dh = q.shape
    nb = n // WA_BLOCK
    pad = ((0, 0), (WA_BLOCK, WA_BLOCK), (0, 0), (0, 0))

    def bands(t):
        tp = jnp.pad(t, pad)
        return jnp.concatenate([tp[:, i * WA_BLOCK:i * WA_BLOCK + n].reshape(bsz, nb, WA_BLOCK, hk, dh)
                                for i in range(3)], axis=2)

    kb, vb = bands(k), bands(v)
    qb = q.reshape(bsz, nb, WA_BLOCK, hk, g, dh)
    scale = dh ** -0.5
    s_w = jnp.einsum('bnqkgd,bnskd->bnkgqs', qb, kb).astype(F32) * scale
    rel = jnp.arange(3 * WA_BLOCK)[None, :] - WA_BLOCK - jnp.arange(WA_BLOCK)[:, None]
    kpos = jnp.arange(nb)[:, None, None] * WA_BLOCK - WA_BLOCK + jnp.arange(3 * WA_BLOCK)[None, None, :]
    valid = (jnp.abs(rel) <= WA_WINDOW)[None] & (kpos >= 0) & (kpos < n)
    s_w = jnp.where(valid[None, :, None, None], s_w, NEG_INF)
    s_c = jnp.einsum('bnqkgd,bckd->bnkgqc', qb, kc).astype(F32) * scale
    s_s = jnp.broadcast_to(sink.astype(F32)[None, None, :, :, None, None], s_w.shape[:-1] + (1,))
    p = jax.nn.softmax(jnp.concatenate([s_w, s_c, s_s], -1), axis=-1).astype(v.dtype)
    nw, nc = 3 * WA_BLOCK, kc.shape[1]
    o = (jnp.einsum('bnkgqs,bnskd->bnqkgd', p[..., :nw], vb)
         + jnp.einsum('bnkgqc,bckd->bnqkgd', p[..., nw:nw + nc], vc))
    return o.reshape(bsz, n, hk * g * dh)


def mixer_window(pc, pcc, q_g, k_g, sink, cos, sin, want_ctx):
    def qkv(p):
        q, k, v = p
        q = rmsnorm(q.reshape(*q.shape[:-1], WA_KV_HEADS, WA_GROUP, HEAD_DIM), q_g)
        k = rmsnorm(k.reshape(*k.shape[:-1], WA_KV_HEADS, HEAD_DIM), k_g)
        return q, k, v.reshape(*v.shape[:-1], WA_KV_HEADS, HEAD_DIM)

    q, k, v = qkv(pc)
    qc, kc, vc = qkv(pcc)
    q = apply_axial_rope(q, cos[None, :, None, None], sin[None, :, None, None])
    k = apply_axial_rope(k, cos[None, :, None], sin[None, :, None])
    sink2 = sink.reshape(WA_KV_HEADS, WA_GROUP)
    o = window_attention(q, k, v, kc, vc, sink2)
    oc = context_attention(qc, kc, vc, sink2).reshape(*qc.shape[:2], GROUP_W) if want_ctx else None
    return o, oc


def _cmul(ar, ai, br, bi):
    return ar * br - ai * bi, ar * bi + ai * br


def s5_discretise(a_re, a_im, log_dt):
    are, aim = a_re.astype(F32), a_im.astype(F32)
    dt = jnp.exp(log_dt.astype(F32))[:, None]
    mag = jnp.exp(dt * are)
    ang = dt * aim
    ab_re, ab_im = mag * jnp.cos(ang), mag * jnp.sin(ang)
    den = are * are + aim * aim
    nr = ab_re - 1.0
    return ab_re, ab_im, (nr * are + ab_im * aim) / den, (ab_im * are - nr * aim) / den


def s5_scan(bu_re, bu_im, disc, s0, reverse):
    ab_re, ab_im, cf_re, cf_im = disc
    b_re, b_im = _cmul(cf_re, cf_im, bu_re, bu_im)
    if s0 is not None:
        i = -1 if reverse else 0
        i_re, i_im = _cmul(ab_re, ab_im, s0[0], s0[1])
        b_re = b_re.at[:, i].add(i_re)
        b_im = b_im.at[:, i].add(i_im)
    a_re = jnp.broadcast_to(ab_re, b_re.shape)
    a_im = jnp.broadcast_to(ab_im, b_im.shape)

    def combine(e1, e2):
        a1r, a1i, b1r, b1i = e1
        a2r, a2i, b2r, b2i = e2
        ar, ai = _cmul(a2r, a2i, a1r, a1i)
        br, bi = _cmul(a2r, a2i, b1r, b1i)
        return ar, ai, br + b2r, bi + b2i

    _, _, x_re, x_im = lax.associative_scan(combine, (a_re, a_im, b_re, b_im), reverse=reverse, axis=1)
    return x_re, x_im


def mixer_s5(u, uc, a_re, a_im, log_dt, b_re, b_im, c_re, c_im, d_skip, glu_w, glu_b, want_ctx):
    discs = [s5_discretise(a_re[d], a_im[d], log_dt[d]) for d in range(2)]
    bre, bim = b_re.astype(F32), b_im.astype(F32)
    cre, cim = c_re.astype(F32), c_im.astype(F32)
    dsk = d_skip.astype(F32).reshape(S5_GROUPS, S5_P)

    def project(t):
        tg = t.astype(F32).reshape(*t.shape[:-1], S5_GROUPS, S5_P)
        return tg, jnp.einsum('gnp,btgp->btgn', bre, tg), jnp.einsum('gnp,btgp->btgn', bim, tg)

    def readout(tg, x_re, x_im):
        y = (jnp.einsum('gpn,btgn->btgp', cre, x_re) - jnp.einsum('gpn,btgn->btgp', cim, x_im) + dsk * tg)
        y = jax.nn.gelu(y.reshape(*y.shape[:2], GROUP_W)).astype(u.dtype)
        return y * jax.nn.sigmoid(y @ glu_w + glu_b)

    ug, bu_re, bu_im = project(u)
    ucg, buc_re, buc_im = project(uc)
    xf_c = s5_scan(buc_re, buc_im, discs[0], None, False)
    xb_c = s5_scan(buc_re, buc_im, discs[1], None, True)
    xf = s5_scan(bu_re, bu_im, discs[0], (xf_c[0][:, -1], xf_c[1][:, -1]), False)
    xb = s5_scan(bu_re, bu_im, discs[1], (xb_c[0][:, 0], xb_c[1][:, 0]), True)
    o = readout(ug, xf[0] + xb[0], xf[1] + xb[1])
    oc = readout(ucg, xf_c[0] + xb_c[0], xf_c[1] + xb_c[1]) if want_ctx else None
    return o, oc


def setup_inputs(seed: int = 0) -> dict:
    key = jax.random.key(seed)
    ks = iter(jax.random.split(key, 48))
    nrm = lambda shape, s=1.0: s * jax.random.normal(next(ks), shape, F32)
    gain = lambda shape: 1.0 + nrm(shape, 0.05)
    D, L = D_MODEL, DEPTH
    n_idx = jnp.arange(S5_N, dtype=F32)
    conv_base = jnp.array([0.25, 0.5, 0.25], F32)[None, :, None]
    return {
        "x": nrm((BATCH, SEQ, D)),
        "c": nrm((BATCH, D)),
        "ctx": nrm((BATCH, CTX_LEN, D)),
        "c_ctx": nrm((D,)),
        "w_ada": nrm((L, D, N_MOD * D), 0.5 * D ** -0.5),
        "b_ada": nrm((L, N_MOD * D), 0.02),
        "norm_g": gain((L, 3, D)),
        "ffn1_wg": nrm((L, D, D_FF), D ** -0.5),
        "ffn1_wu": nrm((L, D, D_FF), D ** -0.5),
        "ffn1_wd": nrm((L, D_FF, D), D_FF ** -0.5),
        "ffn2_wg": nrm((L, D, D_FF), D ** -0.5),
        "ffn2_wu": nrm((L, D, D_FF), D ** -0.5),
        "ffn2_wd": nrm((L, D_FF, D), D_FF ** -0.5),
        "w_in": nrm((L, D, D_IN), D ** -0.5),
        "w_out": nrm((L, D, D), D ** -0.5),
        "na_q_g": gain((L, HEAD_DIM)),
        "na_k_g": gain((L, HEAD_DIM)),
        "na_rpb": nrm((L, NA_HEADS, (2 * NA_WIN_R - 1) * (2 * NA_WIN_C - 1)), 0.1),
        "rw_conv": conv_base + nrm((L, 3, 3 * GROUP_W), 0.05),
        "rw_w0": jax.random.uniform(next(ks), (L, 2, GROUP_W), F32, -7.0, -1.0),
        "rw_w_up": nrm((L, 2, RW_DECAY_RANK, GROUP_W), 0.1),
        "rw_a0": nrm((L, 2, GROUP_W), 0.1),
        "rw_a_up": nrm((L, 2, RW_ICLR_RANK, GROUP_W), 0.1),
        "rw_g_up": nrm((L, RW_GATE_RANK, GROUP_W), RW_GATE_RANK ** -0.5),
        "rw_k_k": 0.85 + nrm((L, GROUP_W), 0.05),
        "rw_k_a": gain((L, GROUP_W)),
        "rw_r_k": nrm((L, RW_HEADS, RW_N), 0.1),
        "rw_gn_w": gain((L, GROUP_W)),
        "rw_gn_b": nrm((L, GROUP_W), 0.02),
        "wa_q_g": gain((L, HEAD_DIM)),
        "wa_k_g": gain((L, HEAD_DIM)),
        "wa_sink": nrm((L, WA_HEADS), 0.5),
        "s5_a_re": -0.5 + nrm((L, 2, S5_GROUPS, S5_N), 0.01),
        "s5_a_im": math.pi * n_idx + nrm((L, 2, S5_GROUPS, S5_N), 0.01),
        "s5_log_dt": jax.random.uniform(next(ks), (L, 2, S5_GROUPS), F32, math.log(1e-3), math.log(1e-1)),
        "s5_b_re": nrm((L, S5_GROUPS, S5_N, S5_P), (2 * S5_P) ** -0.5),
        "s5_b_im": nrm((L, S5_GROUPS, S5_N, S5_P), (2 * S5_P) ** -0.5),
        "s5_c_re": nrm((L, S5_GROUPS, S5_P, S5_N), S5_N ** -0.5),
        "s5_c_im": nrm((L, S5_GROUPS, S5_P, S5_N), S5_N ** -0.5),
        "s5_d": nrm((L, GROUP_W)),
        "s5_glu_w": nrm((L, GROUP_W, GROUP_W), GROUP_W ** -0.5),
        "s5_glu_b": nrm((L, GROUP_W), 0.02),
    }


def reference(x, c, ctx, c_ctx, w_ada, b_ada, norm_g, ffn1_wg, ffn1_wu, ffn1_wd, ffn2_wg, ffn2_wu, ffn2_wd,
              w_in, w_out, na_q_g, na_k_g, na_rpb, rw_conv, rw_w0, rw_w_up, rw_a0, rw_a_up, rw_g_up,
              rw_k_k, rw_k_a, rw_r_k, rw_gn_w, rw_gn_b, wa_q_g, wa_k_g, wa_sink, s5_a_re, s5_a_im,
              s5_log_dt, s5_b_re, s5_b_im, s5_c_re, s5_c_im, s5_d, s5_glu_w, s5_glu_b):
    cos, sin = axial_rope_tables(x.shape[1])
    cuts = np.cumsum(IN_SPLITS)[:-1].tolist()
    xc = ctx
    for l in range(DEPTH):
        want_ctx = l < DEPTH - 1
        mod = jnp.split((jax.nn.silu(c) @ w_ada[l] + b_ada[l])[:, None, :], N_MOD, axis=-1)
        mod_c = jnp.split((jax.nn.silu(c_ctx) @ w_ada[l] + b_ada[l])[None, None, :], N_MOD, axis=-1)
        x = x + 0.5 * mod[2] * swiglu(modulate(x, norm_g[l, 0], mod[0], mod[1]), ffn1_wg[l], ffn1_wu[l], ffn1_wd[l])
        xc = xc + 0.5 * mod_c[2] * swiglu(modulate(xc, norm_g[l, 0], mod_c[0], mod_c[1]),
                                          ffn1_wg[l], ffn1_wu[l], ffn1_wd[l])
        h = modulate(x, norm_g[l, 1], mod[3], mod[4])
        hc = modulate(xc, norm_g[l, 1], mod_c[3], mod_c[4])
        pa = jnp.split(h @ w_in[l], cuts, axis=-1)
        pc = jnp.split(hc @ w_in[l], cuts, axis=-1)
        o_a, oc_a = mixer_neighbourhood(pa[0:3], pc[0:3], na_q_g[l], na_k_g[l], na_rpb[l], want_ctx)
        o_b, oc_b = mixer_rwkv7(pa[3:9], pc[3:9], rw_conv[l], rw_w0[l], rw_w_up[l], rw_a0[l], rw_a_up[l],
                                rw_g_up[l], rw_k_k[l], rw_k_a[l], rw_r_k[l], rw_gn_w[l], rw_gn_b[l], want_ctx)
        o_c, oc_c = mixer_window(pa[9:12], pc[9:12], wa_q_g[l], wa_k_g[l], wa_sink[l], cos, sin, want_ctx)
        o_d, oc_d = mixer_s5(pa[12], pc[12], s5_a_re[l], s5_a_im[l], s5_log_dt[l], s5_b_re[l], s5_b_im[l],
                             s5_c_re[l], s5_c_im[l], s5_d[l], s5_glu_w[l], s5_glu_b[l], want_ctx)
        x = x + mod[5] * (jnp.concatenate([o_a, o_b, o_c, o_d], axis=-1) @ w_out[l])
        x = x + 0.5 * mod[8] * swiglu(modulate(x, norm_g[l, 2], mod[6], mod[7]), ffn2_wg[l], ffn2_wu[l], ffn2_wd[l])
        if want_ctx:
            xc = xc + mod_c[5] * (jnp.concatenate([oc_a, oc_b, oc_c, oc_d], axis=-1) @ w_out[l])
            xc = xc + 0.5 * mod_c[8] * swiglu(modulate(xc, norm_g[l, 2], mod_c[6], mod_c[7]),
                                              ffn2_wg[l], ffn2_wu[l], ffn2_wd[l])
    return x
```

```python
import functools
import math

import numpy as np
import jax
import jax.numpy as jnp
from jax import lax
from jax.experimental import pallas as pl
from jax.experimental.pallas import tpu as pltpu

F32 = jnp.float32
BF16 = jnp.bfloat16

D_MODEL = 2048
GRID_W = 64
HEAD_DIM = 64
GROUP_W = D_MODEL // 4
N_MOD = 9
NORM_EPS = 1e-6
ROPE_BASE = 10000.0
NEG_INF = -1e30

NA_HEADS = GROUP_W // HEAD_DIM
NA_WIN_R = 8
NA_WIN_C = 16
RW_HEADS = GROUP_W // HEAD_DIM
RW_DECAY_RANK = 32
RW_ICLR_RANK = 32
RW_GATE_RANK = 96
RW_GN_EPS = 64e-5
WA_HEADS = GROUP_W // HEAD_DIM
WA_KV_HEADS = 2
WA_GROUP = WA_HEADS // WA_KV_HEADS
WA_WINDOW = 128
S5_P = 16
S5_GROUPS = GROUP_W // S5_P
S5_N = 64

LANE = 128
QBLK = 2 * GRID_W
NA_KBLKS = 5
RW_CHUNK = 64
S5_CHUNK = 16
VMEM_LIMIT = 56 * 1024 * 1024

COL_AQ, COL_AK, COL_AV = 0, 512, 1024
COL_BR = 1536
COL_CQ = 3072
COL_DU = 3584
COL_BLR = 4096
COL_CK, COL_CV = 4352, 4480
D_IN_PAD = 4608

PB_R, PB_V, PB_KK, PB_G = 0, 1, 2, 9
PB_N = 10


def _cparams(sem):
    return pltpu.CompilerParams(dimension_semantics=sem, vmem_limit_bytes=VMEM_LIMIT)


def _dot(a, b):
    return jnp.dot(a, b, preferred_element_type=F32)


def _dot_nt(a, b):
    return lax.dot_general(a, b, (((1,), (1,)), ((), ())), preferred_element_type=F32)


def _dot_tn(a, b):
    return lax.dot_general(a, b, (((0,), (0,)), ((), ())), preferred_element_type=F32)


def _sigmoid(x):
    return 1.0 / (1.0 + jnp.exp(-x))


def _modulate(x, g, shift, scale):
    ms = jnp.mean(x * x, axis=-1, keepdims=True)
    return (x * lax.rsqrt(ms + NORM_EPS) * g) * (1.0 + scale) + shift


def _seg_sum(x, ones_bd):
    hi = x.astype(BF16)
    r1 = x - hi.astype(F32)
    mid = r1.astype(BF16)
    lo = (r1 - mid.astype(F32)).astype(BF16)
    return _dot(hi, ones_bd) + _dot(mid, ones_bd) + _dot(lo, ones_bd)


def _head_rmsnorm(x, ones_bd):
    ms = _seg_sum(x * x, ones_bd) * (1.0 / HEAD_DIM)
    return x * lax.rsqrt(ms + NORM_EPS)


def _ada_kernel(c_ref, w_ref, b_ref, o_ref):
    c = c_ref[...]
    s = (c * _sigmoid(c)).astype(BF16)
    o_ref[...] = _dot(s, w_ref[...].astype(BF16)) + b_ref[...]


def _ada_call(cc, w_ada, b_ada):
    L, D, N = w_ada.shape
    tn = 1024
    return pl.pallas_call(
        _ada_kernel,
        out_shape=jax.ShapeDtypeStruct((L, 8, N), F32),
        grid=(L, N // tn),
        in_specs=[
            pl.BlockSpec((8, D), lambda l, j: (0, 0)),
            pl.BlockSpec((None, D, tn), lambda l, j: (l, 0, j)),
            pl.BlockSpec((None, 1, tn), lambda l, j: (l, 0, j)),
        ],
        out_specs=pl.BlockSpec((None, 8, tn), lambda l, j: (l, 0, j)),
        compiler_params=_cparams(("arbitrary", "arbitrary")),
        name="adaln",
    )(cc, w_ada, b_ada.reshape(L, 1, N))


def _ffn_kernel(x_ref, mod_ref, g_ref, wg_ref, wu_ref, wd_ref, o_ref, h_sc, acc_sc, *, mi):
    j = pl.program_id(1)

    @pl.when(j == 0)
    def _():
        h = _modulate(x_ref[...], g_ref[...], mod_ref[mi:mi + 1, :], mod_ref[mi + 1:mi + 2, :])
        h_sc[...] = h.astype(BF16)
        acc_sc[...] = jnp.zeros_like(acc_sc)

    h = h_sc[...]
    gate = _dot(h, wg_ref[...])
    up = _dot(h, wu_ref[...])
    a = (gate * _sigmoid(gate) * up).astype(BF16)
    acc_sc[...] += _dot(a, wd_ref[...])

    @pl.when(j == pl.num_programs(1) - 1)
    def _():
        o_ref[...] = x_ref[...] + 0.5 * mod_ref[mi + 2:mi + 3, :] * acc_sc[...]


def _ffn_call(xs, mods, g, wg, wu, wd, l, mi, n_rows, rows_per_seq, n_batch, tm, tf):
    D = xs.shape[1]
    F = wg.shape[2]
    tps = rows_per_seq // tm
    return pl.pallas_call(
        functools.partial(_ffn_kernel, mi=mi),
        out_shape=jax.ShapeDtypeStruct((n_rows, D), F32),
        grid=(n_rows // tm, F // tf),
        in_specs=[
            pl.BlockSpec((tm, D), lambda i, j: (i, 0)),
            pl.BlockSpec((None, N_MOD, D), lambda i, j: (jnp.minimum(i // tps, n_batch), 0, 0)),
            pl.BlockSpec((1, D), lambda i, j: (0, 0)),
            pl.BlockSpec((None, D, tf), lambda i, j: (l, 0, j)),
            pl.BlockSpec((None, D, tf), lambda i, j: (l, 0, j)),
            pl.BlockSpec((None, tf, D), lambda i, j: (l, j, 0)),
        ],
        out_specs=pl.BlockSpec((tm, D), lambda i, j: (i, 0)),
        scratch_shapes=[pltpu.VMEM((tm, D), BF16), pltpu.VMEM((tm, D), F32)],
        compiler_params=_cparams(("arbitrary", "arbitrary")),
        name="ffn",
    )(xs, mods, g, wg, wu, wd)


def _win_kernel(x_ref, mod_ref, g_ref, w_ref, o_ref, h_sc):
    @pl.when(pl.program_id(1) == 0)
    def _():
        h = _modulate(x_ref[...], g_ref[...], mod_ref[3:4, :], mod_ref[4:5, :])
        h_sc[...] = h.astype(BF16)

    o_ref[...] = _dot(h_sc[...], w_ref[...])


def _win_call(xs, mods, g, w_in, l, rows_per_seq, n_batch, tm, tn):
    R, D = xs.shape
    N = w_in.shape[2]
    tps = rows_per_seq // tm
    return pl.pallas_call(
        _win_kernel,
        out_shape=jax.ShapeDtypeStruct((R, N), F32),
        grid=(R // tm, N // tn),
        in_specs=[
            pl.BlockSpec((tm, D), lambda i, j: (i, 0)),
            pl.BlockSpec((None, N_MOD, D), lambda i, j: (jnp.minimum(i // tps, n_batch), 0, 0)),
            pl.BlockSpec((1, D), lambda i, j: (0, 0)),
            pl.BlockSpec((None, D, tn), lambda i, j: (l, 0, j)),
        ],
        out_specs=pl.BlockSpec((tm, tn), lambda i, j: (i, j)),
        scratch_shapes=[pltpu.VMEM((tm, D), BF16)],
        compiler_params=_cparams(("arbitrary", "arbitrary")),
        name="in_proj",
    )(xs, mods, g, w_in)


def _wout_kernel(x_ref, oa_ref, ob_ref, oc_ref, od_ref, mod_ref, w_ref, o_ref):
    acc = _dot(oa_ref[...], w_ref[0 * GROUP_W:1 * GROUP_W, :])
    acc += _dot(ob_ref[...], w_ref[1 * GROUP_W:2 * GROUP_W, :])
    acc += _dot(oc_ref[...], w_ref[2 * GROUP_W:3 * GROUP_W, :])
    acc += _dot(od_ref[...], w_ref[3 * GROUP_W:4 * GROUP_W, :])
    o_ref[...] = x_ref[...] + mod_ref[5:6, :] * acc


def _wout_call(xs, outs, mods, w_out, l, n_rows, rows_per_seq, n_batch, tm):
    D = xs.shape[1]
    tps = rows_per_seq // tm
    ospec = pl.BlockSpec((tm, GROUP_W), lambda i: (i, 0))
    return pl.pallas_call(
        _wout_kernel,
        out_shape=jax.ShapeDtypeStruct((n_rows, D), F32),
        grid=(n_rows // tm,),
        in_specs=[
            pl.BlockSpec((tm, D), lambda i: (i, 0)),
            ospec, ospec, ospec, ospec,
            pl.BlockSpec((None, N_MOD, D), lambda i: (jnp.minimum(i // tps, n_batch), 0, 0)),
            pl.BlockSpec((None, D, D), lambda i: (l, 0, 0)),
        ],
        out_specs=pl.BlockSpec((tm, D), lambda i: (i, 0)),
        compiler_params=_cparams(("arbitrary",)),
        name="out_proj",
    )(xs, *outs, mods, w_out)


def _na_case_reps(n_qb):
    return (0, 1, 2, n_qb - 2, n_qb - 1)


def _na_start(p, n_qb):
    return jnp.clip(p - 2, 0, n_qb - NA_KBLKS)


def _na_bias_index(n_qb):
    rows = 2 * n_qb
    idx = np.zeros((5, QBLK, NA_KBLKS * QBLK), np.int32)
    valid = np.zeros((5, QBLK, NA_KBLKS * QBLK), bool)
    qi = np.arange(QBLK)
    kj = np.arange(NA_KBLKS * QBLK)
    for case, p in enumerate(_na_case_reps(n_qb)):
        start = min(max(p - 2, 0), n_qb - NA_KBLKS)
        qr = (2 * p + qi // GRID_W)[:, None]
        qc = (qi % GRID_W)[:, None]
        kr = (2 * start + kj // GRID_W)[None, :]
        kc = (kj % GRID_W)[None, :]
        rs = np.clip(qr - NA_WIN_R // 2, 0, rows - NA_WIN_R)
        cs = np.clip(qc - NA_WIN_C // 2, 0, GRID_W - NA_WIN_C)
        ok = (kr >= rs) & (kr < rs + NA_WIN_R) & (kc >= cs) & (kc < cs + NA_WIN_C)
        off = (kr - qr + NA_WIN_R - 1) * (2 * NA_WIN_C - 1) + (kc - qc + NA_WIN_C - 1)
        idx[case] = np.where(ok, off, 0)
        valid[case] = ok
    return idx, valid


def _na_kernel(q_ref, k_ref, v_ref, kc_ref, vc_ref, bias_ref, qg_ref, kg_ref, ones_ref, o_ref,
               kn_sc, vn_sc, kcn_sc, vcn_sc, *, n_qb, seq, ctx):
    p = pl.program_id(2)
    ones_bd = ones_ref[...]
    scale = HEAD_DIM ** -0.5

    @pl.when(p == 0)
    def _():
        kg = kg_ref[...]
        rows = 512

        def body(i, carry):
            sl = pl.ds(pl.multiple_of(i * rows, rows), rows)
            kn_sc[sl, :] = (_head_rmsnorm(k_ref[sl, :], ones_bd) * kg).astype(BF16)
            vn_sc[sl, :] = v_ref[sl, :].astype(BF16)
            return carry

        lax.fori_loop(0, seq // rows, body, 0)
        kcn_sc[...] = (_head_rmsnorm(kc_ref[...], ones_bd) * kg).astype(BF16)
        vcn_sc[...] = vc_ref[...].astype(BF16)

    q = (_head_rmsnorm(q_ref[...], ones_bd) * (qg_ref[...] * scale)).astype(BF16)
    kcn = kcn_sc[...]
    vcn = vcn_sc[...]

    def finish(outs):
        o_ref[...] = jnp.concatenate(outs, axis=-1).astype(o_ref.dtype)

    @pl.when(p < n_qb)
    def _():
        start = _na_start(p, n_qb)
        case = p - start
        sl = pl.ds(pl.multiple_of(start * QBLK, QBLK), NA_KBLKS * QBLK)
        kw = kn_sc[sl, :]
        vw = vn_sc[sl, :]
        outs = []
        for h in range(2):
            hs = slice(h * HEAD_DIM, (h + 1) * HEAD_DIM)
            qh = q[:, hs]
            s_n = _dot_nt(qh, kw[:, hs]) + bias_ref[case, h]
            s_c = _dot_nt(qh, kcn[:, hs])
            m = jnp.maximum(jnp.max(s_n, axis=-1, keepdims=True), jnp.max(s_c, axis=-1, keepdims=True))
            p_n = jnp.exp(s_n - m)
            p_c = jnp.exp(s_c - m)
            den = jnp.sum(p_n, axis=-1, keepdims=True) + jnp.sum(p_c, axis=-1, keepdims=True)
            o = _dot(p_n.astype(BF16), vw[:, hs]) + _dot(p_c.astype(BF16), vcn[:, hs])
            outs.append(o / den)
        finish(outs)

    @pl.when(p >= n_qb)
    def _():
        outs = []
        for h in range(2):
            hs = slice(h * HEAD_DIM, (h + 1) * HEAD_DIM)
            s_c = _dot_nt(q[:, hs], kcn[:, hs])
            m = jnp.max(s_c, axis=-1, keepdims=True)
            p_c = jnp.exp(s_c - m)
            den = jnp.sum(p_c, axis=-1, keepdims=True)
            outs.append(_dot(p_c.astype(BF16), vcn[:, hs]) / den)
        finish(outs)


def _na_call(proj, bias, qg, kg, ones_bd, n_batch, seq, ctx, want_ctx):
    R = proj.shape[0]
    n_qb = seq // QBLK
    n_cb = ctx // QBLK
    steps = n_qb + (n_cb if want_ctx else 0)
    lat_qb = n_batch * n_qb
    ctx_kb = n_batch * seq // ctx

    def qrow(b, p):
        return jnp.where(p < n_qb, b * n_qb + p, lat_qb + b * n_cb + (p - n_qb))

    cq, ck, cv = COL_AQ // LANE, COL_AK // LANE, COL_AV // LANE
    return pl.pallas_call(
        functools.partial(_na_kernel, n_qb=n_qb, seq=seq, ctx=ctx),
        out_shape=jax.ShapeDtypeStruct((R, GROUP_W), BF16),
        grid=(n_batch, NA_HEADS // 2, steps),
        in_specs=[
            pl.BlockSpec((QBLK, LANE), lambda b, hp, p: (qrow(b, p), cq + hp)),
            pl.BlockSpec((seq, LANE), lambda b, hp, p: (b, ck + hp)),
            pl.BlockSpec((seq, LANE), lambda b, hp, p: (b, cv + hp)),
            pl.BlockSpec((ctx, LANE), lambda b, hp, p: (ctx_kb + b, ck + hp)),
            pl.BlockSpec((ctx, LANE), lambda b, hp, p: (ctx_kb + b, cv + hp)),
            pl.BlockSpec((5, 2, QBLK, NA_KBLKS * QBLK), lambda b, hp, p: (0, hp, 0, 0)),
            pl.BlockSpec((1, LANE), lambda b, hp, p: (0, 0)),
            pl.BlockSpec((1, LANE), lambda b, hp, p: (0, 0)),
            pl.BlockSpec((LANE, LANE), lambda b, hp, p: (0, 0)),
        ],
        out_specs=pl.BlockSpec((QBLK, LANE), lambda b, hp, p: (qrow(b, p), hp)),
        scratch_shapes=[pltpu.VMEM((seq, LANE), BF16), pltpu.VMEM((seq, LANE), BF16),
                        pltpu.VMEM((ctx, LANE), BF16), pltpu.VMEM((ctx, LANE), BF16)],
        compiler_params=_cparams(("arbitrary", "arbitrary", "arbitrary")),
        name="mixer_a",
    )(proj, proj, proj, proj, proj, bias, qg, kg, ones_bd)


def _rope(x, cos, sin_signed, first_half):
    rot = jnp.where(first_half, pltpu.roll(x, LANE - 16, 1), pltpu.roll(x, 16, 1))
    return x * cos + rot * sin_signed


def _wa_kernel(sink_ref, q_ref, k_ref, v_ref, kc_ref, vc_ref, cosk_ref, sin_k_ref, cosq_ref, sinq_ref,
               qg_ref, kg_ref, ones_ref, o_ref, kn_sc, vn_sc, kcn_sc, vcn_sc, *, n_qb, seq, ctx):
    p = pl.program_id(1)
    ones_bd = ones_ref[...]
    scale = HEAD_DIM ** -0.5
    lane = lax.broadcasted_iota(jnp.int32, (1, LANE), 1)
    first_half = (lane % 32) < 16

    @pl.when(p == 0)
    def _():
        kg = kg_ref[...]
        rows = 512

        def body(i, carry):
            sl = pl.ds(pl.multiple_of(i * rows, rows), rows)
            kn = _head_rmsnorm(k_ref[sl, :], ones_bd) * kg
            kn_sc[sl, :] = _rope(kn, cosk_ref[sl, :], sin_k_ref[sl, :], first_half).astype(BF16)
            vn_sc[sl, :] = v_ref[sl, :].astype(BF16)
            return carry

        lax.fori_loop(0, seq // rows, body, 0)
        kcn_sc[...] = (_head_rmsnorm(kc_ref[...], ones_bd) * kg).astype(BF16)
        vcn_sc[...] = vc_ref[...].astype(BF16)

    qg = qg_ref[...] * scale
    is_lat = p < n_qb
    cosq = jnp.where(is_lat, cosq_ref[...], 1.0)
    sinq = jnp.where(is_lat, sinq_ref[...], 0.0)
    qh = []
    for c in range(4):
        qn = _head_rmsnorm(q_ref[:, c * LANE:(c + 1) * LANE], ones_bd) * qg
        qn = _rope(qn, cosq, sinq, first_half).astype(BF16)
        qh += [qn[:, :HEAD_DIM], qn[:, HEAD_DIM:]]
    kcn = kcn_sc[...]
    vcn = vcn_sc[...]

    def sink_col(kh):
        return jnp.concatenate(
            [jnp.full((QBLK, 1), sink_ref[kh * WA_GROUP + g], F32) for g in range(WA_GROUP)], axis=0)

    def finish(o_heads):
        o_ref[...] = jnp.concatenate(o_heads, axis=-1).astype(o_ref.dtype)

    @pl.when(is_lat)
    def _():
        ws = jnp.clip((p - 1) * QBLK, 0, seq - 3 * QBLK)
        sl = pl.ds(pl.multiple_of(ws, QBLK), 3 * QBLK)
        kw = kn_sc[sl, :]
        vw = vn_sc[sl, :]
        shp = (WA_GROUP * QBLK, 3 * QBLK)
        qpos = p * QBLK + (lax.broadcasted_iota(jnp.int32, shp, 0) & (QBLK - 1))
        kpos = ws + lax.broadcasted_iota(jnp.int32, shp, 1)
        ok = jnp.abs(kpos - qpos) <= WA_WINDOW
        o_heads = []
        for kh in range(WA_KV_HEADS):
            hs = slice(kh * HEAD_DIM, (kh + 1) * HEAD_DIM)
            qs = jnp.concatenate(qh[kh * WA_GROUP:(kh + 1) * WA_GROUP], axis=0)
            s_w = jnp.where(ok, _dot_nt(qs, kw[:, hs]), NEG_INF)
            s_c = _dot_nt(qs, kcn[:, hs])
            sk = sink_col(kh)
            m = jnp.maximum(jnp.maximum(jnp.max(s_w, axis=-1, keepdims=True),
                                        jnp.max(s_c, axis=-1, keepdims=True)), sk)
            p_w = jnp.exp(s_w - m)
            p_c = jnp.exp(s_c - m)
            den = (jnp.sum(p_w, axis=-1, keepdims=True) + jnp.sum(p_c, axis=-1, keepdims=True)
                   + jnp.exp(sk - m))
            o = (_dot(p_w.astype(BF16), vw[:, hs]) + _dot(p_c.astype(BF16), vcn[:, hs])) / den
            o_heads += [o[g * QBLK:(g + 1) * QBLK, :] for g in range(WA_GROUP)]
        finish(o_heads)

    @pl.when(jnp.logical_not(is_lat))
    def _():
        o_heads = []
        for kh in range(WA_KV_HEADS):
            hs = slice(kh * HEAD_DIM, (kh + 1) * HEAD_DIM)
            qs = jnp.concatenate(qh[kh * WA_GROUP:(kh + 1) * WA_GROUP], axis=0)
            s_c = _dot_nt(qs, kcn[:, hs])
            sk = sink_col(kh)
            m = jnp.maximum(jnp.max(s_c, axis=-1, keepdims=True), sk)
            p_c = jnp.exp(s_c - m)
            den = jnp.sum(p_c, axis=-1, keepdims=True) + jnp.exp(sk - m)
            o = _dot(p_c.astype(BF16), vcn[:, hs]) / den
            o_heads += [o[g * QBLK:(g + 1) * QBLK, :] for g in range(WA_GROUP)]
        finish(o_heads)


def _wa_call(proj, sink, cos_t, sin_t, qg, kg, ones_bd, n_batch, seq, ctx, want_ctx):
    R = proj.shape[0]
    n_qb = seq // QBLK
    n_cb = ctx // QBLK
    steps = n_qb + (n_cb if want_ctx else 0)
    lat_qb = n_batch * n_qb
    ctx_kb = n_batch * seq // ctx

    def qrow(b, p):
        return jnp.where(p < n_qb, b * n_qb + p, lat_qb + b * n_cb + (p - n_qb))

    ck, cv = COL_CK // LANE, COL_CV // LANE
    return pl.pallas_call(
        functools.partial(_wa_kernel, n_qb=n_qb, seq=seq, ctx=ctx),
        out_shape=jax.ShapeDtypeStruct((R, GROUP_W), BF16),
        grid=(n_batch, steps),
        in_specs=[
            pl.BlockSpec(memory_space=pltpu.SMEM),
            pl.BlockSpec((QBLK, GROUP_W), lambda b, p: (qrow(b, p), COL_CQ // GROUP_W)),
            pl.BlockSpec((seq, LANE), lambda b, p: (b, ck)),
            pl.BlockSpec((seq, LANE), lambda b, p: (b, cv)),
            pl.BlockSpec((ctx, LANE), lambda b, p: (ctx_kb + b, ck)),
            pl.BlockSpec((ctx, LANE), lambda b, p: (ctx_kb + b, cv)),
            pl.BlockSpec((seq, LANE), lambda b, p: (0, 0)),
            pl.BlockSpec((seq, LANE), lambda b, p: (0, 0)),
            pl.BlockSpec((QBLK, LANE), lambda b, p: (jnp.minimum(p, n_qb - 1), 0)),
            pl.BlockSpec((QBLK, LANE), lambda b, p: (jnp.minimum(p, n_qb - 1), 0)),
            pl.BlockSpec((1, LANE), lambda b, p: (0, 0)),
            pl.BlockSpec((1, LANE), lambda b, p: (0, 0)),
            pl.BlockSpec((LANE, LANE), lambda b, p: (0, 0)),
        ],
        out_specs=pl.BlockSpec((QBLK, GROUP_W), lambda b, p: (qrow(b, p), 0)),
        scratch_shapes=[pltpu.VMEM((seq, LANE), BF16), pltpu.VMEM((seq, LANE), BF16),
                        pltpu.VMEM((ctx, LANE), BF16), pltpu.VMEM((ctx, LANE), BF16)],
        compiler_params=_cparams(("arbitrary", "arbitrary")),
        name="mixer_c",
    )(sink, proj, proj, proj, proj, proj, cos_t, sin_t, cos_t, sin_t, qg, kg, ones_bd)


def _rw_prep_kernel(x_ref, prev_ref, next_ref, lr_ref, conv_ref, wlr_ref, w0_ref, a0_ref, kk_ref, ka_ref,
                    ones_ref, o_ref, *, tiles_lat, tiles_ctx, n_lat_tiles):
    i = pl.program_id(0)
    tm = x_ref.shape[0]
    in_lat = i < n_lat_tiles
    first = jnp.where(in_lat, i % tiles_lat == 0, (i - n_lat_tiles) % tiles_ctx == 0)
    last = jnp.where(in_lat, i % tiles_lat == tiles_lat - 1, (i - n_lat_tiles) % tiles_ctx == tiles_ctx - 1)
    x = x_ref[...]
    prev_row = jnp.where(first, 0.0, prev_ref[7:8, :])
    next_row = jnp.where(last, 0.0, next_ref[0:1, :])
    row = lax.broadcasted_iota(jnp.int32, (tm, 1), 0)
    x_prev = jnp.where(row == 0, prev_row, pltpu.roll(x, 1, 0))
    x_next = jnp.where(row == tm - 1, next_row, pltpu.roll(x, tm - 1, 0))
    cw = conv_ref[...]
    y = x_prev * cw[0:1, :] + x * cw[1:2, :] + x_next * cw[2:3, :]
    r = y[:, 0:GROUP_W]
    k = y[:, GROUP_W:2 * GROUP_W]
    v = y[:, 2 * GROUP_W:3 * GROUP_W]

    lr = lr_ref[...]
    lane = lax.broadcasted_iota(jnp.int32, (1, lr.shape[1]), 1)
    c1 = RW_DECAY_RANK
    c2 = c1 + RW_ICLR_RANK
    c3 = c2 + RW_GATE_RANK
    act = jnp.where(lane < c1, jnp.tanh(lr),
                    jnp.where(lane < c2, lr, jnp.where(lane < c3, _sigmoid(lr), 0.0)))
    up = jnp.dot(act, wlr_ref[...], preferred_element_type=F32, precision=lax.Precision.HIGHEST)

    ones_bd = ones_ref[...]
    kk = k * kk_ref[...]
    nrm = jnp.sqrt(_seg_sum(kk * kk, ones_bd))
    kk = kk / jnp.maximum(nrm, 1e-12)

    o_ref[:, PB_R * GROUP_W:(PB_R + 1) * GROUP_W] = r
    o_ref[:, PB_V * GROUP_W:(PB_V + 1) * GROUP_W] = v
    o_ref[:, PB_KK * GROUP_W:(PB_KK + 1) * GROUP_W] = kk
    o_ref[:, PB_G * GROUP_W:(PB_G + 1) * GROUP_W] = up[:, 4 * GROUP_W:5 * GROUP_W]
    ka = ka_ref[...]
    for d in range(2):
        z = w0_ref[d:d + 1, :] + up[:, d * GROUP_W:(d + 1) * GROUP_W]
        sp = jnp.maximum(-z, 0.0) + jnp.log(1.0 + jnp.exp(-jnp.abs(z)))
        lw = -jnp.exp(-sp - 0.5)
        a = _sigmoid(a0_ref[d:d + 1, :] + up[:, (2 + d) * GROUP_W:(3 + d) * GROUP_W])
        kd = k * (1.0 + (a - 1.0) * ka)
        base = 3 + 3 * d
        o_ref[:, base * GROUP_W:(base + 1) * GROUP_W] = lw
        o_ref[:, (base + 1) * GROUP_W:(base + 2) * GROUP_W] = a
        o_ref[:, (base + 2) * GROUP_W:(base + 3) * GROUP_W] = kd


def _rw_prep_call(proj, conv_w, wlr, w0, a0, k_k, k_a, ones512, n_batch, seq, ctx, tm):
    R = proj.shape[0]
    n_lat_tiles = n_batch * seq // tm
    nb8 = R // 8
    t8 = tm // 8
    wide = 3 * GROUP_W
    cb = COL_BR // wide
    return pl.pallas_call(
        functools.partial(_rw_prep_kernel, tiles_lat=seq // tm, tiles_ctx=ctx // tm, n_lat_tiles=n_lat_tiles),
        out_shape=jax.ShapeDtypeStruct((R, PB_N * GROUP_W), F32),
        grid=(R // tm,),
        in_specs=[
            pl.BlockSpec((tm, wide), lambda i: (i, cb)),
            pl.BlockSpec((8, wide), lambda i: (jnp.maximum(i * t8 - 1, 0), cb)),
            pl.BlockSpec((8, wide), lambda i: (jnp.minimum((i + 1) * t8, nb8 - 1), cb)),
            pl.BlockSpec((tm, 256), lambda i: (i, COL_BLR // 256)),
            pl.BlockSpec((3, wide), lambda i: (0, 0)),
            pl.BlockSpec((256, 5 * GROUP_W), lambda i: (0, 0)),
            pl.BlockSpec((2, GROUP_W), lambda i: (0, 0)),
            pl.BlockSpec((2, GROUP_W), lambda i: (0, 0)),
            pl.BlockSpec((1, GROUP_W), lambda i: (0, 0)),
            pl.BlockSpec((1, GROUP_W), lambda i: (0, 0)),
            pl.BlockSpec((GROUP_W, GROUP_W), lambda i: (0, 0)),
        ],
        out_specs=pl.BlockSpec((tm, PB_N * GROUP_W), lambda i: (i, 0)),
        compiler_params=_cparams(("arbitrary",)),
        name="mixer_b_prep",
    )(proj, proj, proj, proj, conv_w, wlr, w0, a0, k_k, k_a, ones512)


RW_SPLIT = 2
_NN = ((1,), (0,))
_NT = ((1,), (1,))
_TN = ((0,), (0,))


def _split(x, n):
    parts = []
    for _ in range(n):
        p = x.astype(BF16)
        parts.append(p)
        x = x - p.astype(F32)
    return parts


def _sdot(a, b, dims):
    n = max(len(a), len(b))
    acc = None
    for i, ai in enumerate(a):
        for j, bj in enumerate(b):
            if i + j < n:
                t = lax.dot_general(ai, bj, (dims, ((), ())), preferred_element_type=F32)
                acc = t if acc is None else acc + t
    return acc


def _rw_chunk_head(at, bt, kt, rt, v, gam, s0, strict, incl):
    C = at.shape[0]
    sp = lambda t: _split(t, RW_SPLIT)
    ar = sp(jnp.concatenate([at, rt], axis=0))
    bk = sp(jnp.concatenate([bt, kt], axis=0))
    g4 = _sdot(ar, bk, _NT)
    a_ab = jnp.where(strict, g4[:C, :C], 0.0)
    a_ak = jnp.where(strict, g4[:C, C:], 0.0)
    a_rb = jnp.where(incl, g4[C:, :C], 0.0)
    a_rk = jnp.where(incl, g4[C:, C:], 0.0)
    wv = _sdot(sp(a_ak), sp(v), _NN)
    x = jnp.concatenate([at, wv], axis=1)
    n = a_ab
    steps = int(math.log2(C))
    for i in range(steps):
        nb = sp(n)
        x = x + _sdot(nb, sp(x), _NN)
        if i < steps - 1:
            n = _sdot(nb, nb, _NN)
    pm = x[:, :HEAD_DIM]
    qm = x[:, HEAD_DIM:]
    qv = sp(jnp.concatenate([qm, v], axis=0))
    y_loc = _sdot(sp(jnp.concatenate([a_rb, a_rk], axis=1)), qv, _NN)
    pb = sp(pm)
    rp = rt + _sdot(sp(a_rb), pb, _NN)
    s0b = sp(s0)
    y = y_loc + _sdot(sp(rp), s0b, _NT)
    w = _sdot(s0b, pb, _NT)
    s1 = s0 + _sdot(sp(w), sp(bt), _NN) + _sdot(qv, bk, _TN)
    return y, s1 * gam


def _rw_scan_kernel(*refs, reverse, readout, n_steps):
    if readout:
        (r_ref, v_ref, kk_ref, lw_ref, a_ref, k_ref, ko_ref, g_ref, yb_ref, rk_ref, gw_ref, gb_ref,
         o_ref, s_sc) = refs
    else:
        r_ref, v_ref, kk_ref, lw_ref, a_ref, k_ref, o_ref, s_sc = refs
    s = pl.program_id(1)

    @pl.when(s == 0)
    def _():
        s_sc[...] = jnp.zeros_like(s_sc)

    C = r_ref.shape[0]
    row = lax.broadcasted_iota(jnp.int32, (C, C), 0)
    col = lax.broadcasted_iota(jnp.int32, (C, C), 1)
    if reverse:
        incl, strict = row <= col, row < col
    else:
        incl, strict = row >= col, row > col
    lw = lw_ref[...]
    r = r_ref[...]
    v = v_ref[...]
    kk = kk_ref[...]
    k = k_ref[...]
    cum = jnp.dot(incl.astype(F32), lw, preferred_element_type=F32, precision=lax.Precision.HIGHEST)
    e_l = jnp.exp(cum)
    e_e = jnp.exp(cum - lw)
    e_n = jnp.exp(-cum)
    at = -(kk * e_e)
    bt = kk * a_ref[...] * e_n
    kt = k * e_n
    rt = r * e_l
    gam = e_l[0:1, :] if reverse else e_l[C - 1:C, :]
    outs = []
    for h in range(RW_HEADS):
        hs = slice(h * HEAD_DIM, (h + 1) * HEAD_DIM)
        y, s1 = _rw_chunk_head(at[:, hs], bt[:, hs], kt[:, hs], rt[:, hs], v[:, hs], gam[:, hs],
                               s_sc[h], strict, incl)
        s_sc[h] = s1
        if readout:
            ksum = k[:, hs] + ko_ref[:, hs]
            yt = y + yb_ref[:, hs]
            yt = yt + jnp.sum(r[:, hs] * ksum * rk_ref[:, hs], axis=-1, keepdims=True) * v[:, hs]
            mu = jnp.mean(yt, axis=-1, keepdims=True)
            yc = yt - mu
            var = jnp.mean(yc * yc, axis=-1, keepdims=True)
            yn = yc * lax.rsqrt(var + RW_GN_EPS)
            y = (yn * gw_ref[:, hs] + gb_ref[:, hs]) * g_ref[:, hs]
        outs.append(y)
    o_ref[...] = jnp.concatenate(outs, axis=-1).astype(o_ref.dtype)


def _rw_scan_call(prep, yb, r_k, gn_w, gn_b, n_batch, seq, ctx, reverse):
    R = prep.shape[0]
    C = RW_CHUNK
    nc, nl = ctx // C, seq // C
    n_steps = nc + nl
    ctx_base = n_batch * nl
    readout = not reverse

    def rowblk(b, s):
        if reverse:
            return jnp.where(s < nc, ctx_base + b * nc + (nc - 1 - s), b * nl + (nl - 1 - (s - nc)))
        return jnp.where(s < nc, ctx_base + b * nc + s, b * nl + (s - nc))

    d = 1 if reverse else 0

    def col(cb):
        return pl.BlockSpec((C, GROUP_W), lambda b, s: (rowblk(b, s), cb))

    in_specs = [col(PB_R), col(PB_V), col(PB_KK), col(3 + 3 * d), col(4 + 3 * d), col(5 + 3 * d)]
    args = [prep] * 6
    if readout:
        vec = pl.BlockSpec((1, GROUP_W), lambda b, s: (0, 0))
        in_specs += [col(5 + 3 * (1 - d)), col(PB_G), col(0), vec, vec, vec]
        args += [prep, prep, yb, r_k, gn_w, gn_b]
    return pl.pallas_call(
        functools.partial(_rw_scan_kernel, reverse=reverse, readout=readout, n_steps=n_steps),
        out_shape=jax.ShapeDtypeStruct((R, GROUP_W), BF16 if readout else F32),
        grid=(n_batch, n_steps),
        in_specs=in_specs,
        out_specs=col(0),
        scratch_shapes=[pltpu.VMEM((RW_HEADS, HEAD_DIM, HEAD_DIM), F32)],
        compiler_params=_cparams(("arbitrary", "arbitrary")),
        name="mixer_b_scan_bwd" if reverse else "mixer_b_scan_fwd",
    )(*args)


def _s5_tables(a_re, a_im, log_dt, b_re, b_im, c_re, c_im):
    Lc, G, N, P = S5_CHUNK, S5_GROUPS, S5_N, S5_P
    dt = jnp.exp(log_dt)[..., None]
    lam_re, lam_im = dt * a_re, dt * a_im
    tau = jnp.arange(Lc + 1, dtype=F32)[:, None, None, None]
    mag = jnp.exp(tau * lam_re)
    pw_re, pw_im = mag * jnp.cos(tau * lam_im), mag * jnp.sin(tau * lam_im)
    ab_re, ab_im = pw_re[1], pw_im[1]
    den = a_re * a_re + a_im * a_im
    nr = ab_re - 1.0
    cf_re, cf_im = (nr * a_re + ab_im * a_im) / den, (ab_im * a_re - nr * a_im) / den
    bp_re = cf_re[..., None] * b_re[None] - cf_im[..., None] * b_im[None]
    bp_im = cf_re[..., None] * b_im[None] + cf_im[..., None] * b_re[None]
    pb_re = pw_re[..., None] * bp_re[None] - pw_im[..., None] * bp_im[None]
    pb_im = pw_re[..., None] * bp_im[None] + pw_im[..., None] * bp_re[None]
    kk = (jnp.einsum('gqn,tdgnp->tdgqp', c_re, pb_re[:Lc]) - jnp.einsum('gqn,tdgnp->tdgqp', c_im, pb_im[:Lc]))
    jj = np.arange(Lc)[:, None]
    ii = np.arange(Lc)[None, :]
    dist = np.abs(ii - jj)
    kf = kk[dist, 0] * jnp.asarray(ii >= jj, F32)[..., None, None, None]
    kb = kk[dist, 1] * jnp.asarray(ii <= jj, F32)[..., None, None, None]
    tz = jnp.transpose(kf + kb, (2, 0, 4, 1, 3)).reshape(G, Lc * P, Lc * P)
    jr = np.arange(Lc)

    def emap(pbx, order, d):
        return jnp.transpose(pbx[order, d], (1, 0, 3, 2)).reshape(G, Lc * P, N)

    ef_re, ef_im = emap(pb_re, Lc - 1 - jr, 0), emap(pb_im, Lc - 1 - jr, 0)
    eb_re, eb_im = emap(pb_re, jr, 1), emap(pb_im, jr, 1)
    def gmap(order, d):
        pr, pi = pw_re[order, d], pw_im[order, d]
        cp_re = c_re[None] * pr[:, :, None, :] - c_im[None] * pi[:, :, None, :]
        cp_im = c_re[None] * pi[:, :, None, :] + c_im[None] * pr[:, :, None, :]
        to = lambda t: jnp.transpose(t, (1, 3, 0, 2)).reshape(G, N, Lc * P)
        return to(cp_re), to(-cp_im)

    gf_re, gf_im = gmap(jr + 1, 0)
    gb_re, gb_im = gmap(Lc - jr, 1)
    eye2 = jnp.eye(2, dtype=F32)

    def pair_bd(t):
        r, c = t.shape[1:]
        t = t.reshape(G // 2, 2, r, c)
        return jnp.einsum('hark,ab->harbk', t, eye2).reshape(G // 2, 2 * r, 2 * c)

    wz = jnp.concatenate([pair_bd(tz), pair_bd(ef_re), pair_bd(ef_im), pair_bd(eb_re), pair_bd(eb_im)], axis=-1)
    gm = jnp.concatenate([pair_bd(gf_re), pair_bd(gf_im), pair_bd(gb_re), pair_bd(gb_im)], axis=1)
    apow = jnp.stack([pw_re[Lc, 0], pw_im[Lc, 0], pw_re[Lc, 1], pw_im[Lc, 1]], axis=1)
    apow = jnp.transpose(apow.reshape(G // 2, 2, 4, N), (0, 2, 1, 3)).reshape(G // 2, 4, 2 * N)
    return wz, gm, apow


def _s5_kernel(u_ref, wz_ref, gm_ref, ap_ref, y_ref, e_sc, xin_sc, *, n_batch, nlc, ncc):
    W = 2 * S5_CHUNK * S5_P
    NS = 2 * S5_N
    z = _dot(u_ref[...], wz_ref[...])
    y_ref[...] = z[:, :W]
    e_sc[...] = z[:, W:]
    af_re, af_im = ap_ref[0:1, :], ap_ref[1:2, :]
    ab_re, ab_im = ap_ref[2:3, :], ap_ref[3:4, :]
    nb = n_batch
    gs = 8 // nb

    def step(a_re, a_im, x_re, x_im, e):
        return a_re * x_re - a_im * x_im + e[:, :NS], a_re * x_im + a_im * x_re + e[:, NS:]

    def phase(c0, n_chunks, carry):
        n_groups = n_chunks // gs

        def body(gi, carry):
            xf_re, xf_im, xb_re, xb_im = carry
            rf = pl.ds(pl.multiple_of((c0 // gs + gi) * 8, 8), 8)
            rb = pl.ds(pl.multiple_of((c0 // gs + n_groups - 1 - gi) * 8, 8), 8)
            ef = e_sc[rf, 0:2 * NS]
            eb = e_sc[rb, 2 * NS:4 * NS]
            xf_in, xb_in = [], [None] * gs
            for j in range(gs):
                xf_in.append(jnp.concatenate([xf_re, xf_im], axis=-1))
                xf_re, xf_im = step(af_re, af_im, xf_re, xf_im, ef[j * nb:(j + 1) * nb, :])
            for j in reversed(range(gs)):
                xb_in[j] = jnp.concatenate([xb_re, xb_im], axis=-1)
                xb_re, xb_im = step(ab_re, ab_im, xb_re, xb_im, eb[j * nb:(j + 1) * nb, :])
            xin_sc[rf, 0:2 * NS] = jnp.concatenate(xf_in, axis=0)
            xin_sc[rb, 2 * NS:4 * NS] = jnp.concatenate(xb_in, axis=0)
            return xf_re, xf_im, xb_re, xb_im

        return lax.fori_loop(0, n_groups, body, carry)

    zero = jnp.zeros((nb, NS), F32)
    carry = phase(nlc, ncc, (zero, zero, zero, zero))
    phase(0, nlc, carry)
    y_ref[...] += _dot(xin_sc[...].astype(BF16), gm_ref[...])


def _s5_call(u_pairs, wz, gm, apow, n_batch, seq, ctx):
    GP, NR, W = u_pairs.shape
    NS = 2 * S5_N
    return pl.pallas_call(
        functools.partial(_s5_kernel, n_batch=n_batch, nlc=seq // S5_CHUNK, ncc=ctx // S5_CHUNK),
        out_shape=jax.ShapeDtypeStruct((GP, NR, W), F32),
        grid=(GP,),
        in_specs=[
            pl.BlockSpec((None, NR, W), lambda g: (g, 0, 0)),
            pl.BlockSpec((None, W, W + 4 * NS), lambda g: (g, 0, 0)),
            pl.BlockSpec((None, 4 * NS, W), lambda g: (g, 0, 0)),
            pl.BlockSpec((None, 4, NS), lambda g: (g, 0, 0)),
        ],
        out_specs=pl.BlockSpec((None, NR, W), lambda g: (g, 0, 0)),
        scratch_shapes=[pltpu.VMEM((NR, 4 * NS), F32), pltpu.VMEM((NR, 4 * NS), F32)],
        compiler_params=_cparams(("arbitrary",)),
        name="mixer_d_scan",
    )(u_pairs, wz, gm, apow)


def _s5_out_kernel(y_ref, u_ref, d_ref, w_ref, b_ref, o_ref):
    y = y_ref[...] + d_ref[...] * u_ref[...]
    c = math.sqrt(2.0 / math.pi)
    y = 0.5 * y * (1.0 + jnp.tanh(c * (y + 0.044715 * (y * y * y))))
    z = _dot(y.astype(BF16), w_ref[...]) + b_ref[...]
    o_ref[...] = (y * _sigmoid(z)).astype(o_ref.dtype)


def _s5_out_call(y_tok, proj, d_skip, glu_w, glu_b, tm):
    R = proj.shape[0]
    vec = pl.BlockSpec((1, GROUP_W), lambda i: (0, 0))
    return pl.pallas_call(
        _s5_out_kernel,
        out_shape=jax.ShapeDtypeStruct((R, GROUP_W), BF16),
        grid=(R // tm,),
        in_specs=[
            pl.BlockSpec((tm, GROUP_W), lambda i: (i, 0)),
            pl.BlockSpec((tm, GROUP_W), lambda i: (i, COL_DU // GROUP_W)),
            vec,
            pl.BlockSpec((GROUP_W, GROUP_W), lambda i: (0, 0)),
            vec,
        ],
        out_specs=pl.BlockSpec((tm, GROUP_W), lambda i: (i, 0)),
        compiler_params=_cparams(("arbitrary",)),
        name="mixer_d_out",
    )(y_tok, proj, d_skip, glu_w, glu_b)


def _s5_to_chunks(u, n_batch, seq, ctx):
    Lc, G, P = S5_CHUNK, S5_GROUPS, S5_P

    def part(t, n):
        t = t.reshape(n_batch, n // Lc, Lc, G // 2, 2, P)
        return jnp.transpose(t, (3, 1, 0, 4, 2, 5)).reshape(G // 2, (n // Lc) * n_batch, 2 * Lc * P)

    nl = n_batch * seq
    return jnp.concatenate([part(u[:nl], seq), part(u[nl:], ctx)], axis=1)


def _s5_from_chunks(y, n_batch, seq, ctx):
    Lc, G, P = S5_CHUNK, S5_GROUPS, S5_P

    def part(t, n):
        t = t.reshape(G // 2, n // Lc, n_batch, 2, Lc, P)
        return jnp.transpose(t, (2, 1, 4, 0, 3, 5)).reshape(n_batch * n, G * P)

    nl = n_batch * (seq // Lc)
    return jnp.concatenate([part(y[:, :nl], seq), part(y[:, nl:], ctx)], axis=0)


def _rope_tables(n_tokens):
    t = jnp.arange(n_tokens)
    nf = HEAD_DIM // 4
    inv = 1.0 / (ROPE_BASE ** (jnp.arange(nf, dtype=F32) / nf))

    def ang(pp):
        a = pp.astype(F32)[:, None] * inv[None, :]
        return jnp.concatenate([a, a], -1)

    a = jnp.concatenate([ang(t // GRID_W), ang(t % GRID_W)], -1)
    cos, sin = jnp.cos(a), jnp.sin(a)
    sign = np.where((np.arange(HEAD_DIM) % 32) < 16, -1.0, 1.0).astype(np.float32)
    cos2 = jnp.concatenate([cos, cos], -1)
    sin2 = jnp.concatenate([sin * sign, sin * sign], -1)
    return cos2, sin2


def _block_ones(n):
    return jnp.asarray(np.kron(np.eye(n // HEAD_DIM), np.ones((HEAD_DIM, HEAD_DIM))), BF16)


def _permute_w_in(w_in):
    cuts = np.cumsum([512, 512, 512, 512, 512, 512, 32, 32, 96, 512, 128, 128, 512])
    seg = lambda i: w_in[..., (0 if i == 0 else cuts[i - 1]):cuts[i]]
    L, D = w_in.shape[:2]
    pad = jnp.zeros((L, D, 96), w_in.dtype)
    parts = [seg(0), seg(1), seg(2), seg(3), seg(4), seg(5), seg(9), seg(12),
             seg(6), seg(7), seg(8), pad, seg(10), seg(11)]
    return jnp.concatenate(parts, axis=-1).astype(BF16)


def _forward(x, c, ctx, c_ctx, w_ada, b_ada, norm_g, ffn1_wg, ffn1_wu, ffn1_wd, ffn2_wg, ffn2_wu, ffn2_wd,
             w_in, w_out, na_q_g, na_k_g, na_rpb, rw_conv, rw_w0, rw_w_up, rw_a0, rw_a_up, rw_g_up,
             rw_k_k, rw_k_a, rw_r_k, rw_gn_w, rw_gn_b, wa_q_g, wa_k_g, wa_sink, s5_a_re, s5_a_im,
             s5_log_dt, s5_b_re, s5_b_im, s5_c_re, s5_c_im, s5_d, s5_glu_w, s5_glu_b, *, tm, tf):
    B, SEQ, D = x.shape
    CTX = ctx.shape[1]
    L = w_ada.shape[0]
    n_lat = B * SEQ
    n_qb = SEQ // QBLK

    bf = lambda t: t.astype(BF16)
    f1g, f1u, f1d, f2g, f2u, f2d = map(bf, (ffn1_wg, ffn1_wu, ffn1_wd, ffn2_wg, ffn2_wu, ffn2_wd))
    w_in_p = _permute_w_in(w_in)
    w_out_b = bf(w_out)
    glu_w_b = bf(s5_glu_w)
    ones128, ones512 = _block_ones(LANE), _block_ones(GROUP_W)
    cos_t, sin_t = _rope_tables(SEQ)
    bidx, bvalid = _na_bias_index(n_qb)
    na_bias = jnp.where(jnp.asarray(bvalid)[None, None], jnp.take(na_rpb.astype(F32), jnp.asarray(bidx), axis=2),
                        NEG_INF)
    na_bias = jnp.transpose(na_bias, (0, 2, 1, 3, 4))
    zr = lambda r, cdim: jnp.zeros((L, r, cdim), F32)
    wlr = jnp.concatenate([
        jnp.concatenate([rw_w_up[:, 0], rw_w_up[:, 1], zr(32, 3 * GROUP_W)], axis=-1),
        jnp.concatenate([zr(32, 2 * GROUP_W), rw_a_up[:, 0], rw_a_up[:, 1], zr(32, GROUP_W)], axis=-1),
        jnp.concatenate([zr(96, 4 * GROUP_W), rw_g_up], axis=-1),
        zr(96, 5 * GROUP_W)], axis=1).astype(F32)
    s5_tabs = jax.vmap(_s5_tables)(s5_a_re, s5_a_im, s5_log_dt, s5_b_re, s5_b_im, s5_c_re, s5_c_im)
    tile2 = lambda t: jnp.concatenate([t, t], axis=-1)

    cc = jnp.concatenate([c, c_ctx[None], jnp.zeros((8 - B - 1, D), F32)], axis=0)
    mods_all = _ada_call(cc, w_ada, b_ada)[:, :B + 1].reshape(L, B + 1, N_MOD, D)

    xs = jnp.concatenate([x.reshape(n_lat, D), ctx.reshape(B * CTX, D)], axis=0)
    R = xs.shape[0]
    for l in range(L):
        want_ctx = l < L - 1
        mods = mods_all[l]
        ng = norm_g[l]
        xs = _ffn_call(xs, mods, ng[0:1], f1g, f1u, f1d, l, 0, R, SEQ, B, tm, tf)
        proj = _win_call(xs, mods, ng[1:2], w_in_p, l, SEQ, B, tm, 512)
        o_a = _na_call(proj, na_bias[l], tile2(na_q_g[l][None]), tile2(na_k_g[l][None]), ones128,
                       B, SEQ, CTX, want_ctx)
        prep = _rw_prep_call(proj, rw_conv[l], wlr[l], rw_w0[l], rw_a0[l], rw_k_k[l][None], rw_k_a[l][None],
                             ones512, B, SEQ, CTX, min(tm, 256))
        vec = lambda t: t.reshape(1, GROUP_W)
        yb = _rw_scan_call(prep, None, None, None, None, B, SEQ, CTX, True)
        o_b = _rw_scan_call(prep, yb, vec(rw_r_k[l]), vec(rw_gn_w[l]), vec(rw_gn_b[l]), B, SEQ, CTX, False)
        o_c = _wa_call(proj, wa_sink[l], cos_t, sin_t, tile2(wa_q_g[l][None]), tile2(wa_k_g[l][None]), ones128,
                       B, SEQ, CTX, want_ctx)
        u_pairs = bf(_s5_to_chunks(proj[:, COL_DU:COL_DU + GROUP_W], B, SEQ, CTX))
        wz, gm, apow = (t[l] for t in s5_tabs)
        y_pairs = _s5_call(u_pairs, bf(wz), bf(gm), apow, B, SEQ, CTX)
        y_tok = _s5_from_chunks(y_pairs, B, SEQ, CTX)
        o_d = _s5_out_call(y_tok, proj, vec(s5_d[l]), glu_w_b[l], vec(s5_glu_b[l]), tm)
        n_rows = R if want_ctx else n_lat
        xs = _wout_call(xs, (o_a, o_b, o_c, o_d), mods, w_out_b, l, n_rows, SEQ, B, tm)
        xs = _ffn_call(xs, mods, ng[2:3], f2g, f2u, f2d, l, 6, n_rows, SEQ, B, tm, tf)
    return xs[:n_lat].reshape(B, SEQ, D)


def kernel(x, c, ctx, c_ctx, w_ada, b_ada, norm_g, ffn1_wg, ffn1_wu, ffn1_wd, ffn2_wg, ffn2_wu, ffn2_wd, w_in, w_out, na_q_g, na_k_g, na_rpb, rw_conv, rw_w0, rw_w_up, rw_a0, rw_a_up, rw_g_up, rw_k_k, rw_k_a, rw_r_k, rw_gn_w, rw_gn_b, wa_q_g, wa_k_g, wa_sink, s5_a_re, s5_a_im, s5_log_dt, s5_b_re, s5_b_im, s5_c_re, s5_c_im, s5_d, s5_glu_w, s5_glu_b):
    return _forward(x, c, ctx, c_ctx, w_ada, b_ada, norm_g, ffn1_wg, ffn1_wu, ffn1_wd, ffn2_wg, ffn2_wu, ffn2_wd,
                    w_in, w_out, na_q_g, na_k_g, na_rpb, rw_conv, rw_w0, rw_w_up, rw_a0, rw_a_up, rw_g_up,
                    rw_k_k, rw_k_a, rw_r_k, rw_gn_w, rw_gn_b, wa_q_g, wa_k_g, wa_sink, s5_a_re, s5_a_im,
                    s5_log_dt, s5_b_re, s5_b_im, s5_c_re, s5_c_im, s5_d, s5_glu_w, s5_glu_b, tm=512, tf=512)
```

```python
import functools
import math

import numpy as np
import jax
import jax.numpy as jnp
from jax import lax
from jax.experimental import pallas as pl
from jax.experimental.pallas import tpu as pltpu

F32 = jnp.float32
BF16 = jnp.bfloat16

D_MODEL = 2048
GRID_W = 64
HEAD_DIM = 64
GROUP_W = D_MODEL // 4
N_MOD = 9
NORM_EPS = 1e-6
ROPE_BASE = 10000.0
NEG_INF = -1e30

NA_HEADS = GROUP_W // HEAD_DIM
NA_WIN_R = 8
NA_WIN_C = 16
RW_HEADS = GROUP_W // HEAD_DIM
RW_DECAY_RANK = 32
RW_ICLR_RANK = 32
RW_GATE_RANK = 96
RW_GN_EPS = 64e-5
WA_HEADS = GROUP_W // HEAD_DIM
WA_KV_HEADS = 2
WA_GROUP = WA_HEADS // WA_KV_HEADS
WA_WINDOW = 128
S5_P = 16
S5_GROUPS = GROUP_W // S5_P
S5_N = 64

LANE = 128
QBLK = 2 * GRID_W
NA_KBLKS = 5
RW_CHUNK = 64
S5_CHUNK = 16
VMEM_LIMIT = 56 * 1024 * 1024

COL_AQ, COL_AK, COL_AV = 0, 512, 1024
COL_BR = 1536
COL_CQ = 3072
COL_DU = 3584
COL_BLR = 4096
COL_CK, COL_CV = 4352, 4480
D_IN_PAD = 4608

PB_R, PB_V, PB_KK, PB_G = 0, 1, 2, 9
PB_N = 10


def _cparams(sem):
    return pltpu.CompilerParams(dimension_semantics=sem, vmem_limit_bytes=VMEM_LIMIT)


def _dot(a, b):
    return jnp.dot(a, b, preferred_element_type=F32)


def _dot_nt(a, b):
    return lax.dot_general(a, b, (((1,), (1,)), ((), ())), preferred_element_type=F32)


def _dot_tn(a, b):
    return lax.dot_general(a, b, (((0,), (0,)), ((), ())), preferred_element_type=F32)


def _sigmoid(x):
    return 1.0 / (1.0 + jnp.exp(-x))


def _modulate(x, g, shift, scale):
    ms = jnp.mean(x * x, axis=-1, keepdims=True)
    return (x * lax.rsqrt(ms + NORM_EPS) * g) * (1.0 + scale) + shift


def _seg_sum(x, ones_bd):
    hi = x.astype(BF16)
    r1 = x - hi.astype(F32)
    mid = r1.astype(BF16)
    lo = (r1 - mid.astype(F32)).astype(BF16)
    return _dot(hi, ones_bd) + _dot(mid, ones_bd) + _dot(lo, ones_bd)


def _head_rmsnorm(x, ones_bd):
    ms = _seg_sum(x * x, ones_bd) * (1.0 / HEAD_DIM)
    return x * lax.rsqrt(ms + NORM_EPS)


def _ada_kernel(c_ref, w_ref, b_ref, o_ref):
    c = c_ref[...]
    s = (c * _sigmoid(c)).astype(BF16)
    o_ref[...] = _dot(s, w_ref[...].astype(BF16)) + b_ref[...]


def _ada_call(cc, w_ada, b_ada):
    L, D, N = w_ada.shape
    tn = 1024
    return pl.pallas_call(
        _ada_kernel,
        out_shape=jax.ShapeDtypeStruct((L, 8, N), F32),
        grid=(L, N // tn),
        in_specs=[
            pl.BlockSpec((8, D), lambda l, j: (0, 0)),
            pl.BlockSpec((None, D, tn), lambda l, j: (l, 0, j)),
            pl.BlockSpec((None, 1, tn), lambda l, j: (l, 0, j)),
        ],
        out_specs=pl.BlockSpec((None, 8, tn), lambda l, j: (l, 0, j)),
        compiler_params=_cparams(("arbitrary", "arbitrary")),
        name="adaln",
    )(cc, w_ada, b_ada.reshape(L, 1, N))


def _ffn_kernel(x_ref, mod_ref, g_ref, wg_ref, wu_ref, wd_ref, o_ref, h_sc, acc_sc, *, mi):
    j = pl.program_id(1)

    @pl.when(j == 0)
    def _():
        h = _modulate(x_ref[...], g_ref[...], mod_ref[mi:mi + 1, :], mod_ref[mi + 1:mi + 2, :])
        h_sc[...] = h.astype(BF16)
        acc_sc[...] = jnp.zeros_like(acc_sc)

    h = h_sc[...]
    gate = _dot(h, wg_ref[...])
    up = _dot(h, wu_ref[...])
    a = (gate * _sigmoid(gate) * up).astype(BF16)
    acc_sc[...] += _dot(a, wd_ref[...])

    @pl.when(j == pl.num_programs(1) - 1)
    def _():
        o_ref[...] = x_ref[...] + 0.5 * mod_ref[mi + 2:mi + 3, :] * acc_sc[...]


def _ffn_call(xs, mods, g, wg, wu, wd, l, mi, n_rows, rows_per_seq, n_batch, tm, tf):
    D = xs.shape[1]
    F = wg.shape[2]
    tps = rows_per_seq // tm
    return pl.pallas_call(
        functools.partial(_ffn_kernel, mi=mi),
        out_shape=jax.ShapeDtypeStruct((n_rows, D), F32),
        grid=(n_rows // tm, F // tf),
        in_specs=[
            pl.BlockSpec((tm, D), lambda i, j: (i, 0)),
            pl.BlockSpec((None, N_MOD, D), lambda i, j: (jnp.minimum(i // tps, n_batch), 0, 0)),
            pl.BlockSpec((1, D), lambda i, j: (0, 0)),
            pl.BlockSpec((None, D, tf), lambda i, j: (l, 0, j)),
            pl.BlockSpec((None, D, tf), lambda i, j: (l, 0, j)),
            pl.BlockSpec((None, tf, D), lambda i, j: (l, j, 0)),
        ],
        out_specs=pl.BlockSpec((tm, D), lambda i, j: (i, 0)),
        scratch_shapes=[pltpu.VMEM((tm, D), BF16), pltpu.VMEM((tm, D), F32)],
        compiler_params=_cparams(("arbitrary", "arbitrary")),
        name="ffn",
    )(xs, mods, g, wg, wu, wd)


def _win_kernel(x_ref, mod_ref, g_ref, w_ref, o_ref, h_sc):
    @pl.when(pl.program_id(1) == 0)
    def _():
        h = _modulate(x_ref[...], g_ref[...], mod_ref[3:4, :], mod_ref[4:5, :])
        h_sc[...] = h.astype(BF16)

    o_ref[...] = _dot(h_sc[...], w_ref[...])


def _win_call(xs, mods, g, w_in, l, rows_per_seq, n_batch, tm, tn):
    R, D = xs.shape
    N = w_in.shape[2]
    tps = rows_per_seq // tm
    return pl.pallas_call(
        _win_kernel,
        out_shape=jax.ShapeDtypeStruct((R, N), F32),
        grid=(R // tm, N // tn),
        in_specs=[
            pl.BlockSpec((tm, D), lambda i, j: (i, 0)),
            pl.BlockSpec((None, N_MOD, D), lambda i, j: (jnp.minimum(i // tps, n_batch), 0, 0)),
            pl.BlockSpec((1, D), lambda i, j: (0, 0)),
            pl.BlockSpec((None, D, tn), lambda i, j: (l, 0, j)),
        ],
        out_specs=pl.BlockSpec((tm, tn), lambda i, j: (i, j)),
        scratch_shapes=[pltpu.VMEM((tm, D), BF16)],
        compiler_params=_cparams(("arbitrary", "arbitrary")),
        name="in_proj",
    )(xs, mods, g, w_in)


def _wout_kernel(x_ref, oa_ref, ob_ref, oc_ref, od_ref, mod_ref, w_ref, o_ref):
    acc = _dot(oa_ref[...], w_ref[0 * GROUP_W:1 * GROUP_W, :])
    acc += _dot(ob_ref[...], w_ref[1 * GROUP_W:2 * GROUP_W, :])
    acc += _dot(oc_ref[...], w_ref[2 * GROUP_W:3 * GROUP_W, :])
    acc += _dot(od_ref[...], w_ref[3 * GROUP_W:4 * GROUP_W, :])
    o_ref[...] = x_ref[...] + mod_ref[5:6, :] * acc


def _wout_call(xs, outs, mods, w_out, l, n_rows, rows_per_seq, n_batch, tm):
    D = xs.shape[1]
    tps = rows_per_seq // tm
    ospec = pl.BlockSpec((tm, GROUP_W), lambda i: (i, 0))
    return pl.pallas_call(
        _wout_kernel,
        out_shape=jax.ShapeDtypeStruct((n_rows, D), F32),
        grid=(n_rows // tm,),
        in_specs=[
            pl.BlockSpec((tm, D), lambda i: (i, 0)),
            ospec, ospec, ospec, ospec,
            pl.BlockSpec((None, N_MOD, D), lambda i: (jnp.minimum(i // tps, n_batch), 0, 0)),
            pl.BlockSpec((None, D, D), lambda i: (l, 0, 0)),
        ],
        out_specs=pl.BlockSpec((tm, D), lambda i: (i, 0)),
        compiler_params=_cparams(("arbitrary",)),
        name="out_proj",
    )(xs, *outs, mods, w_out)


def _na_case_reps(n_qb):
    return (0, 1, 2, n_qb - 2, n_qb - 1)


def _na_start(p, n_qb):
    return jnp.clip(p - 2, 0, n_qb - NA_KBLKS)


def _na_bias_table(rpb, n_qb):
    H = rpb.shape[0]
    rows = 2 * n_qb
    qc = np.arange(GRID_W)[:, None]
    kc = np.arange(GRID_W)[None, :]
    cs = np.clip(qc - NA_WIN_C // 2, 0, GRID_W - NA_WIN_C)
    okc = (kc >= cs) & (kc < cs + NA_WIN_C)
    dc = np.where(okc, kc - qc + NA_WIN_C - 1, 0)
    rp = rpb.astype(F32).reshape(H, 2 * NA_WIN_R - 1, 2 * NA_WIN_C - 1)
    blocks = jnp.where(jnp.asarray(okc)[None, None], jnp.take(rp, jnp.asarray(dc), axis=2), NEG_INF)
    neg = jnp.full((H, GRID_W, GRID_W), NEG_INF, F32)
    cases = []
    for p in _na_case_reps(n_qb):
        start = min(max(p - 2, 0), n_qb - NA_KBLKS)
        qrows = []
        for qr in range(2):
            qa = 2 * p + qr
            rs = min(max(qa - NA_WIN_R // 2, 0), rows - NA_WIN_R)
            krow = []
            for kr in range(2 * NA_KBLKS):
                ka = 2 * start + kr
                krow.append(blocks[:, ka - qa + NA_WIN_R - 1] if rs <= ka < rs + NA_WIN_R else neg)
            qrows.append(jnp.concatenate(krow, axis=-1))
        cases.append(jnp.concatenate(qrows, axis=-2))
    return jnp.stack(cases, axis=0)


def _na_kernel(q_ref, k_ref, v_ref, kc_ref, vc_ref, bias_ref, qg_ref, kg_ref, ones_ref, o_ref,
               kn_sc, vn_sc, kcn_sc, vcn_sc, *, n_qb, seq, ctx):
    p = pl.program_id(2)
    ones_bd = ones_ref[...]
    scale = HEAD_DIM ** -0.5

    @pl.when(p == 0)
    def _():
        kg = kg_ref[...]
        rows = 512

        def body(i, carry):
            sl = pl.ds(pl.multiple_of(i * rows, rows), rows)
            kn_sc[sl, :] = (_head_rmsnorm(k_ref[sl, :], ones_bd) * kg).astype(BF16)
            vn_sc[sl, :] = v_ref[sl, :].astype(BF16)
            return carry

        lax.fori_loop(0, seq // rows, body, 0)
        kcn_sc[...] = (_head_rmsnorm(kc_ref[...], ones_bd) * kg).astype(BF16)
        vcn_sc[...] = vc_ref[...].astype(BF16)

    q = (_head_rmsnorm(q_ref[...], ones_bd) * (qg_ref[...] * scale)).astype(BF16)
    kcn = kcn_sc[...]
    vcn = vcn_sc[...]

    def finish(outs):
        o_ref[...] = jnp.concatenate(outs, axis=-1).astype(o_ref.dtype)

    @pl.when(p < n_qb)
    def _():
        start = _na_start(p, n_qb)
        case = p - start
        sl = pl.ds(pl.multiple_of(start * QBLK, QBLK), NA_KBLKS * QBLK)
        kw = kn_sc[sl, :]
        vw = vn_sc[sl, :]
        outs = []
        for h in range(2):
            hs = slice(h * HEAD_DIM, (h + 1) * HEAD_DIM)
            qh = q[:, hs]
            s_n = _dot_nt(qh, kw[:, hs]) + bias_ref[case, h]
            s_c = _dot_nt(qh, kcn[:, hs])
            m = jnp.maximum(jnp.max(s_n, axis=-1, keepdims=True), jnp.max(s_c, axis=-1, keepdims=True))
            p_n = jnp.exp(s_n - m)
            p_c = jnp.exp(s_c - m)
            den = jnp.sum(p_n, axis=-1, keepdims=True) + jnp.sum(p_c, axis=-1, keepdims=True)
            o = _dot(p_n.astype(BF16), vw[:, hs]) + _dot(p_c.astype(BF16), vcn[:, hs])
            outs.append(o / den)
        finish(outs)

    @pl.when(p >= n_qb)
    def _():
        outs = []
        for h in range(2):
            hs = slice(h * HEAD_DIM, (h + 1) * HEAD_DIM)
            s_c = _dot_nt(q[:, hs], kcn[:, hs])
            m = jnp.max(s_c, axis=-1, keepdims=True)
            p_c = jnp.exp(s_c - m)
            den = jnp.sum(p_c, axis=-1, keepdims=True)
            outs.append(_dot(p_c.astype(BF16), vcn[:, hs]) / den)
        finish(outs)


def _na_call(proj, bias, qg, kg, ones_bd, n_batch, seq, ctx, want_ctx):
    R = proj.shape[0]
    n_qb = seq // QBLK
    n_cb = ctx // QBLK
    steps = n_qb + (n_cb if want_ctx else 0)
    lat_qb = n_batch * n_qb
    ctx_kb = n_batch * seq // ctx

    def qrow(b, p):
        return jnp.where(p < n_qb, b * n_qb + p, lat_qb + b * n_cb + (p - n_qb))

    cq, ck, cv = COL_AQ // LANE, COL_AK // LANE, COL_AV // LANE
    return pl.pallas_call(
        functools.partial(_na_kernel, n_qb=n_qb, seq=seq, ctx=ctx),
        out_shape=jax.ShapeDtypeStruct((R, GROUP_W), BF16),
        grid=(n_batch, NA_HEADS // 2, steps),
        in_specs=[
            pl.BlockSpec((QBLK, LANE), lambda b, hp, p: (qrow(b, p), cq + hp)),
            pl.BlockSpec((seq, LANE), lambda b, hp, p: (b, ck + hp)),
            pl.BlockSpec((seq, LANE), lambda b, hp, p: (b, cv + hp)),
            pl.BlockSpec((ctx, LANE), lambda b, hp, p: (ctx_kb + b, ck + hp)),
            pl.BlockSpec((ctx, LANE), lambda b, hp, p: (ctx_kb + b, cv + hp)),
            pl.BlockSpec((5, 2, QBLK, NA_KBLKS * QBLK), lambda b, hp, p: (0, hp, 0, 0)),
            pl.BlockSpec((1, LANE), lambda b, hp, p: (0, 0)),
            pl.BlockSpec((1, LANE), lambda b, hp, p: (0, 0)),
            pl.BlockSpec((LANE, LANE), lambda b, hp, p: (0, 0)),
        ],
        out_specs=pl.BlockSpec((QBLK, LANE), lambda b, hp, p: (qrow(b, p), hp)),
        scratch_shapes=[pltpu.VMEM((seq, LANE), BF16), pltpu.VMEM((seq, LANE), BF16),
                        pltpu.VMEM((ctx, LANE), BF16), pltpu.VMEM((ctx, LANE), BF16)],
        compiler_params=_cparams(("arbitrary", "arbitrary", "arbitrary")),
        name="mixer_a",
    )(proj, proj, proj, proj, proj, bias, qg, kg, ones_bd)


def _rope(x, cos, sin_signed, first_half):
    rot = jnp.where(first_half, pltpu.roll(x, LANE - 16, 1), pltpu.roll(x, 16, 1))
    return x * cos + rot * sin_signed


def _wa_kernel(sink_ref, q_ref, k_ref, v_ref, kc_ref, vc_ref, cosk_ref, sin_k_ref, cosq_ref, sinq_ref,
               qg_ref, kg_ref, ones_ref, o_ref, kn_sc, vn_sc, kcn_sc, vcn_sc, *, n_qb, seq, ctx):
    p = pl.program_id(1)
    ones_bd = ones_ref[...]
    scale = HEAD_DIM ** -0.5
    lane = lax.broadcasted_iota(jnp.int32, (1, LANE), 1)
    first_half = (lane % 32) < 16

    @pl.when(p == 0)
    def _():
        kg = kg_ref[...]
        rows = 512

        def body(i, carry):
            sl = pl.ds(pl.multiple_of(i * rows, rows), rows)
            kn = _head_rmsnorm(k_ref[sl, :], ones_bd) * kg
            kn_sc[sl, :] = _rope(kn, cosk_ref[sl, :], sin_k_ref[sl, :], first_half).astype(BF16)
            vn_sc[sl, :] = v_ref[sl, :].astype(BF16)
            return carry

        lax.fori_loop(0, seq // rows, body, 0)
        kcn_sc[...] = (_head_rmsnorm(kc_ref[...], ones_bd) * kg).astype(BF16)
        vcn_sc[...] = vc_ref[...].astype(BF16)

    qg = qg_ref[...] * scale
    is_lat = p < n_qb
    cosq = jnp.where(is_lat, cosq_ref[...], 1.0)
    sinq = jnp.where(is_lat, sinq_ref[...], 0.0)
    qh = []
    for c in range(4):
        qn = _head_rmsnorm(q_ref[:, c * LANE:(c + 1) * LANE], ones_bd) * qg
        qn = _rope(qn, cosq, sinq, first_half).astype(BF16)
        qh += [qn[:, :HEAD_DIM], qn[:, HEAD_DIM:]]
    kcn = kcn_sc[...]
    vcn = vcn_sc[...]

    def sink_col(kh):
        return jnp.concatenate(
            [jnp.full((QBLK, 1), sink_ref[kh * WA_GROUP + g], F32) for g in range(WA_GROUP)], axis=0)

    def finish(o_heads):
        o_ref[...] = jnp.concatenate(o_heads, axis=-1).astype(o_ref.dtype)

    @pl.when(is_lat)
    def _():
        ws = jnp.clip((p - 1) * QBLK, 0, seq - 3 * QBLK)
        sl = pl.ds(pl.multiple_of(ws, QBLK), 3 * QBLK)
        kw = kn_sc[sl, :]
        vw = vn_sc[sl, :]
        shp = (WA_GROUP * QBLK, 3 * QBLK)
        qpos = p * QBLK + (lax.broadcasted_iota(jnp.int32, shp, 0) & (QBLK - 1))
        kpos = ws + lax.broadcasted_iota(jnp.int32, shp, 1)
        ok = jnp.abs(kpos - qpos) <= WA_WINDOW
        o_heads = []
        for kh in range(WA_KV_HEADS):
            hs = slice(kh * HEAD_DIM, (kh + 1) * HEAD_DIM)
            qs = jnp.concatenate(qh[kh * WA_GROUP:(kh + 1) * WA_GROUP], axis=0)
            s_w = jnp.where(ok, _dot_nt(qs, kw[:, hs]), NEG_INF)
            s_c = _dot_nt(qs, kcn[:, hs])
            sk = sink_col(kh)
            m = jnp.maximum(jnp.maximum(jnp.max(s_w, axis=-1, keepdims=True),
                                        jnp.max(s_c, axis=-1, keepdims=True)), sk)
            p_w = jnp.exp(s_w - m)
            p_c = jnp.exp(s_c - m)
            den = (jnp.sum(p_w, axis=-1, keepdims=True) + jnp.sum(p_c, axis=-1, keepdims=True)
                   + jnp.exp(sk - m))
            o = (_dot(p_w.astype(BF16), vw[:, hs]) + _dot(p_c.astype(BF16), vcn[:, hs])) / den
            o_heads += [o[g * QBLK:(g + 1) * QBLK, :] for g in range(WA_GROUP)]
        finish(o_heads)

    @pl.when(jnp.logical_not(is_lat))
    def _():
        o_heads = []
        for kh in range(WA_KV_HEADS):
            hs = slice(kh * HEAD_DIM, (kh + 1) * HEAD_DIM)
            qs = jnp.concatenate(qh[kh * WA_GROUP:(kh + 1) * WA_GROUP], axis=0)
            s_c = _dot_nt(qs, kcn[:, hs])
            sk = sink_col(kh)
            m = jnp.maximum(jnp.max(s_c, axis=-1, keepdims=True), sk)
            p_c = jnp.exp(s_c - m)
            den = jnp.sum(p_c, axis=-1, keepdims=True) + jnp.exp(sk - m)
            o = _dot(p_c.astype(BF16), vcn[:, hs]) / den
            o_heads += [o[g * QBLK:(g + 1) * QBLK, :] for g in range(WA_GROUP)]
        finish(o_heads)


def _wa_call(proj, sink, cos_t, sin_t, qg, kg, ones_bd, n_batch, seq, ctx, want_ctx):
    R = proj.shape[0]
    n_qb = seq // QBLK
    n_cb = ctx // QBLK
    steps = n_qb + (n_cb if want_ctx else 0)
    lat_qb = n_batch * n_qb
    ctx_kb = n_batch * seq // ctx

    def qrow(b, p):
        return jnp.where(p < n_qb, b * n_qb + p, lat_qb + b * n_cb + (p - n_qb))

    ck, cv = COL_CK // LANE, COL_CV // LANE
    return pl.pallas_call(
        functools.partial(_wa_kernel, n_qb=n_qb, seq=seq, ctx=ctx),
        out_shape=jax.ShapeDtypeStruct((R, GROUP_W), BF16),
        grid=(n_batch, steps),
        in_specs=[
            pl.BlockSpec(memory_space=pltpu.SMEM),
            pl.BlockSpec((QBLK, GROUP_W), lambda b, p: (qrow(b, p), COL_CQ // GROUP_W)),
            pl.BlockSpec((seq, LANE), lambda b, p: (b, ck)),
            pl.BlockSpec((seq, LANE), lambda b, p: (b, cv)),
            pl.BlockSpec((ctx, LANE), lambda b, p: (ctx_kb + b, ck)),
            pl.BlockSpec((ctx, LANE), lambda b, p: (ctx_kb + b, cv)),
            pl.BlockSpec((seq, LANE), lambda b, p: (0, 0)),
            pl.BlockSpec((seq, LANE), lambda b, p: (0, 0)),
            pl.BlockSpec((QBLK, LANE), lambda b, p: (jnp.minimum(p, n_qb - 1), 0)),
            pl.BlockSpec((QBLK, LANE), lambda b, p: (jnp.minimum(p, n_qb - 1), 0)),
            pl.BlockSpec((1, LANE), lambda b, p: (0, 0)),
            pl.BlockSpec((1, LANE), lambda b, p: (0, 0)),
            pl.BlockSpec((LANE, LANE), lambda b, p: (0, 0)),
        ],
        out_specs=pl.BlockSpec((QBLK, GROUP_W), lambda b, p: (qrow(b, p), 0)),
        scratch_shapes=[pltpu.VMEM((seq, LANE), BF16), pltpu.VMEM((seq, LANE), BF16),
                        pltpu.VMEM((ctx, LANE), BF16), pltpu.VMEM((ctx, LANE), BF16)],
        compiler_params=_cparams(("arbitrary", "arbitrary")),
        name="mixer_c",
    )(sink, proj, proj, proj, proj, proj, cos_t, sin_t, cos_t, sin_t, qg, kg, ones_bd)


def _rw_prep_kernel(x_ref, prev_ref, next_ref, lr_ref, conv_ref, wlr_ref, w0_ref, a0_ref, kk_ref, ka_ref,
                    ones_ref, o_ref, *, tiles_lat, tiles_ctx, n_lat_tiles):
    i = pl.program_id(0)
    tm = x_ref.shape[0]
    in_lat = i < n_lat_tiles
    first = jnp.where(in_lat, i % tiles_lat == 0, (i - n_lat_tiles) % tiles_ctx == 0)
    last = jnp.where(in_lat, i % tiles_lat == tiles_lat - 1, (i - n_lat_tiles) % tiles_ctx == tiles_ctx - 1)
    x = x_ref[...]
    prev_row = jnp.where(first, 0.0, prev_ref[7:8, :])
    next_row = jnp.where(last, 0.0, next_ref[0:1, :])
    row = lax.broadcasted_iota(jnp.int32, (tm, 1), 0)
    x_prev = jnp.where(row == 0, prev_row, pltpu.roll(x, 1, 0))
    x_next = jnp.where(row == tm - 1, next_row, pltpu.roll(x, tm - 1, 0))
    cw = conv_ref[...]
    y = x_prev * cw[0:1, :] + x * cw[1:2, :] + x_next * cw[2:3, :]
    r = y[:, 0:GROUP_W]
    k = y[:, GROUP_W:2 * GROUP_W]
    v = y[:, 2 * GROUP_W:3 * GROUP_W]

    lr = lr_ref[...]
    lane = lax.broadcasted_iota(jnp.int32, (1, lr.shape[1]), 1)
    c1 = RW_DECAY_RANK
    c2 = c1 + RW_ICLR_RANK
    c3 = c2 + RW_GATE_RANK
    act = jnp.where(lane < c1, jnp.tanh(lr),
                    jnp.where(lane < c2, lr, jnp.where(lane < c3, _sigmoid(lr), 0.0)))
    up = jnp.dot(act, wlr_ref[...], preferred_element_type=F32, precision=lax.Precision.HIGHEST)

    ones_bd = ones_ref[...]
    kk = k * kk_ref[...]
    nrm = jnp.sqrt(_seg_sum(kk * kk, ones_bd))
    kk = kk / jnp.maximum(nrm, 1e-12)

    o_ref[:, PB_R * GROUP_W:(PB_R + 1) * GROUP_W] = r
    o_ref[:, PB_V * GROUP_W:(PB_V + 1) * GROUP_W] = v
    o_ref[:, PB_KK * GROUP_W:(PB_KK + 1) * GROUP_W] = kk
    o_ref[:, PB_G * GROUP_W:(PB_G + 1) * GROUP_W] = up[:, 4 * GROUP_W:5 * GROUP_W]
    ka = ka_ref[...]
    for d in range(2):
        z = w0_ref[d:d + 1, :] + up[:, d * GROUP_W:(d + 1) * GROUP_W]
        sp = jnp.maximum(-z, 0.0) + jnp.log(1.0 + jnp.exp(-jnp.abs(z)))
        lw = -jnp.exp(-sp - 0.5)
        a = _sigmoid(a0_ref[d:d + 1, :] + up[:, (2 + d) * GROUP_W:(3 + d) * GROUP_W])
        kd = k * (1.0 + (a - 1.0) * ka)
        base = 3 + 3 * d
        o_ref[:, base * GROUP_W:(base + 1) * GROUP_W] = lw
        o_ref[:, (base + 1) * GROUP_W:(base + 2) * GROUP_W] = a
        o_ref[:, (base + 2) * GROUP_W:(base + 3) * GROUP_W] = kd


def _rw_prep_call(proj, conv_w, wlr, w0, a0, k_k, k_a, ones512, n_batch, seq, ctx, tm):
    R = proj.shape[0]
    n_lat_tiles = n_batch * seq // tm
    nb8 = R // 8
    t8 = tm // 8
    wide = 3 * GROUP_W
    cb = COL_BR // wide
    return pl.pallas_call(
        functools.partial(_rw_prep_kernel, tiles_lat=seq // tm, tiles_ctx=ctx // tm, n_lat_tiles=n_lat_tiles),
        out_shape=jax.ShapeDtypeStruct((R, PB_N * GROUP_W), F32),
        grid=(R // tm,),
        in_specs=[
            pl.BlockSpec((tm, wide), lambda i: (i, cb)),
            pl.BlockSpec((8, wide), lambda i: (jnp.maximum(i * t8 - 1, 0), cb)),
            pl.BlockSpec((8, wide), lambda i: (jnp.minimum((i + 1) * t8, nb8 - 1), cb)),
            pl.BlockSpec((tm, 256), lambda i: (i, COL_BLR // 256)),
            pl.BlockSpec((3, wide), lambda i: (0, 0)),
            pl.BlockSpec((256, 5 * GROUP_W), lambda i: (0, 0)),
            pl.BlockSpec((2, GROUP_W), lambda i: (0, 0)),
            pl.BlockSpec((2, GROUP_W), lambda i: (0, 0)),
            pl.BlockSpec((1, GROUP_W), lambda i: (0, 0)),
            pl.BlockSpec((1, GROUP_W), lambda i: (0, 0)),
            pl.BlockSpec((GROUP_W, GROUP_W), lambda i: (0, 0)),
        ],
        out_specs=pl.BlockSpec((tm, PB_N * GROUP_W), lambda i: (i, 0)),
        compiler_params=_cparams(("arbitrary",)),
        name="mixer_b_prep",
    )(proj, proj, proj, proj, conv_w, wlr, w0, a0, k_k, k_a, ones512)


RW_SPLIT = 2
_NN = ((1,), (0,))
_NT = ((1,), (1,))
_TN = ((0,), (0,))


def _split(x, n):
    parts = []
    for _ in range(n):
        p = x.astype(BF16)
        parts.append(p)
        x = x - p.astype(F32)
    return parts


def _sdot(a, b, dims):
    n = max(len(a), len(b))
    acc = None
    for i, ai in enumerate(a):
        for j, bj in enumerate(b):
            if i + j < n:
                t = lax.dot_general(ai, bj, (dims, ((), ())), preferred_element_type=F32)
                acc = t if acc is None else acc + t
    return acc


def _rw_chunks(at, bt, kt, rt, v, gam, s0, strict, incl):
    n = len(at)
    ids = range(n)
    C = at[0].shape[0]
    sp = lambda t: _split(t, RW_SPLIT)
    ar = [sp(jnp.concatenate([at[i], rt[i]], axis=0)) for i in ids]
    bk = [sp(jnp.concatenate([bt[i], kt[i]], axis=0)) for i in ids]
    g4 = [_sdot(ar[i], bk[i], _NT) for i in ids]
    a_ab = [jnp.where(strict[i], g4[i][:C, :C], 0.0) for i in ids]
    a_ak = [jnp.where(strict[i], g4[i][:C, C:], 0.0) for i in ids]
    a_rb = [jnp.where(incl[i], g4[i][C:, :C], 0.0) for i in ids]
    a_rk = [jnp.where(incl[i], g4[i][C:, C:], 0.0) for i in ids]
    vs = [sp(v[i]) for i in ids]
    wv = [_sdot(sp(a_ak[i]), vs[i], _NN) for i in ids]
    x = [jnp.concatenate([at[i], wv[i]], axis=1) for i in ids]
    nm = a_ab
    steps = int(math.log2(C))
    for st in range(steps):
        nb = [sp(nm[i]) for i in ids]
        x = [x[i] + _sdot(nb[i], sp(x[i]), _NN) for i in ids]
        if st < steps - 1:
            nm = [_sdot(nb[i], nb[i], _NN) for i in ids]
    qv = [sp(jnp.concatenate([x[i][:, HEAD_DIM:], v[i]], axis=0)) for i in ids]
    pb = [sp(x[i][:, :HEAD_DIM]) for i in ids]
    s0b = [sp(s0[i]) for i in ids]
    y_loc = [_sdot(sp(jnp.concatenate([a_rb[i], a_rk[i]], axis=1)), qv[i], _NN) for i in ids]
    rp = [rt[i] + _sdot(sp(a_rb[i]), pb[i], _NN) for i in ids]
    w = [_sdot(s0b[i], pb[i], _NT) for i in ids]
    y = [y_loc[i] + _sdot(sp(rp[i]), s0b[i], _NT) for i in ids]
    s1 = [(s0[i] + _sdot(sp(w[i]), sp(bt[i]), _NN) + _sdot(qv[i], bk[i], _TN)) * gam[i] for i in ids]
    return y, s1


def _rw_scan_kernel(rf_ref, vf_ref, kkf_ref, lwf_ref, af_ref, kf_ref,
                    rb_ref, vb_ref, kkb_ref, lwb_ref, ab_ref, kb_ref, yf_ref, yb_ref, s_sc):
    s = pl.program_id(1)

    @pl.when(s == 0)
    def _():
        s_sc[...] = jnp.zeros_like(s_sc)

    C = rf_ref.shape[0]
    row = lax.broadcasted_iota(jnp.int32, (C, C), 0)
    col = lax.broadcasted_iota(jnp.int32, (C, C), 1)
    chains = dict(at=[], bt=[], kt=[], rt=[], v=[], gam=[], s0=[], strict=[], incl=[])
    dirs = ((False, rf_ref, vf_ref, kkf_ref, lwf_ref, af_ref, kf_ref),
            (True, rb_ref, vb_ref, kkb_ref, lwb_ref, ab_ref, kb_ref))
    for d, (reverse, r_ref, v_ref, kk_ref, lw_ref, a_ref, k_ref) in enumerate(dirs):
        incl, strict = (row <= col, row < col) if reverse else (row >= col, row > col)
        lw = lw_ref[...]
        kk = kk_ref[...]
        v = v_ref[...]
        cum = jnp.dot(incl.astype(F32), lw, preferred_element_type=F32, precision=lax.Precision.HIGHEST)
        e_l = jnp.exp(cum)
        e_n = jnp.exp(-cum)
        at = -(kk * jnp.exp(cum - lw))
        bt = kk * a_ref[...] * e_n
        kt = k_ref[...] * e_n
        rt = r_ref[...] * e_l
        gam = e_l[0:1, :] if reverse else e_l[C - 1:C, :]
        for h in range(RW_HEADS):
            hs = slice(h * HEAD_DIM, (h + 1) * HEAD_DIM)
            for name, val in (("at", at), ("bt", bt), ("kt", kt), ("rt", rt), ("v", v), ("gam", gam)):
                chains[name].append(val[:, hs])
            chains["s0"].append(s_sc[d, h])
            chains["strict"].append(strict)
            chains["incl"].append(incl)
    y, s1 = _rw_chunks(**chains)
    for d in range(2):
        for h in range(RW_HEADS):
            s_sc[d, h] = s1[d * RW_HEADS + h]
    yf_ref[...] = jnp.concatenate(y[:RW_HEADS], axis=-1)
    yb_ref[...] = jnp.concatenate(y[RW_HEADS:], axis=-1)


def _rw_scan_call(prep, n_batch, seq, ctx):
    R = prep.shape[0]
    C = RW_CHUNK
    nc, nl = ctx // C, seq // C
    ctx_base = n_batch * nl

    def blk_f(b, s):
        return jnp.where(s < nc, ctx_base + b * nc + s, b * nl + (s - nc))

    def blk_b(b, s):
        return jnp.where(s < nc, ctx_base + b * nc + (nc - 1 - s), b * nl + (nl - 1 - (s - nc)))

    def col(blk, cb):
        return pl.BlockSpec((C, GROUP_W), lambda b, s: (blk(b, s), cb))

    in_specs = ([col(blk_f, c) for c in (PB_R, PB_V, PB_KK, 3, 4, 5)]
                + [col(blk_b, c) for c in (PB_R, PB_V, PB_KK, 6, 7, 8)])
    out = jax.ShapeDtypeStruct((R, GROUP_W), F32)
    return pl.pallas_call(
        _rw_scan_kernel,
        out_shape=(out, out),
        grid=(n_batch, nc + nl),
        in_specs=in_specs,
        out_specs=(col(blk_f, 0), col(blk_b, 0)),
        scratch_shapes=[pltpu.VMEM((2, RW_HEADS, HEAD_DIM, HEAD_DIM), F32)],
        compiler_params=_cparams(("arbitrary", "arbitrary")),
        name="mixer_b_scan",
    )(*([prep] * 12))


def _rw_readout_kernel(r_ref, v_ref, k0_ref, k1_ref, g_ref, yf_ref, yb_ref, rk_ref, gw_ref, gb_ref, ones_ref,
                       o_ref):
    ones_bd = ones_ref[...]
    v = v_ref[...]
    bonus = _seg_sum(r_ref[...] * (k0_ref[...] + k1_ref[...]) * rk_ref[...], ones_bd)
    y = yf_ref[...] + yb_ref[...] + bonus * v
    mu = _seg_sum(y, ones_bd) * (1.0 / HEAD_DIM)
    yc = y - mu
    var = _seg_sum(yc * yc, ones_bd) * (1.0 / HEAD_DIM)
    yn = yc * lax.rsqrt(var + RW_GN_EPS)
    o_ref[...] = ((yn * gw_ref[...] + gb_ref[...]) * g_ref[...]).astype(o_ref.dtype)


def _rw_readout_call(prep, yf, yb, r_k, gn_w, gn_b, ones512, tm):
    R = prep.shape[0]
    col = lambda cb: pl.BlockSpec((tm, GROUP_W), lambda i: (i, cb))
    vec = pl.BlockSpec((1, GROUP_W), lambda i: (0, 0))
    return pl.pallas_call(
        _rw_readout_kernel,
        out_shape=jax.ShapeDtypeStruct((R, GROUP_W), BF16),
        grid=(R // tm,),
        in_specs=[col(PB_R), col(PB_V), col(5), col(8), col(PB_G), col(0), col(0), vec, vec, vec,
                  pl.BlockSpec((GROUP_W, GROUP_W), lambda i: (0, 0))],
        out_specs=col(0),
        compiler_params=_cparams(("arbitrary",)),
        name="mixer_b_readout",
    )(prep, prep, prep, prep, prep, yf, yb, r_k, gn_w, gn_b, ones512)


def _s5_tables(a_re, a_im, log_dt, b_re, b_im, c_re, c_im):
    Lc, G, N, P = S5_CHUNK, S5_GROUPS, S5_N, S5_P
    dt = jnp.exp(log_dt)[..., None]
    lam_re, lam_im = dt * a_re, dt * a_im
    tau = jnp.arange(Lc + 1, dtype=F32)[:, None, None, None]
    mag = jnp.exp(tau * lam_re)
    pw_re, pw_im = mag * jnp.cos(tau * lam_im), mag * jnp.sin(tau * lam_im)
    ab_re, ab_im = pw_re[1], pw_im[1]
    den = a_re * a_re + a_im * a_im
    nr = ab_re - 1.0
    cf_re, cf_im = (nr * a_re + ab_im * a_im) / den, (ab_im * a_re - nr * a_im) / den
    bp_re = cf_re[..., None] * b_re[None] - cf_im[..., None] * b_im[None]
    bp_im = cf_re[..., None] * b_im[None] + cf_im[..., None] * b_re[None]
    pb_re = pw_re[..., None] * bp_re[None] - pw_im[..., None] * bp_im[None]
    pb_im = pw_re[..., None] * bp_im[None] + pw_im[..., None] * bp_re[None]
    kk = (jnp.einsum('gqn,tdgnp->tdgqp', c_re, pb_re[:Lc]) - jnp.einsum('gqn,tdgnp->tdgqp', c_im, pb_im[:Lc]))
    jj = np.arange(Lc)[:, None]
    ii = np.arange(Lc)[None, :]
    dist = np.abs(ii - jj)
    kf = kk[dist, 0] * jnp.asarray(ii >= jj, F32)[..., None, None, None]
    kb = kk[dist, 1] * jnp.asarray(ii <= jj, F32)[..., None, None, None]
    tz = jnp.transpose(kf + kb, (2, 0, 4, 1, 3)).reshape(G, Lc * P, Lc * P)
    jr = np.arange(Lc)

    def emap(pbx, order, d):
        return jnp.transpose(pbx[order, d], (1, 0, 3, 2)).reshape(G, Lc * P, N)

    ef_re, ef_im = emap(pb_re, Lc - 1 - jr, 0), emap(pb_im, Lc - 1 - jr, 0)
    eb_re, eb_im = emap(pb_re, jr, 1), emap(pb_im, jr, 1)
    def gmap(order, d):
        pr, pi = pw_re[order, d], pw_im[order, d]
        cp_re = c_re[None] * pr[:, :, None, :] - c_im[None] * pi[:, :, None, :]
        cp_im = c_re[None] * pi[:, :, None, :] + c_im[None] * pr[:, :, None, :]
        to = lambda t: jnp.transpose(t, (1, 3, 0, 2)).reshape(G, N, Lc * P)
        return to(cp_re), to(-cp_im)

    gf_re, gf_im = gmap(jr + 1, 0)
    gb_re, gb_im = gmap(Lc - jr, 1)
    eye2 = jnp.eye(2, dtype=F32)

    def pair_bd(t):
        r, c = t.shape[1:]
        t = t.reshape(G // 2, 2, r, c)
        return jnp.einsum('hark,ab->harbk', t, eye2).reshape(G // 2, 2 * r, 2 * c)

    wz = jnp.concatenate([pair_bd(tz), pair_bd(ef_re), pair_bd(ef_im), pair_bd(eb_re), pair_bd(eb_im)], axis=-1)
    gm = jnp.concatenate([pair_bd(gf_re), pair_bd(gf_im), pair_bd(gb_re), pair_bd(gb_im)], axis=1)
    apow = jnp.stack([pw_re[Lc, 0], pw_im[Lc, 0], pw_re[Lc, 1], pw_im[Lc, 1]], axis=1)
    apow = jnp.transpose(apow.reshape(G // 2, 2, 4, N), (0, 2, 1, 3)).reshape(G // 2, 4, 2 * N)
    return wz, gm, apow


def _s5_kernel(u_ref, wz_ref, gm_ref, ap_ref, y_ref, e_sc, xin_sc, *, n_batch, nlc, ncc):
    W = 2 * S5_CHUNK * S5_P
    NS = 2 * S5_N
    z = _dot(u_ref[...], wz_ref[...])
    y_ref[...] = z[:, :W]
    e_sc[...] = z[:, W:]
    af_re, af_im = ap_ref[0:1, :], ap_ref[1:2, :]
    ab_re, ab_im = ap_ref[2:3, :], ap_ref[3:4, :]
    nb = n_batch
    gs = 8 // nb

    def step(a_re, a_im, x_re, x_im, e):
        return a_re * x_re - a_im * x_im + e[:, :NS], a_re * x_im + a_im * x_re + e[:, NS:]

    def phase(c0, n_chunks, carry):
        n_groups = n_chunks // gs

        def body(gi, carry):
            xf_re, xf_im, xb_re, xb_im = carry
            rf = pl.ds(pl.multiple_of((c0 // gs + gi) * 8, 8), 8)
            rb = pl.ds(pl.multiple_of((c0 // gs + n_groups - 1 - gi) * 8, 8), 8)
            ef = e_sc[rf, 0:2 * NS]
            eb = e_sc[rb, 2 * NS:4 * NS]
            xf_in, xb_in = [], [None] * gs
            for j in range(gs):
                xf_in.append(jnp.concatenate([xf_re, xf_im], axis=-1))
                xf_re, xf_im = step(af_re, af_im, xf_re, xf_im, ef[j * nb:(j + 1) * nb, :])
            for j in reversed(range(gs)):
                xb_in[j] = jnp.concatenate([xb_re, xb_im], axis=-1)
                xb_re, xb_im = step(ab_re, ab_im, xb_re, xb_im, eb[j * nb:(j + 1) * nb, :])
            xin_sc[rf, 0:2 * NS] = jnp.concatenate(xf_in, axis=0)
            xin_sc[rb, 2 * NS:4 * NS] = jnp.concatenate(xb_in, axis=0)
            return xf_re, xf_im, xb_re, xb_im

        return lax.fori_loop(0, n_groups, body, carry)

    zero = jnp.zeros((nb, NS), F32)
    carry = phase(nlc, ncc, (zero, zero, zero, zero))
    phase(0, nlc, carry)
    y_ref[...] += _dot(xin_sc[...].astype(BF16), gm_ref[...])


def _s5_call(u_pairs, wz, gm, apow, n_batch, seq, ctx):
    GP, NR, W = u_pairs.shape
    NS = 2 * S5_N
    return pl.pallas_call(
        functools.partial(_s5_kernel, n_batch=n_batch, nlc=seq // S5_CHUNK, ncc=ctx // S5_CHUNK),
        out_shape=jax.ShapeDtypeStruct((GP, NR, W), F32),
        grid=(GP,),
        in_specs=[
            pl.BlockSpec((None, NR, W), lambda g: (g, 0, 0)),
            pl.BlockSpec((None, W, W + 4 * NS), lambda g: (g, 0, 0)),
            pl.BlockSpec((None, 4 * NS, W), lambda g: (g, 0, 0)),
            pl.BlockSpec((None, 4, NS), lambda g: (g, 0, 0)),
        ],
        out_specs=pl.BlockSpec((None, NR, W), lambda g: (g, 0, 0)),
        scratch_shapes=[pltpu.VMEM((NR, 4 * NS), F32), pltpu.VMEM((NR, 4 * NS), F32)],
        compiler_params=_cparams(("arbitrary",)),
        name="mixer_d_scan",
    )(u_pairs, wz, gm, apow)


def _s5_out_kernel(y_ref, u_ref, d_ref, w_ref, b_ref, o_ref):
    y = y_ref[...] + d_ref[...] * u_ref[...]
    c = math.sqrt(2.0 / math.pi)
    y = 0.5 * y * (1.0 + jnp.tanh(c * (y + 0.044715 * (y * y * y))))
    z = _dot(y.astype(BF16), w_ref[...]) + b_ref[...]
    o_ref[...] = (y * _sigmoid(z)).astype(o_ref.dtype)


def _s5_out_call(y_tok, proj, d_skip, glu_w, glu_b, tm):
    R = proj.shape[0]
    vec = pl.BlockSpec((1, GROUP_W), lambda i: (0, 0))
    return pl.pallas_call(
        _s5_out_kernel,
        out_shape=jax.ShapeDtypeStruct((R, GROUP_W), BF16),
        grid=(R // tm,),
        in_specs=[
            pl.BlockSpec((tm, GROUP_W), lambda i: (i, 0)),
            pl.BlockSpec((tm, GROUP_W), lambda i: (i, COL_DU // GROUP_W)),
            vec,
            pl.BlockSpec((GROUP_W, GROUP_W), lambda i: (0, 0)),
            vec,
        ],
        out_specs=pl.BlockSpec((tm, GROUP_W), lambda i: (i, 0)),
        compiler_params=_cparams(("arbitrary",)),
        name="mixer_d_out",
    )(y_tok, proj, d_skip, glu_w, glu_b)


def _s5_to_chunks(u, n_batch, seq, ctx):
    Lc, G, P = S5_CHUNK, S5_GROUPS, S5_P

    def part(t, n):
        t = t.reshape(n_batch, n // Lc, Lc, G // 2, 2, P)
        return jnp.transpose(t, (3, 1, 0, 4, 2, 5)).reshape(G // 2, (n // Lc) * n_batch, 2 * Lc * P)

    nl = n_batch * seq
    return jnp.concatenate([part(u[:nl], seq), part(u[nl:], ctx)], axis=1)


def _s5_from_chunks(y, n_batch, seq, ctx):
    Lc, G, P = S5_CHUNK, S5_GROUPS, S5_P

    def part(t, n):
        t = t.reshape(G // 2, n // Lc, n_batch, 2, Lc, P)
        return jnp.transpose(t, (2, 1, 4, 0, 3, 5)).reshape(n_batch * n, G * P)

    nl = n_batch * (seq // Lc)
    return jnp.concatenate([part(y[:, :nl], seq), part(y[:, nl:], ctx)], axis=0)


def _rope_tables(n_tokens):
    t = jnp.arange(n_tokens)
    nf = HEAD_DIM // 4
    inv = 1.0 / (ROPE_BASE ** (jnp.arange(nf, dtype=F32) / nf))

    def ang(pp):
        a = pp.astype(F32)[:, None] * inv[None, :]
        return jnp.concatenate([a, a], -1)

    a = jnp.concatenate([ang(t // GRID_W), ang(t % GRID_W)], -1)
    cos, sin = jnp.cos(a), jnp.sin(a)
    sign = np.where((np.arange(HEAD_DIM) % 32) < 16, -1.0, 1.0).astype(np.float32)
    cos2 = jnp.concatenate([cos, cos], -1)
    sin2 = jnp.concatenate([sin * sign, sin * sign], -1)
    return cos2, sin2


def _block_ones(n):
    return jnp.asarray(np.kron(np.eye(n // HEAD_DIM), np.ones((HEAD_DIM, HEAD_DIM))), BF16)


def _permute_w_in(w_in):
    cuts = np.cumsum([512, 512, 512, 512, 512, 512, 32, 32, 96, 512, 128, 128, 512])
    seg = lambda i: w_in[..., (0 if i == 0 else cuts[i - 1]):cuts[i]]
    L, D = w_in.shape[:2]
    pad = jnp.zeros((L, D, 96), w_in.dtype)
    parts = [seg(0), seg(1), seg(2), seg(3), seg(4), seg(5), seg(9), seg(12),
             seg(6), seg(7), seg(8), pad, seg(10), seg(11)]
    return jnp.concatenate(parts, axis=-1).astype(BF16)


def _forward(x, c, ctx, c_ctx, w_ada, b_ada, norm_g, ffn1_wg, ffn1_wu, ffn1_wd, ffn2_wg, ffn2_wu, ffn2_wd,
             w_in, w_out, na_q_g, na_k_g, na_rpb, rw_conv, rw_w0, rw_w_up, rw_a0, rw_a_up, rw_g_up,
             rw_k_k, rw_k_a, rw_r_k, rw_gn_w, rw_gn_b, wa_q_g, wa_k_g, wa_sink, s5_a_re, s5_a_im,
             s5_log_dt, s5_b_re, s5_b_im, s5_c_re, s5_c_im, s5_d, s5_glu_w, s5_glu_b, *, tm, tf):
    B, SEQ, D = x.shape
    CTX = ctx.shape[1]
    L = w_ada.shape[0]
    n_lat = B * SEQ
    n_qb = SEQ // QBLK

    bf = lambda t: t.astype(BF16)
    f1g, f1u, f1d, f2g, f2u, f2d = map(bf, (ffn1_wg, ffn1_wu, ffn1_wd, ffn2_wg, ffn2_wu, ffn2_wd))
    w_in_p = _permute_w_in(w_in)
    w_out_b = bf(w_out)
    glu_w_b = bf(s5_glu_w)
    ones128, ones512 = _block_ones(LANE), _block_ones(GROUP_W)
    cos_t, sin_t = _rope_tables(SEQ)
    na_bias = [_na_bias_table(na_rpb[l], n_qb) for l in range(L)]
    zr = lambda r, cdim: jnp.zeros((L, r, cdim), F32)
    wlr = jnp.concatenate([
        jnp.concatenate([rw_w_up[:, 0], rw_w_up[:, 1], zr(32, 3 * GROUP_W)], axis=-1),
        jnp.concatenate([zr(32, 2 * GROUP_W), rw_a_up[:, 0], rw_a_up[:, 1], zr(32, GROUP_W)], axis=-1),
        jnp.concatenate([zr(96, 4 * GROUP_W), rw_g_up], axis=-1),
        zr(96, 5 * GROUP_W)], axis=1).astype(F32)
    s5_tabs = jax.vmap(_s5_tables)(s5_a_re, s5_a_im, s5_log_dt, s5_b_re, s5_b_im, s5_c_re, s5_c_im)
    tile2 = lambda t: jnp.concatenate([t, t], axis=-1)

    cc = jnp.concatenate([c, c_ctx[None], jnp.zeros((8 - B - 1, D), F32)], axis=0)
    mods_all = _ada_call(cc, w_ada, b_ada)[:, :B + 1].reshape(L, B + 1, N_MOD, D)

    xs = jnp.concatenate([x.reshape(n_lat, D), ctx.reshape(B * CTX, D)], axis=0)
    R = xs.shape[0]
    for l in range(L):
        want_ctx = l < L - 1
        mods = mods_all[l]
        ng = norm_g[l]
        xs = _ffn_call(xs, mods, ng[0:1], f1g, f1u, f1d, l, 0, R, SEQ, B, tm, tf)
        proj = _win_call(xs, mods, ng[1:2], w_in_p, l, SEQ, B, tm, D_IN_PAD // 3)
        o_a = _na_call(proj, na_bias[l], tile2(na_q_g[l][None]), tile2(na_k_g[l][None]), ones128,
                       B, SEQ, CTX, want_ctx)
        prep = _rw_prep_call(proj, rw_conv[l], wlr[l], rw_w0[l], rw_a0[l], rw_k_k[l][None], rw_k_a[l][None],
                             ones512, B, SEQ, CTX, min(tm, 256))
        vec = lambda t: t.reshape(1, GROUP_W)
        yf, yb = _rw_scan_call(prep, B, SEQ, CTX)
        o_b = _rw_readout_call(prep, yf, yb, vec(rw_r_k[l]), vec(rw_gn_w[l]), vec(rw_gn_b[l]), ones512, tm)
        o_c = _wa_call(proj, wa_sink[l], cos_t, sin_t, tile2(wa_q_g[l][None]), tile2(wa_k_g[l][None]), ones128,
                       B, SEQ, CTX, want_ctx)
        u_pairs = bf(_s5_to_chunks(proj[:, COL_DU:COL_DU + GROUP_W], B, SEQ, CTX))
        wz, gm, apow = (t[l] for t in s5_tabs)
        y_pairs = _s5_call(u_pairs, bf(wz), bf(gm), apow, B, SEQ, CTX)
        y_tok = _s5_from_chunks(y_pairs, B, SEQ, CTX)
        o_d = _s5_out_call(y_tok, proj, vec(s5_d[l]), glu_w_b[l], vec(s5_glu_b[l]), tm)
        n_rows = R if want_ctx else n_lat
        xs = _wout_call(xs, (o_a, o_b, o_c, o_d), mods, w_out_b, l, n_rows, SEQ, B, tm)
        xs = _ffn_call(xs, mods, ng[2:3], f2g, f2u, f2d, l, 6, n_rows, SEQ, B, tm, tf)
    return xs[:n_lat].reshape(B, SEQ, D)


def kernel(x, c, ctx, c_ctx, w_ada, b_ada, norm_g, ffn1_wg, ffn1_wu, ffn1_wd, ffn2_wg, ffn2_wu, ffn2_wd, w_in, w_out, na_q_g, na_k_g, na_rpb, rw_conv, rw_w0, rw_w_up, rw_a0, rw_a_up, rw_g_up, rw_k_k, rw_k_a, rw_r_k, rw_gn_w, rw_gn_b, wa_q_g, wa_k_g, wa_sink, s5_a_re, s5_a_im, s5_log_dt, s5_b_re, s5_b_im, s5_c_re, s5_c_im, s5_d, s5_glu_w, s5_glu_b):
    return _forward(x, c, ctx, c_ctx, w_ada, b_ada, norm_g, ffn1_wg, ffn1_wu, ffn1_wd, ffn2_wg, ffn2_wu, ffn2_wd,
                    w_in, w_out, na_q_g, na_k_g, na_rpb, rw_conv, rw_w0, rw_w_up, rw_a0, rw_a_up, rw_g_up,
                    rw_k_k, rw_k_a, rw_r_k, rw_gn_w, rw_gn_b, wa_q_g, wa_k_g, wa_sink, s5_a_re, s5_a_im,
                    s5_log_dt, s5_b_re, s5_b_im, s5_c_re, s5_c_im, s5_d, s5_glu_w, s5_glu_b, tm=512, tf=512)
```

```python
import functools
import math

import numpy as np
import jax
import jax.numpy as jnp
from jax import lax
from jax.experimental import pallas as pl
from jax.experimental.pallas import tpu as pltpu

F32 = jnp.float32
BF16 = jnp.bfloat16

D_MODEL = 2048
GRID_W = 64
HEAD_DIM = 64
GROUP_W = D_MODEL // 4
N_MOD = 9
NORM_EPS = 1e-6
ROPE_BASE = 10000.0
NEG_INF = -1e30

NA_HEADS = GROUP_W // HEAD_DIM
NA_WIN_R = 8
NA_WIN_C = 16
RW_HEADS = GROUP_W // HEAD_DIM
RW_DECAY_RANK = 32
RW_ICLR_RANK = 32
RW_GATE_RANK = 96
RW_GN_EPS = 64e-5
WA_HEADS = GROUP_W // HEAD_DIM
WA_KV_HEADS = 2
WA_GROUP = WA_HEADS // WA_KV_HEADS
WA_WINDOW = 128
S5_P = 16
S5_GROUPS = GROUP_W // S5_P
S5_N = 64

LANE = 128
QBLK = 2 * GRID_W
NA_KBLKS = 5
RW_CHUNK = 64
S5_CHUNK = 8
S5_OCTETS = 4
VMEM_LIMIT = 56 * 1024 * 1024

COL_AQ, COL_AK, COL_AV = 0, 512, 1024
COL_BR = 1536
COL_CQ = 3072
COL_DU = 3584
COL_BLR = 4096
COL_CK, COL_CV = 4352, 4480
D_IN_PAD = 4608

PB_R, PB_V, PB_KK, PB_G = 0, 1, 2, 9
PB_N = 10


def _cparams(sem):
    return pltpu.CompilerParams(dimension_semantics=sem, vmem_limit_bytes=VMEM_LIMIT)


def _dot(a, b):
    return jnp.dot(a, b, preferred_element_type=F32)


def _dot_nt(a, b):
    return lax.dot_general(a, b, (((1,), (1,)), ((), ())), preferred_element_type=F32)


def _dot_tn(a, b):
    return lax.dot_general(a, b, (((0,), (0,)), ((), ())), preferred_element_type=F32)


def _sigmoid(x):
    return 1.0 / (1.0 + jnp.exp(-x))


def _modulate(x, g, shift, scale):
    ms = jnp.mean(x * x, axis=-1, keepdims=True)
    return (x * lax.rsqrt(ms + NORM_EPS) * g) * (1.0 + scale) + shift


def _seg_sum(x, ones_bd):
    hi = x.astype(BF16)
    r1 = x - hi.astype(F32)
    mid = r1.astype(BF16)
    lo = (r1 - mid.astype(F32)).astype(BF16)
    return _dot(hi, ones_bd) + _dot(mid, ones_bd) + _dot(lo, ones_bd)


def _head_rmsnorm(x, ones_bd):
    ms = _seg_sum(x * x, ones_bd) * (1.0 / HEAD_DIM)
    return x * lax.rsqrt(ms + NORM_EPS)


def _ada_kernel(c_ref, w_ref, b_ref, o_ref):
    c = c_ref[...]
    s = (c * _sigmoid(c)).astype(BF16)
    o_ref[...] = _dot(s, w_ref[...].astype(BF16)) + b_ref[...]


def _ada_call(cc, w_ada, b_ada):
    L, D, N = w_ada.shape
    tn = 1024
    return pl.pallas_call(
        _ada_kernel,
        out_shape=jax.ShapeDtypeStruct((L, 8, N), F32),
        grid=(L, N // tn),
        in_specs=[
            pl.BlockSpec((8, D), lambda l, j: (0, 0)),
            pl.BlockSpec((None, D, tn), lambda l, j: (l, 0, j)),
            pl.BlockSpec((None, 1, tn), lambda l, j: (l, 0, j)),
        ],
        out_specs=pl.BlockSpec((None, 8, tn), lambda l, j: (l, 0, j)),
        compiler_params=_cparams(("arbitrary", "arbitrary")),
        name="adaln",
    )(cc, w_ada, b_ada.reshape(L, 1, N))


def _ffn_kernel(x_ref, mod_ref, g_ref, wg_ref, wu_ref, wd_ref, o_ref, h_sc, acc_sc, *, mi):
    j = pl.program_id(1)

    @pl.when(j == 0)
    def _():
        h = _modulate(x_ref[...], g_ref[...], mod_ref[mi:mi + 1, :], mod_ref[mi + 1:mi + 2, :])
        h_sc[...] = h.astype(BF16)
        acc_sc[...] = jnp.zeros_like(acc_sc)

    h = h_sc[...]
    gate = _dot(h, wg_ref[...])
    up = _dot(h, wu_ref[...])
    a = (gate * _sigmoid(gate) * up).astype(BF16)
    acc_sc[...] += _dot(a, wd_ref[...])

    @pl.when(j == pl.num_programs(1) - 1)
    def _():
        o_ref[...] = x_ref[...] + 0.5 * mod_ref[mi + 2:mi + 3, :] * acc_sc[...]


def _ffn_call(xs, mods, g, wg, wu, wd, l, mi, n_rows, rows_per_seq, n_batch, tm, tf):
    D = xs.shape[1]
    F = wg.shape[2]
    tps = rows_per_seq // tm
    return pl.pallas_call(
        functools.partial(_ffn_kernel, mi=mi),
        out_shape=jax.ShapeDtypeStruct((n_rows, D), F32),
        grid=(n_rows // tm, F // tf),
        in_specs=[
            pl.BlockSpec((tm, D), lambda i, j: (i, 0)),
            pl.BlockSpec((None, N_MOD, D), lambda i, j: (jnp.minimum(i // tps, n_batch), 0, 0)),
            pl.BlockSpec((1, D), lambda i, j: (0, 0)),
            pl.BlockSpec((None, D, tf), lambda i, j: (l, 0, j)),
            pl.BlockSpec((None, D, tf), lambda i, j: (l, 0, j)),
            pl.BlockSpec((None, tf, D), lambda i, j: (l, j, 0)),
        ],
        out_specs=pl.BlockSpec((tm, D), lambda i, j: (i, 0)),
        scratch_shapes=[pltpu.VMEM((tm, D), BF16), pltpu.VMEM((tm, D), F32)],
        compiler_params=_cparams(("arbitrary", "arbitrary")),
        name="ffn",
    )(xs, mods, g, wg, wu, wd)


def _win_kernel(x_ref, mod_ref, g_ref, w_ref, o_ref, h_sc):
    @pl.when(pl.program_id(1) == 0)
    def _():
        h = _modulate(x_ref[...], g_ref[...], mod_ref[3:4, :], mod_ref[4:5, :])
        h_sc[...] = h.astype(BF16)

    o_ref[...] = _dot(h_sc[...], w_ref[...])


def _win_call(xs, mods, g, w_in, l, rows_per_seq, n_batch, tm, tn):
    R, D = xs.shape
    N = w_in.shape[2]
    tps = rows_per_seq // tm
    return pl.pallas_call(
        _win_kernel,
        out_shape=jax.ShapeDtypeStruct((R, N), F32),
        grid=(R // tm, N // tn),
        in_specs=[
            pl.BlockSpec((tm, D), lambda i, j: (i, 0)),
            pl.BlockSpec((None, N_MOD, D), lambda i, j: (jnp.minimum(i // tps, n_batch), 0, 0)),
            pl.BlockSpec((1, D), lambda i, j: (0, 0)),
            pl.BlockSpec((None, D, tn), lambda i, j: (l, 0, j)),
        ],
        out_specs=pl.BlockSpec((tm, tn), lambda i, j: (i, j)),
        scratch_shapes=[pltpu.VMEM((tm, D), BF16)],
        compiler_params=_cparams(("arbitrary", "arbitrary")),
        name="in_proj",
    )(xs, mods, g, w_in)


def _wout_kernel(x_ref, oa_ref, ob_ref, oc_ref, od_ref, mod_ref, w_ref, o_ref):
    acc = _dot(oa_ref[...], w_ref[0 * GROUP_W:1 * GROUP_W, :])
    acc += _dot(ob_ref[...], w_ref[1 * GROUP_W:2 * GROUP_W, :])
    acc += _dot(oc_ref[...], w_ref[2 * GROUP_W:3 * GROUP_W, :])
    acc += _dot(od_ref[...], w_ref[3 * GROUP_W:4 * GROUP_W, :])
    o_ref[...] = x_ref[...] + mod_ref[5:6, :] * acc


def _wout_call(xs, outs, mods, w_out, l, n_rows, rows_per_seq, n_batch, tm):
    D = xs.shape[1]
    tps = rows_per_seq // tm
    ospec = pl.BlockSpec((tm, GROUP_W), lambda i: (i, 0))
    return pl.pallas_call(
        _wout_kernel,
        out_shape=jax.ShapeDtypeStruct((n_rows, D), F32),
        grid=(n_rows // tm,),
        in_specs=[
            pl.BlockSpec((tm, D), lambda i: (i, 0)),
            ospec, ospec, ospec, ospec,
            pl.BlockSpec((None, N_MOD, D), lambda i: (jnp.minimum(i // tps, n_batch), 0, 0)),
            pl.BlockSpec((None, D, D), lambda i: (l, 0, 0)),
        ],
        out_specs=pl.BlockSpec((tm, D), lambda i: (i, 0)),
        compiler_params=_cparams(("arbitrary",)),
        name="out_proj",
    )(xs, *outs, mods, w_out)


def _na_case_reps(n_qb):
    return (0, 1, 2, n_qb - 2, n_qb - 1)


def _na_start(p, n_qb):
    return jnp.clip(p - 2, 0, n_qb - NA_KBLKS)


def _na_bias_table(rpb, n_qb):
    H = rpb.shape[0]
    rows = 2 * n_qb
    qc = np.arange(GRID_W)[:, None]
    kc = np.arange(GRID_W)[None, :]
    cs = np.clip(qc - NA_WIN_C // 2, 0, GRID_W - NA_WIN_C)
    okc = (kc >= cs) & (kc < cs + NA_WIN_C)
    dc = np.where(okc, kc - qc + NA_WIN_C - 1, 0)
    rp = rpb.astype(F32).reshape(H, 2 * NA_WIN_R - 1, 2 * NA_WIN_C - 1)
    blocks = jnp.where(jnp.asarray(okc)[None, None], jnp.take(rp, jnp.asarray(dc), axis=2), NEG_INF)
    neg = jnp.full((H, GRID_W, GRID_W), NEG_INF, F32)
    cases = []
    for p in _na_case_reps(n_qb):
        start = min(max(p - 2, 0), n_qb - NA_KBLKS)
        qrows = []
        for qr in range(2):
            qa = 2 * p + qr
            rs = min(max(qa - NA_WIN_R // 2, 0), rows - NA_WIN_R)
            krow = []
            for kr in range(2 * NA_KBLKS):
                ka = 2 * start + kr
                krow.append(blocks[:, ka - qa + NA_WIN_R - 1] if rs <= ka < rs + NA_WIN_R else neg)
            qrows.append(jnp.concatenate(krow, axis=-1))
        cases.append(jnp.concatenate(qrows, axis=-2))
    return jnp.stack(cases, axis=0)


def _na_kernel(q_ref, k_ref, v_ref, kc_ref, vc_ref, bias_ref, qg_ref, kg_ref, ones_ref, o_ref,
               kn_sc, vn_sc, kcn_sc, vcn_sc, *, n_qb, seq, ctx):
    p = pl.program_id(2)
    ones_bd = ones_ref[...]
    scale = HEAD_DIM ** -0.5

    @pl.when(p == 0)
    def _():
        kg = kg_ref[...]
        rows = 512

        def body(i, carry):
            sl = pl.ds(pl.multiple_of(i * rows, rows), rows)
            kn_sc[sl, :] = (_head_rmsnorm(k_ref[sl, :], ones_bd) * kg).astype(BF16)
            vn_sc[sl, :] = v_ref[sl, :].astype(BF16)
            return carry

        lax.fori_loop(0, seq // rows, body, 0)
        kcn_sc[...] = (_head_rmsnorm(kc_ref[...], ones_bd) * kg).astype(BF16)
        vcn_sc[...] = vc_ref[...].astype(BF16)

    q = (_head_rmsnorm(q_ref[...], ones_bd) * (qg_ref[...] * scale)).astype(BF16)
    kcn = kcn_sc[...]
    vcn = vcn_sc[...]

    def finish(outs):
        o_ref[...] = jnp.concatenate(outs, axis=-1).astype(o_ref.dtype)

    @pl.when(p < n_qb)
    def _():
        start = _na_start(p, n_qb)
        case = p - start
        sl = pl.ds(pl.multiple_of(start * QBLK, QBLK), NA_KBLKS * QBLK)
        kw = kn_sc[sl, :]
        vw = vn_sc[sl, :]
        outs = []
        for h in range(2):
            hs = slice(h * HEAD_DIM, (h + 1) * HEAD_DIM)
            qh = q[:, hs]
            s_n = _dot_nt(qh, kw[:, hs]) + bias_ref[case, h]
            s_c = _dot_nt(qh, kcn[:, hs])
            m = jnp.maximum(jnp.max(s_n, axis=-1, keepdims=True), jnp.max(s_c, axis=-1, keepdims=True))
            p_n = jnp.exp(s_n - m)
            p_c = jnp.exp(s_c - m)
            den = jnp.sum(p_n, axis=-1, keepdims=True) + jnp.sum(p_c, axis=-1, keepdims=True)
            o = _dot(p_n.astype(BF16), vw[:, hs]) + _dot(p_c.astype(BF16), vcn[:, hs])
            outs.append(o / den)
        finish(outs)

    @pl.when(p >= n_qb)
    def _():
        outs = []
        for h in range(2):
            hs = slice(h * HEAD_DIM, (h + 1) * HEAD_DIM)
            s_c = _dot_nt(q[:, hs], kcn[:, hs])
            m = jnp.max(s_c, axis=-1, keepdims=True)
            p_c = jnp.exp(s_c - m)
            den = jnp.sum(p_c, axis=-1, keepdims=True)
            outs.append(_dot(p_c.astype(BF16), vcn[:, hs]) / den)
        finish(outs)


def _na_call(proj, bias, qg, kg, ones_bd, n_batch, seq, ctx, want_ctx):
    R = proj.shape[0]
    n_qb = seq // QBLK
    n_cb = ctx // QBLK
    steps = n_qb + (n_cb if want_ctx else 0)
    lat_qb = n_batch * n_qb
    ctx_kb = n_batch * seq // ctx

    def qrow(b, p):
        return jnp.where(p < n_qb, b * n_qb + p, lat_qb + b * n_cb + (p - n_qb))

    cq, ck, cv = COL_AQ // LANE, COL_AK // LANE, COL_AV // LANE
    return pl.pallas_call(
        functools.partial(_na_kernel, n_qb=n_qb, seq=seq, ctx=ctx),
        out_shape=jax.ShapeDtypeStruct((R, GROUP_W), BF16),
        grid=(n_batch, NA_HEADS // 2, steps),
        in_specs=[
            pl.BlockSpec((QBLK, LANE), lambda b, hp, p: (qrow(b, p), cq + hp)),
            pl.BlockSpec((seq, LANE), lambda b, hp, p: (b, ck + hp)),
            pl.BlockSpec((seq, LANE), lambda b, hp, p: (b, cv + hp)),
            pl.BlockSpec((ctx, LANE), lambda b, hp, p: (ctx_kb + b, ck + hp)),
            pl.BlockSpec((ctx, LANE), lambda b, hp, p: (ctx_kb + b, cv + hp)),
            pl.BlockSpec((5, 2, QBLK, NA_KBLKS * QBLK), lambda b, hp, p: (0, hp, 0, 0)),
            pl.BlockSpec((1, LANE), lambda b, hp, p: (0, 0)),
            pl.BlockSpec((1, LANE), lambda b, hp, p: (0, 0)),
            pl.BlockSpec((LANE, LANE), lambda b, hp, p: (0, 0)),
        ],
        out_specs=pl.BlockSpec((QBLK, LANE), lambda b, hp, p: (qrow(b, p), hp)),
        scratch_shapes=[pltpu.VMEM((seq, LANE), BF16), pltpu.VMEM((seq, LANE), BF16),
                        pltpu.VMEM((ctx, LANE), BF16), pltpu.VMEM((ctx, LANE), BF16)],
        compiler_params=_cparams(("arbitrary", "arbitrary", "arbitrary")),
        name="mixer_a",
    )(proj, proj, proj, proj, proj, bias, qg, kg, ones_bd)


def _rope(x, cos, sin_signed, first_half):
    rot = jnp.where(first_half, pltpu.roll(x, LANE - 16, 1), pltpu.roll(x, 16, 1))
    return x * cos + rot * sin_signed


def _wa_kernel(sink_ref, q_ref, k_ref, v_ref, kc_ref, vc_ref, cosk_ref, sin_k_ref, cosq_ref, sinq_ref,
               qg_ref, kg_ref, ones_ref, o_ref, kn_sc, vn_sc, kcn_sc, vcn_sc, *, n_qb, seq, ctx):
    p = pl.program_id(1)
    ones_bd = ones_ref[...]
    scale = HEAD_DIM ** -0.5
    lane = lax.broadcasted_iota(jnp.int32, (1, LANE), 1)
    first_half = (lane % 32) < 16

    @pl.when(p == 0)
    def _():
        kg = kg_ref[...]
        rows = 512

        def body(i, carry):
            sl = pl.ds(pl.multiple_of(i * rows, rows), rows)
            kn = _head_rmsnorm(k_ref[sl, :], ones_bd) * kg
            kn_sc[sl, :] = _rope(kn, cosk_ref[sl, :], sin_k_ref[sl, :], first_half).astype(BF16)
            vn_sc[sl, :] = v_ref[sl, :].astype(BF16)
            return carry

        lax.fori_loop(0, seq // rows, body, 0)
        kcn_sc[...] = (_head_rmsnorm(kc_ref[...], ones_bd) * kg).astype(BF16)
        vcn_sc[...] = vc_ref[...].astype(BF16)

    qg = qg_ref[...] * scale
    is_lat = p < n_qb
    cosq = jnp.where(is_lat, cosq_ref[...], 1.0)
    sinq = jnp.where(is_lat, sinq_ref[...], 0.0)
    qh = []
    for c in range(4):
        qn = _head_rmsnorm(q_ref[:, c * LANE:(c + 1) * LANE], ones_bd) * qg
        qn = _rope(qn, cosq, sinq, first_half).astype(BF16)
        qh += [qn[:, :HEAD_DIM], qn[:, HEAD_DIM:]]
    kcn = kcn_sc[...]
    vcn = vcn_sc[...]

    def sink_col(kh):
        return jnp.concatenate(
            [jnp.full((QBLK, 1), sink_ref[kh * WA_GROUP + g], F32) for g in range(WA_GROUP)], axis=0)

    def finish(o_heads):
        o_ref[...] = jnp.concatenate(o_heads, axis=-1).astype(o_ref.dtype)

    @pl.when(is_lat)
    def _():
        ws = jnp.clip((p - 1) * QBLK, 0, seq - 3 * QBLK)
        sl = pl.ds(pl.multiple_of(ws, QBLK), 3 * QBLK)
        kw = kn_sc[sl, :]
        vw = vn_sc[sl, :]
        shp = (WA_GROUP * QBLK, 3 * QBLK)
        qpos = p * QBLK + (lax.broadcasted_iota(jnp.int32, shp, 0) & (QBLK - 1))
        kpos = ws + lax.broadcasted_iota(jnp.int32, shp, 1)
        ok = jnp.abs(kpos - qpos) <= WA_WINDOW
        o_heads = []
        for kh in range(WA_KV_HEADS):
            hs = slice(kh * HEAD_DIM, (kh + 1) * HEAD_DIM)
            qs = jnp.concatenate(qh[kh * WA_GROUP:(kh + 1) * WA_GROUP], axis=0)
            s_w = jnp.where(ok, _dot_nt(qs, kw[:, hs]), NEG_INF)
            s_c = _dot_nt(qs, kcn[:, hs])
            sk = sink_col(kh)
            m = jnp.maximum(jnp.maximum(jnp.max(s_w, axis=-1, keepdims=True),
                                        jnp.max(s_c, axis=-1, keepdims=True)), sk)
            p_w = jnp.exp(s_w - m)
            p_c = jnp.exp(s_c - m)
            den = (jnp.sum(p_w, axis=-1, keepdims=True) + jnp.sum(p_c, axis=-1, keepdims=True)
                   + jnp.exp(sk - m))
            o = (_dot(p_w.astype(BF16), vw[:, hs]) + _dot(p_c.astype(BF16), vcn[:, hs])) / den
            o_heads += [o[g * QBLK:(g + 1) * QBLK, :] for g in range(WA_GROUP)]
        finish(o_heads)

    @pl.when(jnp.logical_not(is_lat))
    def _():
        o_heads = []
        for kh in range(WA_KV_HEADS):
            hs = slice(kh * HEAD_DIM, (kh + 1) * HEAD_DIM)
            qs = jnp.concatenate(qh[kh * WA_GROUP:(kh + 1) * WA_GROUP], axis=0)
            s_c = _dot_nt(qs, kcn[:, hs])
            sk = sink_col(kh)
            m = jnp.maximum(jnp.max(s_c, axis=-1, keepdims=True), sk)
            p_c = jnp.exp(s_c - m)
            den = jnp.sum(p_c, axis=-1, keepdims=True) + jnp.exp(sk - m)
            o = _dot(p_c.astype(BF16), vcn[:, hs]) / den
            o_heads += [o[g * QBLK:(g + 1) * QBLK, :] for g in range(WA_GROUP)]
        finish(o_heads)


def _wa_call(proj, sink, cos_t, sin_t, qg, kg, ones_bd, n_batch, seq, ctx, want_ctx):
    R = proj.shape[0]
    n_qb = seq // QBLK
    n_cb = ctx // QBLK
    steps = n_qb + (n_cb if want_ctx else 0)
    lat_qb = n_batch * n_qb
    ctx_kb = n_batch * seq // ctx

    def qrow(b, p):
        return jnp.where(p < n_qb, b * n_qb + p, lat_qb + b * n_cb + (p - n_qb))

    ck, cv = COL_CK // LANE, COL_CV // LANE
    return pl.pallas_call(
        functools.partial(_wa_kernel, n_qb=n_qb, seq=seq, ctx=ctx),
        out_shape=jax.ShapeDtypeStruct((R, GROUP_W), BF16),
        grid=(n_batch, steps),
        in_specs=[
            pl.BlockSpec(memory_space=pltpu.SMEM),
            pl.BlockSpec((QBLK, GROUP_W), lambda b, p: (qrow(b, p), COL_CQ // GROUP_W)),
            pl.BlockSpec((seq, LANE), lambda b, p: (b, ck)),
            pl.BlockSpec((seq, LANE), lambda b, p: (b, cv)),
            pl.BlockSpec((ctx, LANE), lambda b, p: (ctx_kb + b, ck)),
            pl.BlockSpec((ctx, LANE), lambda b, p: (ctx_kb + b, cv)),
            pl.BlockSpec((seq, LANE), lambda b, p: (0, 0)),
            pl.BlockSpec((seq, LANE), lambda b, p: (0, 0)),
            pl.BlockSpec((QBLK, LANE), lambda b, p: (jnp.minimum(p, n_qb - 1), 0)),
            pl.BlockSpec((QBLK, LANE), lambda b, p: (jnp.minimum(p, n_qb - 1), 0)),
            pl.BlockSpec((1, LANE), lambda b, p: (0, 0)),
            pl.BlockSpec((1, LANE), lambda b, p: (0, 0)),
            pl.BlockSpec((LANE, LANE), lambda b, p: (0, 0)),
        ],
        out_specs=pl.BlockSpec((QBLK, GROUP_W), lambda b, p: (qrow(b, p), 0)),
        scratch_shapes=[pltpu.VMEM((seq, LANE), BF16), pltpu.VMEM((seq, LANE), BF16),
                        pltpu.VMEM((ctx, LANE), BF16), pltpu.VMEM((ctx, LANE), BF16)],
        compiler_params=_cparams(("arbitrary", "arbitrary")),
        name="mixer_c",
    )(sink, proj, proj, proj, proj, proj, cos_t, sin_t, cos_t, sin_t, qg, kg, ones_bd)


def _rw_prep_kernel(x_ref, prev_ref, next_ref, lr_ref, conv_ref, wlr_ref, w0_ref, a0_ref, kk_ref, ka_ref,
                    ones_ref, o_ref, *, tiles_lat, tiles_ctx, n_lat_tiles):
    i = pl.program_id(0)
    tm = x_ref.shape[0]
    in_lat = i < n_lat_tiles
    first = jnp.where(in_lat, i % tiles_lat == 0, (i - n_lat_tiles) % tiles_ctx == 0)
    last = jnp.where(in_lat, i % tiles_lat == tiles_lat - 1, (i - n_lat_tiles) % tiles_ctx == tiles_ctx - 1)
    x = x_ref[...]
    prev_row = jnp.where(first, 0.0, prev_ref[7:8, :])
    next_row = jnp.where(last, 0.0, next_ref[0:1, :])
    row = lax.broadcasted_iota(jnp.int32, (tm, 1), 0)
    x_prev = jnp.where(row == 0, prev_row, pltpu.roll(x, 1, 0))
    x_next = jnp.where(row == tm - 1, next_row, pltpu.roll(x, tm - 1, 0))
    cw = conv_ref[...]
    y = x_prev * cw[0:1, :] + x * cw[1:2, :] + x_next * cw[2:3, :]
    r = y[:, 0:GROUP_W]
    k = y[:, GROUP_W:2 * GROUP_W]
    v = y[:, 2 * GROUP_W:3 * GROUP_W]

    lr = lr_ref[...]
    lane = lax.broadcasted_iota(jnp.int32, (1, lr.shape[1]), 1)
    c1 = RW_DECAY_RANK
    c2 = c1 + RW_ICLR_RANK
    c3 = c2 + RW_GATE_RANK
    act = jnp.where(lane < c1, jnp.tanh(lr),
                    jnp.where(lane < c2, lr, jnp.where(lane < c3, _sigmoid(lr), 0.0)))
    up = jnp.dot(act, wlr_ref[...], preferred_element_type=F32, precision=lax.Precision.HIGHEST)

    ones_bd = ones_ref[...]
    kk = k * kk_ref[...]
    nrm = jnp.sqrt(_seg_sum(kk * kk, ones_bd))
    kk = kk / jnp.maximum(nrm, 1e-12)

    o_ref[:, PB_R * GROUP_W:(PB_R + 1) * GROUP_W] = r
    o_ref[:, PB_V * GROUP_W:(PB_V + 1) * GROUP_W] = v
    o_ref[:, PB_KK * GROUP_W:(PB_KK + 1) * GROUP_W] = kk
    o_ref[:, PB_G * GROUP_W:(PB_G + 1) * GROUP_W] = up[:, 4 * GROUP_W:5 * GROUP_W]
    ka = ka_ref[...]
    for d in range(2):
        z = w0_ref[d:d + 1, :] + up[:, d * GROUP_W:(d + 1) * GROUP_W]
        sp = jnp.maximum(-z, 0.0) + jnp.log(1.0 + jnp.exp(-jnp.abs(z)))
        lw = -jnp.exp(-sp - 0.5)
        a = _sigmoid(a0_ref[d:d + 1, :] + up[:, (2 + d) * GROUP_W:(3 + d) * GROUP_W])
        kd = k * (1.0 + (a - 1.0) * ka)
        base = 3 + 3 * d
        o_ref[:, base * GROUP_W:(base + 1) * GROUP_W] = lw
        o_ref[:, (base + 1) * GROUP_W:(base + 2) * GROUP_W] = a
        o_ref[:, (base + 2) * GROUP_W:(base + 3) * GROUP_W] = kd


def _rw_prep_call(proj, conv_w, wlr, w0, a0, k_k, k_a, ones512, n_batch, seq, ctx, tm):
    R = proj.shape[0]
    n_lat_tiles = n_batch * seq // tm
    nb8 = R // 8
    t8 = tm // 8
    wide = 3 * GROUP_W
    cb = COL_BR // wide
    return pl.pallas_call(
        functools.partial(_rw_prep_kernel, tiles_lat=seq // tm, tiles_ctx=ctx // tm, n_lat_tiles=n_lat_tiles),
        out_shape=jax.ShapeDtypeStruct((R, PB_N * GROUP_W), F32),
        grid=(R // tm,),
        in_specs=[
            pl.BlockSpec((tm, wide), lambda i: (i, cb)),
            pl.BlockSpec((8, wide), lambda i: (jnp.maximum(i * t8 - 1, 0), cb)),
            pl.BlockSpec((8, wide), lambda i: (jnp.minimum((i + 1) * t8, nb8 - 1), cb)),
            pl.BlockSpec((tm, 256), lambda i: (i, COL_BLR // 256)),
            pl.BlockSpec((3, wide), lambda i: (0, 0)),
            pl.BlockSpec((256, 5 * GROUP_W), lambda i: (0, 0)),
            pl.BlockSpec((2, GROUP_W), lambda i: (0, 0)),
            pl.BlockSpec((2, GROUP_W), lambda i: (0, 0)),
            pl.BlockSpec((1, GROUP_W), lambda i: (0, 0)),
            pl.BlockSpec((1, GROUP_W), lambda i: (0, 0)),
            pl.BlockSpec((GROUP_W, GROUP_W), lambda i: (0, 0)),
        ],
        out_specs=pl.BlockSpec((tm, PB_N * GROUP_W), lambda i: (i, 0)),
        compiler_params=_cparams(("arbitrary",)),
        name="mixer_b_prep",
    )(proj, proj, proj, proj, conv_w, wlr, w0, a0, k_k, k_a, ones512)


RW_SPLIT = 2
_NN = ((1,), (0,))
_NT = ((1,), (1,))
_TN = ((0,), (0,))


def _split(x, n):
    parts = []
    for _ in range(n):
        p = x.astype(BF16)
        parts.append(p)
        x = x - p.astype(F32)
    return parts


def _sdot(a, b, dims):
    n = max(len(a), len(b))
    acc = None
    for i, ai in enumerate(a):
        for j, bj in enumerate(b):
            if i + j < n:
                t = lax.dot_general(ai, bj, (dims, ((), ())), preferred_element_type=F32)
                acc = t if acc is None else acc + t
    return acc


def _rw_chunks(at, bt, kt, rt, v, gam, s0, strict, incl):
    n = len(at)
    ids = range(n)
    C = at[0].shape[0]
    sp = lambda t: _split(t, RW_SPLIT)
    ar = [sp(jnp.concatenate([at[i], rt[i]], axis=0)) for i in ids]
    bk = [sp(jnp.concatenate([bt[i], kt[i]], axis=0)) for i in ids]
    g4 = [_sdot(ar[i], bk[i], _NT) for i in ids]
    a_ab = [jnp.where(strict[i], g4[i][:C, :C], 0.0) for i in ids]
    a_ak = [jnp.where(strict[i], g4[i][:C, C:], 0.0) for i in ids]
    a_rb = [jnp.where(incl[i], g4[i][C:, :C], 0.0) for i in ids]
    a_rk = [jnp.where(incl[i], g4[i][C:, C:], 0.0) for i in ids]
    vs = [sp(v[i]) for i in ids]
    wv = [_sdot(sp(a_ak[i]), vs[i], _NN) for i in ids]
    x = [jnp.concatenate([at[i], wv[i]], axis=1) for i in ids]
    nm = a_ab
    steps = int(math.log2(C))
    for st in range(steps):
        nb = [sp(nm[i]) for i in ids]
        x = [x[i] + _sdot(nb[i], sp(x[i]), _NN) for i in ids]
        if st < steps - 1:
            nm = [_sdot(nb[i], nb[i], _NN) for i in ids]
    qv = [sp(jnp.concatenate([x[i][:, HEAD_DIM:], v[i]], axis=0)) for i in ids]
    pb = [sp(x[i][:, :HEAD_DIM]) for i in ids]
    s0b = [sp(s0[i]) for i in ids]
    y_loc = [_sdot(sp(jnp.concatenate([a_rb[i], a_rk[i]], axis=1)), qv[i], _NN) for i in ids]
    rp = [rt[i] + _sdot(sp(a_rb[i]), pb[i], _NN) for i in ids]
    w = [_sdot(s0b[i], pb[i], _NT) for i in ids]
    y = [y_loc[i] + _sdot(sp(rp[i]), s0b[i], _NT) for i in ids]
    s1 = [(s0[i] + _sdot(sp(w[i]), sp(bt[i]), _NN) + _sdot(qv[i], bk[i], _TN)) * gam[i] for i in ids]
    return y, s1


def _rw_scan_kernel(rf_ref, vf_ref, kkf_ref, lwf_ref, af_ref, kf_ref,
                    rb_ref, vb_ref, kkb_ref, lwb_ref, ab_ref, kb_ref, yf_ref, yb_ref, s_sc):
    s = pl.program_id(1)

    @pl.when(s == 0)
    def _():
        s_sc[...] = jnp.zeros_like(s_sc)

    C = rf_ref.shape[0]
    row = lax.broadcasted_iota(jnp.int32, (C, C), 0)
    col = lax.broadcasted_iota(jnp.int32, (C, C), 1)
    chains = dict(at=[], bt=[], kt=[], rt=[], v=[], gam=[], s0=[], strict=[], incl=[])
    dirs = ((False, rf_ref, vf_ref, kkf_ref, lwf_ref, af_ref, kf_ref),
            (True, rb_ref, vb_ref, kkb_ref, lwb_ref, ab_ref, kb_ref))
    for d, (reverse, r_ref, v_ref, kk_ref, lw_ref, a_ref, k_ref) in enumerate(dirs):
        incl, strict = (row <= col, row < col) if reverse else (row >= col, row > col)
        lw = lw_ref[...]
        kk = kk_ref[...]
        v = v_ref[...]
        cum = jnp.dot(incl.astype(F32), lw, preferred_element_type=F32, precision=lax.Precision.HIGHEST)
        e_l = jnp.exp(cum)
        e_n = jnp.exp(-cum)
        at = -(kk * jnp.exp(cum - lw))
        bt = kk * a_ref[...] * e_n
        kt = k_ref[...] * e_n
        rt = r_ref[...] * e_l
        gam = e_l[0:1, :] if reverse else e_l[C - 1:C, :]
        for h in range(RW_HEADS):
            hs = slice(h * HEAD_DIM, (h + 1) * HEAD_DIM)
            for name, val in (("at", at), ("bt", bt), ("kt", kt), ("rt", rt), ("v", v), ("gam", gam)):
                chains[name].append(val[:, hs])
            chains["s0"].append(s_sc[d, h])
            chains["strict"].append(strict)
            chains["incl"].append(incl)
    y, s1 = _rw_chunks(**chains)
    for d in range(2):
        for h in range(RW_HEADS):
            s_sc[d, h] = s1[d * RW_HEADS + h]
    yf_ref[...] = jnp.concatenate(y[:RW_HEADS], axis=-1)
    yb_ref[...] = jnp.concatenate(y[RW_HEADS:], axis=-1)


def _rw_scan_call(prep, n_batch, seq, ctx):
    R = prep.shape[0]
    C = RW_CHUNK
    nc, nl = ctx // C, seq // C
    ctx_base = n_batch * nl

    def blk_f(b, s):
        return jnp.where(s < nc, ctx_base + b * nc + s, b * nl + (s - nc))

    def blk_b(b, s):
        return jnp.where(s < nc, ctx_base + b * nc + (nc - 1 - s), b * nl + (nl - 1 - (s - nc)))

    def col(blk, cb):
        return pl.BlockSpec((C, GROUP_W), lambda b, s: (blk(b, s), cb))

    in_specs = ([col(blk_f, c) for c in (PB_R, PB_V, PB_KK, 3, 4, 5)]
                + [col(blk_b, c) for c in (PB_R, PB_V, PB_KK, 6, 7, 8)])
    out = jax.ShapeDtypeStruct((R, GROUP_W), F32)
    return pl.pallas_call(
        _rw_scan_kernel,
        out_shape=(out, out),
        grid=(n_batch, nc + nl),
        in_specs=in_specs,
        out_specs=(col(blk_f, 0), col(blk_b, 0)),
        scratch_shapes=[pltpu.VMEM((2, RW_HEADS, HEAD_DIM, HEAD_DIM), F32)],
        compiler_params=_cparams(("arbitrary", "arbitrary")),
        name="mixer_b_scan",
    )(*([prep] * 12))


def _rw_readout_kernel(r_ref, v_ref, k0_ref, k1_ref, g_ref, yf_ref, yb_ref, rk_ref, gw_ref, gb_ref, ones_ref,
                       o_ref):
    ones_bd = ones_ref[...]
    v = v_ref[...]
    bonus = _seg_sum(r_ref[...] * (k0_ref[...] + k1_ref[...]) * rk_ref[...], ones_bd)
    y = yf_ref[...] + yb_ref[...] + bonus * v
    mu = _seg_sum(y, ones_bd) * (1.0 / HEAD_DIM)
    yc = y - mu
    var = _seg_sum(yc * yc, ones_bd) * (1.0 / HEAD_DIM)
    yn = yc * lax.rsqrt(var + RW_GN_EPS)
    o_ref[...] = ((yn * gw_ref[...] + gb_ref[...]) * g_ref[...]).astype(o_ref.dtype)


def _rw_readout_call(prep, yf, yb, r_k, gn_w, gn_b, ones512, tm):
    R = prep.shape[0]
    col = lambda cb: pl.BlockSpec((tm, GROUP_W), lambda i: (i, cb))
    vec = pl.BlockSpec((1, GROUP_W), lambda i: (0, 0))
    return pl.pallas_call(
        _rw_readout_kernel,
        out_shape=jax.ShapeDtypeStruct((R, GROUP_W), BF16),
        grid=(R // tm,),
        in_specs=[col(PB_R), col(PB_V), col(5), col(8), col(PB_G), col(0), col(0), vec, vec, vec,
                  pl.BlockSpec((GROUP_W, GROUP_W), lambda i: (0, 0))],
        out_specs=col(0),
        compiler_params=_cparams(("arbitrary",)),
        name="mixer_b_readout",
    )(prep, prep, prep, prep, prep, yf, yb, r_k, gn_w, gn_b, ones512)


def _s5_tables(a_re, a_im, log_dt, b_re, b_im, c_re, c_im):
    Lc, G, N, P = S5_CHUNK, S5_GROUPS, S5_N, S5_P
    dt = jnp.exp(log_dt)[..., None]
    lam_re, lam_im = dt * a_re, dt * a_im
    tau = jnp.arange(Lc + 1, dtype=F32)[:, None, None, None]
    mag = jnp.exp(tau * lam_re)
    pw_re, pw_im = mag * jnp.cos(tau * lam_im), mag * jnp.sin(tau * lam_im)
    ab_re, ab_im = pw_re[1], pw_im[1]
    den = a_re * a_re + a_im * a_im
    nr = ab_re - 1.0
    cf_re, cf_im = (nr * a_re + ab_im * a_im) / den, (ab_im * a_re - nr * a_im) / den
    bp_re = cf_re[..., None] * b_re[None] - cf_im[..., None] * b_im[None]
    bp_im = cf_re[..., None] * b_im[None] + cf_im[..., None] * b_re[None]
    pb_re = pw_re[..., None] * bp_re[None] - pw_im[..., None] * bp_im[None]
    pb_im = pw_re[..., None] * bp_im[None] + pw_im[..., None] * bp_re[None]
    kk = (jnp.einsum('gqn,tdgnp->tdgqp', c_re, pb_re[:Lc]) - jnp.einsum('gqn,tdgnp->tdgqp', c_im, pb_im[:Lc]))
    jj = np.arange(Lc)[:, None]
    ii = np.arange(Lc)[None, :]
    dist = np.abs(ii - jj)
    kf = kk[dist, 0] * jnp.asarray(ii >= jj, F32)[..., None, None, None]
    kb = kk[dist, 1] * jnp.asarray(ii <= jj, F32)[..., None, None, None]
    tz = jnp.transpose(kf + kb, (2, 0, 4, 1, 3))
    jr = np.arange(Lc)
    emap = lambda pbx, order, d: jnp.transpose(pbx[order, d], (1, 0, 3, 2))
    em = jnp.stack([emap(pb_re, Lc - 1 - jr, 0), emap(pb_im, Lc - 1 - jr, 0),
                    emap(pb_re, jr, 1), emap(pb_im, jr, 1)], axis=3)

    def gmap(order, d):
        pr, pi = pw_re[order, d], pw_im[order, d]
        cp_re = c_re[None] * pr[:, :, None, :] - c_im[None] * pi[:, :, None, :]
        cp_im = c_re[None] * pi[:, :, None, :] + c_im[None] * pr[:, :, None, :]
        to = lambda t: jnp.transpose(t, (1, 3, 0, 2))
        return to(cp_re), to(-cp_im)

    gk = jnp.stack(gmap(jr + 1, 0) + gmap(Lc - jr, 1), axis=1)
    NO, NQ, NG = S5_OCTETS, 2, 4
    eye8, eye2, eye4 = (jnp.eye(k, dtype=F32) for k in (8, NQ, NG))
    wy = jnp.einsum('ogjpiq,gh->ojgpihq', tz.reshape(NO, 8, Lc, P, Lc, P), eye8)
    wy = wy.reshape(NO, Lc * LANE, Lc * LANE)
    we = jnp.einsum('orgjpkn,rs,gh->ojrgpskhn', em.reshape(NO, NQ, NG, Lc, P, 4, N), eye2, eye4)
    we = we.reshape(NO, Lc * LANE, NQ * 4 * NG * N)
    wz = jnp.concatenate([wy, we], axis=-1)
    gm = jnp.einsum('orhkniq,rs,ht->orkhnistq', gk.reshape(NO, NQ, NG, 4, N, Lc, P), eye2, eye4)
    gm = gm.reshape(NO, NQ * 4 * NG * N, Lc * LANE)
    apow = jnp.stack([pw_re[Lc, 0], pw_im[Lc, 0], pw_re[Lc, 1], pw_im[Lc, 1]], axis=1)
    apow = jnp.transpose(apow.reshape(G // NG, NG, 4, N), (0, 2, 1, 3)).reshape(G // NG, 4, NG * N)
    return wz, gm, apow


def _s5_local_kernel(u_ref, wz_ref, y_ref, e_ref):
    Lc = u_ref.shape[1]
    lhs = jnp.concatenate([u_ref[:, j, :] for j in range(Lc)], axis=-1).astype(BF16)
    z = _dot(lhs, wz_ref[...])
    for i in range(Lc):
        y_ref[:, i, :] = z[:, i * LANE:(i + 1) * LANE]
    e_ref[...] = z[:, Lc * LANE:]


def _s5_local_call(proj3, wz):
    NR, Lc, _ = proj3.shape
    NO, K, N = wz.shape
    NE = N - Lc * LANE
    tr = NR // 8
    cu = COL_DU // LANE
    return pl.pallas_call(
        _s5_local_kernel,
        out_shape=(jax.ShapeDtypeStruct((NR, Lc, GROUP_W), F32), jax.ShapeDtypeStruct((NR, NO * NE), F32)),
        grid=(NO, NR // tr),
        in_specs=[
            pl.BlockSpec((tr, Lc, LANE), lambda o, i: (i, 0, cu + o)),
            pl.BlockSpec((None, K, N), lambda o, i: (o, 0, 0)),
        ],
        out_specs=(pl.BlockSpec((tr, Lc, LANE), lambda o, i: (i, 0, o)),
                   pl.BlockSpec((tr, NE), lambda o, i: (i, o))),
        compiler_params=_cparams(("arbitrary", "arbitrary")),
        name="mixer_d_local",
    )(proj3, wz)


def _s5_scan_kernel(e_ref, ap_ref, x_ref, *, n_batch, nlc, ncc):
    NS = e_ref.shape[1] // 4
    af_re, af_im = ap_ref[0:1, :], ap_ref[1:2, :]
    ab_re, ab_im = ap_ref[2:3, :], ap_ref[3:4, :]

    def step(a_re, a_im, x_re, x_im, e):
        return a_re * x_re - a_im * x_im + e[:, :NS], a_re * x_im + a_im * x_re + e[:, NS:]

    def phase(row0, n_chunks, stride, carry):
        n_tiles = n_chunks // 8

        def body(t, carry):
            out = []
            for b in range(n_batch):
                xf_re, xf_im, xb_re, xb_im = carry[4 * b:4 * b + 4]
                rf = pl.ds(pl.multiple_of(row0 + b * stride + t * 8, 8), 8)
                rb = pl.ds(pl.multiple_of(row0 + b * stride + (n_tiles - 1 - t) * 8, 8), 8)
                ef = e_ref[rf, 0:2 * NS]
                eb = e_ref[rb, 2 * NS:4 * NS]
                xf_in, xb_in = [], [None] * 8
                for j in range(8):
                    xf_in.append(jnp.concatenate([xf_re, xf_im], axis=-1))
                    xf_re, xf_im = step(af_re, af_im, xf_re, xf_im, ef[j:j + 1, :])
                for j in reversed(range(8)):
                    xb_in[j] = jnp.concatenate([xb_re, xb_im], axis=-1)
                    xb_re, xb_im = step(ab_re, ab_im, xb_re, xb_im, eb[j:j + 1, :])
                x_ref[rf, 0:2 * NS] = jnp.concatenate(xf_in, axis=0)
                x_ref[rb, 2 * NS:4 * NS] = jnp.concatenate(xb_in, axis=0)
                out += [xf_re, xf_im, xb_re, xb_im]
            return tuple(out)

        return lax.fori_loop(0, n_tiles, body, carry)

    zero = jnp.zeros((1, NS), F32)
    carry = phase(n_batch * nlc, ncc, ncc, (zero,) * (4 * n_batch))
    phase(0, nlc, nlc, carry)


def _s5_scan_call(e, apow, n_batch, seq, ctx):
    NR, NEall = e.shape
    NQ8, _, NS = apow.shape
    return pl.pallas_call(
        functools.partial(_s5_scan_kernel, n_batch=n_batch, nlc=seq // S5_CHUNK, ncc=ctx // S5_CHUNK),
        out_shape=jax.ShapeDtypeStruct((NR, NEall), F32),
        grid=(NQ8,),
        in_specs=[
            pl.BlockSpec((NR, 4 * NS), lambda q: (0, q)),
            pl.BlockSpec((None, 4, NS), lambda q: (q, 0, 0)),
        ],
        out_specs=pl.BlockSpec((NR, 4 * NS), lambda q: (0, q)),
        compiler_params=_cparams(("arbitrary",)),
        name="mixer_d_scan",
    )(e, apow)


def _s5_carry_kernel(x_ref, gm_ref, yl_ref, y_ref):
    Lc = yl_ref.shape[1]
    y = _dot(x_ref[...].astype(BF16), gm_ref[...])
    for i in range(Lc):
        y_ref[:, i, :] = yl_ref[:, i, :] + y[:, i * LANE:(i + 1) * LANE]


def _s5_carry_call(xin, gm, yloc):
    NR, Lc, _ = yloc.shape
    NO, NE, N = gm.shape
    tr = NR // 8
    return pl.pallas_call(
        _s5_carry_kernel,
        out_shape=jax.ShapeDtypeStruct((NR, Lc, GROUP_W), F32),
        grid=(NO, NR // tr),
        in_specs=[
            pl.BlockSpec((tr, NE), lambda o, i: (i, o)),
            pl.BlockSpec((None, NE, N), lambda o, i: (o, 0, 0)),
            pl.BlockSpec((tr, Lc, LANE), lambda o, i: (i, 0, o)),
        ],
        out_specs=pl.BlockSpec((tr, Lc, LANE), lambda o, i: (i, 0, o)),
        compiler_params=_cparams(("arbitrary", "arbitrary")),
        name="mixer_d_carry",
    )(xin, gm, yloc)


def _s5_out_kernel(y_ref, u_ref, d_ref, w_ref, b_ref, o_ref):
    y = y_ref[...] + d_ref[...] * u_ref[...]
    c = math.sqrt(2.0 / math.pi)
    y = 0.5 * y * (1.0 + jnp.tanh(c * (y + 0.044715 * (y * y * y))))
    z = _dot(y.astype(BF16), w_ref[...]) + b_ref[...]
    o_ref[...] = (y * _sigmoid(z)).astype(o_ref.dtype)


def _s5_out_call(y_tok, proj, d_skip, glu_w, glu_b, tm):
    R = proj.shape[0]
    vec = pl.BlockSpec((1, GROUP_W), lambda i: (0, 0))
    return pl.pallas_call(
        _s5_out_kernel,
        out_shape=jax.ShapeDtypeStruct((R, GROUP_W), BF16),
        grid=(R // tm,),
        in_specs=[
            pl.BlockSpec((tm, GROUP_W), lambda i: (i, 0)),
            pl.BlockSpec((tm, GROUP_W), lambda i: (i, COL_DU // GROUP_W)),
            vec,
            pl.BlockSpec((GROUP_W, GROUP_W), lambda i: (0, 0)),
            vec,
        ],
        out_specs=pl.BlockSpec((tm, GROUP_W), lambda i: (i, 0)),
        compiler_params=_cparams(("arbitrary",)),
        name="mixer_d_out",
    )(y_tok, proj, d_skip, glu_w, glu_b)


def _rope_tables(n_tokens):
    t = jnp.arange(n_tokens)
    nf = HEAD_DIM // 4
    inv = 1.0 / (ROPE_BASE ** (jnp.arange(nf, dtype=F32) / nf))

    def ang(pp):
        a = pp.astype(F32)[:, None] * inv[None, :]
        return jnp.concatenate([a, a], -1)

    a = jnp.concatenate([ang(t // GRID_W), ang(t % GRID_W)], -1)
    cos, sin = jnp.cos(a), jnp.sin(a)
    sign = np.where((np.arange(HEAD_DIM) % 32) < 16, -1.0, 1.0).astype(np.float32)
    cos2 = jnp.concatenate([cos, cos], -1)
    sin2 = jnp.concatenate([sin * sign, sin * sign], -1)
    return cos2, sin2


def _block_ones(n):
    return jnp.asarray(np.kron(np.eye(n // HEAD_DIM), np.ones((HEAD_DIM, HEAD_DIM))), BF16)


def _permute_w_in(w_in):
    cuts = np.cumsum([512, 512, 512, 512, 512, 512, 32, 32, 96, 512, 128, 128, 512])
    seg = lambda i: w_in[..., (0 if i == 0 else cuts[i - 1]):cuts[i]]
    L, D = w_in.shape[:2]
    pad = jnp.zeros((L, D, 96), w_in.dtype)
    parts = [seg(0), seg(1), seg(2), seg(3), seg(4), seg(5), seg(9), seg(12),
             seg(6), seg(7), seg(8), pad, seg(10), seg(11)]
    return jnp.concatenate(parts, axis=-1).astype(BF16)


def _forward(x, c, ctx, c_ctx, w_ada, b_ada, norm_g, ffn1_wg, ffn1_wu, ffn1_wd, ffn2_wg, ffn2_wu, ffn2_wd,
             w_in, w_out, na_q_g, na_k_g, na_rpb, rw_conv, rw_w0, rw_w_up, rw_a0, rw_a_up, rw_g_up,
             rw_k_k, rw_k_a, rw_r_k, rw_gn_w, rw_gn_b, wa_q_g, wa_k_g, wa_sink, s5_a_re, s5_a_im,
             s5_log_dt, s5_b_re, s5_b_im, s5_c_re, s5_c_im, s5_d, s5_glu_w, s5_glu_b, *, tm, tf):
    B, SEQ, D = x.shape
    CTX = ctx.shape[1]
    L = w_ada.shape[0]
    n_lat = B * SEQ
    n_qb = SEQ // QBLK

    bf = lambda t: t.astype(BF16)
    f1g, f1u, f1d, f2g, f2u, f2d = map(bf, (ffn1_wg, ffn1_wu, ffn1_wd, ffn2_wg, ffn2_wu, ffn2_wd))
    w_in_p = _permute_w_in(w_in)
    w_out_b = bf(w_out)
    glu_w_b = bf(s5_glu_w)
    ones128, ones512 = _block_ones(LANE), _block_ones(GROUP_W)
    cos_t, sin_t = _rope_tables(SEQ)
    na_bias = [_na_bias_table(na_rpb[l], n_qb) for l in range(L)]
    zr = lambda r, cdim: jnp.zeros((L, r, cdim), F32)
    wlr = jnp.concatenate([
        jnp.concatenate([rw_w_up[:, 0], rw_w_up[:, 1], zr(32, 3 * GROUP_W)], axis=-1),
        jnp.concatenate([zr(32, 2 * GROUP_W), rw_a_up[:, 0], rw_a_up[:, 1], zr(32, GROUP_W)], axis=-1),
        jnp.concatenate([zr(96, 4 * GROUP_W), rw_g_up], axis=-1),
        zr(96, 5 * GROUP_W)], axis=1).astype(F32)
    s5_tabs = jax.vmap(_s5_tables)(s5_a_re, s5_a_im, s5_log_dt, s5_b_re, s5_b_im, s5_c_re, s5_c_im)
    tile2 = lambda t: jnp.concatenate([t, t], axis=-1)

    cc = jnp.concatenate([c, c_ctx[None], jnp.zeros((8 - B - 1, D), F32)], axis=0)
    mods_all = _ada_call(cc, w_ada, b_ada)[:, :B + 1].reshape(L, B + 1, N_MOD, D)

    xs = jnp.concatenate([x.reshape(n_lat, D), ctx.reshape(B * CTX, D)], axis=0)
    R = xs.shape[0]
    for l in range(L):
        want_ctx = l < L - 1
        mods = mods_all[l]
        ng = norm_g[l]
        xs = _ffn_call(xs, mods, ng[0:1], f1g, f1u, f1d, l, 0, R, SEQ, B, tm, tf)
        proj = _win_call(xs, mods, ng[1:2], w_in_p, l, SEQ, B, tm, D_IN_PAD // 3)
        o_a = _na_call(proj, na_bias[l], tile2(na_q_g[l][None]), tile2(na_k_g[l][None]), ones128,
                       B, SEQ, CTX, want_ctx)
        prep = _rw_prep_call(proj, rw_conv[l], wlr[l], rw_w0[l], rw_a0[l], rw_k_k[l][None], rw_k_a[l][None],
                             ones512, B, SEQ, CTX, min(tm, 256))
        vec = lambda t: t.reshape(1, GROUP_W)
        yf, yb = _rw_scan_call(prep, B, SEQ, CTX)
        o_b = _rw_readout_call(prep, yf, yb, vec(rw_r_k[l]), vec(rw_gn_w[l]), vec(rw_gn_b[l]), ones512, tm)
        o_c = _wa_call(proj, wa_sink[l], cos_t, sin_t, tile2(wa_q_g[l][None]), tile2(wa_k_g[l][None]), ones128,
                       B, SEQ, CTX, want_ctx)
        wz, gm, apow = (t[l] for t in s5_tabs)
        yloc, e_loc = _s5_local_call(proj.reshape(R // S5_CHUNK, S5_CHUNK, D_IN_PAD), bf(wz))
        xin = _s5_scan_call(e_loc, apow, B, SEQ, CTX)
        y_tok = _s5_carry_call(xin, bf(gm), yloc).reshape(R, GROUP_W)
        o_d = _s5_out_call(y_tok, proj, vec(s5_d[l]), glu_w_b[l], vec(s5_glu_b[l]), tm)
        n_rows = R if want_ctx else n_lat
        xs = _wout_call(xs, (o_a, o_b, o_c, o_d), mods, w_out_b, l, n_rows, SEQ, B, tm)
        xs = _ffn_call(xs, mods, ng[2:3], f2g, f2u, f2d, l, 6, n_rows, SEQ, B, tm, tf)
    return xs[:n_lat].reshape(B, SEQ, D)


def kernel(x, c, ctx, c_ctx, w_ada, b_ada, norm_g, ffn1_wg, ffn1_wu, ffn1_wd, ffn2_wg, ffn2_wu, ffn2_wd, w_in, w_out, na_q_g, na_k_g, na_rpb, rw_conv, rw_w0, rw_w_up, rw_a0, rw_a_up, rw_g_up, rw_k_k, rw_k_a, rw_r_k, rw_gn_w, rw_gn_b, wa_q_g, wa_k_g, wa_sink, s5_a_re, s5_a_im, s5_log_dt, s5_b_re, s5_b_im, s5_c_re, s5_c_im, s5_d, s5_glu_w, s5_glu_b):
    return _forward(x, c, ctx, c_ctx, w_ada, b_ada, norm_g, ffn1_wg, ffn1_wu, ffn1_wd, ffn2_wg, ffn2_wu, ffn2_wd,
                    w_in, w_out, na_q_g, na_k_g, na_rpb, rw_conv, rw_w0, rw_w_up, rw_a0, rw_a_up, rw_g_up,
                    rw_k_k, rw_k_a, rw_r_k, rw_gn_w, rw_gn_b, wa_q_g, wa_k_g, wa_sink, s5_a_re, s5_a_im,
                    s5_log_dt, s5_b_re, s5_b_im, s5_c_re, s5_c_im, s5_d, s5_glu_w, s5_glu_b, tm=512, tf=512)
```

```python
import functools
import math

import numpy as np
import jax
import jax.numpy as jnp
from jax import lax
from jax.experimental import pallas as pl
from jax.experimental.pallas import tpu as pltpu

F32 = jnp.float32
BF16 = jnp.bfloat16

D_MODEL = 2048
GRID_W = 64
HEAD_DIM = 64
GROUP_W = D_MODEL // 4
N_MOD = 9
NORM_EPS = 1e-6
ROPE_BASE = 10000.0
NEG_INF = -1e30

NA_HEADS = GROUP_W // HEAD_DIM
NA_WIN_R = 8
NA_WIN_C = 16
RW_HEADS = GROUP_W // HEAD_DIM
RW_DECAY_RANK = 32
RW_ICLR_RANK = 32
RW_GATE_RANK = 96
RW_GN_EPS = 64e-5
WA_HEADS = GROUP_W // HEAD_DIM
WA_KV_HEADS = 2
WA_GROUP = WA_HEADS // WA_KV_HEADS
WA_WINDOW = 128
S5_P = 16
S5_GROUPS = GROUP_W // S5_P
S5_N = 64

LANE = 128
QBLK = 2 * GRID_W
NA_KBLKS = 5
RW_CHUNK = 64
S5_CHUNK = 8
S5_OCTETS = 4
VMEM_LIMIT = 56 * 1024 * 1024

COL_AQ, COL_AK, COL_AV = 0, 512, 1024
COL_BR = 1536
COL_CQ = 3072
COL_DU = 3584
COL_BLR = 4096
COL_CK, COL_CV = 4352, 4480
D_IN_PAD = 4608

PB_R, PB_V, PB_KK, PB_G = 0, 1, 2, 9
PB_N = 10


def _cparams(sem):
    return pltpu.CompilerParams(dimension_semantics=sem, vmem_limit_bytes=VMEM_LIMIT)


def _dot(a, b):
    return jnp.dot(a, b, preferred_element_type=F32)


def _dot_nt(a, b):
    return lax.dot_general(a, b, (((1,), (1,)), ((), ())), preferred_element_type=F32)


def _dot_tn(a, b):
    return lax.dot_general(a, b, (((0,), (0,)), ((), ())), preferred_element_type=F32)


def _sigmoid(x):
    return 1.0 / (1.0 + jnp.exp(-x))


def _modulate(x, g, shift, scale):
    ms = jnp.mean(x * x, axis=-1, keepdims=True)
    return (x * lax.rsqrt(ms + NORM_EPS) * g) * (1.0 + scale) + shift


def _seg_sum(x, ones_bd):
    hi = x.astype(BF16)
    r1 = x - hi.astype(F32)
    mid = r1.astype(BF16)
    lo = (r1 - mid.astype(F32)).astype(BF16)
    return _dot(hi, ones_bd) + _dot(mid, ones_bd) + _dot(lo, ones_bd)


def _head_rmsnorm(x, ones_bd):
    ms = _seg_sum(x * x, ones_bd) * (1.0 / HEAD_DIM)
    return x * lax.rsqrt(ms + NORM_EPS)


def _ada_kernel(c_ref, w_ref, b_ref, o_ref):
    c = c_ref[...]
    s = (c * _sigmoid(c)).astype(BF16)
    o_ref[...] = _dot(s, w_ref[...].astype(BF16)) + b_ref[...]


def _ada_call(cc, w_ada, b_ada):
    L, D, N = w_ada.shape
    tn = 1024
    return pl.pallas_call(
        _ada_kernel,
        out_shape=jax.ShapeDtypeStruct((L, 8, N), F32),
        grid=(L, N // tn),
        in_specs=[
            pl.BlockSpec((8, D), lambda l, j: (0, 0)),
            pl.BlockSpec((None, D, tn), lambda l, j: (l, 0, j)),
            pl.BlockSpec((None, 1, tn), lambda l, j: (l, 0, j)),
        ],
        out_specs=pl.BlockSpec((None, 8, tn), lambda l, j: (l, 0, j)),
        compiler_params=_cparams(("arbitrary", "arbitrary")),
        name="adaln",
    )(cc, w_ada, b_ada.reshape(L, 1, N))


def _ffn_kernel(x_ref, mod_ref, g_ref, wg_ref, wu_ref, wd_ref, o_ref, h_sc, a_sc, *, mi, nj):
    j = pl.program_id(1)
    tf = wg_ref.shape[1]
    tn = o_ref.shape[1]

    @pl.when(j == 0)
    def _():
        h = _modulate(x_ref[...], g_ref[...], mod_ref[mi:mi + 1, :], mod_ref[mi + 1:mi + 2, :])
        h_sc[...] = h.astype(BF16)

    @pl.when(j < nj)
    def _():
        h = h_sc[...]
        gate = _dot(h, wg_ref[...])
        up = _dot(h, wu_ref[...])
        a_sc[:, pl.ds(pl.multiple_of(j * tf, tf), tf)] = (gate * _sigmoid(gate) * up).astype(BF16)

    @pl.when(j >= nj)
    def _():
        cols = pl.ds(pl.multiple_of((j - nj) * tn, tn), tn)
        acc = _dot(a_sc[...], wd_ref[...])
        o_ref[...] = x_ref[:, cols] + 0.5 * mod_ref[mi + 2:mi + 3, cols] * acc


def _ffn_call(xs, mods, g, wg, wu, wd, l, mi, n_rows, rows_per_seq, n_batch, tm, tf):
    D = xs.shape[1]
    F = wg.shape[2]
    tn = 512
    nj = F // tf
    tps = rows_per_seq // tm
    return pl.pallas_call(
        functools.partial(_ffn_kernel, mi=mi, nj=nj),
        out_shape=jax.ShapeDtypeStruct((n_rows, D), F32),
        grid=(n_rows // tm, nj + D // tn),
        in_specs=[
            pl.BlockSpec((tm, D), lambda i, j: (i, 0)),
            pl.BlockSpec((None, N_MOD, D), lambda i, j: (jnp.minimum(i // tps, n_batch), 0, 0)),
            pl.BlockSpec((1, D), lambda i, j: (0, 0)),
            pl.BlockSpec((None, D, tf), lambda i, j: (l, 0, jnp.minimum(j, nj - 1))),
            pl.BlockSpec((None, D, tf), lambda i, j: (l, 0, jnp.minimum(j, nj - 1))),
            pl.BlockSpec((None, F, tn), lambda i, j: (l, 0, jnp.maximum(j - nj, 0))),
        ],
        out_specs=pl.BlockSpec((tm, tn), lambda i, j: (i, jnp.maximum(j - nj, 0))),
        scratch_shapes=[pltpu.VMEM((tm, D), BF16), pltpu.VMEM((tm, F), BF16)],
        compiler_params=_cparams(("arbitrary", "arbitrary")),
        name="ffn",
    )(xs, mods, g, wg, wu, wd)


def _win_kernel(x_ref, mod_ref, g_ref, w_ref, o_ref, h_sc):
    @pl.when(pl.program_id(1) == 0)
    def _():
        h = _modulate(x_ref[...], g_ref[...], mod_ref[3:4, :], mod_ref[4:5, :])
        h_sc[...] = h.astype(BF16)

    o_ref[...] = _dot(h_sc[...], w_ref[...])


def _win_call(xs, mods, g, w_in, l, rows_per_seq, n_batch, tm, tn):
    R, D = xs.shape
    N = w_in.shape[2]
    tps = rows_per_seq // tm
    return pl.pallas_call(
        _win_kernel,
        out_shape=jax.ShapeDtypeStruct((R, N), F32),
        grid=(R // tm, N // tn),
        in_specs=[
            pl.BlockSpec((tm, D), lambda i, j: (i, 0)),
            pl.BlockSpec((None, N_MOD, D), lambda i, j: (jnp.minimum(i // tps, n_batch), 0, 0)),
            pl.BlockSpec((1, D), lambda i, j: (0, 0)),
            pl.BlockSpec((None, D, tn), lambda i, j: (l, 0, j)),
        ],
        out_specs=pl.BlockSpec((tm, tn), lambda i, j: (i, j)),
        scratch_shapes=[pltpu.VMEM((tm, D), BF16)],
        compiler_params=_cparams(("arbitrary", "arbitrary")),
        name="in_proj",
    )(xs, mods, g, w_in)


def _wout_kernel(x_ref, oa_ref, ob_ref, oc_ref, od_ref, mod_ref, w_ref, o_ref):
    acc = _dot(oa_ref[...], w_ref[0 * GROUP_W:1 * GROUP_W, :])
    acc += _dot(ob_ref[...], w_ref[1 * GROUP_W:2 * GROUP_W, :])
    acc += _dot(oc_ref[...], w_ref[2 * GROUP_W:3 * GROUP_W, :])
    acc += _dot(od_ref[...], w_ref[3 * GROUP_W:4 * GROUP_W, :])
    o_ref[...] = x_ref[...] + mod_ref[5:6, :] * acc


def _wout_call(xs, outs, mods, w_out, l, n_rows, rows_per_seq, n_batch, tm):
    D = xs.shape[1]
    tps = rows_per_seq // tm
    ospec = pl.BlockSpec((tm, GROUP_W), lambda i: (i, 0))
    return pl.pallas_call(
        _wout_kernel,
        out_shape=jax.ShapeDtypeStruct((n_rows, D), F32),
        grid=(n_rows // tm,),
        in_specs=[
            pl.BlockSpec((tm, D), lambda i: (i, 0)),
            ospec, ospec, ospec, ospec,
            pl.BlockSpec((None, N_MOD, D), lambda i: (jnp.minimum(i // tps, n_batch), 0, 0)),
            pl.BlockSpec((None, D, D), lambda i: (l, 0, 0)),
        ],
        out_specs=pl.BlockSpec((tm, D), lambda i: (i, 0)),
        compiler_params=_cparams(("arbitrary",)),
        name="out_proj",
    )(xs, *outs, mods, w_out)


def _na_case_reps(n_qb):
    return (0, 1, 2, n_qb - 2, n_qb - 1)


def _na_start(p, n_qb):
    return jnp.clip(p - 2, 0, n_qb - NA_KBLKS)


def _na_bias_table(rpb, n_qb):
    H = rpb.shape[0]
    rows = 2 * n_qb
    qc = np.arange(GRID_W)[:, None]
    kc = np.arange(GRID_W)[None, :]
    cs = np.clip(qc - NA_WIN_C // 2, 0, GRID_W - NA_WIN_C)
    okc = (kc >= cs) & (kc < cs + NA_WIN_C)
    dc = np.where(okc, kc - qc + NA_WIN_C - 1, 0)
    rp = rpb.astype(F32).reshape(H, 2 * NA_WIN_R - 1, 2 * NA_WIN_C - 1)
    blocks = jnp.where(jnp.asarray(okc)[None, None], jnp.take(rp, jnp.asarray(dc), axis=2), NEG_INF)
    neg = jnp.full((H, GRID_W, GRID_W), NEG_INF, F32)
    cases = []
    for p in _na_case_reps(n_qb):
        start = min(max(p - 2, 0), n_qb - NA_KBLKS)
        qrows = []
        for qr in range(2):
            qa = 2 * p + qr
            rs = min(max(qa - NA_WIN_R // 2, 0), rows - NA_WIN_R)
            krow = []
            for kr in range(2 * NA_KBLKS):
                ka = 2 * start + kr
                krow.append(blocks[:, ka - qa + NA_WIN_R - 1] if rs <= ka < rs + NA_WIN_R else neg)
            qrows.append(jnp.concatenate(krow, axis=-1))
        cases.append(jnp.concatenate(qrows, axis=-2))
    return jnp.stack(cases, axis=0)


def _na_kernel(q_ref, k_ref, v_ref, kc_ref, vc_ref, bias_ref, qg_ref, kg_ref, ones_ref, o_ref,
               kn_sc, vn_sc, kcn_sc, vcn_sc, *, n_qb, seq, ctx):
    p = pl.program_id(2)
    ones_bd = ones_ref[...]
    scale = HEAD_DIM ** -0.5

    @pl.when(p == 0)
    def _():
        kg = kg_ref[...]
        rows = 512

        def body(i, carry):
            sl = pl.ds(pl.multiple_of(i * rows, rows), rows)
            kn_sc[sl, :] = (_head_rmsnorm(k_ref[sl, :], ones_bd) * kg).astype(BF16)
            vn_sc[sl, :] = v_ref[sl, :].astype(BF16)
            return carry

        lax.fori_loop(0, seq // rows, body, 0)
        kcn_sc[...] = (_head_rmsnorm(kc_ref[...], ones_bd) * kg).astype(BF16)
        vcn_sc[...] = vc_ref[...].astype(BF16)

    q = (_head_rmsnorm(q_ref[...], ones_bd) * (qg_ref[...] * scale)).astype(BF16)
    kcn = kcn_sc[...]
    vcn = vcn_sc[...]

    def finish(outs):
        o_ref[...] = jnp.concatenate(outs, axis=-1).astype(o_ref.dtype)

    @pl.when(p < n_qb)
    def _():
        start = _na_start(p, n_qb)
        case = p - start
        sl = pl.ds(pl.multiple_of(start * QBLK, QBLK), NA_KBLKS * QBLK)
        kw = kn_sc[sl, :]
        vw = vn_sc[sl, :]
        outs = []
        for h in range(2):
            hs = slice(h * HEAD_DIM, (h + 1) * HEAD_DIM)
            qh = q[:, hs]
            s_n = _dot_nt(qh, kw[:, hs]) + bias_ref[case, h]
            s_c = _dot_nt(qh, kcn[:, hs])
            m = jnp.maximum(jnp.max(s_n, axis=-1, keepdims=True), jnp.max(s_c, axis=-1, keepdims=True))
            p_n = jnp.exp(s_n - m)
            p_c = jnp.exp(s_c - m)
            den = jnp.sum(p_n, axis=-1, keepdims=True) + jnp.sum(p_c, axis=-1, keepdims=True)
            o = _dot(p_n.astype(BF16), vw[:, hs]) + _dot(p_c.astype(BF16), vcn[:, hs])
            outs.append(o / den)
        finish(outs)

    @pl.when(p >= n_qb)
    def _():
        outs = []
        for h in range(2):
            hs = slice(h * HEAD_DIM, (h + 1) * HEAD_DIM)
            s_c = _dot_nt(q[:, hs], kcn[:, hs])
            m = jnp.max(s_c, axis=-1, keepdims=True)
            p_c = jnp.exp(s_c - m)
            den = jnp.sum(p_c, axis=-1, keepdims=True)
            outs.append(_dot(p_c.astype(BF16), vcn[:, hs]) / den)
        finish(outs)


def _na_call(proj, bias, qg, kg, ones_bd, n_batch, seq, ctx, want_ctx):
    R = proj.shape[0]
    n_qb = seq // QBLK
    n_cb = ctx // QBLK
    steps = n_qb + (n_cb if want_ctx else 0)
    lat_qb = n_batch * n_qb
    ctx_kb = n_batch * seq // ctx

    def qrow(b, p):
        return jnp.where(p < n_qb, b * n_qb + p, lat_qb + b * n_cb + (p - n_qb))

    cq, ck, cv = COL_AQ // LANE, COL_AK // LANE, COL_AV // LANE
    return pl.pallas_call(
        functools.partial(_na_kernel, n_qb=n_qb, seq=seq, ctx=ctx),
        out_shape=jax.ShapeDtypeStruct((R, GROUP_W), BF16),
        grid=(n_batch, NA_HEADS // 2, steps),
        in_specs=[
            pl.BlockSpec((QBLK, LANE), lambda b, hp, p: (qrow(b, p), cq + hp)),
            pl.BlockSpec((seq, LANE), lambda b, hp, p: (b, ck + hp)),
            pl.BlockSpec((seq, LANE), lambda b, hp, p: (b, cv + hp)),
            pl.BlockSpec((ctx, LANE), lambda b, hp, p: (ctx_kb + b, ck + hp)),
            pl.BlockSpec((ctx, LANE), lambda b, hp, p: (ctx_kb + b, cv + hp)),
            pl.BlockSpec((5, 2, QBLK, NA_KBLKS * QBLK), lambda b, hp, p: (0, hp, 0, 0)),
            pl.BlockSpec((1, LANE), lambda b, hp, p: (0, 0)),
            pl.BlockSpec((1, LANE), lambda b, hp, p: (0, 0)),
            pl.BlockSpec((LANE, LANE), lambda b, hp, p: (0, 0)),
        ],
        out_specs=pl.BlockSpec((QBLK, LANE), lambda b, hp, p: (qrow(b, p), hp)),
        scratch_shapes=[pltpu.VMEM((seq, LANE), BF16), pltpu.VMEM((seq, LANE), BF16),
                        pltpu.VMEM((ctx, LANE), BF16), pltpu.VMEM((ctx, LANE), BF16)],
        compiler_params=_cparams(("arbitrary", "arbitrary", "arbitrary")),
        name="mixer_a",
    )(proj, proj, proj, proj, proj, bias, qg, kg, ones_bd)


def _rope(x, cos, sin_signed, first_half):
    rot = jnp.where(first_half, pltpu.roll(x, LANE - 16, 1), pltpu.roll(x, 16, 1))
    return x * cos + rot * sin_signed


def _wa_kernel(sink_ref, q_ref, k_ref, v_ref, kc_ref, vc_ref, cosk_ref, sin_k_ref, cosq_ref, sinq_ref,
               qg_ref, kg_ref, ones_ref, o_ref, kn_sc, vn_sc, kcn_sc, vcn_sc, *, n_qb, seq, ctx):
    p = pl.program_id(1)
    ones_bd = ones_ref[...]
    scale = HEAD_DIM ** -0.5
    lane = lax.broadcasted_iota(jnp.int32, (1, LANE), 1)
    first_half = (lane % 32) < 16

    @pl.when(p == 0)
    def _():
        kg = kg_ref[...]
        rows = 512

        def body(i, carry):
            sl = pl.ds(pl.multiple_of(i * rows, rows), rows)
            kn = _head_rmsnorm(k_ref[sl, :], ones_bd) * kg
            kn_sc[sl, :] = _rope(kn, cosk_ref[sl, :], sin_k_ref[sl, :], first_half).astype(BF16)
            vn_sc[sl, :] = v_ref[sl, :].astype(BF16)
            return carry

        lax.fori_loop(0, seq // rows, body, 0)
        kcn_sc[...] = (_head_rmsnorm(kc_ref[...], ones_bd) * kg).astype(BF16)
        vcn_sc[...] = vc_ref[...].astype(BF16)

    qg = qg_ref[...] * scale
    is_lat = p < n_qb
    cosq = jnp.where(is_lat, cosq_ref[...], 1.0)
    sinq = jnp.where(is_lat, sinq_ref[...], 0.0)
    qh = []
    for c in range(4):
        qn = _head_rmsnorm(q_ref[:, c * LANE:(c + 1) * LANE], ones_bd) * qg
        qn = _rope(qn, cosq, sinq, first_half).astype(BF16)
        qh += [qn[:, :HEAD_DIM], qn[:, HEAD_DIM:]]
    kcn = kcn_sc[...]
    vcn = vcn_sc[...]

    def sink_col(kh):
        return jnp.concatenate(
            [jnp.full((QBLK, 1), sink_ref[kh * WA_GROUP + g], F32) for g in range(WA_GROUP)], axis=0)

    def finish(o_heads):
        o_ref[...] = jnp.concatenate(o_heads, axis=-1).astype(o_ref.dtype)

    @pl.when(is_lat)
    def _():
        ws = jnp.clip((p - 1) * QBLK, 0, seq - 3 * QBLK)
        sl = pl.ds(pl.multiple_of(ws, QBLK), 3 * QBLK)
        kw = kn_sc[sl, :]
        vw = vn_sc[sl, :]
        shp = (WA_GROUP * QBLK, 3 * QBLK)
        qpos = p * QBLK + (lax.broadcasted_iota(jnp.int32, shp, 0) & (QBLK - 1))
        kpos = ws + lax.broadcasted_iota(jnp.int32, shp, 1)
        ok = jnp.abs(kpos - qpos) <= WA_WINDOW
        o_heads = []
        for kh in range(WA_KV_HEADS):
            hs = slice(kh * HEAD_DIM, (kh + 1) * HEAD_DIM)
            qs = jnp.concatenate(qh[kh * WA_GROUP:(kh + 1) * WA_GROUP], axis=0)
            s_w = jnp.where(ok, _dot_nt(qs, kw[:, hs]), NEG_INF)
            s_c = _dot_nt(qs, kcn[:, hs])
            sk = sink_col(kh)
            m = jnp.maximum(jnp.maximum(jnp.max(s_w, axis=-1, keepdims=True),
                                        jnp.max(s_c, axis=-1, keepdims=True)), sk)
            p_w = jnp.exp(s_w - m)
            p_c = jnp.exp(s_c - m)
            den = (jnp.sum(p_w, axis=-1, keepdims=True) + jnp.sum(p_c, axis=-1, keepdims=True)
                   + jnp.exp(sk - m))
            o = (_dot(p_w.astype(BF16), vw[:, hs]) + _dot(p_c.astype(BF16), vcn[:, hs])) / den
            o_heads += [o[g * QBLK:(g + 1) * QBLK, :] for g in range(WA_GROUP)]
        finish(o_heads)

    @pl.when(jnp.logical_not(is_lat))
    def _():
        o_heads = []
        for kh in range(WA_KV_HEADS):
            hs = slice(kh * HEAD_DIM, (kh + 1) * HEAD_DIM)
            qs = jnp.concatenate(qh[kh * WA_GROUP:(kh + 1) * WA_GROUP], axis=0)
            s_c = _dot_nt(qs, kcn[:, hs])
            sk = sink_col(kh)
            m = jnp.maximum(jnp.max(s_c, axis=-1, keepdims=True), sk)
            p_c = jnp.exp(s_c - m)
            den = jnp.sum(p_c, axis=-1, keepdims=True) + jnp.exp(sk - m)
            o = _dot(p_c.astype(BF16), vcn[:, hs]) / den
            o_heads += [o[g * QBLK:(g + 1) * QBLK, :] for g in range(WA_GROUP)]
        finish(o_heads)


def _wa_call(proj, sink, cos_t, sin_t, qg, kg, ones_bd, n_batch, seq, ctx, want_ctx):
    R = proj.shape[0]
    n_qb = seq // QBLK
    n_cb = ctx // QBLK
    steps = n_qb + (n_cb if want_ctx else 0)
    lat_qb = n_batch * n_qb
    ctx_kb = n_batch * seq // ctx

    def qrow(b, p):
        return jnp.where(p < n_qb, b * n_qb + p, lat_qb + b * n_cb + (p - n_qb))

    ck, cv = COL_CK // LANE, COL_CV // LANE
    return pl.pallas_call(
        functools.partial(_wa_kernel, n_qb=n_qb, seq=seq, ctx=ctx),
        out_shape=jax.ShapeDtypeStruct((R, GROUP_W), BF16),
        grid=(n_batch, steps),
        in_specs=[
            pl.BlockSpec(memory_space=pltpu.SMEM),
            pl.BlockSpec((QBLK, GROUP_W), lambda b, p: (qrow(b, p), COL_CQ // GROUP_W)),
            pl.BlockSpec((seq, LANE), lambda b, p: (b, ck)),
            pl.BlockSpec((seq, LANE), lambda b, p: (b, cv)),
            pl.BlockSpec((ctx, LANE), lambda b, p: (ctx_kb + b, ck)),
            pl.BlockSpec((ctx, LANE), lambda b, p: (ctx_kb + b, cv)),
            pl.BlockSpec((seq, LANE), lambda b, p: (0, 0)),
            pl.BlockSpec((seq, LANE), lambda b, p: (0, 0)),
            pl.BlockSpec((QBLK, LANE), lambda b, p: (jnp.minimum(p, n_qb - 1), 0)),
            pl.BlockSpec((QBLK, LANE), lambda b, p: (jnp.minimum(p, n_qb - 1), 0)),
            pl.BlockSpec((1, LANE), lambda b, p: (0, 0)),
            pl.BlockSpec((1, LANE), lambda b, p: (0, 0)),
            pl.BlockSpec((LANE, LANE), lambda b, p: (0, 0)),
        ],
        out_specs=pl.BlockSpec((QBLK, GROUP_W), lambda b, p: (qrow(b, p), 0)),
        scratch_shapes=[pltpu.VMEM((seq, LANE), BF16), pltpu.VMEM((seq, LANE), BF16),
                        pltpu.VMEM((ctx, LANE), BF16), pltpu.VMEM((ctx, LANE), BF16)],
        compiler_params=_cparams(("arbitrary", "arbitrary")),
        name="mixer_c",
    )(sink, proj, proj, proj, proj, proj, cos_t, sin_t, cos_t, sin_t, qg, kg, ones_bd)


def _rw_prep_kernel(x_ref, prev_ref, next_ref, lr_ref, conv_ref, wlr_ref, w0_ref, a0_ref, kk_ref, ka_ref,
                    ones_ref, o_ref, *, tiles_lat, tiles_ctx, n_lat_tiles):
    i = pl.program_id(0)
    tm = x_ref.shape[0]
    in_lat = i < n_lat_tiles
    first = jnp.where(in_lat, i % tiles_lat == 0, (i - n_lat_tiles) % tiles_ctx == 0)
    last = jnp.where(in_lat, i % tiles_lat == tiles_lat - 1, (i - n_lat_tiles) % tiles_ctx == tiles_ctx - 1)
    x = x_ref[...]
    prev_row = jnp.where(first, 0.0, prev_ref[7:8, :])
    next_row = jnp.where(last, 0.0, next_ref[0:1, :])
    row = lax.broadcasted_iota(jnp.int32, (tm, 1), 0)
    x_prev = jnp.where(row == 0, prev_row, pltpu.roll(x, 1, 0))
    x_next = jnp.where(row == tm - 1, next_row, pltpu.roll(x, tm - 1, 0))
    cw = conv_ref[...]
    y = x_prev * cw[0:1, :] + x * cw[1:2, :] + x_next * cw[2:3, :]
    r = y[:, 0:GROUP_W]
    k = y[:, GROUP_W:2 * GROUP_W]
    v = y[:, 2 * GROUP_W:3 * GROUP_W]

    lr = lr_ref[...]
    lane = lax.broadcasted_iota(jnp.int32, (1, lr.shape[1]), 1)
    c1 = RW_DECAY_RANK
    c2 = c1 + RW_ICLR_RANK
    c3 = c2 + RW_GATE_RANK
    act = jnp.where(lane < c1, jnp.tanh(lr),
                    jnp.where(lane < c2, lr, jnp.where(lane < c3, _sigmoid(lr), 0.0)))
    up = jnp.dot(act, wlr_ref[...], preferred_element_type=F32, precision=lax.Precision.HIGHEST)

    ones_bd = ones_ref[...]
    kk = k * kk_ref[...]
    nrm = jnp.sqrt(_seg_sum(kk * kk, ones_bd))
    kk = kk / jnp.maximum(nrm, 1e-12)

    o_ref[:, PB_R * GROUP_W:(PB_R + 1) * GROUP_W] = r
    o_ref[:, PB_V * GROUP_W:(PB_V + 1) * GROUP_W] = v
    o_ref[:, PB_KK * GROUP_W:(PB_KK + 1) * GROUP_W] = kk
    o_ref[:, PB_G * GROUP_W:(PB_G + 1) * GROUP_W] = up[:, 4 * GROUP_W:5 * GROUP_W]
    ka = ka_ref[...]
    for d in range(2):
        z = w0_ref[d:d + 1, :] + up[:, d * GROUP_W:(d + 1) * GROUP_W]
        sp = jnp.maximum(-z, 0.0) + jnp.log(1.0 + jnp.exp(-jnp.abs(z)))
        lw = -jnp.exp(-sp - 0.5)
        a = _sigmoid(a0_ref[d:d + 1, :] + up[:, (2 + d) * GROUP_W:(3 + d) * GROUP_W])
        kd = k * (1.0 + (a - 1.0) * ka)
        base = 3 + 3 * d
        o_ref[:, base * GROUP_W:(base + 1) * GROUP_W] = lw
        o_ref[:, (base + 1) * GROUP_W:(base + 2) * GROUP_W] = a
        o_ref[:, (base + 2) * GROUP_W:(base + 3) * GROUP_W] = kd


def _rw_prep_call(proj, conv_w, wlr, w0, a0, k_k, k_a, ones512, n_batch, seq, ctx, tm):
    R = proj.shape[0]
    n_lat_tiles = n_batch * seq // tm
    nb8 = R // 8
    t8 = tm // 8
    wide = 3 * GROUP_W
    cb = COL_BR // wide
    return pl.pallas_call(
        functools.partial(_rw_prep_kernel, tiles_lat=seq // tm, tiles_ctx=ctx // tm, n_lat_tiles=n_lat_tiles),
        out_shape=jax.ShapeDtypeStruct((R, PB_N * GROUP_W), F32),
        grid=(R // tm,),
        in_specs=[
            pl.BlockSpec((tm, wide), lambda i: (i, cb)),
            pl.BlockSpec((8, wide), lambda i: (jnp.maximum(i * t8 - 1, 0), cb)),
            pl.BlockSpec((8, wide), lambda i: (jnp.minimum((i + 1) * t8, nb8 - 1), cb)),
            pl.BlockSpec((tm, 256), lambda i: (i, COL_BLR // 256)),
            pl.BlockSpec((3, wide), lambda i: (0, 0)),
            pl.BlockSpec((256, 5 * GROUP_W), lambda i: (0, 0)),
            pl.BlockSpec((2, GROUP_W), lambda i: (0, 0)),
            pl.BlockSpec((2, GROUP_W), lambda i: (0, 0)),
            pl.BlockSpec((1, GROUP_W), lambda i: (0, 0)),
            pl.BlockSpec((1, GROUP_W), lambda i: (0, 0)),
            pl.BlockSpec((GROUP_W, GROUP_W), lambda i: (0, 0)),
        ],
        out_specs=pl.BlockSpec((tm, PB_N * GROUP_W), lambda i: (i, 0)),
        compiler_params=_cparams(("arbitrary",)),
        name="mixer_b_prep",
    )(proj, proj, proj, proj, conv_w, wlr, w0, a0, k_k, k_a, ones512)


RW_SPLIT = 1
_NN = ((1,), (0,))
_NT = ((1,), (1,))
_TN = ((0,), (0,))


def _split(x, n):
    parts = []
    for _ in range(n):
        p = x.astype(BF16)
        parts.append(p)
        x = x - p.astype(F32)
    return parts


def _sdot(a, b, dims):
    n = max(len(a), len(b))
    acc = None
    for i, ai in enumerate(a):
        for j, bj in enumerate(b):
            if i + j < n:
                t = lax.dot_general(ai, bj, (dims, ((), ())), preferred_element_type=F32)
                acc = t if acc is None else acc + t
    return acc


def _rw_chunks(at, bt, kt, rt, v, gam, s0, strict, incl):
    n = len(at)
    ids = range(n)
    C = at[0].shape[0]
    sp = lambda t: _split(t, RW_SPLIT)
    b1 = lambda t: [t.astype(BF16)]
    ar = [b1(jnp.concatenate([at[i], rt[i]], axis=0)) for i in ids]
    bk = [sp(jnp.concatenate([bt[i], kt[i]], axis=0)) for i in ids]
    g4 = [_sdot(ar[i], bk[i][:1], _NT) for i in ids]
    a_ab = [jnp.where(strict[i], g4[i][:C, :C], 0.0) for i in ids]
    a_ak = [jnp.where(strict[i], g4[i][:C, C:], 0.0) for i in ids]
    a_rb = [jnp.where(incl[i], g4[i][C:, :C], 0.0) for i in ids]
    a_rk = [jnp.where(incl[i], g4[i][C:, C:], 0.0) for i in ids]
    wv = [_sdot(b1(a_ak[i]), b1(v[i]), _NN) for i in ids]
    x = [jnp.concatenate([at[i], wv[i]], axis=1) for i in ids]

    row = lax.broadcasted_iota(jnp.int32, (C, C), 0)
    col = lax.broadcasted_iota(jnp.int32, (C, C), 1)
    eye = (row == col).astype(F32)
    blk = 16
    same = (row // blk) == (col // blk)
    p = [jnp.where(same, a_ab[i], 0.0) for i in ids]
    m = [eye + p[i] for i in ids]
    for _ in range(int(math.log2(blk)) - 1):
        pb_ = [b1(p[i]) for i in ids]
        p = [_sdot(pb_[i], pb_[i], _NN) for i in ids]
        m = [m[i] + _sdot(b1(m[i]), b1(p[i]), _NN) for i in ids]
    while blk < C:
        wider = (row // (2 * blk)) == (col // (2 * blk))
        join = wider & jnp.logical_not(same)
        mb = [b1(m[i]) for i in ids]
        t = [_sdot(b1(jnp.where(join, a_ab[i], 0.0)), mb[i], _NN) for i in ids]
        m = [m[i] + _sdot(mb[i], b1(t[i]), _NN) for i in ids]
        same, blk = wider, 2 * blk
    x = [_sdot(b1(m[i]), b1(x[i]), _NN) for i in ids]

    qv = [sp(jnp.concatenate([x[i][:, HEAD_DIM:], v[i]], axis=0)) for i in ids]
    pb = [sp(x[i][:, :HEAD_DIM]) for i in ids]
    s0b = [sp(s0[i]) for i in ids]
    y_loc = [_sdot(b1(jnp.concatenate([a_rb[i], a_rk[i]], axis=1)), qv[i][:1], _NN) for i in ids]
    rp = [rt[i] + _sdot(b1(a_rb[i]), pb[i][:1], _NN) for i in ids]
    w = [_sdot(s0b[i], pb[i], _NT) for i in ids]
    y = [y_loc[i] + _sdot(sp(rp[i]), s0b[i], _NT) for i in ids]
    s1 = [(s0[i] + _sdot(sp(w[i]), sp(bt[i]), _NN) + _sdot(qv[i], bk[i], _TN)) * gam[i] for i in ids]
    return y, s1


def _rw_scan_kernel(rf_ref, vf_ref, kkf_ref, lwf_ref, af_ref, kf_ref,
                    rb_ref, vb_ref, kkb_ref, lwb_ref, ab_ref, kb_ref, yf_ref, yb_ref, s_sc):
    s = pl.program_id(1)

    @pl.when(s == 0)
    def _():
        s_sc[...] = jnp.zeros_like(s_sc)

    C = rf_ref.shape[0]
    row = lax.broadcasted_iota(jnp.int32, (C, C), 0)
    col = lax.broadcasted_iota(jnp.int32, (C, C), 1)
    chains = dict(at=[], bt=[], kt=[], rt=[], v=[], gam=[], s0=[], strict=[], incl=[])
    dirs = ((False, rf_ref, vf_ref, kkf_ref, lwf_ref, af_ref, kf_ref),
            (True, rb_ref, vb_ref, kkb_ref, lwb_ref, ab_ref, kb_ref))
    for d, (reverse, r_ref, v_ref, kk_ref, lw_ref, a_ref, k_ref) in enumerate(dirs):
        incl, strict = (row <= col, row < col) if reverse else (row >= col, row > col)
        lw = lw_ref[...]
        kk = kk_ref[...]
        v = v_ref[...]
        cum = jnp.dot(incl.astype(F32), lw, preferred_element_type=F32, precision=lax.Precision.HIGHEST)
        e_l = jnp.exp(cum)
        e_n = jnp.exp(-cum)
        at = -(kk * jnp.exp(cum - lw))
        bt = kk * a_ref[...] * e_n
        kt = k_ref[...] * e_n
        rt = r_ref[...] * e_l
        gam = e_l[0:1, :] if reverse else e_l[C - 1:C, :]
        for h in range(RW_HEADS):
            hs = slice(h * HEAD_DIM, (h + 1) * HEAD_DIM)
            for name, val in (("at", at), ("bt", bt), ("kt", kt), ("rt", rt), ("v", v), ("gam", gam)):
                chains[name].append(val[:, hs])
            chains["s0"].append(s_sc[d, h])
            chains["strict"].append(strict)
            chains["incl"].append(incl)
    y, s1 = _rw_chunks(**chains)
    for d in range(2):
        for h in range(RW_HEADS):
            s_sc[d, h] = s1[d * RW_HEADS + h]
    yf_ref[...] = jnp.concatenate(y[:RW_HEADS], axis=-1)
    yb_ref[...] = jnp.concatenate(y[RW_HEADS:], axis=-1)


def _rw_scan_call(prep, n_batch, seq, ctx):
    R = prep.shape[0]
    C = RW_CHUNK
    nc, nl = ctx // C, seq // C
    ctx_base = n_batch * nl

    def blk_f(b, s):
        return jnp.where(s < nc, ctx_base + b * nc + s, b * nl + (s - nc))

    def blk_b(b, s):
        return jnp.where(s < nc, ctx_base + b * nc + (nc - 1 - s), b * nl + (nl - 1 - (s - nc)))

    def col(blk, cb):
        return pl.BlockSpec((C, GROUP_W), lambda b, s: (blk(b, s), cb))

    in_specs = ([col(blk_f, c) for c in (PB_R, PB_V, PB_KK, 3, 4, 5)]
                + [col(blk_b, c) for c in (PB_R, PB_V, PB_KK, 6, 7, 8)])
    out = jax.ShapeDtypeStruct((R, GROUP_W), F32)
    return pl.pallas_call(
        _rw_scan_kernel,
        out_shape=(out, out),
        grid=(n_batch, nc + nl),
        in_specs=in_specs,
        out_specs=(col(blk_f, 0), col(blk_b, 0)),
        scratch_shapes=[pltpu.VMEM((2, RW_HEADS, HEAD_DIM, HEAD_DIM), F32)],
        compiler_params=_cparams(("arbitrary", "arbitrary")),
        name="mixer_b_scan",
    )(*([prep] * 12))


def _rw_readout_kernel(r_ref, v_ref, k0_ref, k1_ref, g_ref, yf_ref, yb_ref, rk_ref, gw_ref, gb_ref, ones_ref,
                       o_ref):
    ones_bd = ones_ref[...]
    v = v_ref[...]
    bonus = _seg_sum(r_ref[...] * (k0_ref[...] + k1_ref[...]) * rk_ref[...], ones_bd)
    y = yf_ref[...] + yb_ref[...] + bonus * v
    mu = _seg_sum(y, ones_bd) * (1.0 / HEAD_DIM)
    yc = y - mu
    var = _seg_sum(yc * yc, ones_bd) * (1.0 / HEAD_DIM)
    yn = yc * lax.rsqrt(var + RW_GN_EPS)
    o_ref[...] = ((yn * gw_ref[...] + gb_ref[...]) * g_ref[...]).astype(o_ref.dtype)


def _rw_readout_call(prep, yf, yb, r_k, gn_w, gn_b, ones512, tm):
    R = prep.shape[0]
    col = lambda cb: pl.BlockSpec((tm, GROUP_W), lambda i: (i, cb))
    vec = pl.BlockSpec((1, GROUP_W), lambda i: (0, 0))
    return pl.pallas_call(
        _rw_readout_kernel,
        out_shape=jax.ShapeDtypeStruct((R, GROUP_W), BF16),
        grid=(R // tm,),
        in_specs=[col(PB_R), col(PB_V), col(5), col(8), col(PB_G), col(0), col(0), vec, vec, vec,
                  pl.BlockSpec((GROUP_W, GROUP_W), lambda i: (0, 0))],
        out_specs=col(0),
        compiler_params=_cparams(("arbitrary",)),
        name="mixer_b_readout",
    )(prep, prep, prep, prep, prep, yf, yb, r_k, gn_w, gn_b, ones512)


def _s5_tables(a_re, a_im, log_dt, b_re, b_im, c_re, c_im):
    Lc, G, N, P = S5_CHUNK, S5_GROUPS, S5_N, S5_P
    dt = jnp.exp(log_dt)[..., None]
    lam_re, lam_im = dt * a_re, dt * a_im
    tau = jnp.arange(Lc + 1, dtype=F32)[:, None, None, None]
    mag = jnp.exp(tau * lam_re)
    pw_re, pw_im = mag * jnp.cos(tau * lam_im), mag * jnp.sin(tau * lam_im)
    ab_re, ab_im = pw_re[1], pw_im[1]
    den = a_re * a_re + a_im * a_im
    nr = ab_re - 1.0
    cf_re, cf_im = (nr * a_re + ab_im * a_im) / den, (ab_im * a_re - nr * a_im) / den
    bp_re = cf_re[..., None] * b_re[None] - cf_im[..., None] * b_im[None]
    bp_im = cf_re[..., None] * b_im[None] + cf_im[..., None] * b_re[None]
    pb_re = pw_re[..., None] * bp_re[None] - pw_im[..., None] * bp_im[None]
    pb_im = pw_re[..., None] * bp_im[None] + pw_im[..., None] * bp_re[None]
    kk = (jnp.einsum('gqn,tdgnp->tdgqp', c_re, pb_re[:Lc]) - jnp.einsum('gqn,tdgnp->tdgqp', c_im, pb_im[:Lc]))
    jj = np.arange(Lc)[:, None]
    ii = np.arange(Lc)[None, :]
    dist = np.abs(ii - jj)
    kf = kk[dist, 0] * jnp.asarray(ii >= jj, F32)[..., None, None, None]
    kb = kk[dist, 1] * jnp.asarray(ii <= jj, F32)[..., None, None, None]
    tz = jnp.transpose(kf + kb, (2, 0, 4, 1, 3))
    jr = np.arange(Lc)
    emap = lambda pbx, order, d: jnp.transpose(pbx[order, d], (1, 0, 3, 2))
    em = jnp.stack([emap(pb_re, Lc - 1 - jr, 0), emap(pb_im, Lc - 1 - jr, 0),
                    emap(pb_re, jr, 1), emap(pb_im, jr, 1)], axis=3)

    def gmap(order, d):
        pr, pi = pw_re[order, d], pw_im[order, d]
        cp_re = c_re[None] * pr[:, :, None, :] - c_im[None] * pi[:, :, None, :]
        cp_im = c_re[None] * pi[:, :, None, :] + c_im[None] * pr[:, :, None, :]
        to = lambda t: jnp.transpose(t, (1, 3, 0, 2))
        return to(cp_re), to(-cp_im)

    gk = jnp.stack(gmap(jr + 1, 0) + gmap(Lc - jr, 1), axis=1)
    NO, NQ, NG = S5_OCTETS, 2, 4
    eye8, eye2, eye4 = (jnp.eye(k, dtype=F32) for k in (8, NQ, NG))
    wy = jnp.einsum('ogjpiq,gh->ojgpihq', tz.reshape(NO, 8, Lc, P, Lc, P), eye8)
    wy = wy.reshape(NO, Lc * LANE, Lc * LANE)
    we = jnp.einsum('orgjpkn,rs,gh->ojrgpskhn', em.reshape(NO, NQ, NG, Lc, P, 4, N), eye2, eye4)
    we = we.reshape(NO, Lc * LANE, NQ * 4 * NG * N)
    wz = jnp.concatenate([wy, we], axis=-1)
    gm = jnp.einsum('orhkniq,rs,ht->orkhnistq', gk.reshape(NO, NQ, NG, 4, N, Lc, P), eye2, eye4)
    gm = gm.reshape(NO, NQ * 4 * NG * N, Lc * LANE)
    apow = jnp.stack([pw_re[Lc, 0], pw_im[Lc, 0], pw_re[Lc, 1], pw_im[Lc, 1]], axis=1)
    apow = jnp.transpose(apow.reshape(G // NG, NG, 4, N), (0, 2, 1, 3)).reshape(G // NG, 4, NG * N)
    return wz, gm, apow


def _s5_local_kernel(u_ref, wz_ref, y_ref, e_ref):
    Lc = u_ref.shape[1]
    lhs = jnp.concatenate([u_ref[:, j, :] for j in range(Lc)], axis=-1).astype(BF16)
    z = _dot(lhs, wz_ref[...])
    for i in range(Lc):
        y_ref[:, i, :] = z[:, i * LANE:(i + 1) * LANE]
    e_ref[...] = z[:, Lc * LANE:]


def _s5_local_call(proj3, wz):
    NR, Lc, _ = proj3.shape
    NO, K, N = wz.shape
    NE = N - Lc * LANE
    tr = NR // 8
    cu = COL_DU // LANE
    return pl.pallas_call(
        _s5_local_kernel,
        out_shape=(jax.ShapeDtypeStruct((NR, Lc, GROUP_W), F32), jax.ShapeDtypeStruct((NR, NO * NE), F32)),
        grid=(NO, NR // tr),
        in_specs=[
            pl.BlockSpec((tr, Lc, LANE), lambda o, i: (i, 0, cu + o)),
            pl.BlockSpec((None, K, N), lambda o, i: (o, 0, 0)),
        ],
        out_specs=(pl.BlockSpec((tr, Lc, LANE), lambda o, i: (i, 0, o)),
                   pl.BlockSpec((tr, NE), lambda o, i: (i, o))),
        compiler_params=_cparams(("arbitrary", "arbitrary")),
        name="mixer_d_local",
    )(proj3, wz)


def _s5_scan_kernel(e_ref, ap_ref, x_ref, *, n_batch, nlc, ncc):
    NS = e_ref.shape[1] // 4
    af_re, af_im = ap_ref[0:1, :], ap_ref[1:2, :]
    ab_re, ab_im = ap_ref[2:3, :], ap_ref[3:4, :]

    def step(a_re, a_im, x_re, x_im, e):
        return a_re * x_re - a_im * x_im + e[:, :NS], a_re * x_im + a_im * x_re + e[:, NS:]

    def phase(row0, n_chunks, stride, carry):
        n_tiles = n_chunks // 8

        def body(t, carry):
            out = []
            for b in range(n_batch):
                xf_re, xf_im, xb_re, xb_im = carry[4 * b:4 * b + 4]
                rf = pl.ds(pl.multiple_of(row0 + b * stride + t * 8, 8), 8)
                rb = pl.ds(pl.multiple_of(row0 + b * stride + (n_tiles - 1 - t) * 8, 8), 8)
                ef = e_ref[rf, 0:2 * NS]
                eb = e_ref[rb, 2 * NS:4 * NS]
                xf_in, xb_in = [], [None] * 8
                for j in range(8):
                    xf_in.append(jnp.concatenate([xf_re, xf_im], axis=-1))
                    xf_re, xf_im = step(af_re, af_im, xf_re, xf_im, ef[j:j + 1, :])
                for j in reversed(range(8)):
                    xb_in[j] = jnp.concatenate([xb_re, xb_im], axis=-1)
                    xb_re, xb_im = step(ab_re, ab_im, xb_re, xb_im, eb[j:j + 1, :])
                x_ref[rf, 0:2 * NS] = jnp.concatenate(xf_in, axis=0)
                x_ref[rb, 2 * NS:4 * NS] = jnp.concatenate(xb_in, axis=0)
                out += [xf_re, xf_im, xb_re, xb_im]
            return tuple(out)

        return lax.fori_loop(0, n_tiles, body, carry)

    zero = jnp.zeros((1, NS), F32)
    carry = phase(n_batch * nlc, ncc, ncc, (zero,) * (4 * n_batch))
    phase(0, nlc, nlc, carry)


def _s5_scan_call(e, apow, n_batch, seq, ctx):
    NR, NEall = e.shape
    NQ8, _, NS = apow.shape
    return pl.pallas_call(
        functools.partial(_s5_scan_kernel, n_batch=n_batch, nlc=seq // S5_CHUNK, ncc=ctx // S5_CHUNK),
        out_shape=jax.ShapeDtypeStruct((NR, NEall), F32),
        grid=(NQ8,),
        in_specs=[
            pl.BlockSpec((NR, 4 * NS), lambda q: (0, q)),
            pl.BlockSpec((None, 4, NS), lambda q: (q, 0, 0)),
        ],
        out_specs=pl.BlockSpec((NR, 4 * NS), lambda q: (0, q)),
        compiler_params=_cparams(("arbitrary",)),
        name="mixer_d_scan",
    )(e, apow)


def _s5_carry_kernel(x_ref, gm_ref, yl_ref, y_ref):
    Lc = yl_ref.shape[1]
    y = _dot(x_ref[...].astype(BF16), gm_ref[...])
    for i in range(Lc):
        y_ref[:, i, :] = yl_ref[:, i, :] + y[:, i * LANE:(i + 1) * LANE]


def _s5_carry_call(xin, gm, yloc):
    NR, Lc, _ = yloc.shape
    NO, NE, N = gm.shape
    tr = NR // 8
    return pl.pallas_call(
        _s5_carry_kernel,
        out_shape=jax.ShapeDtypeStruct((NR, Lc, GROUP_W), F32),
        grid=(NO, NR // tr),
        in_specs=[
            pl.BlockSpec((tr, NE), lambda o, i: (i, o)),
            pl.BlockSpec((None, NE, N), lambda o, i: (o, 0, 0)),
            pl.BlockSpec((tr, Lc, LANE), lambda o, i: (i, 0, o)),
        ],
        out_specs=pl.BlockSpec((tr, Lc, LANE), lambda o, i: (i, 0, o)),
        compiler_params=_cparams(("arbitrary", "arbitrary")),
        name="mixer_d_carry",
    )(xin, gm, yloc)


def _s5_out_kernel(y_ref, u_ref, d_ref, w_ref, b_ref, o_ref):
    y = y_ref[...] + d_ref[...] * u_ref[...]
    c = math.sqrt(2.0 / math.pi)
    y = 0.5 * y * (1.0 + jnp.tanh(c * (y + 0.044715 * (y * y * y))))
    z = _dot(y.astype(BF16), w_ref[...]) + b_ref[...]
    o_ref[...] = (y * _sigmoid(z)).astype(o_ref.dtype)


def _s5_out_call(y_tok, proj, d_skip, glu_w, glu_b, tm):
    R = proj.shape[0]
    vec = pl.BlockSpec((1, GROUP_W), lambda i: (0, 0))
    return pl.pallas_call(
        _s5_out_kernel,
        out_shape=jax.ShapeDtypeStruct((R, GROUP_W), BF16),
        grid=(R // tm,),
        in_specs=[
            pl.BlockSpec((tm, GROUP_W), lambda i: (i, 0)),
            pl.BlockSpec((tm, GROUP_W), lambda i: (i, COL_DU // GROUP_W)),
            vec,
            pl.BlockSpec((GROUP_W, GROUP_W), lambda i: (0, 0)),
            vec,
        ],
        out_specs=pl.BlockSpec((tm, GROUP_W), lambda i: (i, 0)),
        compiler_params=_cparams(("arbitrary",)),
        name="mixer_d_out",
    )(y_tok, proj, d_skip, glu_w, glu_b)


def _rope_tables(n_tokens):
    t = jnp.arange(n_tokens)
    nf = HEAD_DIM // 4
    inv = 1.0 / (ROPE_BASE ** (jnp.arange(nf, dtype=F32) / nf))

    def ang(pp):
        a = pp.astype(F32)[:, None] * inv[None, :]
        return jnp.concatenate([a, a], -1)

    a = jnp.concatenate([ang(t // GRID_W), ang(t % GRID_W)], -1)
    cos, sin = jnp.cos(a), jnp.sin(a)
    sign = np.where((np.arange(HEAD_DIM) % 32) < 16, -1.0, 1.0).astype(np.float32)
    cos2 = jnp.concatenate([cos, cos], -1)
    sin2 = jnp.concatenate([sin * sign, sin * sign], -1)
    return cos2, sin2


def _block_ones(n):
    return jnp.asarray(np.kron(np.eye(n // HEAD_DIM), np.ones((HEAD_DIM, HEAD_DIM))), BF16)


def _permute_w_in(w_in):
    cuts = np.cumsum([512, 512, 512, 512, 512, 512, 32, 32, 96, 512, 128, 128, 512])
    seg = lambda i: w_in[..., (0 if i == 0 else cuts[i - 1]):cuts[i]]
    L, D = w_in.shape[:2]
    pad = jnp.zeros((L, D, 96), w_in.dtype)
    parts = [seg(0), seg(1), seg(2), seg(3), seg(4), seg(5), seg(9), seg(12),
             seg(6), seg(7), seg(8), pad, seg(10), seg(11)]
    return jnp.concatenate(parts, axis=-1).astype(BF16)


def _forward(x, c, ctx, c_ctx, w_ada, b_ada, norm_g, ffn1_wg, ffn1_wu, ffn1_wd, ffn2_wg, ffn2_wu, ffn2_wd,
             w_in, w_out, na_q_g, na_k_g, na_rpb, rw_conv, rw_w0, rw_w_up, rw_a0, rw_a_up, rw_g_up,
             rw_k_k, rw_k_a, rw_r_k, rw_gn_w, rw_gn_b, wa_q_g, wa_k_g, wa_sink, s5_a_re, s5_a_im,
             s5_log_dt, s5_b_re, s5_b_im, s5_c_re, s5_c_im, s5_d, s5_glu_w, s5_glu_b, *, tm, tf):
    B, SEQ, D = x.shape
    CTX = ctx.shape[1]
    L = w_ada.shape[0]
    n_lat = B * SEQ
    n_qb = SEQ // QBLK

    bf = lambda t: t.astype(BF16)
    f1g, f1u, f1d, f2g, f2u, f2d = map(bf, (ffn1_wg, ffn1_wu, ffn1_wd, ffn2_wg, ffn2_wu, ffn2_wd))
    w_in_p = _permute_w_in(w_in)
    w_out_b = bf(w_out)
    glu_w_b = bf(s5_glu_w)
    ones128, ones512 = _block_ones(LANE), _block_ones(GROUP_W)
    cos_t, sin_t = _rope_tables(SEQ)
    na_bias = [_na_bias_table(na_rpb[l], n_qb) for l in range(L)]
    zr = lambda r, cdim: jnp.zeros((L, r, cdim), F32)
    wlr = jnp.concatenate([
        jnp.concatenate([rw_w_up[:, 0], rw_w_up[:, 1], zr(32, 3 * GROUP_W)], axis=-1),
        jnp.concatenate([zr(32, 2 * GROUP_W), rw_a_up[:, 0], rw_a_up[:, 1], zr(32, GROUP_W)], axis=-1),
        jnp.concatenate([zr(96, 4 * GROUP_W), rw_g_up], axis=-1),
        zr(96, 5 * GROUP_W)], axis=1).astype(F32)
    s5_tabs = jax.vmap(_s5_tables)(s5_a_re, s5_a_im, s5_log_dt, s5_b_re, s5_b_im, s5_c_re, s5_c_im)
    tile2 = lambda t: jnp.concatenate([t, t], axis=-1)

    cc = jnp.concatenate([c, c_ctx[None], jnp.zeros((8 - B - 1, D), F32)], axis=0)
    mods_all = _ada_call(cc, w_ada, b_ada)[:, :B + 1].reshape(L, B + 1, N_MOD, D)

    xs = jnp.concatenate([x.reshape(n_lat, D), ctx.reshape(B * CTX, D)], axis=0)
    R = xs.shape[0]
    for l in range(L):
        want_ctx = l < L - 1
        mods = mods_all[l]
        ng = norm_g[l]
        xs = _ffn_call(xs, mods, ng[0:1], f1g, f1u, f1d, l, 0, R, SEQ, B, tm, tf)
        proj = _win_call(xs, mods, ng[1:2], w_in_p, l, SEQ, B, tm, D_IN_PAD // 3)
        o_a = _na_call(proj, na_bias[l], tile2(na_q_g[l][None]), tile2(na_k_g[l][None]), ones128,
                       B, SEQ, CTX, want_ctx)
        prep = _rw_prep_call(proj, rw_conv[l], wlr[l], rw_w0[l], rw_a0[l], rw_k_k[l][None], rw_k_a[l][None],
                             ones512, B, SEQ, CTX, min(tm, 256))
        vec = lambda t: t.reshape(1, GROUP_W)
        yf, yb = _rw_scan_call(prep, B, SEQ, CTX)
        o_b = _rw_readout_call(prep, yf, yb, vec(rw_r_k[l]), vec(rw_gn_w[l]), vec(rw_gn_b[l]), ones512, tm)
        o_c = _wa_call(proj, wa_sink[l], cos_t, sin_t, tile2(wa_q_g[l][None]), tile2(wa_k_g[l][None]), ones128,
                       B, SEQ, CTX, want_ctx)
        wz, gm, apow = (t[l] for t in s5_tabs)
        yloc, e_loc = _s5_local_call(proj.reshape(R // S5_CHUNK, S5_CHUNK, D_IN_PAD), bf(wz))
        xin = _s5_scan_call(e_loc, apow, B, SEQ, CTX)
        y_tok = _s5_carry_call(xin, bf(gm), yloc).reshape(R, GROUP_W)
        o_d = _s5_out_call(y_tok, proj, vec(s5_d[l]), glu_w_b[l], vec(s5_glu_b[l]), tm)
        n_rows = R if want_ctx else n_lat
        xs = _wout_call(xs, (o_a, o_b, o_c, o_d), mods, w_out_b, l, n_rows, SEQ, B, tm)
        xs = _ffn_call(xs, mods, ng[2:3], f2g, f2u, f2d, l, 6, n_rows, SEQ, B, tm, tf)
    return xs[:n_lat].reshape(B, SEQ, D)


def kernel(x, c, ctx, c_ctx, w_ada, b_ada, norm_g, ffn1_wg, ffn1_wu, ffn1_wd, ffn2_wg, ffn2_wu, ffn2_wd, w_in, w_out, na_q_g, na_k_g, na_rpb, rw_conv, rw_w0, rw_w_up, rw_a0, rw_a_up, rw_g_up, rw_k_k, rw_k_a, rw_r_k, rw_gn_w, rw_gn_b, wa_q_g, wa_k_g, wa_sink, s5_a_re, s5_a_im, s5_log_dt, s5_b_re, s5_b_im, s5_c_re, s5_c_im, s5_d, s5_glu_w, s5_glu_b):
    return _forward(x, c, ctx, c_ctx, w_ada, b_ada, norm_g, ffn1_wg, ffn1_wu, ffn1_wd, ffn2_wg, ffn2_wu, ffn2_wd,
                    w_in, w_out, na_q_g, na_k_g, na_rpb, rw_conv, rw_w0, rw_w_up, rw_a0, rw_a_up, rw_g_up,
                    rw_k_k, rw_k_a, rw_r_k, rw_gn_w, rw_gn_b, wa_q_g, wa_k_g, wa_sink, s5_a_re, s5_a_im,
                    s5_log_dt, s5_b_re, s5_b_im, s5_c_re, s5_c_im, s5_d, s5_glu_w, s5_glu_b, tm=512, tf=512)
```

```python
import functools
import math

import numpy as np
import jax
import jax.numpy as jnp
from jax import lax
from jax.experimental import pallas as pl
from jax.experimental.pallas import tpu as pltpu

F32 = jnp.float32
BF16 = jnp.bfloat16

D_MODEL = 2048
GRID_W = 64
HEAD_DIM = 64
GROUP_W = D_MODEL // 4
N_MOD = 9
NORM_EPS = 1e-6
ROPE_BASE = 10000.0
NEG_INF = -1e30

NA_HEADS = GROUP_W // HEAD_DIM
NA_WIN_R = 8
NA_WIN_C = 16
RW_HEADS = GROUP_W // HEAD_DIM
RW_DECAY_RANK = 32
RW_ICLR_RANK = 32
RW_GATE_RANK = 96
RW_GN_EPS = 64e-5
WA_HEADS = GROUP_W // HEAD_DIM
WA_KV_HEADS = 2
WA_GROUP = WA_HEADS // WA_KV_HEADS
WA_WINDOW = 128
S5_P = 16
S5_GROUPS = GROUP_W // S5_P
S5_N = 64

LANE = 128
QBLK = 2 * GRID_W
NA_KBLKS = 5
RW_CHUNK = 64
S5_CHUNK = 8
S5_OCTETS = 4
VMEM_LIMIT = 56 * 1024 * 1024

COL_AQ, COL_AK, COL_AV = 0, 512, 1024
COL_BR = 1536
COL_CQ = 3072
COL_DU = 3584
COL_BLR = 4096
COL_CK, COL_CV = 4352, 4480
D_IN_PAD = 4608

PB_R, PB_V, PB_KK, PB_G = 0, 1, 2, 9
PB_N = 10


def _cparams(sem):
    return pltpu.CompilerParams(dimension_semantics=sem, vmem_limit_bytes=VMEM_LIMIT)


def _dot(a, b):
    return jnp.dot(a, b, preferred_element_type=F32)


def _dot_nt(a, b):
    return lax.dot_general(a, b, (((1,), (1,)), ((), ())), preferred_element_type=F32)


def _dot_tn(a, b):
    return lax.dot_general(a, b, (((0,), (0,)), ((), ())), preferred_element_type=F32)


def _sigmoid(x):
    return 1.0 / (1.0 + jnp.exp(-x))


def _modulate(x, g, shift, scale):
    ms = jnp.mean(x * x, axis=-1, keepdims=True)
    return (x * lax.rsqrt(ms + NORM_EPS) * g) * (1.0 + scale) + shift


def _seg_sum(x, ones_bd):
    hi = x.astype(BF16)
    r1 = x - hi.astype(F32)
    mid = r1.astype(BF16)
    lo = (r1 - mid.astype(F32)).astype(BF16)
    return _dot(hi, ones_bd) + _dot(mid, ones_bd) + _dot(lo, ones_bd)


def _head_rmsnorm(x, ones_bd):
    ms = _seg_sum(x * x, ones_bd) * (1.0 / HEAD_DIM)
    return x * lax.rsqrt(ms + NORM_EPS)


def _ada_kernel(c_ref, w_ref, b_ref, o_ref):
    c = c_ref[...]
    s = (c * _sigmoid(c)).astype(BF16)
    o_ref[...] = _dot(s, w_ref[...].astype(BF16)) + b_ref[...]


def _ada_call(cc, w_ada, b_ada):
    L, D, N = w_ada.shape
    tn = 1024
    return pl.pallas_call(
        _ada_kernel,
        out_shape=jax.ShapeDtypeStruct((L, 8, N), F32),
        grid=(L, N // tn),
        in_specs=[
            pl.BlockSpec((8, D), lambda l, j: (0, 0)),
            pl.BlockSpec((None, D, tn), lambda l, j: (l, 0, j)),
            pl.BlockSpec((None, 1, tn), lambda l, j: (l, 0, j)),
        ],
        out_specs=pl.BlockSpec((None, 8, tn), lambda l, j: (l, 0, j)),
        compiler_params=_cparams(("arbitrary", "arbitrary")),
        name="adaln",
    )(cc, w_ada, b_ada.reshape(L, 1, N))


def _ffn_kernel(x_ref, mod_ref, g_ref, wg_ref, wu_ref, wd_ref, o_ref, h_sc, acc_sc, *, mi):
    j = pl.program_id(1)

    @pl.when(j == 0)
    def _():
        h = _modulate(x_ref[...], g_ref[...], mod_ref[mi:mi + 1, :], mod_ref[mi + 1:mi + 2, :])
        h_sc[...] = h.astype(BF16)
        acc_sc[...] = jnp.zeros_like(acc_sc)

    h = h_sc[...]
    gate = _dot(h, wg_ref[...])
    up = _dot(h, wu_ref[...])
    a = (gate * _sigmoid(gate) * up).astype(BF16)
    acc_sc[...] += _dot(a, wd_ref[...])

    @pl.when(j == pl.num_programs(1) - 1)
    def _():
        o_ref[...] = x_ref[...] + 0.5 * mod_ref[mi + 2:mi + 3, :] * acc_sc[...]


def _ffn_call(xs, mods, g, wg, wu, wd, l, mi, n_rows, rows_per_seq, n_batch, tm, tf):
    D = xs.shape[1]
    F = wg.shape[2]
    tps = rows_per_seq // tm
    return pl.pallas_call(
        functools.partial(_ffn_kernel, mi=mi),
        out_shape=jax.ShapeDtypeStruct((n_rows, D), F32),
        grid=(n_rows // tm, F // tf),
        in_specs=[
            pl.BlockSpec((tm, D), lambda i, j: (i, 0)),
            pl.BlockSpec((None, N_MOD, D), lambda i, j: (jnp.minimum(i // tps, n_batch), 0, 0)),
            pl.BlockSpec((1, D), lambda i, j: (0, 0)),
            pl.BlockSpec((None, D, tf), lambda i, j: (l, 0, j)),
            pl.BlockSpec((None, D, tf), lambda i, j: (l, 0, j)),
            pl.BlockSpec((None, tf, D), lambda i, j: (l, j, 0)),
        ],
        out_specs=pl.BlockSpec((tm, D), lambda i, j: (i, 0)),
        scratch_shapes=[pltpu.VMEM((tm, D), BF16), pltpu.VMEM((tm, D), F32)],
        compiler_params=_cparams(("arbitrary", "arbitrary")),
        name="ffn",
    )(xs, mods, g, wg, wu, wd)


def _win_kernel(x_ref, mod_ref, g_ref, w_ref, o_ref, h_sc):
    @pl.when(pl.program_id(1) == 0)
    def _():
        h = _modulate(x_ref[...], g_ref[...], mod_ref[3:4, :], mod_ref[4:5, :])
        h_sc[...] = h.astype(BF16)

    o_ref[...] = _dot(h_sc[...], w_ref[...])


def _win_call(xs, mods, g, w_in, l, rows_per_seq, n_batch, tm, tn):
    R, D = xs.shape
    N = w_in.shape[2]
    tps = rows_per_seq // tm
    return pl.pallas_call(
        _win_kernel,
        out_shape=jax.ShapeDtypeStruct((R, N), F32),
        grid=(R // tm, N // tn),
        in_specs=[
            pl.BlockSpec((tm, D), lambda i, j: (i, 0)),
            pl.BlockSpec((None, N_MOD, D), lambda i, j: (jnp.minimum(i // tps, n_batch), 0, 0)),
            pl.BlockSpec((1, D), lambda i, j: (0, 0)),
            pl.BlockSpec((None, D, tn), lambda i, j: (l, 0, j)),
        ],
        out_specs=pl.BlockSpec((tm, tn), lambda i, j: (i, j)),
        scratch_shapes=[pltpu.VMEM((tm, D), BF16)],
        compiler_params=_cparams(("arbitrary", "arbitrary")),
        name="in_proj",
    )(xs, mods, g, w_in)


def _wout_kernel(x_ref, oa_ref, ob_ref, oc_ref, od_ref, mod_ref, w_ref, o_ref):
    acc = _dot(oa_ref[...], w_ref[0 * GROUP_W:1 * GROUP_W, :])
    acc += _dot(ob_ref[...], w_ref[1 * GROUP_W:2 * GROUP_W, :])
    acc += _dot(oc_ref[...], w_ref[2 * GROUP_W:3 * GROUP_W, :])
    acc += _dot(od_ref[...], w_ref[3 * GROUP_W:4 * GROUP_W, :])
    o_ref[...] = x_ref[...] + mod_ref[5:6, :] * acc


def _wout_call(xs, outs, mods, w_out, l, n_rows, rows_per_seq, n_batch, tm):
    D = xs.shape[1]
    tps = rows_per_seq // tm
    ospec = pl.BlockSpec((tm, GROUP_W), lambda i: (i, 0))
    return pl.pallas_call(
        _wout_kernel,
        out_shape=jax.ShapeDtypeStruct((n_rows, D), F32),
        grid=(n_rows // tm,),
        in_specs=[
            pl.BlockSpec((tm, D), lambda i: (i, 0)),
            ospec, ospec, ospec, ospec,
            pl.BlockSpec((None, N_MOD, D), lambda i: (jnp.minimum(i // tps, n_batch), 0, 0)),
            pl.BlockSpec((None, D, D), lambda i: (l, 0, 0)),
        ],
        out_specs=pl.BlockSpec((tm, D), lambda i: (i, 0)),
        compiler_params=_cparams(("arbitrary",)),
        name="out_proj",
    )(xs, *outs, mods, w_out)


def _na_case_reps(n_qb):
    return (0, 1, 2, n_qb - 2, n_qb - 1)


def _na_start(p, n_qb):
    return jnp.clip(p - 2, 0, n_qb - NA_KBLKS)


def _na_bias_table(rpb, n_qb):
    H = rpb.shape[0]
    rows = 2 * n_qb
    qc = np.arange(GRID_W)[:, None]
    kc = np.arange(GRID_W)[None, :]
    cs = np.clip(qc - NA_WIN_C // 2, 0, GRID_W - NA_WIN_C)
    okc = (kc >= cs) & (kc < cs + NA_WIN_C)
    dc = np.where(okc, kc - qc + NA_WIN_C - 1, 0)
    rp = rpb.astype(F32).reshape(H, 2 * NA_WIN_R - 1, 2 * NA_WIN_C - 1)
    blocks = jnp.where(jnp.asarray(okc)[None, None], jnp.take(rp, jnp.asarray(dc), axis=2), NEG_INF)
    neg = jnp.full((H, GRID_W, GRID_W), NEG_INF, F32)
    cases = []
    for p in _na_case_reps(n_qb):
        start = min(max(p - 2, 0), n_qb - NA_KBLKS)
        qrows = []
        for qr in range(2):
            qa = 2 * p + qr
            rs = min(max(qa - NA_WIN_R // 2, 0), rows - NA_WIN_R)
            krow = []
            for kr in range(2 * NA_KBLKS):
                ka = 2 * start + kr
                krow.append(blocks[:, ka - qa + NA_WIN_R - 1] if rs <= ka < rs + NA_WIN_R else neg)
            qrows.append(jnp.concatenate(krow, axis=-1))
        cases.append(jnp.concatenate(qrows, axis=-2))
    return jnp.stack(cases, axis=0)


def _na_kernel(q_ref, k_ref, v_ref, kc_ref, vc_ref, bias_ref, qg_ref, kg_ref, ones_ref, o_ref,
               kn_sc, vn_sc, kcn_sc, vcn_sc, *, n_qb, seq, ctx):
    p = pl.program_id(2)
    ones_bd = ones_ref[...]
    scale = HEAD_DIM ** -0.5

    @pl.when(p == 0)
    def _():
        kg = kg_ref[...]
        rows = 512

        def body(i, carry):
            sl = pl.ds(pl.multiple_of(i * rows, rows), rows)
            kn_sc[sl, :] = (_head_rmsnorm(k_ref[sl, :], ones_bd) * kg).astype(BF16)
            vn_sc[sl, :] = v_ref[sl, :].astype(BF16)
            return carry

        lax.fori_loop(0, seq // rows, body, 0)
        kcn_sc[...] = (_head_rmsnorm(kc_ref[...], ones_bd) * kg).astype(BF16)
        vcn_sc[...] = vc_ref[...].astype(BF16)

    nq = q_ref.shape[0] // QBLK
    n_lat = n_qb // nq
    q = (_head_rmsnorm(q_ref[...], ones_bd) * (qg_ref[...] * scale)).astype(BF16)
    kcn = kcn_sc[...]
    vcn = vcn_sc[...]
    chains = [(qb, h) for qb in range(nq) for h in range(2)]
    hsl = lambda h: slice(h * HEAD_DIM, (h + 1) * HEAD_DIM)
    qh = [q[qb * QBLK:(qb + 1) * QBLK, hsl(h)] for qb, h in chains]
    s_c = [_dot_nt(qh[c], kcn[:, hsl(h)]) for c, (qb, h) in enumerate(chains)]
    m_c = [jnp.max(t, axis=-1, keepdims=True) for t in s_c]

    def finish(o, den):
        rows = [jnp.concatenate([o[qb * 2 + h] / den[qb * 2 + h] for h in range(2)], axis=-1) for qb in range(nq)]
        o_ref[...] = jnp.concatenate(rows, axis=0).astype(o_ref.dtype)

    @pl.when(p < n_lat)
    def _():
        starts = [_na_start(p * nq + qb, n_qb) for qb in range(nq)]
        cases = [p * nq + qb - starts[qb] for qb in range(nq)]
        sls = [pl.ds(pl.multiple_of(st * QBLK, QBLK), NA_KBLKS * QBLK) for st in starts]
        kw = [kn_sc[sl, :] for sl in sls]
        vw = [vn_sc[sl, :] for sl in sls]
        s_n = [_dot_nt(qh[c], kw[qb][:, hsl(h)]) + bias_ref[cases[qb], h] for c, (qb, h) in enumerate(chains)]
        m = [jnp.maximum(jnp.max(s_n[c], axis=-1, keepdims=True), m_c[c]) for c in range(len(chains))]
        p_n = [jnp.exp(s_n[c] - m[c]) for c in range(len(chains))]
        p_c = [jnp.exp(s_c[c] - m[c]) for c in range(len(chains))]
        den = [jnp.sum(p_n[c], axis=-1, keepdims=True) + jnp.sum(p_c[c], axis=-1, keepdims=True)
               for c in range(len(chains))]
        o = [_dot(p_n[c].astype(BF16), vw[qb][:, hsl(h)]) + _dot(p_c[c].astype(BF16), vcn[:, hsl(h)])
             for c, (qb, h) in enumerate(chains)]
        finish(o, den)

    @pl.when(p >= n_lat)
    def _():
        p_c = [jnp.exp(s_c[c] - m_c[c]) for c in range(len(chains))]
        den = [jnp.sum(t, axis=-1, keepdims=True) for t in p_c]
        o = [_dot(p_c[c].astype(BF16), vcn[:, hsl(h)]) for c, (qb, h) in enumerate(chains)]
        finish(o, den)


def _na_call(proj, bias, qg, kg, ones_bd, n_batch, seq, ctx, want_ctx):
    R = proj.shape[0]
    nq = 2
    n_qb = seq // QBLK
    n_lat, n_ctx = n_qb // nq, ctx // (nq * QBLK)
    steps = n_lat + (n_ctx if want_ctx else 0)
    ctx_kb = n_batch * seq // ctx

    def qrow(b, p):
        return jnp.where(p < n_lat, b * n_lat + p, n_batch * n_lat + b * n_ctx + (p - n_lat))

    cq, ck, cv = COL_AQ // LANE, COL_AK // LANE, COL_AV // LANE
    return pl.pallas_call(
        functools.partial(_na_kernel, n_qb=n_qb, seq=seq, ctx=ctx),
        out_shape=jax.ShapeDtypeStruct((R, GROUP_W), BF16),
        grid=(n_batch, NA_HEADS // 2, steps),
        in_specs=[
            pl.BlockSpec((nq * QBLK, LANE), lambda b, hp, p: (qrow(b, p), cq + hp)),
            pl.BlockSpec((seq, LANE), lambda b, hp, p: (b, ck + hp)),
            pl.BlockSpec((seq, LANE), lambda b, hp, p: (b, cv + hp)),
            pl.BlockSpec((ctx, LANE), lambda b, hp, p: (ctx_kb + b, ck + hp)),
            pl.BlockSpec((ctx, LANE), lambda b, hp, p: (ctx_kb + b, cv + hp)),
            pl.BlockSpec((5, 2, QBLK, NA_KBLKS * QBLK), lambda b, hp, p: (0, hp, 0, 0)),
            pl.BlockSpec((1, LANE), lambda b, hp, p: (0, 0)),
            pl.BlockSpec((1, LANE), lambda b, hp, p: (0, 0)),
            pl.BlockSpec((LANE, LANE), lambda b, hp, p: (0, 0)),
        ],
        out_specs=pl.BlockSpec((nq * QBLK, LANE), lambda b, hp, p: (qrow(b, p), hp)),
        scratch_shapes=[pltpu.VMEM((seq, LANE), BF16), pltpu.VMEM((seq, LANE), BF16),
                        pltpu.VMEM((ctx, LANE), BF16), pltpu.VMEM((ctx, LANE), BF16)],
        compiler_params=_cparams(("arbitrary", "arbitrary", "arbitrary")),
        name="mixer_a",
    )(proj, proj, proj, proj, proj, bias, qg, kg, ones_bd)


def _rope(x, cos, sin_signed, first_half):
    rot = jnp.where(first_half, pltpu.roll(x, LANE - 16, 1), pltpu.roll(x, 16, 1))
    return x * cos + rot * sin_signed


def _wa_kernel(sink_ref, q_ref, k_ref, v_ref, kc_ref, vc_ref, cosk_ref, sin_k_ref, cosq_ref, sinq_ref,
               qg_ref, kg_ref, ones_ref, o_ref, kn_sc, vn_sc, kcn_sc, vcn_sc, *, n_qb, seq, ctx):
    p = pl.program_id(1)
    ones_bd = ones_ref[...]
    scale = HEAD_DIM ** -0.5
    lane = lax.broadcasted_iota(jnp.int32, (1, LANE), 1)
    first_half = (lane % 32) < 16

    @pl.when(p == 0)
    def _():
        kg = kg_ref[...]
        rows = 512

        def body(i, carry):
            sl = pl.ds(pl.multiple_of(i * rows, rows), rows)
            kn = _head_rmsnorm(k_ref[sl, :], ones_bd) * kg
            kn_sc[sl, :] = _rope(kn, cosk_ref[sl, :], sin_k_ref[sl, :], first_half).astype(BF16)
            vn_sc[sl, :] = v_ref[sl, :].astype(BF16)
            return carry

        lax.fori_loop(0, seq // rows, body, 0)
        kcn_sc[...] = (_head_rmsnorm(kc_ref[...], ones_bd) * kg).astype(BF16)
        vcn_sc[...] = vc_ref[...].astype(BF16)

    nq = q_ref.shape[0] // QBLK
    n_lat = n_qb // nq
    qg = qg_ref[...] * scale
    is_lat = p < n_lat
    cosq = jnp.where(is_lat, cosq_ref[...], 1.0)
    sinq = jnp.where(is_lat, sinq_ref[...], 0.0)
    qh = []
    for c in range(4):
        qn = _head_rmsnorm(q_ref[:, c * LANE:(c + 1) * LANE], ones_bd) * qg
        qn = _rope(qn, cosq, sinq, first_half).astype(BF16)
        qh += [qn[:, :HEAD_DIM], qn[:, HEAD_DIM:]]
    kcn = kcn_sc[...]
    vcn = vcn_sc[...]
    chains = [(qb, kh) for qb in range(nq) for kh in range(WA_KV_HEADS)]
    nch = range(len(chains))
    hsl = lambda kh: slice(kh * HEAD_DIM, (kh + 1) * HEAD_DIM)
    qs = [jnp.concatenate([qh[kh * WA_GROUP + g][qb * QBLK:(qb + 1) * QBLK, :] for g in range(WA_GROUP)], axis=0)
          for qb, kh in chains]
    sk = [jnp.concatenate([jnp.full((QBLK, 1), sink_ref[kh * WA_GROUP + g], F32) for g in range(WA_GROUP)],
                          axis=0) for qb, kh in chains]
    s_c = [_dot_nt(qs[c], kcn[:, hsl(chains[c][1])]) for c in nch]
    m_c = [jnp.maximum(jnp.max(s_c[c], axis=-1, keepdims=True), sk[c]) for c in nch]

    def finish(o, den):
        rows = []
        for qb in range(nq):
            heads = []
            for kh in range(WA_KV_HEADS):
                c = qb * WA_KV_HEADS + kh
                oc = o[c] / den[c]
                heads += [oc[g * QBLK:(g + 1) * QBLK, :] for g in range(WA_GROUP)]
            rows.append(jnp.concatenate(heads, axis=-1))
        o_ref[...] = jnp.concatenate(rows, axis=0).astype(o_ref.dtype)

    @pl.when(is_lat)
    def _():
        shp = (WA_GROUP * QBLK, 3 * QBLK)
        qoff = lax.broadcasted_iota(jnp.int32, shp, 0) & (QBLK - 1)
        koff = lax.broadcasted_iota(jnp.int32, shp, 1)
        blk = [p * nq + qb for qb in range(nq)]
        ws = [jnp.clip((bq - 1) * QBLK, 0, seq - 3 * QBLK) for bq in blk]
        sls = [pl.ds(pl.multiple_of(w, QBLK), 3 * QBLK) for w in ws]
        kw = [kn_sc[sl, :] for sl in sls]
        vw = [vn_sc[sl, :] for sl in sls]
        ok = [jnp.abs((ws[qb] + koff) - (blk[qb] * QBLK + qoff)) <= WA_WINDOW for qb in range(nq)]
        s_w = [jnp.where(ok[chains[c][0]], _dot_nt(qs[c], kw[chains[c][0]][:, hsl(chains[c][1])]), NEG_INF)
               for c in nch]
        m = [jnp.maximum(jnp.max(s_w[c], axis=-1, keepdims=True), m_c[c]) for c in nch]
        p_w = [jnp.exp(s_w[c] - m[c]) for c in nch]
        p_c = [jnp.exp(s_c[c] - m[c]) for c in nch]
        den = [jnp.sum(p_w[c], axis=-1, keepdims=True) + jnp.sum(p_c[c], axis=-1, keepdims=True)
               + jnp.exp(sk[c] - m[c]) for c in nch]
        o = [_dot(p_w[c].astype(BF16), vw[chains[c][0]][:, hsl(chains[c][1])])
             + _dot(p_c[c].astype(BF16), vcn[:, hsl(chains[c][1])]) for c in nch]
        finish(o, den)

    @pl.when(jnp.logical_not(is_lat))
    def _():
        p_c = [jnp.exp(s_c[c] - m_c[c]) for c in nch]
        den = [jnp.sum(p_c[c], axis=-1, keepdims=True) + jnp.exp(sk[c] - m_c[c]) for c in nch]
        o = [_dot(p_c[c].astype(BF16), vcn[:, hsl(chains[c][1])]) for c in nch]
        finish(o, den)


def _wa_call(proj, sink, cos_t, sin_t, qg, kg, ones_bd, n_batch, seq, ctx, want_ctx):
    R = proj.shape[0]
    nq = 2
    n_qb = seq // QBLK
    n_lat, n_ctx = n_qb // nq, ctx // (nq * QBLK)
    steps = n_lat + (n_ctx if want_ctx else 0)
    ctx_kb = n_batch * seq // ctx

    def qrow(b, p):
        return jnp.where(p < n_lat, b * n_lat + p, n_batch * n_lat + b * n_ctx + (p - n_lat))

    ck, cv = COL_CK // LANE, COL_CV // LANE
    return pl.pallas_call(
        functools.partial(_wa_kernel, n_qb=n_qb, seq=seq, ctx=ctx),
        out_shape=jax.ShapeDtypeStruct((R, GROUP_W), BF16),
        grid=(n_batch, steps),
        in_specs=[
            pl.BlockSpec(memory_space=pltpu.SMEM),
            pl.BlockSpec((nq * QBLK, GROUP_W), lambda b, p: (qrow(b, p), COL_CQ // GROUP_W)),
            pl.BlockSpec((seq, LANE), lambda b, p: (b, ck)),
            pl.BlockSpec((seq, LANE), lambda b, p: (b, cv)),
            pl.BlockSpec((ctx, LANE), lambda b, p: (ctx_kb + b, ck)),
            pl.BlockSpec((ctx, LANE), lambda b, p: (ctx_kb + b, cv)),
            pl.BlockSpec((seq, LANE), lambda b, p: (0, 0)),
            pl.BlockSpec((seq, LANE), lambda b, p: (0, 0)),
            pl.BlockSpec((nq * QBLK, LANE), lambda b, p: (jnp.minimum(p, n_lat - 1), 0)),
            pl.BlockSpec((nq * QBLK, LANE), lambda b, p: (jnp.minimum(p, n_lat - 1), 0)),
            pl.BlockSpec((1, LANE), lambda b, p: (0, 0)),
            pl.BlockSpec((1, LANE), lambda b, p: (0, 0)),
            pl.BlockSpec((LANE, LANE), lambda b, p: (0, 0)),
        ],
        out_specs=pl.BlockSpec((nq * QBLK, GROUP_W), lambda b, p: (qrow(b, p), 0)),
        scratch_shapes=[pltpu.VMEM((seq, LANE), BF16), pltpu.VMEM((seq, LANE), BF16),
                        pltpu.VMEM((ctx, LANE), BF16), pltpu.VMEM((ctx, LANE), BF16)],
        compiler_params=_cparams(("arbitrary", "arbitrary")),
        name="mixer_c",
    )(sink, proj, proj, proj, proj, proj, cos_t, sin_t, cos_t, sin_t, qg, kg, ones_bd)


def _rw_prep_kernel(x_ref, prev_ref, next_ref, lr_ref, conv_ref, wlr_ref, w0_ref, a0_ref, kk_ref, ka_ref,
                    ones_ref, o_ref, *, tiles_lat, tiles_ctx, n_lat_tiles):
    i = pl.program_id(0)
    tm = x_ref.shape[0]
    in_lat = i < n_lat_tiles
    first = jnp.where(in_lat, i % tiles_lat == 0, (i - n_lat_tiles) % tiles_ctx == 0)
    last = jnp.where(in_lat, i % tiles_lat == tiles_lat - 1, (i - n_lat_tiles) % tiles_ctx == tiles_ctx - 1)
    x = x_ref[...]
    prev_row = jnp.where(first, 0.0, prev_ref[7:8, :])
    next_row = jnp.where(last, 0.0, next_ref[0:1, :])
    row = lax.broadcasted_iota(jnp.int32, (tm, 1), 0)
    x_prev = jnp.where(row == 0, prev_row, pltpu.roll(x, 1, 0))
    x_next = jnp.where(row == tm - 1, next_row, pltpu.roll(x, tm - 1, 0))
    cw = conv_ref[...]
    y = x_prev * cw[0:1, :] + x * cw[1:2, :] + x_next * cw[2:3, :]
    r = y[:, 0:GROUP_W]
    k = y[:, GROUP_W:2 * GROUP_W]
    v = y[:, 2 * GROUP_W:3 * GROUP_W]

    lr = lr_ref[...]
    lane = lax.broadcasted_iota(jnp.int32, (1, lr.shape[1]), 1)
    c1 = RW_DECAY_RANK
    c2 = c1 + RW_ICLR_RANK
    c3 = c2 + RW_GATE_RANK
    act = jnp.where(lane < c1, jnp.tanh(lr),
                    jnp.where(lane < c2, lr, jnp.where(lane < c3, _sigmoid(lr), 0.0)))
    up = jnp.dot(act, wlr_ref[...], preferred_element_type=F32, precision=lax.Precision.HIGHEST)

    ones_bd = ones_ref[...]
    kk = k * kk_ref[...]
    nrm = jnp.sqrt(_seg_sum(kk * kk, ones_bd))
    kk = kk / jnp.maximum(nrm, 1e-12)

    o_ref[:, PB_R * GROUP_W:(PB_R + 1) * GROUP_W] = r
    o_ref[:, PB_V * GROUP_W:(PB_V + 1) * GROUP_W] = v
    o_ref[:, PB_KK * GROUP_W:(PB_KK + 1) * GROUP_W] = kk
    o_ref[:, PB_G * GROUP_W:(PB_G + 1) * GROUP_W] = up[:, 4 * GROUP_W:5 * GROUP_W]
    ka = ka_ref[...]
    for d in range(2):
        z = w0_ref[d:d + 1, :] + up[:, d * GROUP_W:(d + 1) * GROUP_W]
        sp = jnp.maximum(-z, 0.0) + jnp.log(1.0 + jnp.exp(-jnp.abs(z)))
        lw = -jnp.exp(-sp - 0.5)
        a = _sigmoid(a0_ref[d:d + 1, :] + up[:, (2 + d) * GROUP_W:(3 + d) * GROUP_W])
        kd = k * (1.0 + (a - 1.0) * ka)
        base = 3 + 3 * d
        o_ref[:, base * GROUP_W:(base + 1) * GROUP_W] = lw
        o_ref[:, (base + 1) * GROUP_W:(base + 2) * GROUP_W] = a
        o_ref[:, (base + 2) * GROUP_W:(base + 3) * GROUP_W] = kd


def _rw_prep_call(proj, conv_w, wlr, w0, a0, k_k, k_a, ones512, n_batch, seq, ctx, tm):
    R = proj.shape[0]
    n_lat_tiles = n_batch * seq // tm
    nb8 = R // 8
    t8 = tm // 8
    wide = 3 * GROUP_W
    cb = COL_BR // wide
    return pl.pallas_call(
        functools.partial(_rw_prep_kernel, tiles_lat=seq // tm, tiles_ctx=ctx // tm, n_lat_tiles=n_lat_tiles),
        out_shape=jax.ShapeDtypeStruct((R, PB_N * GROUP_W), F32),
        grid=(R // tm,),
        in_specs=[
            pl.BlockSpec((tm, wide), lambda i: (i, cb)),
            pl.BlockSpec((8, wide), lambda i: (jnp.maximum(i * t8 - 1, 0), cb)),
            pl.BlockSpec((8, wide), lambda i: (jnp.minimum((i + 1) * t8, nb8 - 1), cb)),
            pl.BlockSpec((tm, 256), lambda i: (i, COL_BLR // 256)),
            pl.BlockSpec((3, wide), lambda i: (0, 0)),
            pl.BlockSpec((256, 5 * GROUP_W), lambda i: (0, 0)),
            pl.BlockSpec((2, GROUP_W), lambda i: (0, 0)),
            pl.BlockSpec((2, GROUP_W), lambda i: (0, 0)),
            pl.BlockSpec((1, GROUP_W), lambda i: (0, 0)),
            pl.BlockSpec((1, GROUP_W), lambda i: (0, 0)),
            pl.BlockSpec((GROUP_W, GROUP_W), lambda i: (0, 0)),
        ],
        out_specs=pl.BlockSpec((tm, PB_N * GROUP_W), lambda i: (i, 0)),
        compiler_params=_cparams(("arbitrary",)),
        name="mixer_b_prep",
    )(proj, proj, proj, proj, conv_w, wlr, w0, a0, k_k, k_a, ones512)


RW_SPLIT = 1
_NN = ((1,), (0,))
_NT = ((1,), (1,))
_TN = ((0,), (0,))


def _split(x, n):
    parts = []
    for _ in range(n):
        p = x.astype(BF16)
        parts.append(p)
        x = x - p.astype(F32)
    return parts


def _sdot(a, b, dims):
    n = max(len(a), len(b))
    acc = None
    for i, ai in enumerate(a):
        for j, bj in enumerate(b):
            if i + j < n:
                t = lax.dot_general(ai, bj, (dims, ((), ())), preferred_element_type=F32)
                acc = t if acc is None else acc + t
    return acc


def _rw_chunks(at, bt, kt, rt, v, gam, s0, strict, incl):
    n = len(at)
    ids = range(n)
    C = at[0].shape[0]
    sp = lambda t: _split(t, RW_SPLIT)
    b1 = lambda t: [t.astype(BF16)]
    ar = [b1(jnp.concatenate([at[i], rt[i]], axis=0)) for i in ids]
    bk = [sp(jnp.concatenate([bt[i], kt[i]], axis=0)) for i in ids]
    g4 = [_sdot(ar[i], bk[i][:1], _NT) for i in ids]
    a_ab = [jnp.where(strict[i], g4[i][:C, :C], 0.0) for i in ids]
    a_ak = [jnp.where(strict[i], g4[i][:C, C:], 0.0) for i in ids]
    a_rb = [jnp.where(incl[i], g4[i][C:, :C], 0.0) for i in ids]
    a_rk = [jnp.where(incl[i], g4[i][C:, C:], 0.0) for i in ids]
    wv = [_sdot(b1(a_ak[i]), b1(v[i]), _NN) for i in ids]
    x = [jnp.concatenate([at[i], wv[i]], axis=1) for i in ids]

    row = lax.broadcasted_iota(jnp.int32, (C, C), 0)
    col = lax.broadcasted_iota(jnp.int32, (C, C), 1)
    eye = (row == col).astype(F32)
    blk = 16
    same = (row // blk) == (col // blk)
    p = [jnp.where(same, a_ab[i], 0.0) for i in ids]
    m = [eye + p[i] for i in ids]
    for _ in range(int(math.log2(blk)) - 1):
        pb_ = [b1(p[i]) for i in ids]
        p = [_sdot(pb_[i], pb_[i], _NN) for i in ids]
        m = [m[i] + _sdot(b1(m[i]), b1(p[i]), _NN) for i in ids]
    while blk < C:
        wider = (row // (2 * blk)) == (col // (2 * blk))
        join = wider & jnp.logical_not(same)
        mb = [b1(m[i]) for i in ids]
        t = [_sdot(b1(jnp.where(join, a_ab[i], 0.0)), mb[i], _NN) for i in ids]
        m = [m[i] + _sdot(mb[i], b1(t[i]), _NN) for i in ids]
        same, blk = wider, 2 * blk
    x = [_sdot(b1(m[i]), b1(x[i]), _NN) for i in ids]

    qv = [sp(jnp.concatenate([x[i][:, HEAD_DIM:], v[i]], axis=0)) for i in ids]
    pb = [sp(x[i][:, :HEAD_DIM]) for i in ids]
    s0b = [sp(s0[i]) for i in ids]
    y_loc = [_sdot(b1(jnp.concatenate([a_rb[i], a_rk[i]], axis=1)), qv[i][:1], _NN) for i in ids]
    rp = [rt[i] + _sdot(b1(a_rb[i]), pb[i][:1], _NN) for i in ids]
    w = [_sdot(s0b[i], pb[i], _NT) for i in ids]
    y = [y_loc[i] + _sdot(sp(rp[i]), s0b[i], _NT) for i in ids]
    s1 = [(s0[i] + _sdot(sp(w[i]), sp(bt[i]), _NN) + _sdot(qv[i], bk[i], _TN)) * gam[i] for i in ids]
    return y, s1


def _rw_scan_kernel(rf_ref, vf_ref, kkf_ref, lwf_ref, af_ref, kf_ref,
                    rb_ref, vb_ref, kkb_ref, lwb_ref, ab_ref, kb_ref, yf_ref, yb_ref, s_sc):
    s = pl.program_id(1)

    @pl.when(s == 0)
    def _():
        s_sc[...] = jnp.zeros_like(s_sc)

    C = rf_ref.shape[0]
    row = lax.broadcasted_iota(jnp.int32, (C, C), 0)
    col = lax.broadcasted_iota(jnp.int32, (C, C), 1)
    chains = dict(at=[], bt=[], kt=[], rt=[], v=[], gam=[], s0=[], strict=[], incl=[])
    dirs = ((False, rf_ref, vf_ref, kkf_ref, lwf_ref, af_ref, kf_ref),
            (True, rb_ref, vb_ref, kkb_ref, lwb_ref, ab_ref, kb_ref))
    for d, (reverse, r_ref, v_ref, kk_ref, lw_ref, a_ref, k_ref) in enumerate(dirs):
        incl, strict = (row <= col, row < col) if reverse else (row >= col, row > col)
        lw = lw_ref[...]
        kk = kk_ref[...]
        v = v_ref[...]
        cum = jnp.dot(incl.astype(F32), lw, preferred_element_type=F32, precision=lax.Precision.HIGHEST)
        e_l = jnp.exp(cum)
        e_n = jnp.exp(-cum)
        at = -(kk * jnp.exp(cum - lw))
        bt = kk * a_ref[...] * e_n
        kt = k_ref[...] * e_n
        rt = r_ref[...] * e_l
        gam = e_l[0:1, :] if reverse else e_l[C - 1:C, :]
        for h in range(RW_HEADS):
            hs = slice(h * HEAD_DIM, (h + 1) * HEAD_DIM)
            for name, val in (("at", at), ("bt", bt), ("kt", kt), ("rt", rt), ("v", v), ("gam", gam)):
                chains[name].append(val[:, hs])
            chains["s0"].append(s_sc[d, h])
            chains["strict"].append(strict)
            chains["incl"].append(incl)
    y, s1 = _rw_chunks(**chains)
    for d in range(2):
        for h in range(RW_HEADS):
            s_sc[d, h] = s1[d * RW_HEADS + h]
    yf_ref[...] = jnp.concatenate(y[:RW_HEADS], axis=-1)
    yb_ref[...] = jnp.concatenate(y[RW_HEADS:], axis=-1)


def _rw_scan_call(prep, n_batch, seq, ctx):
    R = prep.shape[0]
    C = RW_CHUNK
    nc, nl = ctx // C, seq // C
    ctx_base = n_batch * nl

    def blk_f(b, s):
        return jnp.where(s < nc, ctx_base + b * nc + s, b * nl + (s - nc))

    def blk_b(b, s):
        return jnp.where(s < nc, ctx_base + b * nc + (nc - 1 - s), b * nl + (nl - 1 - (s - nc)))

    def col(blk, cb):
        return pl.BlockSpec((C, GROUP_W), lambda b, s: (blk(b, s), cb))

    in_specs = ([col(blk_f, c) for c in (PB_R, PB_V, PB_KK, 3, 4, 5)]
                + [col(blk_b, c) for c in (PB_R, PB_V, PB_KK, 6, 7, 8)])
    out = jax.ShapeDtypeStruct((R, GROUP_W), F32)
    return pl.pallas_call(
        _rw_scan_kernel,
        out_shape=(out, out),
        grid=(n_batch, nc + nl),
        in_specs=in_specs,
        out_specs=(col(blk_f, 0), col(blk_b, 0)),
        scratch_shapes=[pltpu.VMEM((2, RW_HEADS, HEAD_DIM, HEAD_DIM), F32)],
        compiler_params=_cparams(("arbitrary", "arbitrary")),
        name="mixer_b_scan",
    )(*([prep] * 12))


def _rw_readout_kernel(r_ref, v_ref, k0_ref, k1_ref, g_ref, yf_ref, yb_ref, rk_ref, gw_ref, gb_ref, ones_ref,
                       o_ref):
    ones_bd = ones_ref[...]
    v = v_ref[...]
    bonus = _seg_sum(r_ref[...] * (k0_ref[...] + k1_ref[...]) * rk_ref[...], ones_bd)
    y = yf_ref[...] + yb_ref[...] + bonus * v
    mu = _seg_sum(y, ones_bd) * (1.0 / HEAD_DIM)
    yc = y - mu
    var = _seg_sum(yc * yc, ones_bd) * (1.0 / HEAD_DIM)
    yn = yc * lax.rsqrt(var + RW_GN_EPS)
    o_ref[...] = ((yn * gw_ref[...] + gb_ref[...]) * g_ref[...]).astype(o_ref.dtype)


def _rw_readout_call(prep, yf, yb, r_k, gn_w, gn_b, ones512, tm):
    R = prep.shape[0]
    col = lambda cb: pl.BlockSpec((tm, GROUP_W), lambda i: (i, cb))
    vec = pl.BlockSpec((1, GROUP_W), lambda i: (0, 0))
    return pl.pallas_call(
        _rw_readout_kernel,
        out_shape=jax.ShapeDtypeStruct((R, GROUP_W), BF16),
        grid=(R // tm,),
        in_specs=[col(PB_R), col(PB_V), col(5), col(8), col(PB_G), col(0), col(0), vec, vec, vec,
                  pl.BlockSpec((GROUP_W, GROUP_W), lambda i: (0, 0))],
        out_specs=col(0),
        compiler_params=_cparams(("arbitrary",)),
        name="mixer_b_readout",
    )(prep, prep, prep, prep, prep, yf, yb, r_k, gn_w, gn_b, ones512)


def _s5_tables(a_re, a_im, log_dt, b_re, b_im, c_re, c_im):
    Lc, G, N, P = S5_CHUNK, S5_GROUPS, S5_N, S5_P
    dt = jnp.exp(log_dt)[..., None]
    lam_re, lam_im = dt * a_re, dt * a_im
    tau = jnp.arange(Lc + 1, dtype=F32)[:, None, None, None]
    mag = jnp.exp(tau * lam_re)
    pw_re, pw_im = mag * jnp.cos(tau * lam_im), mag * jnp.sin(tau * lam_im)
    ab_re, ab_im = pw_re[1], pw_im[1]
    den = a_re * a_re + a_im * a_im
    nr = ab_re - 1.0
    cf_re, cf_im = (nr * a_re + ab_im * a_im) / den, (ab_im * a_re - nr * a_im) / den
    bp_re = cf_re[..., None] * b_re[None] - cf_im[..., None] * b_im[None]
    bp_im = cf_re[..., None] * b_im[None] + cf_im[..., None] * b_re[None]
    pb_re = pw_re[..., None] * bp_re[None] - pw_im[..., None] * bp_im[None]
    pb_im = pw_re[..., None] * bp_im[None] + pw_im[..., None] * bp_re[None]
    kk = (jnp.einsum('gqn,tdgnp->tdgqp', c_re, pb_re[:Lc]) - jnp.einsum('gqn,tdgnp->tdgqp', c_im, pb_im[:Lc]))
    jj = np.arange(Lc)[:, None]
    ii = np.arange(Lc)[None, :]
    dist = np.abs(ii - jj)
    kf = kk[dist, 0] * jnp.asarray(ii >= jj, F32)[..., None, None, None]
    kb = kk[dist, 1] * jnp.asarray(ii <= jj, F32)[..., None, None, None]
    tz = jnp.transpose(kf + kb, (2, 0, 4, 1, 3))
    jr = np.arange(Lc)
    emap = lambda pbx, order, d: jnp.transpose(pbx[order, d], (1, 0, 3, 2))
    em = jnp.stack([emap(pb_re, Lc - 1 - jr, 0), emap(pb_im, Lc - 1 - jr, 0),
                    emap(pb_re, jr, 1), emap(pb_im, jr, 1)], axis=3)

    def gmap(order, d):
        pr, pi = pw_re[order, d], pw_im[order, d]
        cp_re = c_re[None] * pr[:, :, None, :] - c_im[None] * pi[:, :, None, :]
        cp_im = c_re[None] * pi[:, :, None, :] + c_im[None] * pr[:, :, None, :]
        to = lambda t: jnp.transpose(t, (1, 3, 0, 2))
        return to(cp_re), to(-cp_im)

    gk = jnp.stack(gmap(jr + 1, 0) + gmap(Lc - jr, 1), axis=1)
    NO, NQ, NG = S5_OCTETS, 2, 4
    eye8, eye2, eye4 = (jnp.eye(k, dtype=F32) for k in (8, NQ, NG))
    wy = jnp.einsum('ogjpiq,gh->ojgpihq', tz.reshape(NO, 8, Lc, P, Lc, P), eye8)
    wy = wy.reshape(NO, Lc * LANE, Lc * LANE)
    we = jnp.einsum('orgjpkn,rs,gh->ojrgpskhn', em.reshape(NO, NQ, NG, Lc, P, 4, N), eye2, eye4)
    we = we.reshape(NO, Lc * LANE, NQ * 4 * NG * N)
    wz = jnp.concatenate([wy, we], axis=-1)
    gm = jnp.einsum('orhkniq,rs,ht->orkhnistq', gk.reshape(NO, NQ, NG, 4, N, Lc, P), eye2, eye4)
    gm = gm.reshape(NO, NQ * 4 * NG * N, Lc * LANE)
    apow = jnp.stack([pw_re[Lc, 0], pw_im[Lc, 0], pw_re[Lc, 1], pw_im[Lc, 1]], axis=1)
    apow = jnp.transpose(apow.reshape(G // NG, NG, 4, N), (0, 2, 1, 3)).reshape(G // NG, 4, NG * N)
    return wz, gm, apow


def _s5_local_kernel(u_ref, wz_ref, y_ref, e_ref):
    Lc = u_ref.shape[1]
    lhs = jnp.concatenate([u_ref[:, j, :] for j in range(Lc)], axis=-1).astype(BF16)
    z = _dot(lhs, wz_ref[...])
    for i in range(Lc):
        y_ref[:, i, :] = z[:, i * LANE:(i + 1) * LANE]
    e_ref[...] = z[:, Lc * LANE:]


def _s5_local_call(proj3, wz):
    NR, Lc, _ = proj3.shape
    NO, K, N = wz.shape
    NE = N - Lc * LANE
    tr = NR // 8
    cu = COL_DU // LANE
    return pl.pallas_call(
        _s5_local_kernel,
        out_shape=(jax.ShapeDtypeStruct((NR, Lc, GROUP_W), F32), jax.ShapeDtypeStruct((NR, NO * NE), F32)),
        grid=(NO, NR // tr),
        in_specs=[
            pl.BlockSpec((tr, Lc, LANE), lambda o, i: (i, 0, cu + o)),
            pl.BlockSpec((None, K, N), lambda o, i: (o, 0, 0)),
        ],
        out_specs=(pl.BlockSpec((tr, Lc, LANE), lambda o, i: (i, 0, o)),
                   pl.BlockSpec((tr, NE), lambda o, i: (i, o))),
        compiler_params=_cparams(("arbitrary", "arbitrary")),
        name="mixer_d_local",
    )(proj3, wz)


def _s5_scan_kernel(e_ref, ap_ref, x_ref, *, n_batch, nlc, ncc):
    NS = e_ref.shape[1] // 4
    af_re, af_im = ap_ref[0:1, :], ap_ref[1:2, :]
    ab_re, ab_im = ap_ref[2:3, :], ap_ref[3:4, :]

    def step(a_re, a_im, x_re, x_im, e):
        return a_re * x_re - a_im * x_im + e[:, :NS], a_re * x_im + a_im * x_re + e[:, NS:]

    def phase(row0, n_chunks, stride, carry):
        n_tiles = n_chunks // 8

        def body(t, carry):
            out = []
            for b in range(n_batch):
                xf_re, xf_im, xb_re, xb_im = carry[4 * b:4 * b + 4]
                rf = pl.ds(pl.multiple_of(row0 + b * stride + t * 8, 8), 8)
                rb = pl.ds(pl.multiple_of(row0 + b * stride + (n_tiles - 1 - t) * 8, 8), 8)
                ef = e_ref[rf, 0:2 * NS]
                eb = e_ref[rb, 2 * NS:4 * NS]
                xf_in, xb_in = [], [None] * 8
                for j in range(8):
                    xf_in.append(jnp.concatenate([xf_re, xf_im], axis=-1))
                    xf_re, xf_im = step(af_re, af_im, xf_re, xf_im, ef[j:j + 1, :])
                for j in reversed(range(8)):
                    xb_in[j] = jnp.concatenate([xb_re, xb_im], axis=-1)
                    xb_re, xb_im = step(ab_re, ab_im, xb_re, xb_im, eb[j:j + 1, :])
                x_ref[rf, 0:2 * NS] = jnp.concatenate(xf_in, axis=0)
                x_ref[rb, 2 * NS:4 * NS] = jnp.concatenate(xb_in, axis=0)
                out += [xf_re, xf_im, xb_re, xb_im]
            return tuple(out)

        return lax.fori_loop(0, n_tiles, body, carry)

    zero = jnp.zeros((1, NS), F32)
    carry = phase(n_batch * nlc, ncc, ncc, (zero,) * (4 * n_batch))
    phase(0, nlc, nlc, carry)


def _s5_scan_call(e, apow, n_batch, seq, ctx):
    NR, NEall = e.shape
    NQ8, _, NS = apow.shape
    return pl.pallas_call(
        functools.partial(_s5_scan_kernel, n_batch=n_batch, nlc=seq // S5_CHUNK, ncc=ctx // S5_CHUNK),
        out_shape=jax.ShapeDtypeStruct((NR, NEall), F32),
        grid=(NQ8,),
        in_specs=[
            pl.BlockSpec((NR, 4 * NS), lambda q: (0, q)),
            pl.BlockSpec((None, 4, NS), lambda q: (q, 0, 0)),
        ],
        out_specs=pl.BlockSpec((NR, 4 * NS), lambda q: (0, q)),
        compiler_params=_cparams(("arbitrary",)),
        name="mixer_d_scan",
    )(e, apow)


def _s5_carry_kernel(x_ref, gm_ref, yl_ref, y_ref):
    Lc = yl_ref.shape[1]
    y = _dot(x_ref[...].astype(BF16), gm_ref[...])
    for i in range(Lc):
        y_ref[:, i, :] = yl_ref[:, i, :] + y[:, i * LANE:(i + 1) * LANE]


def _s5_carry_call(xin, gm, yloc):
    NR, Lc, _ = yloc.shape
    NO, NE, N = gm.shape
    tr = NR // 8
    return pl.pallas_call(
        _s5_carry_kernel,
        out_shape=jax.ShapeDtypeStruct((NR, Lc, GROUP_W), F32),
        grid=(NO, NR // tr),
        in_specs=[
            pl.BlockSpec((tr, NE), lambda o, i: (i, o)),
            pl.BlockSpec((None, NE, N), lambda o, i: (o, 0, 0)),
            pl.BlockSpec((tr, Lc, LANE), lambda o, i: (i, 0, o)),
        ],
        out_specs=pl.BlockSpec((tr, Lc, LANE), lambda o, i: (i, 0, o)),
        compiler_params=_cparams(("arbitrary", "arbitrary")),
        name="mixer_d_carry",
    )(xin, gm, yloc)


def _s5_out_kernel(y_ref, u_ref, d_ref, w_ref, b_ref, o_ref):
    y = y_ref[...] + d_ref[...] * u_ref[...]
    c = math.sqrt(2.0 / math.pi)
    y = 0.5 * y * (1.0 + jnp.tanh(c * (y + 0.044715 * (y * y * y))))
    z = _dot(y.astype(BF16), w_ref[...]) + b_ref[...]
    o_ref[...] = (y * _sigmoid(z)).astype(o_ref.dtype)


def _s5_out_call(y_tok, proj, d_skip, glu_w, glu_b, tm):
    R = proj.shape[0]
    vec = pl.BlockSpec((1, GROUP_W), lambda i: (0, 0))
    return pl.pallas_call(
        _s5_out_kernel,
        out_shape=jax.ShapeDtypeStruct((R, GROUP_W), BF16),
        grid=(R // tm,),
        in_specs=[
            pl.BlockSpec((tm, GROUP_W), lambda i: (i, 0)),
            pl.BlockSpec((tm, GROUP_W), lambda i: (i, COL_DU // GROUP_W)),
            vec,
            pl.BlockSpec((GROUP_W, GROUP_W), lambda i: (0, 0)),
            vec,
        ],
        out_specs=pl.BlockSpec((tm, GROUP_W), lambda i: (i, 0)),
        compiler_params=_cparams(("arbitrary",)),
        name="mixer_d_out",
    )(y_tok, proj, d_skip, glu_w, glu_b)


def _rope_tables(n_tokens):
    t = jnp.arange(n_tokens)
    nf = HEAD_DIM // 4
    inv = 1.0 / (ROPE_BASE ** (jnp.arange(nf, dtype=F32) / nf))

    def ang(pp):
        a = pp.astype(F32)[:, None] * inv[None, :]
        return jnp.concatenate([a, a], -1)

    a = jnp.concatenate([ang(t // GRID_W), ang(t % GRID_W)], -1)
    cos, sin = jnp.cos(a), jnp.sin(a)
    sign = np.where((np.arange(HEAD_DIM) % 32) < 16, -1.0, 1.0).astype(np.float32)
    cos2 = jnp.concatenate([cos, cos], -1)
    sin2 = jnp.concatenate([sin * sign, sin * sign], -1)
    return cos2, sin2


def _block_ones(n):
    return jnp.asarray(np.kron(np.eye(n // HEAD_DIM), np.ones((HEAD_DIM, HEAD_DIM))), BF16)


def _permute_w_in(w_in):
    cuts = np.cumsum([512, 512, 512, 512, 512, 512, 32, 32, 96, 512, 128, 128, 512])
    seg = lambda i: w_in[..., (0 if i == 0 else cuts[i - 1]):cuts[i]]
    L, D = w_in.shape[:2]
    pad = jnp.zeros((L, D, 96), w_in.dtype)
    parts = [seg(0), seg(1), seg(2), seg(3), seg(4), seg(5), seg(9), seg(12),
             seg(6), seg(7), seg(8), pad, seg(10), seg(11)]
    return jnp.concatenate(parts, axis=-1).astype(BF16)


def _forward(x, c, ctx, c_ctx, w_ada, b_ada, norm_g, ffn1_wg, ffn1_wu, ffn1_wd, ffn2_wg, ffn2_wu, ffn2_wd,
             w_in, w_out, na_q_g, na_k_g, na_rpb, rw_conv, rw_w0, rw_w_up, rw_a0, rw_a_up, rw_g_up,
             rw_k_k, rw_k_a, rw_r_k, rw_gn_w, rw_gn_b, wa_q_g, wa_k_g, wa_sink, s5_a_re, s5_a_im,
             s5_log_dt, s5_b_re, s5_b_im, s5_c_re, s5_c_im, s5_d, s5_glu_w, s5_glu_b, *, tm, tf):
    B, SEQ, D = x.shape
    CTX = ctx.shape[1]
    L = w_ada.shape[0]
    n_lat = B * SEQ
    n_qb = SEQ // QBLK

    bf = lambda t: t.astype(BF16)
    f1g, f1u, f1d, f2g, f2u, f2d = map(bf, (ffn1_wg, ffn1_wu, ffn1_wd, ffn2_wg, ffn2_wu, ffn2_wd))
    w_in_p = _permute_w_in(w_in)
    w_out_b = bf(w_out)
    glu_w_b = bf(s5_glu_w)
    ones128, ones512 = _block_ones(LANE), _block_ones(GROUP_W)
    cos_t, sin_t = _rope_tables(SEQ)
    na_bias = [_na_bias_table(na_rpb[l], n_qb) for l in range(L)]
    zr = lambda r, cdim: jnp.zeros((L, r, cdim), F32)
    wlr = jnp.concatenate([
        jnp.concatenate([rw_w_up[:, 0], rw_w_up[:, 1], zr(32, 3 * GROUP_W)], axis=-1),
        jnp.concatenate([zr(32, 2 * GROUP_W), rw_a_up[:, 0], rw_a_up[:, 1], zr(32, GROUP_W)], axis=-1),
        jnp.concatenate([zr(96, 4 * GROUP_W), rw_g_up], axis=-1),
        zr(96, 5 * GROUP_W)], axis=1).astype(F32)
    s5_tabs = jax.vmap(_s5_tables)(s5_a_re, s5_a_im, s5_log_dt, s5_b_re, s5_b_im, s5_c_re, s5_c_im)
    tile2 = lambda t: jnp.concatenate([t, t], axis=-1)

    cc = jnp.concatenate([c, c_ctx[None], jnp.zeros((8 - B - 1, D), F32)], axis=0)
    mods_all = _ada_call(cc, w_ada, b_ada)[:, :B + 1].reshape(L, B + 1, N_MOD, D)

    xs = jnp.concatenate([x.reshape(n_lat, D), ctx.reshape(B * CTX, D)], axis=0)
    R = xs.shape[0]
    for l in range(L):
        want_ctx = l < L - 1
        mods = mods_all[l]
        ng = norm_g[l]
        xs = _ffn_call(xs, mods, ng[0:1], f1g, f1u, f1d, l, 0, R, SEQ, B, tm, tf)
        proj = _win_call(xs, mods, ng[1:2], w_in_p, l, SEQ, B, tm, D_IN_PAD // 3)
        o_a = _na_call(proj, na_bias[l], tile2(na_q_g[l][None]), tile2(na_k_g[l][None]), ones128,
                       B, SEQ, CTX, want_ctx)
        prep = _rw_prep_call(proj, rw_conv[l], wlr[l], rw_w0[l], rw_a0[l], rw_k_k[l][None], rw_k_a[l][None],
                             ones512, B, SEQ, CTX, min(tm, 256))
        vec = lambda t: t.reshape(1, GROUP_W)
        yf, yb = _rw_scan_call(prep, B, SEQ, CTX)
        o_b = _rw_readout_call(prep, yf, yb, vec(rw_r_k[l]), vec(rw_gn_w[l]), vec(rw_gn_b[l]), ones512, tm)
        o_c = _wa_call(proj, wa_sink[l], cos_t, sin_t, tile2(wa_q_g[l][None]), tile2(wa_k_g[l][None]), ones128,
                       B, SEQ, CTX, want_ctx)
        wz, gm, apow = (t[l] for t in s5_tabs)
        yloc, e_loc = _s5_local_call(proj.reshape(R // S5_CHUNK, S5_CHUNK, D_IN_PAD), bf(wz))
        xin = _s5_scan_call(e_loc, apow, B, SEQ, CTX)
        y_tok = _s5_carry_call(xin, bf(gm), yloc).reshape(R, GROUP_W)
        o_d = _s5_out_call(y_tok, proj, vec(s5_d[l]), glu_w_b[l], vec(s5_glu_b[l]), tm)
        n_rows = R if want_ctx else n_lat
        xs = _wout_call(xs, (o_a, o_b, o_c, o_d), mods, w_out_b, l, n_rows, SEQ, B, tm)
        xs = _ffn_call(xs, mods, ng[2:3], f2g, f2u, f2d, l, 6, n_rows, SEQ, B, tm, tf)
    return xs[:n_lat].reshape(B, SEQ, D)


def kernel(x, c, ctx, c_ctx, w_ada, b_ada, norm_g, ffn1_wg, ffn1_wu, ffn1_wd, ffn2_wg, ffn2_wu, ffn2_wd, w_in, w_out, na_q_g, na_k_g, na_rpb, rw_conv, rw_w0, rw_w_up, rw_a0, rw_a_up, rw_g_up, rw_k_k, rw_k_a, rw_r_k, rw_gn_w, rw_gn_b, wa_q_g, wa_k_g, wa_sink, s5_a_re, s5_a_im, s5_log_dt, s5_b_re, s5_b_im, s5_c_re, s5_c_im, s5_d, s5_glu_w, s5_glu_b):
    return _forward(x, c, ctx, c_ctx, w_ada, b_ada, norm_g, ffn1_wg, ffn1_wu, ffn1_wd, ffn2_wg, ffn2_wu, ffn2_wd,
                    w_in, w_out, na_q_g, na_k_g, na_rpb, rw_conv, rw_w0, rw_w_up, rw_a0, rw_a_up, rw_g_up,
                    rw_k_k, rw_k_a, rw_r_k, rw_gn_w, rw_gn_b, wa_q_g, wa_k_g, wa_sink, s5_a_re, s5_a_im,
                    s5_log_dt, s5_b_re, s5_b_im, s5_c_re, s5_c_im, s5_d, s5_glu_w, s5_glu_b, tm=512, tf=512)
```

```python
import functools
import math

import numpy as np
import jax
import jax.numpy as jnp
from jax import lax
from jax.experimental import pallas as pl
from jax.experimental.pallas import tpu as pltpu

F32 = jnp.float32
BF16 = jnp.bfloat16

D_MODEL = 2048
GRID_W = 64
HEAD_DIM = 64
GROUP_W = D_MODEL // 4
N_MOD = 9
NORM_EPS = 1e-6
ROPE_BASE = 10000.0
NEG_INF = -1e30

NA_HEADS = GROUP_W // HEAD_DIM
NA_WIN_R = 8
NA_WIN_C = 16
RW_HEADS = GROUP_W // HEAD_DIM
RW_DECAY_RANK = 32
RW_ICLR_RANK = 32
RW_GATE_RANK = 96
RW_GN_EPS = 64e-5
WA_HEADS = GROUP_W // HEAD_DIM
WA_KV_HEADS = 2
WA_GROUP = WA_HEADS // WA_KV_HEADS
WA_WINDOW = 128
S5_P = 16
S5_GROUPS = GROUP_W // S5_P
S5_N = 64

LANE = 128
QBLK = 2 * GRID_W
NA_KBLKS = 5
RW_CHUNK = 64
S5_CHUNK = 8
S5_OCTETS = 4
VMEM_LIMIT = 56 * 1024 * 1024

COL_AQ, COL_AK, COL_AV = 0, 512, 1024
COL_BR = 1536
COL_CQ = 3072
COL_DU = 3584
COL_BLR = 4096
COL_CK, COL_CV = 4352, 4480
D_IN_PAD = 4608

PB_R, PB_V, PB_KK, PB_G = 0, 1, 2, 9
PB_N = 10


def _cparams(sem):
    return pltpu.CompilerParams(dimension_semantics=sem, vmem_limit_bytes=VMEM_LIMIT)


def _dot(a, b):
    return jnp.dot(a, b, preferred_element_type=F32)


def _dot_nt(a, b):
    return lax.dot_general(a, b, (((1,), (1,)), ((), ())), preferred_element_type=F32)


def _dot_tn(a, b):
    return lax.dot_general(a, b, (((0,), (0,)), ((), ())), preferred_element_type=F32)


def _sigmoid(x):
    return 1.0 / (1.0 + jnp.exp(-x))


def _modulate(x, g, shift, scale):
    ms = jnp.mean(x * x, axis=-1, keepdims=True)
    return (x * lax.rsqrt(ms + NORM_EPS) * g) * (1.0 + scale) + shift


def _seg_sum(x, ones_bd):
    hi = x.astype(BF16)
    r1 = x - hi.astype(F32)
    mid = r1.astype(BF16)
    lo = (r1 - mid.astype(F32)).astype(BF16)
    return _dot(hi, ones_bd) + _dot(mid, ones_bd) + _dot(lo, ones_bd)


def _head_rmsnorm(x, ones_bd):
    ms = _seg_sum(x * x, ones_bd) * (1.0 / HEAD_DIM)
    return x * lax.rsqrt(ms + NORM_EPS)


def _ada_kernel(c_ref, w_ref, b_ref, o_ref):
    c = c_ref[...]
    s = (c * _sigmoid(c)).astype(BF16)
    o_ref[...] = _dot(s, w_ref[...].astype(BF16)) + b_ref[...]


def _ada_call(cc, w_ada, b_ada):
    L, D, N = w_ada.shape
    tn = 1024
    return pl.pallas_call(
        _ada_kernel,
        out_shape=jax.ShapeDtypeStruct((L, 8, N), F32),
        grid=(L, N // tn),
        in_specs=[
            pl.BlockSpec((8, D), lambda l, j: (0, 0)),
            pl.BlockSpec((None, D, tn), lambda l, j: (l, 0, j)),
            pl.BlockSpec((None, 1, tn), lambda l, j: (l, 0, j)),
        ],
        out_specs=pl.BlockSpec((None, 8, tn), lambda l, j: (l, 0, j)),
        compiler_params=_cparams(("arbitrary", "arbitrary")),
        name="adaln",
    )(cc, w_ada, b_ada.reshape(L, 1, N))


def _ffn_kernel(x_ref, mod_ref, g_ref, wg_ref, wu_ref, wd_ref, o_ref, h_sc, acc_sc, *, mi):
    j = pl.program_id(1)

    @pl.when(j == 0)
    def _():
        h = _modulate(x_ref[...], g_ref[...], mod_ref[mi:mi + 1, :], mod_ref[mi + 1:mi + 2, :])
        h_sc[...] = h.astype(BF16)
        acc_sc[...] = jnp.zeros_like(acc_sc)

    h = h_sc[...]
    gate = _dot(h, wg_ref[...])
    up = _dot(h, wu_ref[...])
    a = (gate * _sigmoid(gate) * up).astype(BF16)
    acc_sc[...] += _dot(a, wd_ref[...])

    @pl.when(j == pl.num_programs(1) - 1)
    def _():
        o_ref[...] = x_ref[...] + 0.5 * mod_ref[mi + 2:mi + 3, :] * acc_sc[...]


def _ffn_call(xs, mods, g, wg, wu, wd, l, mi, n_rows, rows_per_seq, n_batch, tm, tf):
    D = xs.shape[1]
    F = wg.shape[2]
    tps = rows_per_seq // tm
    return pl.pallas_call(
        functools.partial(_ffn_kernel, mi=mi),
        out_shape=jax.ShapeDtypeStruct((n_rows, D), F32),
        grid=(n_rows // tm, F // tf),
        in_specs=[
            pl.BlockSpec((tm, D), lambda i, j: (i, 0)),
            pl.BlockSpec((None, N_MOD, D), lambda i, j: (jnp.minimum(i // tps, n_batch), 0, 0)),
            pl.BlockSpec((1, D), lambda i, j: (0, 0)),
            pl.BlockSpec((None, D, tf), lambda i, j: (l, 0, j)),
            pl.BlockSpec((None, D, tf), lambda i, j: (l, 0, j)),
            pl.BlockSpec((None, tf, D), lambda i, j: (l, j, 0)),
        ],
        out_specs=pl.BlockSpec((tm, D), lambda i, j: (i, 0)),
        scratch_shapes=[pltpu.VMEM((tm, D), BF16), pltpu.VMEM((tm, D), F32)],
        compiler_params=_cparams(("arbitrary", "arbitrary")),
        name="ffn",
    )(xs, mods, g, wg, wu, wd)


def _win_kernel(x_ref, mod_ref, g_ref, w_ref, o_ref, h_sc):
    @pl.when(pl.program_id(1) == 0)
    def _():
        h = _modulate(x_ref[...], g_ref[...], mod_ref[3:4, :], mod_ref[4:5, :])
        h_sc[...] = h.astype(BF16)

    o_ref[...] = _dot(h_sc[...], w_ref[...])


def _win_call(xs, mods, g, w_in, l, rows_per_seq, n_batch, tm, tn):
    R, D = xs.shape
    N = w_in.shape[2]
    tps = rows_per_seq // tm
    return pl.pallas_call(
        _win_kernel,
        out_shape=jax.ShapeDtypeStruct((R, N), F32),
        grid=(R // tm, N // tn),
        in_specs=[
            pl.BlockSpec((tm, D), lambda i, j: (i, 0)),
            pl.BlockSpec((None, N_MOD, D), lambda i, j: (jnp.minimum(i // tps, n_batch), 0, 0)),
            pl.BlockSpec((1, D), lambda i, j: (0, 0)),
            pl.BlockSpec((None, D, tn), lambda i, j: (l, 0, j)),
        ],
        out_specs=pl.BlockSpec((tm, tn), lambda i, j: (i, j)),
        scratch_shapes=[pltpu.VMEM((tm, D), BF16)],
        compiler_params=_cparams(("arbitrary", "arbitrary")),
        name="in_proj",
    )(xs, mods, g, w_in)


def _wout_kernel(x_ref, oa_ref, ob_ref, oc_ref, od_ref, mod_ref, w_ref, o_ref):
    acc = _dot(oa_ref[...], w_ref[0 * GROUP_W:1 * GROUP_W, :])
    acc += _dot(ob_ref[...], w_ref[1 * GROUP_W:2 * GROUP_W, :])
    acc += _dot(oc_ref[...], w_ref[2 * GROUP_W:3 * GROUP_W, :])
    acc += _dot(od_ref[...], w_ref[3 * GROUP_W:4 * GROUP_W, :])
    o_ref[...] = x_ref[...] + mod_ref[5:6, :] * acc


def _wout_call(xs, outs, mods, w_out, l, n_rows, rows_per_seq, n_batch, tm):
    D = xs.shape[1]
    tps = rows_per_seq // tm
    ospec = pl.BlockSpec((tm, GROUP_W), lambda i: (i, 0))
    return pl.pallas_call(
        _wout_kernel,
        out_shape=jax.ShapeDtypeStruct((n_rows, D), F32),
        grid=(n_rows // tm,),
        in_specs=[
            pl.BlockSpec((tm, D), lambda i: (i, 0)),
            ospec, ospec, ospec, ospec,
            pl.BlockSpec((None, N_MOD, D), lambda i: (jnp.minimum(i // tps, n_batch), 0, 0)),
            pl.BlockSpec((None, D, D), lambda i: (l, 0, 0)),
        ],
        out_specs=pl.BlockSpec((tm, D), lambda i: (i, 0)),
        compiler_params=_cparams(("arbitrary",)),
        name="out_proj",
    )(xs, *outs, mods, w_out)


def _na_case_reps(n_qb):
    return (0, 1, 2, n_qb - 2, n_qb - 1)


def _na_start(p, n_qb):
    return jnp.clip(p - 2, 0, n_qb - NA_KBLKS)


def _na_bias_table(rpb, n_qb):
    H = rpb.shape[0]
    rows = 2 * n_qb
    qc = np.arange(GRID_W)[:, None]
    kc = np.arange(GRID_W)[None, :]
    cs = np.clip(qc - NA_WIN_C // 2, 0, GRID_W - NA_WIN_C)
    okc = (kc >= cs) & (kc < cs + NA_WIN_C)
    dc = np.where(okc, kc - qc + NA_WIN_C - 1, 0)
    rp = rpb.astype(F32).reshape(H, 2 * NA_WIN_R - 1, 2 * NA_WIN_C - 1)
    blocks = jnp.where(jnp.asarray(okc)[None, None], jnp.take(rp, jnp.asarray(dc), axis=2), NEG_INF)
    neg = jnp.full((H, GRID_W, GRID_W), NEG_INF, F32)
    cases = []
    for p in _na_case_reps(n_qb):
        start = min(max(p - 2, 0), n_qb - NA_KBLKS)
        qrows = []
        for qr in range(2):
            qa = 2 * p + qr
            rs = min(max(qa - NA_WIN_R // 2, 0), rows - NA_WIN_R)
            krow = []
            for kr in range(2 * NA_KBLKS):
                ka = 2 * start + kr
                krow.append(blocks[:, ka - qa + NA_WIN_R - 1] if rs <= ka < rs + NA_WIN_R else neg)
            qrows.append(jnp.concatenate(krow, axis=-1))
        cases.append(jnp.concatenate(qrows, axis=-2))
    return jnp.stack(cases, axis=0)


def _na_kernel(q_ref, k_ref, v_ref, kc_ref, vc_ref, bias_ref, qg_ref, kg_ref, ones_ref, o_ref,
               kn_sc, vn_sc, kcn_sc, vcn_sc, *, n_qb, seq, ctx):
    p = pl.program_id(2)
    ones_bd = ones_ref[...]
    scale = HEAD_DIM ** -0.5

    @pl.when(p == 0)
    def _():
        kg = kg_ref[...]
        rows = 512

        def body(i, carry):
            sl = pl.ds(pl.multiple_of(i * rows, rows), rows)
            kn_sc[sl, :] = (_head_rmsnorm(k_ref[sl, :], ones_bd) * kg).astype(BF16)
            vn_sc[sl, :] = v_ref[sl, :].astype(BF16)
            return carry

        lax.fori_loop(0, seq // rows, body, 0)
        kcn_sc[...] = (_head_rmsnorm(kc_ref[...], ones_bd) * kg).astype(BF16)
        vcn_sc[...] = vc_ref[...].astype(BF16)

    nq = q_ref.shape[0] // QBLK
    n_lat = n_qb // nq
    q = (_head_rmsnorm(q_ref[...], ones_bd) * (qg_ref[...] * scale)).astype(BF16)
    kcn = kcn_sc[...]
    vcn = vcn_sc[...]
    chains = [(qb, h) for qb in range(nq) for h in range(2)]
    hsl = lambda h: slice(h * HEAD_DIM, (h + 1) * HEAD_DIM)
    qh = [q[qb * QBLK:(qb + 1) * QBLK, hsl(h)] for qb, h in chains]
    s_c = [_dot_nt(qh[c], kcn[:, hsl(h)]) for c, (qb, h) in enumerate(chains)]
    m_c = [jnp.max(t, axis=-1, keepdims=True) for t in s_c]

    def finish(o, den):
        rows = [jnp.concatenate([o[qb * 2 + h] / den[qb * 2 + h] for h in range(2)], axis=-1) for qb in range(nq)]
        o_ref[...] = jnp.concatenate(rows, axis=0).astype(o_ref.dtype)

    @pl.when(p < n_lat)
    def _():
        starts = [_na_start(p * nq + qb, n_qb) for qb in range(nq)]
        cases = [p * nq + qb - starts[qb] for qb in range(nq)]
        sls = [pl.ds(pl.multiple_of(st * QBLK, QBLK), NA_KBLKS * QBLK) for st in starts]
        kw = [kn_sc[sl, :] for sl in sls]
        vw = [vn_sc[sl, :] for sl in sls]
        s_n = [_dot_nt(qh[c], kw[qb][:, hsl(h)]) + bias_ref[cases[qb], h] for c, (qb, h) in enumerate(chains)]
        m = [jnp.maximum(jnp.max(s_n[c], axis=-1, keepdims=True), m_c[c]) for c in range(len(chains))]
        p_n = [jnp.exp(s_n[c] - m[c]) for c in range(len(chains))]
        p_c = [jnp.exp(s_c[c] - m[c]) for c in range(len(chains))]
        den = [jnp.sum(p_n[c], axis=-1, keepdims=True) + jnp.sum(p_c[c], axis=-1, keepdims=True)
               for c in range(len(chains))]
        o = [_dot(p_n[c].astype(BF16), vw[qb][:, hsl(h)]) + _dot(p_c[c].astype(BF16), vcn[:, hsl(h)])
             for c, (qb, h) in enumerate(chains)]
        finish(o, den)

    @pl.when(p >= n_lat)
    def _():
        p_c = [jnp.exp(s_c[c] - m_c[c]) for c in range(len(chains))]
        den = [jnp.sum(t, axis=-1, keepdims=True) for t in p_c]
        o = [_dot(p_c[c].astype(BF16), vcn[:, hsl(h)]) for c, (qb, h) in enumerate(chains)]
        finish(o, den)


def _na_call(proj, bias, qg, kg, ones_bd, n_batch, seq, ctx, want_ctx):
    R = proj.shape[0]
    nq = 2
    n_qb = seq // QBLK
    n_lat, n_ctx = n_qb // nq, ctx // (nq * QBLK)
    steps = n_lat + (n_ctx if want_ctx else 0)
    ctx_kb = n_batch * seq // ctx

    def qrow(b, p):
        return jnp.where(p < n_lat, b * n_lat + p, n_batch * n_lat + b * n_ctx + (p - n_lat))

    cq, ck, cv = COL_AQ // LANE, COL_AK // LANE, COL_AV // LANE
    return pl.pallas_call(
        functools.partial(_na_kernel, n_qb=n_qb, seq=seq, ctx=ctx),
        out_shape=jax.ShapeDtypeStruct((R, GROUP_W), BF16),
        grid=(n_batch, NA_HEADS // 2, steps),
        in_specs=[
            pl.BlockSpec((nq * QBLK, LANE), lambda b, hp, p: (qrow(b, p), cq + hp)),
            pl.BlockSpec((seq, LANE), lambda b, hp, p: (b, ck + hp)),
            pl.BlockSpec((seq, LANE), lambda b, hp, p: (b, cv + hp)),
            pl.BlockSpec((ctx, LANE), lambda b, hp, p: (ctx_kb + b, ck + hp)),
            pl.BlockSpec((ctx, LANE), lambda b, hp, p: (ctx_kb + b, cv + hp)),
            pl.BlockSpec((5, 2, QBLK, NA_KBLKS * QBLK), lambda b, hp, p: (0, hp, 0, 0)),
            pl.BlockSpec((1, LANE), lambda b, hp, p: (0, 0)),
            pl.BlockSpec((1, LANE), lambda b, hp, p: (0, 0)),
            pl.BlockSpec((LANE, LANE), lambda b, hp, p: (0, 0)),
        ],
        out_specs=pl.BlockSpec((nq * QBLK, LANE), lambda b, hp, p: (qrow(b, p), hp)),
        scratch_shapes=[pltpu.VMEM((seq, LANE), BF16), pltpu.VMEM((seq, LANE), BF16),
                        pltpu.VMEM((ctx, LANE), BF16), pltpu.VMEM((ctx, LANE), BF16)],
        compiler_params=_cparams(("arbitrary", "arbitrary", "arbitrary")),
        name="mixer_a",
    )(proj, proj, proj, proj, proj, bias, qg, kg, ones_bd)


def _rope(x, cos, sin_signed, first_half):
    rot = jnp.where(first_half, pltpu.roll(x, LANE - 16, 1), pltpu.roll(x, 16, 1))
    return x * cos + rot * sin_signed


def _wa_kernel(sink_ref, q_ref, k_ref, v_ref, kc_ref, vc_ref, cosk_ref, sin_k_ref, cosq_ref, sinq_ref,
               qg_ref, kg_ref, ones_ref, o_ref, kn_sc, vn_sc, kcn_sc, vcn_sc, *, n_qb, seq, ctx):
    p = pl.program_id(1)
    ones_bd = ones_ref[...]
    scale = HEAD_DIM ** -0.5
    lane = lax.broadcasted_iota(jnp.int32, (1, LANE), 1)
    first_half = (lane % 32) < 16

    @pl.when(p == 0)
    def _():
        kg = kg_ref[...]
        rows = 512

        def body(i, carry):
            sl = pl.ds(pl.multiple_of(i * rows, rows), rows)
            kn = _head_rmsnorm(k_ref[sl, :], ones_bd) * kg
            kn_sc[sl, :] = _rope(kn, cosk_ref[sl, :], sin_k_ref[sl, :], first_half).astype(BF16)
            vn_sc[sl, :] = v_ref[sl, :].astype(BF16)
            return carry

        lax.fori_loop(0, seq // rows, body, 0)
        kcn_sc[...] = (_head_rmsnorm(kc_ref[...], ones_bd) * kg).astype(BF16)
        vcn_sc[...] = vc_ref[...].astype(BF16)

    nq = q_ref.shape[0] // QBLK
    n_lat = n_qb // nq
    qg = qg_ref[...] * scale
    is_lat = p < n_lat
    cosq = jnp.where(is_lat, cosq_ref[...], 1.0)
    sinq = jnp.where(is_lat, sinq_ref[...], 0.0)
    qh = []
    for c in range(4):
        qn = _head_rmsnorm(q_ref[:, c * LANE:(c + 1) * LANE], ones_bd) * qg
        qn = _rope(qn, cosq, sinq, first_half).astype(BF16)
        qh += [qn[:, :HEAD_DIM], qn[:, HEAD_DIM:]]
    kcn = kcn_sc[...]
    vcn = vcn_sc[...]
    chains = [(qb, kh) for qb in range(nq) for kh in range(WA_KV_HEADS)]
    nch = range(len(chains))
    hsl = lambda kh: slice(kh * HEAD_DIM, (kh + 1) * HEAD_DIM)
    qs = [jnp.concatenate([qh[kh * WA_GROUP + g][qb * QBLK:(qb + 1) * QBLK, :] for g in range(WA_GROUP)], axis=0)
          for qb, kh in chains]
    sk = [jnp.concatenate([jnp.full((QBLK, 1), sink_ref[kh * WA_GROUP + g], F32) for g in range(WA_GROUP)],
                          axis=0) for qb, kh in chains]
    s_c = [_dot_nt(qs[c], kcn[:, hsl(chains[c][1])]) for c in nch]
    m_c = [jnp.maximum(jnp.max(s_c[c], axis=-1, keepdims=True), sk[c]) for c in nch]

    def finish(o, den):
        rows = []
        for qb in range(nq):
            heads = []
            for kh in range(WA_KV_HEADS):
                c = qb * WA_KV_HEADS + kh
                oc = o[c] / den[c]
                heads += [oc[g * QBLK:(g + 1) * QBLK, :] for g in range(WA_GROUP)]
            rows.append(jnp.concatenate(heads, axis=-1))
        o_ref[...] = jnp.concatenate(rows, axis=0).astype(o_ref.dtype)

    @pl.when(is_lat)
    def _():
        shp = (WA_GROUP * QBLK, 3 * QBLK)
        qoff = lax.broadcasted_iota(jnp.int32, shp, 0) & (QBLK - 1)
        koff = lax.broadcasted_iota(jnp.int32, shp, 1)
        blk = [p * nq + qb for qb in range(nq)]
        ws = [jnp.clip((bq - 1) * QBLK, 0, seq - 3 * QBLK) for bq in blk]
        sls = [pl.ds(pl.multiple_of(w, QBLK), 3 * QBLK) for w in ws]
        kw = [kn_sc[sl, :] for sl in sls]
        vw = [vn_sc[sl, :] for sl in sls]
        ok = [jnp.abs((ws[qb] + koff) - (blk[qb] * QBLK + qoff)) <= WA_WINDOW for qb in range(nq)]
        s_w = [jnp.where(ok[chains[c][0]], _dot_nt(qs[c], kw[chains[c][0]][:, hsl(chains[c][1])]), NEG_INF)
               for c in nch]
        m = [jnp.maximum(jnp.max(s_w[c], axis=-1, keepdims=True), m_c[c]) for c in nch]
        p_w = [jnp.exp(s_w[c] - m[c]) for c in nch]
        p_c = [jnp.exp(s_c[c] - m[c]) for c in nch]
        den = [jnp.sum(p_w[c], axis=-1, keepdims=True) + jnp.sum(p_c[c], axis=-1, keepdims=True)
               + jnp.exp(sk[c] - m[c]) for c in nch]
        o = [_dot(p_w[c].astype(BF16), vw[chains[c][0]][:, hsl(chains[c][1])])
             + _dot(p_c[c].astype(BF16), vcn[:, hsl(chains[c][1])]) for c in nch]
        finish(o, den)

    @pl.when(jnp.logical_not(is_lat))
    def _():
        p_c = [jnp.exp(s_c[c] - m_c[c]) for c in nch]
        den = [jnp.sum(p_c[c], axis=-1, keepdims=True) + jnp.exp(sk[c] - m_c[c]) for c in nch]
        o = [_dot(p_c[c].astype(BF16), vcn[:, hsl(chains[c][1])]) for c in nch]
        finish(o, den)


def _wa_call(proj, sink, cos_t, sin_t, qg, kg, ones_bd, n_batch, seq, ctx, want_ctx):
    R = proj.shape[0]
    nq = 2
    n_qb = seq // QBLK
    n_lat, n_ctx = n_qb // nq, ctx // (nq * QBLK)
    steps = n_lat + (n_ctx if want_ctx else 0)
    ctx_kb = n_batch * seq // ctx

    def qrow(b, p):
        return jnp.where(p < n_lat, b * n_lat + p, n_batch * n_lat + b * n_ctx + (p - n_lat))

    ck, cv = COL_CK // LANE, COL_CV // LANE
    return pl.pallas_call(
        functools.partial(_wa_kernel, n_qb=n_qb, seq=seq, ctx=ctx),
        out_shape=jax.ShapeDtypeStruct((R, GROUP_W), BF16),
        grid=(n_batch, steps),
        in_specs=[
            pl.BlockSpec(memory_space=pltpu.SMEM),
            pl.BlockSpec((nq * QBLK, GROUP_W), lambda b, p: (qrow(b, p), COL_CQ // GROUP_W)),
            pl.BlockSpec((seq, LANE), lambda b, p: (b, ck)),
            pl.BlockSpec((seq, LANE), lambda b, p: (b, cv)),
            pl.BlockSpec((ctx, LANE), lambda b, p: (ctx_kb + b, ck)),
            pl.BlockSpec((ctx, LANE), lambda b, p: (ctx_kb + b, cv)),
            pl.BlockSpec((seq, LANE), lambda b, p: (0, 0)),
            pl.BlockSpec((seq, LANE), lambda b, p: (0, 0)),
            pl.BlockSpec((nq * QBLK, LANE), lambda b, p: (jnp.minimum(p, n_lat - 1), 0)),
            pl.BlockSpec((nq * QBLK, LANE), lambda b, p: (jnp.minimum(p, n_lat - 1), 0)),
            pl.BlockSpec((1, LANE), lambda b, p: (0, 0)),
            pl.BlockSpec((1, LANE), lambda b, p: (0, 0)),
            pl.BlockSpec((LANE, LANE), lambda b, p: (0, 0)),
        ],
        out_specs=pl.BlockSpec((nq * QBLK, GROUP_W), lambda b, p: (qrow(b, p), 0)),
        scratch_shapes=[pltpu.VMEM((seq, LANE), BF16), pltpu.VMEM((seq, LANE), BF16),
                        pltpu.VMEM((ctx, LANE), BF16), pltpu.VMEM((ctx, LANE), BF16)],
        compiler_params=_cparams(("arbitrary", "arbitrary")),
        name="mixer_c",
    )(sink, proj, proj, proj, proj, proj, cos_t, sin_t, cos_t, sin_t, qg, kg, ones_bd)


def _rw_prep_kernel(x_ref, prev_ref, next_ref, lr_ref, conv_ref, wlr_ref, w0_ref, a0_ref, kk_ref, ka_ref,
                    ones_ref, o_ref, *, tiles_lat, tiles_ctx, n_lat_tiles):
    i = pl.program_id(0)
    tm = x_ref.shape[0]
    in_lat = i < n_lat_tiles
    first = jnp.where(in_lat, i % tiles_lat == 0, (i - n_lat_tiles) % tiles_ctx == 0)
    last = jnp.where(in_lat, i % tiles_lat == tiles_lat - 1, (i - n_lat_tiles) % tiles_ctx == tiles_ctx - 1)
    x = x_ref[...]
    prev_row = jnp.where(first, 0.0, prev_ref[7:8, :])
    next_row = jnp.where(last, 0.0, next_ref[0:1, :])
    row = lax.broadcasted_iota(jnp.int32, (tm, 1), 0)
    x_prev = jnp.where(row == 0, prev_row, pltpu.roll(x, 1, 0))
    x_next = jnp.where(row == tm - 1, next_row, pltpu.roll(x, tm - 1, 0))
    cw = conv_ref[...]
    y = x_prev * cw[0:1, :] + x * cw[1:2, :] + x_next * cw[2:3, :]
    r = y[:, 0:GROUP_W]
    k = y[:, GROUP_W:2 * GROUP_W]
    v = y[:, 2 * GROUP_W:3 * GROUP_W]

    lr = lr_ref[...]
    lane = lax.broadcasted_iota(jnp.int32, (1, lr.shape[1]), 1)
    c1 = RW_DECAY_RANK
    c2 = c1 + RW_ICLR_RANK
    c3 = c2 + RW_GATE_RANK
    act = jnp.where(lane < c1, jnp.tanh(lr),
                    jnp.where(lane < c2, lr, jnp.where(lane < c3, _sigmoid(lr), 0.0)))
    up = _dot(act.astype(BF16), wlr_ref[...])

    ones_bd = ones_ref[...]
    kk = k * kk_ref[...]
    nrm = jnp.sqrt(_seg_sum(kk * kk, ones_bd))
    kk = kk / jnp.maximum(nrm, 1e-12)

    o_ref[:, PB_R * GROUP_W:(PB_R + 1) * GROUP_W] = r
    o_ref[:, PB_V * GROUP_W:(PB_V + 1) * GROUP_W] = v
    o_ref[:, PB_KK * GROUP_W:(PB_KK + 1) * GROUP_W] = kk
    o_ref[:, PB_G * GROUP_W:(PB_G + 1) * GROUP_W] = up[:, 4 * GROUP_W:5 * GROUP_W]
    ka = ka_ref[...]
    for d in range(2):
        z = w0_ref[d:d + 1, :] + up[:, d * GROUP_W:(d + 1) * GROUP_W]
        sp = jnp.maximum(-z, 0.0) + jnp.log(1.0 + jnp.exp(-jnp.abs(z)))
        lw = -jnp.exp(-sp - 0.5)
        a = _sigmoid(a0_ref[d:d + 1, :] + up[:, (2 + d) * GROUP_W:(3 + d) * GROUP_W])
        kd = k * (1.0 + (a - 1.0) * ka)
        base = 3 + 3 * d
        o_ref[:, base * GROUP_W:(base + 1) * GROUP_W] = lw
        o_ref[:, (base + 1) * GROUP_W:(base + 2) * GROUP_W] = a
        o_ref[:, (base + 2) * GROUP_W:(base + 3) * GROUP_W] = kd


def _rw_prep_call(proj, conv_w, wlr, w0, a0, k_k, k_a, ones512, n_batch, seq, ctx, tm):
    R = proj.shape[0]
    n_lat_tiles = n_batch * seq // tm
    nb8 = R // 8
    t8 = tm // 8
    wide = 3 * GROUP_W
    cb = COL_BR // wide
    return pl.pallas_call(
        functools.partial(_rw_prep_kernel, tiles_lat=seq // tm, tiles_ctx=ctx // tm, n_lat_tiles=n_lat_tiles),
        out_shape=jax.ShapeDtypeStruct((R, PB_N * GROUP_W), F32),
        grid=(R // tm,),
        in_specs=[
            pl.BlockSpec((tm, wide), lambda i: (i, cb)),
            pl.BlockSpec((8, wide), lambda i: (jnp.maximum(i * t8 - 1, 0), cb)),
            pl.BlockSpec((8, wide), lambda i: (jnp.minimum((i + 1) * t8, nb8 - 1), cb)),
            pl.BlockSpec((tm, 256), lambda i: (i, COL_BLR // 256)),
            pl.BlockSpec((3, wide), lambda i: (0, 0)),
            pl.BlockSpec((256, 5 * GROUP_W), lambda i: (0, 0)),
            pl.BlockSpec((2, GROUP_W), lambda i: (0, 0)),
            pl.BlockSpec((2, GROUP_W), lambda i: (0, 0)),
            pl.BlockSpec((1, GROUP_W), lambda i: (0, 0)),
            pl.BlockSpec((1, GROUP_W), lambda i: (0, 0)),
            pl.BlockSpec((GROUP_W, GROUP_W), lambda i: (0, 0)),
        ],
        out_specs=pl.BlockSpec((tm, PB_N * GROUP_W), lambda i: (i, 0)),
        compiler_params=_cparams(("arbitrary",)),
        name="mixer_b_prep",
    )(proj, proj, proj, proj, conv_w, wlr, w0, a0, k_k, k_a, ones512)


RW_SPLIT = 1
_NN = ((1,), (0,))
_NT = ((1,), (1,))
_TN = ((0,), (0,))


def _split(x, n):
    parts = []
    for _ in range(n):
        p = x.astype(BF16)
        parts.append(p)
        x = x - p.astype(F32)
    return parts


def _sdot(a, b, dims):
    n = max(len(a), len(b))
    acc = None
    for i, ai in enumerate(a):
        for j, bj in enumerate(b):
            if i + j < n:
                t = lax.dot_general(ai, bj, (dims, ((), ())), preferred_element_type=F32)
                acc = t if acc is None else acc + t
    return acc


def _rw_chunks(at, bt, kt, rt, v, gam, s0, strict, incl):
    n = len(at)
    ids = range(n)
    C = at[0].shape[0]
    sp = lambda t: _split(t, RW_SPLIT)
    b1 = lambda t: [t.astype(BF16)]
    ar = [b1(jnp.concatenate([at[i], rt[i]], axis=0)) for i in ids]
    bk = [sp(jnp.concatenate([bt[i], kt[i]], axis=0)) for i in ids]
    g4 = [_sdot(ar[i], bk[i][:1], _NT) for i in ids]
    a_ab = [jnp.where(strict[i], g4[i][:C, :C], 0.0) for i in ids]
    a_ak = [jnp.where(strict[i], g4[i][:C, C:], 0.0) for i in ids]
    a_rb = [jnp.where(incl[i], g4[i][C:, :C], 0.0) for i in ids]
    a_rk = [jnp.where(incl[i], g4[i][C:, C:], 0.0) for i in ids]
    wv = [_sdot(b1(a_ak[i]), b1(v[i]), _NN) for i in ids]
    x = [jnp.concatenate([at[i], wv[i]], axis=1) for i in ids]

    row = lax.broadcasted_iota(jnp.int32, (C, C), 0)
    col = lax.broadcasted_iota(jnp.int32, (C, C), 1)
    eye = (row == col).astype(F32)
    blk = 16
    same = (row // blk) == (col // blk)
    p = [jnp.where(same, a_ab[i], 0.0) for i in ids]
    m = [eye + p[i] for i in ids]
    for _ in range(int(math.log2(blk)) - 1):
        pb_ = [b1(p[i]) for i in ids]
        p = [_sdot(pb_[i], pb_[i], _NN) for i in ids]
        m = [m[i] + _sdot(b1(m[i]), b1(p[i]), _NN) for i in ids]
    while blk < C:
        wider = (row // (2 * blk)) == (col // (2 * blk))
        join = wider & jnp.logical_not(same)
        mb = [b1(m[i]) for i in ids]
        t = [_sdot(b1(jnp.where(join, a_ab[i], 0.0)), mb[i], _NN) for i in ids]
        m = [m[i] + _sdot(mb[i], b1(t[i]), _NN) for i in ids]
        same, blk = wider, 2 * blk
    x = [_sdot(b1(m[i]), b1(x[i]), _NN) for i in ids]

    qv = [sp(jnp.concatenate([x[i][:, HEAD_DIM:], v[i]], axis=0)) for i in ids]
    pb = [sp(x[i][:, :HEAD_DIM]) for i in ids]
    s0b = [sp(s0[i]) for i in ids]
    y_loc = [_sdot(b1(jnp.concatenate([a_rb[i], a_rk[i]], axis=1)), qv[i][:1], _NN) for i in ids]
    rp = [rt[i] + _sdot(b1(a_rb[i]), pb[i][:1], _NN) for i in ids]
    w = [_sdot(s0b[i], pb[i], _NT) for i in ids]
    y = [y_loc[i] + _sdot(sp(rp[i]), s0b[i], _NT) for i in ids]
    s1 = [(s0[i] + _sdot(sp(w[i]), sp(bt[i]), _NN) + _sdot(qv[i], bk[i], _TN)) * gam[i] for i in ids]
    return y, s1


def _rw_scan_kernel(*refs, n_batch, nc):
    n_in = 12 * n_batch
    yfl_ref, ybl_ref, yfc_ref, ybc_ref, s_sc = refs[n_in:]
    s = pl.program_id(0)

    @pl.when(s == 0)
    def _():
        s_sc[...] = jnp.zeros_like(s_sc)

    C = refs[0].shape[0]
    row = lax.broadcasted_iota(jnp.int32, (C, C), 0)
    col = lax.broadcasted_iota(jnp.int32, (C, C), 1)
    chains = dict(at=[], bt=[], kt=[], rt=[], v=[], gam=[], s0=[], strict=[], incl=[])
    for b in range(n_batch):
        for d, reverse in enumerate((False, True)):
            r_ref, v_ref, kk_ref, lw_ref, a_ref, k_ref = refs[12 * b + 6 * d:12 * b + 6 * d + 6]
            incl, strict = (row <= col, row < col) if reverse else (row >= col, row > col)
            lw = lw_ref[...]
            kk = kk_ref[...]
            v = v_ref[...]
            cum = jnp.dot(incl.astype(F32), lw, preferred_element_type=F32, precision=lax.Precision.HIGHEST)
            e_l = jnp.exp(cum)
            e_n = jnp.exp(-cum)
            at = -(kk * jnp.exp(cum - lw))
            bt = kk * a_ref[...] * e_n
            kt = k_ref[...] * e_n
            rt = r_ref[...] * e_l
            gam = e_l[0:1, :] if reverse else e_l[C - 1:C, :]
            for h in range(RW_HEADS):
                hs = slice(h * HEAD_DIM, (h + 1) * HEAD_DIM)
                for name, val in (("at", at), ("bt", bt), ("kt", kt), ("rt", rt), ("v", v), ("gam", gam)):
                    chains[name].append(val[:, hs])
                chains["s0"].append(s_sc[b, d, h])
                chains["strict"].append(strict)
                chains["incl"].append(incl)
    y, s1 = _rw_chunks(**chains)
    ys = {}
    for b in range(n_batch):
        for d in range(2):
            base = (b * 2 + d) * RW_HEADS
            for h in range(RW_HEADS):
                s_sc[b, d, h] = s1[base + h]
            ys[b, d] = jnp.concatenate(y[base:base + RW_HEADS], axis=-1)

    @pl.when(s < nc)
    def _():
        for b in range(n_batch):
            yfc_ref[b] = ys[b, 0]
            ybc_ref[b] = ys[b, 1]

    @pl.when(s >= nc)
    def _():
        for b in range(n_batch):
            yfl_ref[b] = ys[b, 0]
            ybl_ref[b] = ys[b, 1]


def _rw_scan_call(prep, n_batch, seq, ctx):
    C = RW_CHUNK
    nc, nl = ctx // C, seq // C
    ctx_base = n_batch * nl

    def blk_f(b, s):
        return jnp.where(s < nc, ctx_base + b * nc + s, b * nl + (s - nc))

    def blk_b(b, s):
        return jnp.where(s < nc, ctx_base + b * nc + (nc - 1 - s), b * nl + (nl - 1 - (s - nc)))

    def col(blk, b, cb):
        return pl.BlockSpec((C, GROUP_W), lambda s: (blk(b, s), cb))

    in_specs = []
    for b in range(n_batch):
        in_specs += [col(blk_f, b, c) for c in (PB_R, PB_V, PB_KK, 3, 4, 5)]
        in_specs += [col(blk_b, b, c) for c in (PB_R, PB_V, PB_KK, 6, 7, 8)]
    lat = jax.ShapeDtypeStruct((n_batch, seq, GROUP_W), F32)
    cx = jax.ShapeDtypeStruct((n_batch, ctx, GROUP_W), F32)
    blk3 = (n_batch, C, GROUP_W)
    out_specs = (
        pl.BlockSpec(blk3, lambda s: (0, jnp.maximum(s - nc, 0), 0)),
        pl.BlockSpec(blk3, lambda s: (0, nl - 1 - jnp.maximum(s - nc, 0), 0)),
        pl.BlockSpec(blk3, lambda s: (0, jnp.minimum(s, nc - 1), 0)),
        pl.BlockSpec(blk3, lambda s: (0, nc - 1 - jnp.minimum(s, nc - 1), 0)),
    )
    yfl, ybl, yfc, ybc = pl.pallas_call(
        functools.partial(_rw_scan_kernel, n_batch=n_batch, nc=nc),
        out_shape=(lat, lat, cx, cx),
        grid=(nc + nl,),
        in_specs=in_specs,
        out_specs=out_specs,
        scratch_shapes=[pltpu.VMEM((n_batch, 2, RW_HEADS, HEAD_DIM, HEAD_DIM), F32)],
        compiler_params=_cparams(("arbitrary",)),
        name="mixer_b_scan",
    )(*([prep] * (12 * n_batch)))
    flat = lambda a, c: jnp.concatenate([a.reshape(n_batch * seq, GROUP_W), c.reshape(n_batch * ctx, GROUP_W)], 0)
    return flat(yfl, yfc), flat(ybl, ybc)


def _rw_readout_kernel(r_ref, v_ref, k0_ref, k1_ref, g_ref, yf_ref, yb_ref, rk_ref, gw_ref, gb_ref, ones_ref,
                       o_ref):
    ones_bd = ones_ref[...]
    v = v_ref[...]
    bonus = _seg_sum(r_ref[...] * (k0_ref[...] + k1_ref[...]) * rk_ref[...], ones_bd)
    y = yf_ref[...] + yb_ref[...] + bonus * v
    mu = _seg_sum(y, ones_bd) * (1.0 / HEAD_DIM)
    yc = y - mu
    var = _seg_sum(yc * yc, ones_bd) * (1.0 / HEAD_DIM)
    yn = yc * lax.rsqrt(var + RW_GN_EPS)
    o_ref[...] = ((yn * gw_ref[...] + gb_ref[...]) * g_ref[...]).astype(o_ref.dtype)


def _rw_readout_call(prep, yf, yb, r_k, gn_w, gn_b, ones512, tm):
    R = prep.shape[0]
    col = lambda cb: pl.BlockSpec((tm, GROUP_W), lambda i: (i, cb))
    vec = pl.BlockSpec((1, GROUP_W), lambda i: (0, 0))
    return pl.pallas_call(
        _rw_readout_kernel,
        out_shape=jax.ShapeDtypeStruct((R, GROUP_W), BF16),
        grid=(R // tm,),
        in_specs=[col(PB_R), col(PB_V), col(5), col(8), col(PB_G), col(0), col(0), vec, vec, vec,
                  pl.BlockSpec((GROUP_W, GROUP_W), lambda i: (0, 0))],
        out_specs=col(0),
        compiler_params=_cparams(("arbitrary",)),
        name="mixer_b_readout",
    )(prep, prep, prep, prep, prep, yf, yb, r_k, gn_w, gn_b, ones512)


def _s5_tables(a_re, a_im, log_dt, b_re, b_im, c_re, c_im):
    Lc, G, N, P = S5_CHUNK, S5_GROUPS, S5_N, S5_P
    dt = jnp.exp(log_dt)[..., None]
    lam_re, lam_im = dt * a_re, dt * a_im
    tau = jnp.arange(Lc + 1, dtype=F32)[:, None, None, None]
    mag = jnp.exp(tau * lam_re)
    pw_re, pw_im = mag * jnp.cos(tau * lam_im), mag * jnp.sin(tau * lam_im)
    ab_re, ab_im = pw_re[1], pw_im[1]
    den = a_re * a_re + a_im * a_im
    nr = ab_re - 1.0
    cf_re, cf_im = (nr * a_re + ab_im * a_im) / den, (ab_im * a_re - nr * a_im) / den
    bp_re = cf_re[..., None] * b_re[None] - cf_im[..., None] * b_im[None]
    bp_im = cf_re[..., None] * b_im[None] + cf_im[..., None] * b_re[None]
    pb_re = pw_re[..., None] * bp_re[None] - pw_im[..., None] * bp_im[None]
    pb_im = pw_re[..., None] * bp_im[None] + pw_im[..., None] * bp_re[None]
    kk = (jnp.einsum('gqn,tdgnp->tdgqp', c_re, pb_re[:Lc]) - jnp.einsum('gqn,tdgnp->tdgqp', c_im, pb_im[:Lc]))
    jj = np.arange(Lc)[:, None]
    ii = np.arange(Lc)[None, :]
    dist = np.abs(ii - jj)
    kf = kk[dist, 0] * jnp.asarray(ii >= jj, F32)[..., None, None, None]
    kb = kk[dist, 1] * jnp.asarray(ii <= jj, F32)[..., None, None, None]
    tz = jnp.transpose(kf + kb, (2, 0, 4, 1, 3))
    jr = np.arange(Lc)
    emap = lambda pbx, order, d: jnp.transpose(pbx[order, d], (1, 0, 3, 2))
    em = jnp.stack([emap(pb_re, Lc - 1 - jr, 0), emap(pb_im, Lc - 1 - jr, 0),
                    emap(pb_re, jr, 1), emap(pb_im, jr, 1)], axis=3)

    def gmap(order, d):
        pr, pi = pw_re[order, d], pw_im[order, d]
        cp_re = c_re[None] * pr[:, :, None, :] - c_im[None] * pi[:, :, None, :]
        cp_im = c_re[None] * pi[:, :, None, :] + c_im[None] * pr[:, :, None, :]
        to = lambda t: jnp.transpose(t, (1, 3, 0, 2))
        return to(cp_re), to(-cp_im)

    gk = jnp.stack(gmap(jr + 1, 0) + gmap(Lc - jr, 1), axis=1)
    NO, NQ, NG = S5_OCTETS, 2, 4
    eye8, eye2, eye4 = (jnp.eye(k, dtype=F32) for k in (8, NQ, NG))
    tz6 = jnp.transpose(tz.reshape(NO, 8, Lc, P, Lc, P), (0, 2, 1, 3, 4, 5))
    wy = tz6[:, :, :, :, :, None, :] * eye8[None, None, :, None, None, :, None]
    wy = wy.reshape(NO, Lc * LANE, Lc * LANE)
    em7 = jnp.transpose(em.reshape(NO, NQ, NG, Lc, P, 4, N), (0, 3, 1, 2, 4, 5, 6))
    we = (em7[:, :, :, :, :, None, :, None, :] * eye2[None, None, :, None, None, :, None, None, None]
          * eye4[None, None, None, :, None, None, None, :, None])
    we = we.reshape(NO, Lc * LANE, NQ * 4 * NG * N)
    wz = jnp.concatenate([wy, we], axis=-1)
    gk7 = jnp.transpose(gk.reshape(NO, NQ, NG, 4, N, Lc, P), (0, 1, 3, 2, 4, 5, 6))
    gm = (gk7[:, :, :, :, :, :, None, None, :] * eye2[None, :, None, None, None, None, :, None, None]
          * eye4[None, None, None, :, None, None, None, :, None])
    gm = gm.reshape(NO, NQ * 4 * NG * N, Lc * LANE)
    apow = jnp.stack([pw_re[Lc, 0], pw_im[Lc, 0], pw_re[Lc, 1], pw_im[Lc, 1]], axis=1)
    apow = jnp.transpose(apow.reshape(G // NG, NG, 4, N), (0, 2, 1, 3)).reshape(G // NG, 4, NG * N)
    return wz, gm, apow


def _s5_local_kernel(u_ref, wz_ref, y_ref, e_ref):
    Lc = u_ref.shape[1]
    lhs = jnp.concatenate([u_ref[:, j, :] for j in range(Lc)], axis=-1).astype(BF16)
    z = _dot(lhs, wz_ref[...])
    for i in range(Lc):
        y_ref[:, i, :] = z[:, i * LANE:(i + 1) * LANE]
    e_ref[...] = z[:, Lc * LANE:]


def _s5_local_call(proj3, wz):
    NR, Lc, _ = proj3.shape
    NO, K, N = wz.shape
    NE = N - Lc * LANE
    tr = NR // 8
    cu = COL_DU // LANE
    return pl.pallas_call(
        _s5_local_kernel,
        out_shape=(jax.ShapeDtypeStruct((NR, Lc, GROUP_W), F32), jax.ShapeDtypeStruct((NR, NO * NE), F32)),
        grid=(NO, NR // tr),
        in_specs=[
            pl.BlockSpec((tr, Lc, LANE), lambda o, i: (i, 0, cu + o)),
            pl.BlockSpec((None, K, N), lambda o, i: (o, 0, 0)),
        ],
        out_specs=(pl.BlockSpec((tr, Lc, LANE), lambda o, i: (i, 0, o)),
                   pl.BlockSpec((tr, NE), lambda o, i: (i, o))),
        compiler_params=_cparams(("arbitrary", "arbitrary")),
        name="mixer_d_local",
    )(proj3, wz)


def _s5_scan_kernel(e_ref, ap_ref, x_ref, *, n_batch, nlc, ncc):
    NS = e_ref.shape[1] // 4
    af_re, af_im = ap_ref[0:1, :], ap_ref[1:2, :]
    ab_re, ab_im = ap_ref[2:3, :], ap_ref[3:4, :]

    def step(a_re, a_im, x_re, x_im, e):
        return a_re * x_re - a_im * x_im + e[:, :NS], a_re * x_im + a_im * x_re + e[:, NS:]

    def phase(row0, n_chunks, stride, carry):
        n_tiles = n_chunks // 8

        def body(t, carry):
            out = []
            for b in range(n_batch):
                xf_re, xf_im, xb_re, xb_im = carry[4 * b:4 * b + 4]
                rf = pl.ds(pl.multiple_of(row0 + b * stride + t * 8, 8), 8)
                rb = pl.ds(pl.multiple_of(row0 + b * stride + (n_tiles - 1 - t) * 8, 8), 8)
                ef = e_ref[rf, 0:2 * NS]
                eb = e_ref[rb, 2 * NS:4 * NS]
                xf_in, xb_in = [], [None] * 8
                for j in range(8):
                    xf_in.append(jnp.concatenate([xf_re, xf_im], axis=-1))
                    xf_re, xf_im = step(af_re, af_im, xf_re, xf_im, ef[j:j + 1, :])
                for j in reversed(range(8)):
                    xb_in[j] = jnp.concatenate([xb_re, xb_im], axis=-1)
                    xb_re, xb_im = step(ab_re, ab_im, xb_re, xb_im, eb[j:j + 1, :])
                x_ref[rf, 0:2 * NS] = jnp.concatenate(xf_in, axis=0)
                x_ref[rb, 2 * NS:4 * NS] = jnp.concatenate(xb_in, axis=0)
                out += [xf_re, xf_im, xb_re, xb_im]
            return tuple(out)

        return lax.fori_loop(0, n_tiles, body, carry)

    zero = jnp.zeros((1, NS), F32)
    carry = phase(n_batch * nlc, ncc, ncc, (zero,) * (4 * n_batch))
    phase(0, nlc, nlc, carry)


def _s5_scan_call(e, apow, n_batch, seq, ctx):
    NR, NEall = e.shape
    NQ8, _, NS = apow.shape
    return pl.pallas_call(
        functools.partial(_s5_scan_kernel, n_batch=n_batch, nlc=seq // S5_CHUNK, ncc=ctx // S5_CHUNK),
        out_shape=jax.ShapeDtypeStruct((NR, NEall), F32),
        grid=(NQ8,),
        in_specs=[
            pl.BlockSpec((NR, 4 * NS), lambda q: (0, q)),
            pl.BlockSpec((None, 4, NS), lambda q: (q, 0, 0)),
        ],
        out_specs=pl.BlockSpec((NR, 4 * NS), lambda q: (0, q)),
        compiler_params=_cparams(("arbitrary",)),
        name="mixer_d_scan",
    )(e, apow)


def _s5_carry_kernel(x_ref, gm_ref, yl_ref, y_ref):
    Lc = yl_ref.shape[1]
    y = _dot(x_ref[...].astype(BF16), gm_ref[...])
    for i in range(Lc):
        y_ref[:, i, :] = yl_ref[:, i, :] + y[:, i * LANE:(i + 1) * LANE]


def _s5_carry_call(xin, gm, yloc):
    NR, Lc, _ = yloc.shape
    NO, NE, N = gm.shape
    tr = NR // 8
    return pl.pallas_call(
        _s5_carry_kernel,
        out_shape=jax.ShapeDtypeStruct((NR, Lc, GROUP_W), F32),
        grid=(NO, NR // tr),
        in_specs=[
            pl.BlockSpec((tr, NE), lambda o, i: (i, o)),
            pl.BlockSpec((None, NE, N), lambda o, i: (o, 0, 0)),
            pl.BlockSpec((tr, Lc, LANE), lambda o, i: (i, 0, o)),
        ],
        out_specs=pl.BlockSpec((tr, Lc, LANE), lambda o, i: (i, 0, o)),
        compiler_params=_cparams(("arbitrary", "arbitrary")),
        name="mixer_d_carry",
    )(xin, gm, yloc)


def _s5_out_kernel(y_ref, u_ref, d_ref, w_ref, b_ref, o_ref):
    y = y_ref[...] + d_ref[...] * u_ref[...]
    c = math.sqrt(2.0 / math.pi)
    y = 0.5 * y * (1.0 + jnp.tanh(c * (y + 0.044715 * (y * y * y))))
    z = _dot(y.astype(BF16), w_ref[...]) + b_ref[...]
    o_ref[...] = (y * _sigmoid(z)).astype(o_ref.dtype)


def _s5_out_call(y_tok, proj, d_skip, glu_w, glu_b, tm):
    R = proj.shape[0]
    vec = pl.BlockSpec((1, GROUP_W), lambda i: (0, 0))
    return pl.pallas_call(
        _s5_out_kernel,
        out_shape=jax.ShapeDtypeStruct((R, GROUP_W), BF16),
        grid=(R // tm,),
        in_specs=[
            pl.BlockSpec((tm, GROUP_W), lambda i: (i, 0)),
            pl.BlockSpec((tm, GROUP_W), lambda i: (i, COL_DU // GROUP_W)),
            vec,
            pl.BlockSpec((GROUP_W, GROUP_W), lambda i: (0, 0)),
            vec,
        ],
        out_specs=pl.BlockSpec((tm, GROUP_W), lambda i: (i, 0)),
        compiler_params=_cparams(("arbitrary",)),
        name="mixer_d_out",
    )(y_tok, proj, d_skip, glu_w, glu_b)


def _rope_tables(n_tokens):
    t = jnp.arange(n_tokens)
    nf = HEAD_DIM // 4
    inv = 1.0 / (ROPE_BASE ** (jnp.arange(nf, dtype=F32) / nf))

    def ang(pp):
        a = pp.astype(F32)[:, None] * inv[None, :]
        return jnp.concatenate([a, a], -1)

    a = jnp.concatenate([ang(t // GRID_W), ang(t % GRID_W)], -1)
    cos, sin = jnp.cos(a), jnp.sin(a)
    sign = np.where((np.arange(HEAD_DIM) % 32) < 16, -1.0, 1.0).astype(np.float32)
    cos2 = jnp.concatenate([cos, cos], -1)
    sin2 = jnp.concatenate([sin * sign, sin * sign], -1)
    return cos2, sin2


def _block_ones(n):
    return jnp.asarray(np.kron(np.eye(n // HEAD_DIM), np.ones((HEAD_DIM, HEAD_DIM))), BF16)


def _permute_w_in(w_in):
    cuts = np.cumsum([512, 512, 512, 512, 512, 512, 32, 32, 96, 512, 128, 128, 512])
    seg = lambda i: w_in[..., (0 if i == 0 else cuts[i - 1]):cuts[i]]
    L, D = w_in.shape[:2]
    pad = jnp.zeros((L, D, 96), w_in.dtype)
    parts = [seg(0), seg(1), seg(2), seg(3), seg(4), seg(5), seg(9), seg(12),
             seg(6), seg(7), seg(8), pad, seg(10), seg(11)]
    return jnp.concatenate(parts, axis=-1).astype(BF16)


def _forward(x, c, ctx, c_ctx, w_ada, b_ada, norm_g, ffn1_wg, ffn1_wu, ffn1_wd, ffn2_wg, ffn2_wu, ffn2_wd,
             w_in, w_out, na_q_g, na_k_g, na_rpb, rw_conv, rw_w0, rw_w_up, rw_a0, rw_a_up, rw_g_up,
             rw_k_k, rw_k_a, rw_r_k, rw_gn_w, rw_gn_b, wa_q_g, wa_k_g, wa_sink, s5_a_re, s5_a_im,
             s5_log_dt, s5_b_re, s5_b_im, s5_c_re, s5_c_im, s5_d, s5_glu_w, s5_glu_b, *, tm, tf):
    B, SEQ, D = x.shape
    CTX = ctx.shape[1]
    L = w_ada.shape[0]
    n_lat = B * SEQ
    n_qb = SEQ // QBLK

    bf = lambda t: t.astype(BF16)
    f1g, f1u, f1d, f2g, f2u, f2d = map(bf, (ffn1_wg, ffn1_wu, ffn1_wd, ffn2_wg, ffn2_wu, ffn2_wd))
    w_in_p = _permute_w_in(w_in)
    w_out_b = bf(w_out)
    glu_w_b = bf(s5_glu_w)
    ones128, ones512 = _block_ones(LANE), _block_ones(GROUP_W)
    cos_t, sin_t = _rope_tables(SEQ)
    na_bias = [_na_bias_table(na_rpb[l], n_qb) for l in range(L)]
    zr = lambda r, cdim: jnp.zeros((L, r, cdim), F32)
    wlr = jnp.concatenate([
        jnp.concatenate([rw_w_up[:, 0], rw_w_up[:, 1], zr(32, 3 * GROUP_W)], axis=-1),
        jnp.concatenate([zr(32, 2 * GROUP_W), rw_a_up[:, 0], rw_a_up[:, 1], zr(32, GROUP_W)], axis=-1),
        jnp.concatenate([zr(96, 4 * GROUP_W), rw_g_up], axis=-1),
        zr(96, 5 * GROUP_W)], axis=1).astype(BF16)
    s5_tabs = jax.vmap(_s5_tables)(s5_a_re, s5_a_im, s5_log_dt, s5_b_re, s5_b_im, s5_c_re, s5_c_im)
    tile2 = lambda t: jnp.concatenate([t, t], axis=-1)

    cc = jnp.concatenate([c, c_ctx[None], jnp.zeros((8 - B - 1, D), F32)], axis=0)
    mods_all = _ada_call(cc, w_ada, b_ada)[:, :B + 1].reshape(L, B + 1, N_MOD, D)

    xs = jnp.concatenate([x.reshape(n_lat, D), ctx.reshape(B * CTX, D)], axis=0)
    R = xs.shape[0]
    for l in range(L):
        want_ctx = l < L - 1
        mods = mods_all[l]
        ng = norm_g[l]
        xs = _ffn_call(xs, mods, ng[0:1], f1g, f1u, f1d, l, 0, R, SEQ, B, tm, tf)
        proj = _win_call(xs, mods, ng[1:2], w_in_p, l, SEQ, B, tm, D_IN_PAD // 3)
        o_a = _na_call(proj, na_bias[l], tile2(na_q_g[l][None]), tile2(na_k_g[l][None]), ones128,
                       B, SEQ, CTX, want_ctx)
        prep = _rw_prep_call(proj, rw_conv[l], wlr[l], rw_w0[l], rw_a0[l], rw_k_k[l][None], rw_k_a[l][None],
                             ones512, B, SEQ, CTX, min(tm, 256))
        vec = lambda t: t.reshape(1, GROUP_W)
        yf, yb = _rw_scan_call(prep, B, SEQ, CTX)
        o_b = _rw_readout_call(prep, yf, yb, vec(rw_r_k[l]), vec(rw_gn_w[l]), vec(rw_gn_b[l]), ones512, tm)
        o_c = _wa_call(proj, wa_sink[l], cos_t, sin_t, tile2(wa_q_g[l][None]), tile2(wa_k_g[l][None]), ones128,
                       B, SEQ, CTX, want_ctx)
        wz, gm, apow = (t[l] for t in s5_tabs)
        yloc, e_loc = _s5_local_call(proj.reshape(R // S5_CHUNK, S5_CHUNK, D_IN_PAD), bf(wz))
        xin = _s5_scan_call(e_loc, apow, B, SEQ, CTX)
        y_tok = _s5_carry_call(xin, bf(gm), yloc).reshape(R, GROUP_W)
        o_d = _s5_out_call(y_tok, proj, vec(s5_d[l]), glu_w_b[l], vec(s5_glu_b[l]), tm)
        n_rows = R if want_ctx else n_lat
        xs = _wout_call(xs, (o_a, o_b, o_c, o_d), mods, w_out_b, l, n_rows, SEQ, B, tm)
        xs = _ffn_call(xs, mods, ng[2:3], f2g, f2u, f2d, l, 6, n_rows, SEQ, B, tm, tf)
    return xs[:n_lat].reshape(B, SEQ, D)


def kernel(x, c, ctx, c_ctx, w_ada, b_ada, norm_g, ffn1_wg, ffn1_wu, ffn1_wd, ffn2_wg, ffn2_wu, ffn2_wd, w_in, w_out, na_q_g, na_k_g, na_rpb, rw_conv, rw_w0, rw_w_up, rw_a0, rw_a_up, rw_g_up, rw_k_k, rw_k_a, rw_r_k, rw_gn_w, rw_gn_b, wa_q_g, wa_k_g, wa_sink, s5_a_re, s5_a_im, s5_log_dt, s5_b_re, s5_b_im, s5_c_re, s5_c_im, s5_d, s5_glu_w, s5_glu_b):
    return _forward(x, c, ctx, c_ctx, w_ada, b_ada, norm_g, ffn1_wg, ffn1_wu, ffn1_wd, ffn2_wg, ffn2_wu, ffn2_wd,
                    w_in, w_out, na_q_g, na_k_g, na_rpb, rw_conv, rw_w0, rw_w_up, rw_a0, rw_a_up, rw_g_up,
                    rw_k_k, rw_k_a, rw_r_k, rw_gn_w, rw_gn_b, wa_q_g, wa_k_g, wa_sink, s5_a_re, s5_a_im,
                    s5_log_dt, s5_b_re, s5_b_im, s5_c_re, s5_c_im, s5_d, s5_glu_w, s5_glu_b, tm=512, tf=512)
```

```python
import functools
import math

import numpy as np
import jax
import jax.numpy as jnp
from jax import lax
from jax.experimental import pallas as pl
from jax.experimental.pallas import tpu as pltpu

F32 = jnp.float32
BF16 = jnp.bfloat16

D_MODEL = 2048
GRID_W = 64
HEAD_DIM = 64
GROUP_W = D_MODEL // 4
N_MOD = 9
NORM_EPS = 1e-6
ROPE_BASE = 10000.0
NEG_INF = -1e30

NA_HEADS = GROUP_W // HEAD_DIM
NA_WIN_R = 8
NA_WIN_C = 16
RW_HEADS = GROUP_W // HEAD_DIM
RW_DECAY_RANK = 32
RW_ICLR_RANK = 32
RW_GATE_RANK = 96
RW_GN_EPS = 64e-5
WA_HEADS = GROUP_W // HEAD_DIM
WA_KV_HEADS = 2
WA_GROUP = WA_HEADS // WA_KV_HEADS
WA_WINDOW = 128
S5_P = 16
S5_GROUPS = GROUP_W // S5_P
S5_N = 64

LANE = 128
QBLK = 2 * GRID_W
NA_KBLKS = 5
RW_CHUNK = 64
S5_CHUNK = 8
S5_OCTETS = 4
VMEM_LIMIT = 56 * 1024 * 1024

COL_AQ, COL_AK, COL_AV = 0, 512, 1024
COL_BR = 1536
COL_CQ = 3072
COL_DU = 3584
COL_BLR = 4096
COL_CK, COL_CV = 4352, 4480
D_IN_PAD = 4608

PB_R, PB_V, PB_KK, PB_G = 0, 1, 2, 9
PB_N = 10


def _cparams(sem):
    return pltpu.CompilerParams(dimension_semantics=sem, vmem_limit_bytes=VMEM_LIMIT)


def _dot(a, b):
    return jnp.dot(a, b, preferred_element_type=F32)


def _dot_nt(a, b):
    return lax.dot_general(a, b, (((1,), (1,)), ((), ())), preferred_element_type=F32)


def _dot_tn(a, b):
    return lax.dot_general(a, b, (((0,), (0,)), ((), ())), preferred_element_type=F32)


def _sigmoid(x):
    return 1.0 / (1.0 + jnp.exp(-x))


def _modulate(x, g, shift, scale):
    ms = jnp.mean(x * x, axis=-1, keepdims=True)
    return (x * lax.rsqrt(ms + NORM_EPS) * g) * (1.0 + scale) + shift


def _seg_sum(x, ones_bd):
    hi = x.astype(BF16)
    r1 = x - hi.astype(F32)
    mid = r1.astype(BF16)
    lo = (r1 - mid.astype(F32)).astype(BF16)
    return _dot(hi, ones_bd) + _dot(mid, ones_bd) + _dot(lo, ones_bd)


def _head_rmsnorm(x, ones_bd):
    ms = _seg_sum(x * x, ones_bd) * (1.0 / HEAD_DIM)
    return x * lax.rsqrt(ms + NORM_EPS)


def _ada_kernel(c_ref, w_ref, b_ref, o_ref):
    c = c_ref[...]
    s = (c * _sigmoid(c)).astype(BF16)
    o_ref[...] = _dot(s, w_ref[...].astype(BF16)) + b_ref[...]


def _ada_call(cc, w_ada, b_ada):
    L, D, N = w_ada.shape
    tn = 1024
    return pl.pallas_call(
        _ada_kernel,
        out_shape=jax.ShapeDtypeStruct((L, 8, N), F32),
        grid=(L, N // tn),
        in_specs=[
            pl.BlockSpec((8, D), lambda l, j: (0, 0)),
            pl.BlockSpec((None, D, tn), lambda l, j: (l, 0, j)),
            pl.BlockSpec((None, 1, tn), lambda l, j: (l, 0, j)),
        ],
        out_specs=pl.BlockSpec((None, 8, tn), lambda l, j: (l, 0, j)),
        compiler_params=_cparams(("arbitrary", "arbitrary")),
        name="adaln",
    )(cc, w_ada, b_ada.reshape(L, 1, N))


def _ffn_kernel(x_ref, mod_ref, g_ref, wg_ref, wu_ref, wd_ref, o_ref, h_sc, acc_sc, *, mi):
    j = pl.program_id(1)

    @pl.when(j == 0)
    def _():
        h = _modulate(x_ref[...], g_ref[...], mod_ref[mi:mi + 1, :], mod_ref[mi + 1:mi + 2, :])
        h_sc[...] = h.astype(BF16)
        acc_sc[...] = jnp.zeros_like(acc_sc)

    h = h_sc[...]
    gate = _dot(h, wg_ref[...])
    up = _dot(h, wu_ref[...])
    a = (gate * _sigmoid(gate) * up).astype(BF16)
    acc_sc[...] += _dot(a, wd_ref[...])

    @pl.when(j == pl.num_programs(1) - 1)
    def _():
        o_ref[...] = x_ref[...] + 0.5 * mod_ref[mi + 2:mi + 3, :] * acc_sc[...]


def _ffn_call(xs, mods, g, wg, wu, wd, l, mi, n_rows, rows_per_seq, n_batch, tm, tf):
    D = xs.shape[1]
    F = wg.shape[2]
    tps = rows_per_seq // tm
    return pl.pallas_call(
        functools.partial(_ffn_kernel, mi=mi),
        out_shape=jax.ShapeDtypeStruct((n_rows, D), F32),
        grid=(n_rows // tm, F // tf),
        in_specs=[
            pl.BlockSpec((tm, D), lambda i, j: (i, 0)),
            pl.BlockSpec((None, N_MOD, D), lambda i, j: (jnp.minimum(i // tps, n_batch), 0, 0)),
            pl.BlockSpec((1, D), lambda i, j: (0, 0)),
            pl.BlockSpec((None, D, tf), lambda i, j: (l, 0, j)),
            pl.BlockSpec((None, D, tf), lambda i, j: (l, 0, j)),
            pl.BlockSpec((None, tf, D), lambda i, j: (l, j, 0)),
        ],
        out_specs=pl.BlockSpec((tm, D), lambda i, j: (i, 0)),
        scratch_shapes=[pltpu.VMEM((tm, D), BF16), pltpu.VMEM((tm, D), F32)],
        compiler_params=_cparams(("arbitrary", "arbitrary")),
        name="ffn",
    )(xs, mods, g, wg, wu, wd)


def _win_kernel(x_ref, mod_ref, g_ref, w_ref, o_ref, h_sc):
    @pl.when(pl.program_id(1) == 0)
    def _():
        h = _modulate(x_ref[...], g_ref[...], mod_ref[3:4, :], mod_ref[4:5, :])
        h_sc[...] = h.astype(BF16)

    o_ref[...] = _dot(h_sc[...], w_ref[...])


def _win_call(xs, mods, g, w_in, l, rows_per_seq, n_batch, tm, tn):
    R, D = xs.shape
    N = w_in.shape[2]
    tps = rows_per_seq // tm
    return pl.pallas_call(
        _win_kernel,
        out_shape=jax.ShapeDtypeStruct((R, N), F32),
        grid=(R // tm, N // tn),
        in_specs=[
            pl.BlockSpec((tm, D), lambda i, j: (i, 0)),
            pl.BlockSpec((None, N_MOD, D), lambda i, j: (jnp.minimum(i // tps, n_batch), 0, 0)),
            pl.BlockSpec((1, D), lambda i, j: (0, 0)),
            pl.BlockSpec((None, D, tn), lambda i, j: (l, 0, j)),
        ],
        out_specs=pl.BlockSpec((tm, tn), lambda i, j: (i, j)),
        scratch_shapes=[pltpu.VMEM((tm, D), BF16)],
        compiler_params=_cparams(("arbitrary", "arbitrary")),
        name="in_proj",
    )(xs, mods, g, w_in)


def _wout_kernel(x_ref, oa_ref, ob_ref, oc_ref, od_ref, mod_ref, w_ref, o_ref):
    acc = _dot(oa_ref[...], w_ref[0 * GROUP_W:1 * GROUP_W, :])
    acc += _dot(ob_ref[...], w_ref[1 * GROUP_W:2 * GROUP_W, :])
    acc += _dot(oc_ref[...], w_ref[2 * GROUP_W:3 * GROUP_W, :])
    acc += _dot(od_ref[...], w_ref[3 * GROUP_W:4 * GROUP_W, :])
    o_ref[...] = x_ref[...] + mod_ref[5:6, :] * acc


def _wout_call(xs, outs, mods, w_out, l, n_rows, rows_per_seq, n_batch, tm):
    D = xs.shape[1]
    tps = rows_per_seq // tm
    ospec = pl.BlockSpec((tm, GROUP_W), lambda i: (i, 0))
    return pl.pallas_call(
        _wout_kernel,
        out_shape=jax.ShapeDtypeStruct((n_rows, D), F32),
        grid=(n_rows // tm,),
        in_specs=[
            pl.BlockSpec((tm, D), lambda i: (i, 0)),
            ospec, ospec, ospec, ospec,
            pl.BlockSpec((None, N_MOD, D), lambda i: (jnp.minimum(i // tps, n_batch), 0, 0)),
            pl.BlockSpec((None, D, D), lambda i: (l, 0, 0)),
        ],
        out_specs=pl.BlockSpec((tm, D), lambda i: (i, 0)),
        compiler_params=_cparams(("arbitrary",)),
        name="out_proj",
    )(xs, *outs, mods, w_out)


def _na_case_reps(n_qb):
    return (0, 1, 2, n_qb - 2, n_qb - 1)


def _na_start(p, n_qb):
    return jnp.clip(p - 2, 0, n_qb - NA_KBLKS)


def _na_bias_table(rpb, n_qb):
    H = rpb.shape[0]
    rows = 2 * n_qb
    qc = np.arange(GRID_W)[:, None]
    kc = np.arange(GRID_W)[None, :]
    cs = np.clip(qc - NA_WIN_C // 2, 0, GRID_W - NA_WIN_C)
    okc = (kc >= cs) & (kc < cs + NA_WIN_C)
    dc = np.where(okc, kc - qc + NA_WIN_C - 1, 0)
    rp = rpb.astype(F32).reshape(H, 2 * NA_WIN_R - 1, 2 * NA_WIN_C - 1)
    blocks = jnp.where(jnp.asarray(okc)[None, None], jnp.take(rp, jnp.asarray(dc), axis=2), NEG_INF)
    blocks = blocks.astype(BF16)
    neg = jnp.full((H, GRID_W, GRID_W), NEG_INF, BF16)
    cases = []
    for p in _na_case_reps(n_qb):
        start = min(max(p - 2, 0), n_qb - NA_KBLKS)
        qrows = []
        for qr in range(2):
            qa = 2 * p + qr
            rs = min(max(qa - NA_WIN_R // 2, 0), rows - NA_WIN_R)
            krow = []
            for kr in range(2 * NA_KBLKS):
                ka = 2 * start + kr
                krow.append(blocks[:, ka - qa + NA_WIN_R - 1] if rs <= ka < rs + NA_WIN_R else neg)
            qrows.append(jnp.concatenate(krow, axis=-1))
        cases.append(jnp.concatenate(qrows, axis=-2))
    return jnp.stack(cases, axis=0)


def _na_kernel(q_ref, k_ref, v_ref, kc_ref, vc_ref, bias_ref, qg_ref, kg_ref, ones_ref, o_ref,
               kn_sc, vn_sc, kcn_sc, vcn_sc, *, n_qb, seq, ctx):
    p = pl.program_id(2)
    ones_bd = ones_ref[...]
    scale = HEAD_DIM ** -0.5

    @pl.when(p == 0)
    def _():
        kg = kg_ref[...]
        rows = 512

        def body(i, carry):
            sl = pl.ds(pl.multiple_of(i * rows, rows), rows)
            kn_sc[sl, :] = (_head_rmsnorm(k_ref[sl, :], ones_bd) * kg).astype(BF16)
            vn_sc[sl, :] = v_ref[sl, :].astype(BF16)
            return carry

        lax.fori_loop(0, seq // rows, body, 0)
        kcn_sc[...] = (_head_rmsnorm(kc_ref[...], ones_bd) * kg).astype(BF16)
        vcn_sc[...] = vc_ref[...].astype(BF16)

    nq = q_ref.shape[0] // QBLK
    n_lat = n_qb // nq
    q = (_head_rmsnorm(q_ref[...], ones_bd) * (qg_ref[...] * scale)).astype(BF16)
    kcn = kcn_sc[...]
    vcn = vcn_sc[...]
    chains = [(qb, h) for qb in range(nq) for h in range(2)]
    hsl = lambda h: slice(h * HEAD_DIM, (h + 1) * HEAD_DIM)
    qh = [q[qb * QBLK:(qb + 1) * QBLK, hsl(h)] for qb, h in chains]
    s_c = [_dot_nt(qh[c], kcn[:, hsl(h)]) for c, (qb, h) in enumerate(chains)]
    m_c = [jnp.max(t, axis=-1, keepdims=True) for t in s_c]

    def finish(o, den):
        rows = [jnp.concatenate([o[qb * 2 + h] / den[qb * 2 + h] for h in range(2)], axis=-1) for qb in range(nq)]
        o_ref[...] = jnp.concatenate(rows, axis=0).astype(o_ref.dtype)

    @pl.when(p < n_lat)
    def _():
        starts = [_na_start(p * nq + qb, n_qb) for qb in range(nq)]
        cases = [p * nq + qb - starts[qb] for qb in range(nq)]
        sls = [pl.ds(pl.multiple_of(st * QBLK, QBLK), NA_KBLKS * QBLK) for st in starts]
        kw = [kn_sc[sl, :] for sl in sls]
        vw = [vn_sc[sl, :] for sl in sls]
        s_n = [_dot_nt(qh[c], kw[qb][:, hsl(h)]) + bias_ref[cases[qb], h].astype(F32)
               for c, (qb, h) in enumerate(chains)]
        m = [jnp.maximum(jnp.max(s_n[c], axis=-1, keepdims=True), m_c[c]) for c in range(len(chains))]
        p_n = [jnp.exp(s_n[c] - m[c]) for c in range(len(chains))]
        p_c = [jnp.exp(s_c[c] - m[c]) for c in range(len(chains))]
        den = [jnp.sum(p_n[c], axis=-1, keepdims=True) + jnp.sum(p_c[c], axis=-1, keepdims=True)
               for c in range(len(chains))]
        o = [_dot(p_n[c].astype(BF16), vw[qb][:, hsl(h)]) + _dot(p_c[c].astype(BF16), vcn[:, hsl(h)])
             for c, (qb, h) in enumerate(chains)]
        finish(o, den)

    @pl.when(p >= n_lat)
    def _():
        p_c = [jnp.exp(s_c[c] - m_c[c]) for c in range(len(chains))]
        den = [jnp.sum(t, axis=-1, keepdims=True) for t in p_c]
        o = [_dot(p_c[c].astype(BF16), vcn[:, hsl(h)]) for c, (qb, h) in enumerate(chains)]
        finish(o, den)


def _na_call(proj, bias, qg, kg, ones_bd, n_batch, seq, ctx, want_ctx):
    R = proj.shape[0]
    nq = 2
    n_qb = seq // QBLK
    n_lat, n_ctx = n_qb // nq, ctx // (nq * QBLK)
    steps = n_lat + (n_ctx if want_ctx else 0)
    ctx_kb = n_batch * seq // ctx

    def qrow(b, p):
        return jnp.where(p < n_lat, b * n_lat + p, n_batch * n_lat + b * n_ctx + (p - n_lat))

    cq, ck, cv = COL_AQ // LANE, COL_AK // LANE, COL_AV // LANE
    return pl.pallas_call(
        functools.partial(_na_kernel, n_qb=n_qb, seq=seq, ctx=ctx),
        out_shape=jax.ShapeDtypeStruct((R, GROUP_W), BF16),
        grid=(n_batch, NA_HEADS // 2, steps),
        in_specs=[
            pl.BlockSpec((nq * QBLK, LANE), lambda b, hp, p: (qrow(b, p), cq + hp)),
            pl.BlockSpec((seq, LANE), lambda b, hp, p: (b, ck + hp)),
            pl.BlockSpec((seq, LANE), lambda b, hp, p: (b, cv + hp)),
            pl.BlockSpec((ctx, LANE), lambda b, hp, p: (ctx_kb + b, ck + hp)),
            pl.BlockSpec((ctx, LANE), lambda b, hp, p: (ctx_kb + b, cv + hp)),
            pl.BlockSpec((5, 2, QBLK, NA_KBLKS * QBLK), lambda b, hp, p: (0, hp, 0, 0)),
            pl.BlockSpec((1, LANE), lambda b, hp, p: (0, 0)),
            pl.BlockSpec((1, LANE), lambda b, hp, p: (0, 0)),
            pl.BlockSpec((LANE, LANE), lambda b, hp, p: (0, 0)),
        ],
        out_specs=pl.BlockSpec((nq * QBLK, LANE), lambda b, hp, p: (qrow(b, p), hp)),
        scratch_shapes=[pltpu.VMEM((seq, LANE), BF16), pltpu.VMEM((seq, LANE), BF16),
                        pltpu.VMEM((ctx, LANE), BF16), pltpu.VMEM((ctx, LANE), BF16)],
        compiler_params=_cparams(("arbitrary", "arbitrary", "arbitrary")),
        name="mixer_a",
    )(proj, proj, proj, proj, proj, bias, qg, kg, ones_bd)


def _rope(x, cos, sin_signed, first_half):
    rot = jnp.where(first_half, pltpu.roll(x, LANE - 16, 1), pltpu.roll(x, 16, 1))
    return x * cos + rot * sin_signed


def _wa_kernel(sink_ref, q_ref, k_ref, v_ref, kc_ref, vc_ref, cosk_ref, sin_k_ref, cosq_ref, sinq_ref,
               qg_ref, kg_ref, ones_ref, wmask_ref, o_ref, kn_sc, vn_sc, kcn_sc, vcn_sc, *, n_qb, seq, ctx):
    p = pl.program_id(1)
    ones_bd = ones_ref[...]
    scale = HEAD_DIM ** -0.5
    lane = lax.broadcasted_iota(jnp.int32, (1, LANE), 1)
    first_half = (lane % 32) < 16

    @pl.when(p == 0)
    def _():
        kg = kg_ref[...]
        rows = 512

        def body(i, carry):
            sl = pl.ds(pl.multiple_of(i * rows, rows), rows)
            kn = _head_rmsnorm(k_ref[sl, :], ones_bd) * kg
            kn_sc[sl, :] = _rope(kn, cosk_ref[sl, :], sin_k_ref[sl, :], first_half).astype(BF16)
            vn_sc[sl, :] = v_ref[sl, :].astype(BF16)
            return carry

        lax.fori_loop(0, seq // rows, body, 0)
        kcn_sc[...] = (_head_rmsnorm(kc_ref[...], ones_bd) * kg).astype(BF16)
        vcn_sc[...] = vc_ref[...].astype(BF16)

    nq = q_ref.shape[0] // QBLK
    n_lat = n_qb // nq
    qg = qg_ref[...] * scale
    is_lat = p < n_lat
    cosq = jnp.where(is_lat, cosq_ref[...], 1.0)
    sinq = jnp.where(is_lat, sinq_ref[...], 0.0)
    qh = []
    for c in range(4):
        qn = _head_rmsnorm(q_ref[:, c * LANE:(c + 1) * LANE], ones_bd) * qg
        qn = _rope(qn, cosq, sinq, first_half).astype(BF16)
        qh += [qn[:, :HEAD_DIM], qn[:, HEAD_DIM:]]
    kcn = kcn_sc[...]
    vcn = vcn_sc[...]
    chains = [(qb, kh) for qb in range(nq) for kh in range(WA_KV_HEADS)]
    nch = range(len(chains))
    hsl = lambda kh: slice(kh * HEAD_DIM, (kh + 1) * HEAD_DIM)
    qs = [jnp.concatenate([qh[kh * WA_GROUP + g][qb * QBLK:(qb + 1) * QBLK, :] for g in range(WA_GROUP)], axis=0)
          for qb, kh in chains]
    sk = [jnp.concatenate([jnp.full((QBLK, 1), sink_ref[kh * WA_GROUP + g], F32) for g in range(WA_GROUP)],
                          axis=0) for qb, kh in chains]
    s_c = [_dot_nt(qs[c], kcn[:, hsl(chains[c][1])]) for c in nch]
    m_c = [jnp.maximum(jnp.max(s_c[c], axis=-1, keepdims=True), sk[c]) for c in nch]

    def finish(o, den):
        rows = []
        for qb in range(nq):
            heads = []
            for kh in range(WA_KV_HEADS):
                c = qb * WA_KV_HEADS + kh
                oc = o[c] / den[c]
                heads += [oc[g * QBLK:(g + 1) * QBLK, :] for g in range(WA_GROUP)]
            rows.append(jnp.concatenate(heads, axis=-1))
        o_ref[...] = jnp.concatenate(rows, axis=0).astype(o_ref.dtype)

    @pl.when(is_lat)
    def _():
        blk = [p * nq + qb for qb in range(nq)]
        ws = [jnp.clip((bq - 1) * QBLK, 0, seq - 3 * QBLK) for bq in blk]
        sls = [pl.ds(pl.multiple_of(w, QBLK), 3 * QBLK) for w in ws]
        kw = [kn_sc[sl, :] for sl in sls]
        vw = [vn_sc[sl, :] for sl in sls]
        case = [jnp.where(bq == 0, 0, jnp.where(bq == n_qb - 1, 2, 1)) for bq in blk]
        msk = [jnp.concatenate([wmask_ref[cs]] * WA_GROUP, axis=0) for cs in case]
        s_w = [_dot_nt(qs[c], kw[chains[c][0]][:, hsl(chains[c][1])]) + msk[chains[c][0]] for c in nch]
        m = [jnp.maximum(jnp.max(s_w[c], axis=-1, keepdims=True), m_c[c]) for c in nch]
        p_w = [jnp.exp(s_w[c] - m[c]) for c in nch]
        p_c = [jnp.exp(s_c[c] - m[c]) for c in nch]
        den = [jnp.sum(p_w[c], axis=-1, keepdims=True) + jnp.sum(p_c[c], axis=-1, keepdims=True)
               + jnp.exp(sk[c] - m[c]) for c in nch]
        o = [_dot(p_w[c].astype(BF16), vw[chains[c][0]][:, hsl(chains[c][1])])
             + _dot(p_c[c].astype(BF16), vcn[:, hsl(chains[c][1])]) for c in nch]
        finish(o, den)

    @pl.when(jnp.logical_not(is_lat))
    def _():
        p_c = [jnp.exp(s_c[c] - m_c[c]) for c in nch]
        den = [jnp.sum(p_c[c], axis=-1, keepdims=True) + jnp.exp(sk[c] - m_c[c]) for c in nch]
        o = [_dot(p_c[c].astype(BF16), vcn[:, hsl(chains[c][1])]) for c in nch]
        finish(o, den)


def _wa_mask_table():
    i = np.arange(QBLK)[:, None]
    j = np.arange(3 * QBLK)[None, :]
    shifts = (0, QBLK, 2 * QBLK)
    return jnp.asarray(np.stack([np.where(np.abs(j - sh - i) <= WA_WINDOW, 0.0, NEG_INF) for sh in shifts]), F32)


def _wa_call(proj, sink, cos_t, sin_t, qg, kg, ones_bd, n_batch, seq, ctx, want_ctx):
    R = proj.shape[0]
    nq = 2
    n_qb = seq // QBLK
    n_lat, n_ctx = n_qb // nq, ctx // (nq * QBLK)
    steps = n_lat + (n_ctx if want_ctx else 0)
    ctx_kb = n_batch * seq // ctx

    def qrow(b, p):
        return jnp.where(p < n_lat, b * n_lat + p, n_batch * n_lat + b * n_ctx + (p - n_lat))

    ck, cv = COL_CK // LANE, COL_CV // LANE
    return pl.pallas_call(
        functools.partial(_wa_kernel, n_qb=n_qb, seq=seq, ctx=ctx),
        out_shape=jax.ShapeDtypeStruct((R, GROUP_W), BF16),
        grid=(n_batch, steps),
        in_specs=[
            pl.BlockSpec(memory_space=pltpu.SMEM),
            pl.BlockSpec((nq * QBLK, GROUP_W), lambda b, p: (qrow(b, p), COL_CQ // GROUP_W)),
            pl.BlockSpec((seq, LANE), lambda b, p: (b, ck)),
            pl.BlockSpec((seq, LANE), lambda b, p: (b, cv)),
            pl.BlockSpec((ctx, LANE), lambda b, p: (ctx_kb + b, ck)),
            pl.BlockSpec((ctx, LANE), lambda b, p: (ctx_kb + b, cv)),
            pl.BlockSpec((seq, LANE), lambda b, p: (0, 0)),
            pl.BlockSpec((seq, LANE), lambda b, p: (0, 0)),
            pl.BlockSpec((nq * QBLK, LANE), lambda b, p: (jnp.minimum(p, n_lat - 1), 0)),
            pl.BlockSpec((nq * QBLK, LANE), lambda b, p: (jnp.minimum(p, n_lat - 1), 0)),
            pl.BlockSpec((1, LANE), lambda b, p: (0, 0)),
            pl.BlockSpec((1, LANE), lambda b, p: (0, 0)),
            pl.BlockSpec((LANE, LANE), lambda b, p: (0, 0)),
            pl.BlockSpec((3, QBLK, 3 * QBLK), lambda b, p: (0, 0, 0)),
        ],
        out_specs=pl.BlockSpec((nq * QBLK, GROUP_W), lambda b, p: (qrow(b, p), 0)),
        scratch_shapes=[pltpu.VMEM((seq, LANE), BF16), pltpu.VMEM((seq, LANE), BF16),
                        pltpu.VMEM((ctx, LANE), BF16), pltpu.VMEM((ctx, LANE), BF16)],
        compiler_params=_cparams(("arbitrary", "arbitrary")),
        name="mixer_c",
    )(sink, proj, proj, proj, proj, proj, cos_t, sin_t, cos_t, sin_t, qg, kg, ones_bd, _wa_mask_table())


def _rw_prep_kernel(x_ref, prev_ref, next_ref, lr_ref, conv_ref, wlr_ref, w0_ref, a0_ref, kk_ref, ka_ref,
                    ones_ref, o_ref, *, tiles_lat, tiles_ctx, n_lat_tiles):
    i = pl.program_id(0)
    tm = x_ref.shape[0]
    in_lat = i < n_lat_tiles
    first = jnp.where(in_lat, i % tiles_lat == 0, (i - n_lat_tiles) % tiles_ctx == 0)
    last = jnp.where(in_lat, i % tiles_lat == tiles_lat - 1, (i - n_lat_tiles) % tiles_ctx == tiles_ctx - 1)
    x = x_ref[...]
    prev_row = jnp.where(first, 0.0, prev_ref[7:8, :])
    next_row = jnp.where(last, 0.0, next_ref[0:1, :])
    row = lax.broadcasted_iota(jnp.int32, (tm, 1), 0)
    x_prev = jnp.where(row == 0, prev_row, pltpu.roll(x, 1, 0))
    x_next = jnp.where(row == tm - 1, next_row, pltpu.roll(x, tm - 1, 0))
    cw = conv_ref[...]
    y = x_prev * cw[0:1, :] + x * cw[1:2, :] + x_next * cw[2:3, :]
    r = y[:, 0:GROUP_W]
    k = y[:, GROUP_W:2 * GROUP_W]
    v = y[:, 2 * GROUP_W:3 * GROUP_W]

    lr = lr_ref[...]
    lane = lax.broadcasted_iota(jnp.int32, (1, lr.shape[1]), 1)
    c1 = RW_DECAY_RANK
    c2 = c1 + RW_ICLR_RANK
    c3 = c2 + RW_GATE_RANK
    act = jnp.where(lane < c1, jnp.tanh(lr),
                    jnp.where(lane < c2, lr, jnp.where(lane < c3, _sigmoid(lr), 0.0)))
    up = _dot(act.astype(BF16), wlr_ref[...])

    ones_bd = ones_ref[...]
    kk = k * kk_ref[...]
    nrm = jnp.sqrt(_seg_sum(kk * kk, ones_bd))
    kk = kk / jnp.maximum(nrm, 1e-12)

    o_ref[:, PB_R * GROUP_W:(PB_R + 1) * GROUP_W] = r
    o_ref[:, PB_V * GROUP_W:(PB_V + 1) * GROUP_W] = v
    o_ref[:, PB_KK * GROUP_W:(PB_KK + 1) * GROUP_W] = kk
    o_ref[:, PB_G * GROUP_W:(PB_G + 1) * GROUP_W] = up[:, 4 * GROUP_W:5 * GROUP_W]
    ka = ka_ref[...]
    for d in range(2):
        z = w0_ref[d:d + 1, :] + up[:, d * GROUP_W:(d + 1) * GROUP_W]
        sp = jnp.maximum(-z, 0.0) + jnp.log(1.0 + jnp.exp(-jnp.abs(z)))
        lw = -jnp.exp(-sp - 0.5)
        a = _sigmoid(a0_ref[d:d + 1, :] + up[:, (2 + d) * GROUP_W:(3 + d) * GROUP_W])
        kd = k * (1.0 + (a - 1.0) * ka)
        base = 3 + 3 * d
        o_ref[:, base * GROUP_W:(base + 1) * GROUP_W] = lw
        o_ref[:, (base + 1) * GROUP_W:(base + 2) * GROUP_W] = a
        o_ref[:, (base + 2) * GROUP_W:(base + 3) * GROUP_W] = kd


def _rw_prep_call(proj, conv_w, wlr, w0, a0, k_k, k_a, ones512, n_batch, seq, ctx, tm):
    R = proj.shape[0]
    n_lat_tiles = n_batch * seq // tm
    nb8 = R // 8
    t8 = tm // 8
    wide = 3 * GROUP_W
    cb = COL_BR // wide
    return pl.pallas_call(
        functools.partial(_rw_prep_kernel, tiles_lat=seq // tm, tiles_ctx=ctx // tm, n_lat_tiles=n_lat_tiles),
        out_shape=jax.ShapeDtypeStruct((R, PB_N * GROUP_W), F32),
        grid=(R // tm,),
        in_specs=[
            pl.BlockSpec((tm, wide), lambda i: (i, cb)),
            pl.BlockSpec((8, wide), lambda i: (jnp.maximum(i * t8 - 1, 0), cb)),
            pl.BlockSpec((8, wide), lambda i: (jnp.minimum((i + 1) * t8, nb8 - 1), cb)),
            pl.BlockSpec((tm, 256), lambda i: (i, COL_BLR // 256)),
            pl.BlockSpec((3, wide), lambda i: (0, 0)),
            pl.BlockSpec((256, 5 * GROUP_W), lambda i: (0, 0)),
            pl.BlockSpec((2, GROUP_W), lambda i: (0, 0)),
            pl.BlockSpec((2, GROUP_W), lambda i: (0, 0)),
            pl.BlockSpec((1, GROUP_W), lambda i: (0, 0)),
            pl.BlockSpec((1, GROUP_W), lambda i: (0, 0)),
            pl.BlockSpec((GROUP_W, GROUP_W), lambda i: (0, 0)),
        ],
        out_specs=pl.BlockSpec((tm, PB_N * GROUP_W), lambda i: (i, 0)),
        compiler_params=_cparams(("arbitrary",)),
        name="mixer_b_prep",
    )(proj, proj, proj, proj, conv_w, wlr, w0, a0, k_k, k_a, ones512)


RW_SPLIT = 1
_NN = ((1,), (0,))
_NT = ((1,), (1,))
_TN = ((0,), (0,))


def _split(x, n):
    parts = []
    for _ in range(n):
        p = x.astype(BF16)
        parts.append(p)
        x = x - p.astype(F32)
    return parts


def _sdot(a, b, dims):
    n = max(len(a), len(b))
    acc = None
    for i, ai in enumerate(a):
        for j, bj in enumerate(b):
            if i + j < n:
                t = lax.dot_general(ai, bj, (dims, ((), ())), preferred_element_type=F32)
                acc = t if acc is None else acc + t
    return acc


def _rw_chunks(at, bt, kt, rt, v, gam, s0, strict, incl):
    n = len(at)
    ids = range(n)
    C = at[0].shape[0]
    sp = lambda t: _split(t, RW_SPLIT)
    b1 = lambda t: [t.astype(BF16)]
    ar = [b1(jnp.concatenate([at[i], rt[i]], axis=0)) for i in ids]
    bk = [sp(jnp.concatenate([bt[i], kt[i]], axis=0)) for i in ids]
    g4 = [_sdot(ar[i], bk[i][:1], _NT) for i in ids]
    a_ab = [jnp.where(strict[i], g4[i][:C, :C], 0.0) for i in ids]
    a_ak = [jnp.where(strict[i], g4[i][:C, C:], 0.0) for i in ids]
    a_rb = [jnp.where(incl[i], g4[i][C:, :C], 0.0) for i in ids]
    a_rk = [jnp.where(incl[i], g4[i][C:, C:], 0.0) for i in ids]
    wv = [_sdot(b1(a_ak[i]), b1(v[i]), _NN) for i in ids]
    x = [jnp.concatenate([at[i], wv[i]], axis=1) for i in ids]

    row = lax.broadcasted_iota(jnp.int32, (C, C), 0)
    col = lax.broadcasted_iota(jnp.int32, (C, C), 1)
    eye = (row == col).astype(F32)
    blk = 16
    same = (row // blk) == (col // blk)
    p = [jnp.where(same, a_ab[i], 0.0) for i in ids]
    m = [eye + p[i] for i in ids]
    for _ in range(int(math.log2(blk)) - 1):
        pb_ = [b1(p[i]) for i in ids]
        p = [_sdot(pb_[i], pb_[i], _NN) for i in ids]
        m = [m[i] + _sdot(b1(m[i]), b1(p[i]), _NN) for i in ids]
    while blk < C:
        wider = (row // (2 * blk)) == (col // (2 * blk))
        join = wider & jnp.logical_not(same)
        mb = [b1(m[i]) for i in ids]
        t = [_sdot(b1(jnp.where(join, a_ab[i], 0.0)), mb[i], _NN) for i in ids]
        m = [m[i] + _sdot(mb[i], b1(t[i]), _NN) for i in ids]
        same, blk = wider, 2 * blk
    x = [_sdot(b1(m[i]), b1(x[i]), _NN) for i in ids]

    qv = [sp(jnp.concatenate([x[i][:, HEAD_DIM:], v[i]], axis=0)) for i in ids]
    pb = [sp(x[i][:, :HEAD_DIM]) for i in ids]
    s0b = [sp(s0[i]) for i in ids]
    y_loc = [_sdot(b1(jnp.concatenate([a_rb[i], a_rk[i]], axis=1)), qv[i][:1], _NN) for i in ids]
    rp = [rt[i] + _sdot(b1(a_rb[i]), pb[i][:1], _NN) for i in ids]
    w = [_sdot(s0b[i], pb[i], _NT) for i in ids]
    y = [y_loc[i] + _sdot(sp(rp[i]), s0b[i], _NT) for i in ids]
    s1 = [(s0[i] + _sdot(sp(w[i]), sp(bt[i]), _NN) + _sdot(qv[i], bk[i], _TN)) * gam[i] for i in ids]
    return y, s1


def _rw_scan_kernel(*refs, n_batch, nc):
    n_in = 12 * n_batch
    yfl_ref, ybl_ref, yfc_ref, ybc_ref, s_sc = refs[n_in:]
    s = pl.program_id(0)

    @pl.when(s == 0)
    def _():
        s_sc[...] = jnp.zeros_like(s_sc)

    C = refs[0].shape[0]
    row = lax.broadcasted_iota(jnp.int32, (C, C), 0)
    col = lax.broadcasted_iota(jnp.int32, (C, C), 1)
    chains = dict(at=[], bt=[], kt=[], rt=[], v=[], gam=[], s0=[], strict=[], incl=[])
    for b in range(n_batch):
        for d, reverse in enumerate((False, True)):
            r_ref, v_ref, kk_ref, lw_ref, a_ref, k_ref = refs[12 * b + 6 * d:12 * b + 6 * d + 6]
            incl, strict = (row <= col, row < col) if reverse else (row >= col, row > col)
            lw = lw_ref[...]
            kk = kk_ref[...]
            v = v_ref[...]
            cum = jnp.dot(incl.astype(F32), lw, preferred_element_type=F32, precision=lax.Precision.HIGHEST)
            e_l = jnp.exp(cum)
            e_n = jnp.exp(-cum)
            at = -(kk * jnp.exp(cum - lw))
            bt = kk * a_ref[...] * e_n
            kt = k_ref[...] * e_n
            rt = r_ref[...] * e_l
            gam = e_l[0:1, :] if reverse else e_l[C - 1:C, :]
            for h in range(RW_HEADS):
                hs = slice(h * HEAD_DIM, (h + 1) * HEAD_DIM)
                for name, val in (("at", at), ("bt", bt), ("kt", kt), ("rt", rt), ("v", v), ("gam", gam)):
                    chains[name].append(val[:, hs])
                chains["s0"].append(s_sc[b, d, h])
                chains["strict"].append(strict)
                chains["incl"].append(incl)
    y, s1 = _rw_chunks(**chains)
    ys = {}
    for b in range(n_batch):
        for d in range(2):
            base = (b * 2 + d) * RW_HEADS
            for h in range(RW_HEADS):
                s_sc[b, d, h] = s1[base + h]
            ys[b, d] = jnp.concatenate(y[base:base + RW_HEADS], axis=-1)

    @pl.when(s < nc)
    def _():
        for b in range(n_batch):
            yfc_ref[b] = ys[b, 0]
            ybc_ref[b] = ys[b, 1]

    @pl.when(s >= nc)
    def _():
        for b in range(n_batch):
            yfl_ref[b] = ys[b, 0]
            ybl_ref[b] = ys[b, 1]


def _rw_scan_call(prep, n_batch, seq, ctx):
    C = RW_CHUNK
    nc, nl = ctx // C, seq // C
    ctx_base = n_batch * nl

    def blk_f(b, s):
        return jnp.where(s < nc, ctx_base + b * nc + s, b * nl + (s - nc))

    def blk_b(b, s):
        return jnp.where(s < nc, ctx_base + b * nc + (nc - 1 - s), b * nl + (nl - 1 - (s - nc)))

    def col(blk, b, cb):
        return pl.BlockSpec((C, GROUP_W), lambda s: (blk(b, s), cb))

    in_specs = []
    for b in range(n_batch):
        in_specs += [col(blk_f, b, c) for c in (PB_R, PB_V, PB_KK, 3, 4, 5)]
        in_specs += [col(blk_b, b, c) for c in (PB_R, PB_V, PB_KK, 6, 7, 8)]
    lat = jax.ShapeDtypeStruct((n_batch, seq, GROUP_W), F32)
    cx = jax.ShapeDtypeStruct((n_batch, ctx, GROUP_W), F32)
    blk3 = (n_batch, C, GROUP_W)
    out_specs = (
        pl.BlockSpec(blk3, lambda s: (0, jnp.maximum(s - nc, 0), 0)),
        pl.BlockSpec(blk3, lambda s: (0, nl - 1 - jnp.maximum(s - nc, 0), 0)),
        pl.BlockSpec(blk3, lambda s: (0, jnp.minimum(s, nc - 1), 0)),
        pl.BlockSpec(blk3, lambda s: (0, nc - 1 - jnp.minimum(s, nc - 1), 0)),
    )
    yfl, ybl, yfc, ybc = pl.pallas_call(
        functools.partial(_rw_scan_kernel, n_batch=n_batch, nc=nc),
        out_shape=(lat, lat, cx, cx),
        grid=(nc + nl,),
        in_specs=in_specs,
        out_specs=out_specs,
        scratch_shapes=[pltpu.VMEM((n_batch, 2, RW_HEADS, HEAD_DIM, HEAD_DIM), F32)],
        compiler_params=_cparams(("arbitrary",)),
        name="mixer_b_scan",
    )(*([prep] * (12 * n_batch)))
    flat = lambda a, c: jnp.concatenate([a.reshape(n_batch * seq, GROUP_W), c.reshape(n_batch * ctx, GROUP_W)], 0)
    return flat(yfl, yfc), flat(ybl, ybc)


def _rw_readout_kernel(r_ref, v_ref, k0_ref, k1_ref, g_ref, yf_ref, yb_ref, rk_ref, gw_ref, gb_ref, ones_ref,
                       o_ref):
    ones_bd = ones_ref[...]
    v = v_ref[...]
    bonus = _seg_sum(r_ref[...] * (k0_ref[...] + k1_ref[...]) * rk_ref[...], ones_bd)
    y = yf_ref[...] + yb_ref[...] + bonus * v
    mu = _seg_sum(y, ones_bd) * (1.0 / HEAD_DIM)
    yc = y - mu
    var = _seg_sum(yc * yc, ones_bd) * (1.0 / HEAD_DIM)
    yn = yc * lax.rsqrt(var + RW_GN_EPS)
    o_ref[...] = ((yn * gw_ref[...] + gb_ref[...]) * g_ref[...]).astype(o_ref.dtype)


def _rw_readout_call(prep, yf, yb, r_k, gn_w, gn_b, ones512, tm):
    R = prep.shape[0]
    col = lambda cb: pl.BlockSpec((tm, GROUP_W), lambda i: (i, cb))
    vec = pl.BlockSpec((1, GROUP_W), lambda i: (0, 0))
    return pl.pallas_call(
        _rw_readout_kernel,
        out_shape=jax.ShapeDtypeStruct((R, GROUP_W), BF16),
        grid=(R // tm,),
        in_specs=[col(PB_R), col(PB_V), col(5), col(8), col(PB_G), col(0), col(0), vec, vec, vec,
                  pl.BlockSpec((GROUP_W, GROUP_W), lambda i: (0, 0))],
        out_specs=col(0),
        compiler_params=_cparams(("arbitrary",)),
        name="mixer_b_readout",
    )(prep, prep, prep, prep, prep, yf, yb, r_k, gn_w, gn_b, ones512)


def _s5_tables(a_re, a_im, log_dt, b_re, b_im, c_re, c_im):
    Lc, G, N, P = S5_CHUNK, S5_GROUPS, S5_N, S5_P
    dt = jnp.exp(log_dt)[..., None]
    lam_re, lam_im = dt * a_re, dt * a_im
    tau = jnp.arange(Lc + 1, dtype=F32)[:, None, None, None]
    mag = jnp.exp(tau * lam_re)
    pw_re, pw_im = mag * jnp.cos(tau * lam_im), mag * jnp.sin(tau * lam_im)
    ab_re, ab_im = pw_re[1], pw_im[1]
    den = a_re * a_re + a_im * a_im
    nr = ab_re - 1.0
    cf_re, cf_im = (nr * a_re + ab_im * a_im) / den, (ab_im * a_re - nr * a_im) / den
    bp_re = cf_re[..., None] * b_re[None] - cf_im[..., None] * b_im[None]
    bp_im = cf_re[..., None] * b_im[None] + cf_im[..., None] * b_re[None]
    pb_re = pw_re[..., None] * bp_re[None] - pw_im[..., None] * bp_im[None]
    pb_im = pw_re[..., None] * bp_im[None] + pw_im[..., None] * bp_re[None]
    kk = (jnp.einsum('gqn,tdgnp->tdgqp', c_re, pb_re[:Lc]) - jnp.einsum('gqn,tdgnp->tdgqp', c_im, pb_im[:Lc]))
    jj = np.arange(Lc)[:, None]
    ii = np.arange(Lc)[None, :]
    dist = np.abs(ii - jj)
    kf = kk[dist, 0] * jnp.asarray(ii >= jj, F32)[..., None, None, None]
    kb = kk[dist, 1] * jnp.asarray(ii <= jj, F32)[..., None, None, None]
    tz = jnp.transpose(kf + kb, (2, 0, 4, 1, 3))
    jr = np.arange(Lc)
    emap = lambda pbx, order, d: jnp.transpose(pbx[order, d], (1, 0, 3, 2))
    em = jnp.stack([emap(pb_re, Lc - 1 - jr, 0), emap(pb_im, Lc - 1 - jr, 0),
                    emap(pb_re, jr, 1), emap(pb_im, jr, 1)], axis=3)

    def gmap(order, d):
        pr, pi = pw_re[order, d], pw_im[order, d]
        cp_re = c_re[None] * pr[:, :, None, :] - c_im[None] * pi[:, :, None, :]
        cp_im = c_re[None] * pi[:, :, None, :] + c_im[None] * pr[:, :, None, :]
        to = lambda t: jnp.transpose(t, (1, 3, 0, 2))
        return to(cp_re), to(-cp_im)

    gk = jnp.stack(gmap(jr + 1, 0) + gmap(Lc - jr, 1), axis=1)
    NO, NQ, NG = S5_OCTETS, 2, 4
    eye8, eye2, eye4 = (jnp.eye(k, dtype=BF16) for k in (8, NQ, NG))
    tz, em, gk = tz.astype(BF16), em.astype(BF16), gk.astype(BF16)
    tz6 = jnp.transpose(tz.reshape(NO, 8, Lc, P, Lc, P), (0, 2, 1, 3, 4, 5))
    wy = tz6[:, :, :, :, :, None, :] * eye8[None, None, :, None, None, :, None]
    wy = wy.reshape(NO, Lc * LANE, Lc * LANE)
    em7 = jnp.transpose(em.reshape(NO, NQ, NG, Lc, P, 4, N), (0, 3, 1, 2, 4, 5, 6))
    we = (em7[:, :, :, :, :, None, :, None, :] * eye2[None, None, :, None, None, :, None, None, None]
          * eye4[None, None, None, :, None, None, None, :, None])
    we = we.reshape(NO, Lc * LANE, NQ * 4 * NG * N)
    wz = jnp.concatenate([wy, we], axis=-1)
    gk7 = jnp.transpose(gk.reshape(NO, NQ, NG, 4, N, Lc, P), (0, 1, 3, 2, 4, 5, 6))
    gm = (gk7[:, :, :, :, :, :, None, None, :] * eye2[None, :, None, None, None, None, :, None, None]
          * eye4[None, None, None, :, None, None, None, :, None])
    gm = gm.reshape(NO, NQ * 4 * NG * N, Lc * LANE)
    apow = jnp.stack([pw_re[Lc, 0], pw_im[Lc, 0], pw_re[Lc, 1], pw_im[Lc, 1]], axis=1)
    apow = jnp.transpose(apow.reshape(G // NG, NG, 4, N), (0, 2, 1, 3)).reshape(G // NG, 4, NG * N)
    return wz, gm, apow


def _s5_local_kernel(u_ref, wz_ref, y_ref, e_ref):
    Lc = u_ref.shape[1]
    lhs = jnp.concatenate([u_ref[:, j, :] for j in range(Lc)], axis=-1).astype(BF16)
    z = _dot(lhs, wz_ref[...])
    for i in range(Lc):
        y_ref[:, i, :] = z[:, i * LANE:(i + 1) * LANE]
    e_ref[...] = z[:, Lc * LANE:]


def _s5_local_call(proj3, wz):
    NR, Lc, _ = proj3.shape
    NO, K, N = wz.shape
    NE = N - Lc * LANE
    tr = NR // 8
    cu = COL_DU // LANE
    return pl.pallas_call(
        _s5_local_kernel,
        out_shape=(jax.ShapeDtypeStruct((NR, Lc, GROUP_W), F32), jax.ShapeDtypeStruct((NR, NO * NE), F32)),
        grid=(NO, NR // tr),
        in_specs=[
            pl.BlockSpec((tr, Lc, LANE), lambda o, i: (i, 0, cu + o)),
            pl.BlockSpec((None, K, N), lambda o, i: (o, 0, 0)),
        ],
        out_specs=(pl.BlockSpec((tr, Lc, LANE), lambda o, i: (i, 0, o)),
                   pl.BlockSpec((tr, NE), lambda o, i: (i, o))),
        compiler_params=_cparams(("arbitrary", "arbitrary")),
        name="mixer_d_local",
    )(proj3, wz)


def _s5_scan_kernel(e_ref, ap_ref, x_ref, *, n_batch, nlc, ncc):
    NS = e_ref.shape[1] // 4
    af_re, af_im = ap_ref[0:1, :], ap_ref[1:2, :]
    ab_re, ab_im = ap_ref[2:3, :], ap_ref[3:4, :]

    def step(a_re, a_im, x_re, x_im, e):
        return a_re * x_re - a_im * x_im + e[:, :NS], a_re * x_im + a_im * x_re + e[:, NS:]

    def phase(row0, n_chunks, stride, carry):
        n_tiles = n_chunks // 8

        def body(t, carry):
            out = []
            for b in range(n_batch):
                xf_re, xf_im, xb_re, xb_im = carry[4 * b:4 * b + 4]
                rf = pl.ds(pl.multiple_of(row0 + b * stride + t * 8, 8), 8)
                rb = pl.ds(pl.multiple_of(row0 + b * stride + (n_tiles - 1 - t) * 8, 8), 8)
                ef = e_ref[rf, 0:2 * NS]
                eb = e_ref[rb, 2 * NS:4 * NS]
                xf_in, xb_in = [], [None] * 8
                for j in range(8):
                    xf_in.append(jnp.concatenate([xf_re, xf_im], axis=-1))
                    xf_re, xf_im = step(af_re, af_im, xf_re, xf_im, ef[j:j + 1, :])
                for j in reversed(range(8)):
                    xb_in[j] = jnp.concatenate([xb_re, xb_im], axis=-1)
                    xb_re, xb_im = step(ab_re, ab_im, xb_re, xb_im, eb[j:j + 1, :])
                x_ref[rf, 0:2 * NS] = jnp.concatenate(xf_in, axis=0)
                x_ref[rb, 2 * NS:4 * NS] = jnp.concatenate(xb_in, axis=0)
                out += [xf_re, xf_im, xb_re, xb_im]
            return tuple(out)

        return lax.fori_loop(0, n_tiles, body, carry)

    zero = jnp.zeros((1, NS), F32)
    carry = phase(n_batch * nlc, ncc, ncc, (zero,) * (4 * n_batch))
    phase(0, nlc, nlc, carry)


def _s5_scan_call(e, apow, n_batch, seq, ctx):
    NR, NEall = e.shape
    NQ8, _, NS = apow.shape
    return pl.pallas_call(
        functools.partial(_s5_scan_kernel, n_batch=n_batch, nlc=seq // S5_CHUNK, ncc=ctx // S5_CHUNK),
        out_shape=jax.ShapeDtypeStruct((NR, NEall), F32),
        grid=(NQ8,),
        in_specs=[
            pl.BlockSpec((NR, 4 * NS), lambda q: (0, q)),
            pl.BlockSpec((None, 4, NS), lambda q: (q, 0, 0)),
        ],
        out_specs=pl.BlockSpec((NR, 4 * NS), lambda q: (0, q)),
        compiler_params=_cparams(("arbitrary",)),
        name="mixer_d_scan",
    )(e, apow)


def _s5_carry_kernel(x_ref, gm_ref, yl_ref, y_ref):
    Lc = yl_ref.shape[1]
    y = _dot(x_ref[...].astype(BF16), gm_ref[...])
    for i in range(Lc):
        y_ref[:, i, :] = yl_ref[:, i, :] + y[:, i * LANE:(i + 1) * LANE]


def _s5_carry_call(xin, gm, yloc):
    NR, Lc, _ = yloc.shape
    NO, NE, N = gm.shape
    tr = NR // 8
    return pl.pallas_call(
        _s5_carry_kernel,
        out_shape=jax.ShapeDtypeStruct((NR, Lc, GROUP_W), F32),
        grid=(NO, NR // tr),
        in_specs=[
            pl.BlockSpec((tr, NE), lambda o, i: (i, o)),
            pl.BlockSpec((None, NE, N), lambda o, i: (o, 0, 0)),
            pl.BlockSpec((tr, Lc, LANE), lambda o, i: (i, 0, o)),
        ],
        out_specs=pl.BlockSpec((tr, Lc, LANE), lambda o, i: (i, 0, o)),
        compiler_params=_cparams(("arbitrary", "arbitrary")),
        name="mixer_d_carry",
    )(xin, gm, yloc)


def _s5_out_kernel(y_ref, u_ref, d_ref, w_ref, b_ref, o_ref):
    y = y_ref[...] + d_ref[...] * u_ref[...]
    c = math.sqrt(2.0 / math.pi)
    y = 0.5 * y * (1.0 + jnp.tanh(c * (y + 0.044715 * (y * y * y))))
    z = _dot(y.astype(BF16), w_ref[...]) + b_ref[...]
    o_ref[...] = (y * _sigmoid(z)).astype(o_ref.dtype)


def _s5_out_call(y_tok, proj, d_skip, glu_w, glu_b, tm):
    R = proj.shape[0]
    vec = pl.BlockSpec((1, GROUP_W), lambda i: (0, 0))
    return pl.pallas_call(
        _s5_out_kernel,
        out_shape=jax.ShapeDtypeStruct((R, GROUP_W), BF16),
        grid=(R // tm,),
        in_specs=[
            pl.BlockSpec((tm, GROUP_W), lambda i: (i, 0)),
            pl.BlockSpec((tm, GROUP_W), lambda i: (i, COL_DU // GROUP_W)),
            vec,
            pl.BlockSpec((GROUP_W, GROUP_W), lambda i: (0, 0)),
            vec,
        ],
        out_specs=pl.BlockSpec((tm, GROUP_W), lambda i: (i, 0)),
        compiler_params=_cparams(("arbitrary",)),
        name="mixer_d_out",
    )(y_tok, proj, d_skip, glu_w, glu_b)


def _rope_tables(n_tokens):
    t = jnp.arange(n_tokens)
    nf = HEAD_DIM // 4
    inv = 1.0 / (ROPE_BASE ** (jnp.arange(nf, dtype=F32) / nf))

    def ang(pp):
        a = pp.astype(F32)[:, None] * inv[None, :]
        return jnp.concatenate([a, a], -1)

    a = jnp.concatenate([ang(t // GRID_W), ang(t % GRID_W)], -1)
    cos, sin = jnp.cos(a), jnp.sin(a)
    sign = np.where((np.arange(HEAD_DIM) % 32) < 16, -1.0, 1.0).astype(np.float32)
    cos2 = jnp.concatenate([cos, cos], -1)
    sin2 = jnp.concatenate([sin * sign, sin * sign], -1)
    return cos2, sin2


def _block_ones(n):
    return jnp.asarray(np.kron(np.eye(n // HEAD_DIM), np.ones((HEAD_DIM, HEAD_DIM))), BF16)


def _permute_w_in(w_in):
    cuts = np.cumsum([512, 512, 512, 512, 512, 512, 32, 32, 96, 512, 128, 128, 512])
    seg = lambda i: w_in[..., (0 if i == 0 else cuts[i - 1]):cuts[i]]
    L, D = w_in.shape[:2]
    pad = jnp.zeros((L, D, 96), w_in.dtype)
    parts = [seg(0), seg(1), seg(2), seg(3), seg(4), seg(5), seg(9), seg(12),
             seg(6), seg(7), seg(8), pad, seg(10), seg(11)]
    return jnp.concatenate(parts, axis=-1).astype(BF16)


def _forward(x, c, ctx, c_ctx, w_ada, b_ada, norm_g, ffn1_wg, ffn1_wu, ffn1_wd, ffn2_wg, ffn2_wu, ffn2_wd,
             w_in, w_out, na_q_g, na_k_g, na_rpb, rw_conv, rw_w0, rw_w_up, rw_a0, rw_a_up, rw_g_up,
             rw_k_k, rw_k_a, rw_r_k, rw_gn_w, rw_gn_b, wa_q_g, wa_k_g, wa_sink, s5_a_re, s5_a_im,
             s5_log_dt, s5_b_re, s5_b_im, s5_c_re, s5_c_im, s5_d, s5_glu_w, s5_glu_b, *, tm, tf):
    B, SEQ, D = x.shape
    CTX = ctx.shape[1]
    L = w_ada.shape[0]
    n_lat = B * SEQ
    n_qb = SEQ // QBLK

    bf = lambda t: t.astype(BF16)
    f1g, f1u, f1d, f2g, f2u, f2d = map(bf, (ffn1_wg, ffn1_wu, ffn1_wd, ffn2_wg, ffn2_wu, ffn2_wd))
    w_in_p = _permute_w_in(w_in)
    w_out_b = bf(w_out)
    glu_w_b = bf(s5_glu_w)
    ones128, ones512 = _block_ones(LANE), _block_ones(GROUP_W)
    cos_t, sin_t = _rope_tables(SEQ)
    na_bias = [_na_bias_table(na_rpb[l], n_qb) for l in range(L)]
    zr = lambda r, cdim: jnp.zeros((L, r, cdim), F32)
    wlr = jnp.concatenate([
        jnp.concatenate([rw_w_up[:, 0], rw_w_up[:, 1], zr(32, 3 * GROUP_W)], axis=-1),
        jnp.concatenate([zr(32, 2 * GROUP_W), rw_a_up[:, 0], rw_a_up[:, 1], zr(32, GROUP_W)], axis=-1),
        jnp.concatenate([zr(96, 4 * GROUP_W), rw_g_up], axis=-1),
        zr(96, 5 * GROUP_W)], axis=1).astype(BF16)
    s5_tabs = jax.vmap(_s5_tables)(s5_a_re, s5_a_im, s5_log_dt, s5_b_re, s5_b_im, s5_c_re, s5_c_im)
    tile2 = lambda t: jnp.concatenate([t, t], axis=-1)

    cc = jnp.concatenate([c, c_ctx[None], jnp.zeros((8 - B - 1, D), F32)], axis=0)
    mods_all = _ada_call(cc, w_ada, b_ada)[:, :B + 1].reshape(L, B + 1, N_MOD, D)

    xs = jnp.concatenate([x.reshape(n_lat, D), ctx.reshape(B * CTX, D)], axis=0)
    R = xs.shape[0]
    for l in range(L):
        want_ctx = l < L - 1
        mods = mods_all[l]
        ng = norm_g[l]
        xs = _ffn_call(xs, mods, ng[0:1], f1g, f1u, f1d, l, 0, R, SEQ, B, tm, tf)
        proj = _win_call(xs, mods, ng[1:2], w_in_p, l, SEQ, B, tm, D_IN_PAD // 3)
        o_a = _na_call(proj, na_bias[l], tile2(na_q_g[l][None]), tile2(na_k_g[l][None]), ones128,
                       B, SEQ, CTX, want_ctx)
        prep = _rw_prep_call(proj, rw_conv[l], wlr[l], rw_w0[l], rw_a0[l], rw_k_k[l][None], rw_k_a[l][None],
                             ones512, B, SEQ, CTX, min(tm, 256))
        vec = lambda t: t.reshape(1, GROUP_W)
        yf, yb = _rw_scan_call(prep, B, SEQ, CTX)
        o_b = _rw_readout_call(prep, yf, yb, vec(rw_r_k[l]), vec(rw_gn_w[l]), vec(rw_gn_b[l]), ones512, tm)
        o_c = _wa_call(proj, wa_sink[l], cos_t, sin_t, tile2(wa_q_g[l][None]), tile2(wa_k_g[l][None]), ones128,
                       B, SEQ, CTX, want_ctx)
        wz, gm, apow = (t[l] for t in s5_tabs)
        yloc, e_loc = _s5_local_call(proj.reshape(R // S5_CHUNK, S5_CHUNK, D_IN_PAD), bf(wz))
        xin = _s5_scan_call(e_loc, apow, B, SEQ, CTX)
        y_tok = _s5_carry_call(xin, bf(gm), yloc).reshape(R, GROUP_W)
        o_d = _s5_out_call(y_tok, proj, vec(s5_d[l]), glu_w_b[l], vec(s5_glu_b[l]), tm)
        n_rows = R if want_ctx else n_lat
        xs = _wout_call(xs, (o_a, o_b, o_c, o_d), mods, w_out_b, l, n_rows, SEQ, B, tm)
        xs = _ffn_call(xs, mods, ng[2:3], f2g, f2u, f2d, l, 6, n_rows, SEQ, B, tm, tf)
    return xs[:n_lat].reshape(B, SEQ, D)


def kernel(x, c, ctx, c_ctx, w_ada, b_ada, norm_g, ffn1_wg, ffn1_wu, ffn1_wd, ffn2_wg, ffn2_wu, ffn2_wd, w_in, w_out, na_q_g, na_k_g, na_rpb, rw_conv, rw_w0, rw_w_up, rw_a0, rw_a_up, rw_g_up, rw_k_k, rw_k_a, rw_r_k, rw_gn_w, rw_gn_b, wa_q_g, wa_k_g, wa_sink, s5_a_re, s5_a_im, s5_log_dt, s5_b_re, s5_b_im, s5_c_re, s5_c_im, s5_d, s5_glu_w, s5_glu_b):
    return _forward(x, c, ctx, c_ctx, w_ada, b_ada, norm_g, ffn1_wg, ffn1_wu, ffn1_wd, ffn2_wg, ffn2_wu, ffn2_wd,
                    w_in, w_out, na_q_g, na_k_g, na_rpb, rw_conv, rw_w0, rw_w_up, rw_a0, rw_a_up, rw_g_up,
                    rw_k_k, rw_k_a, rw_r_k, rw_gn_w, rw_gn_b, wa_q_g, wa_k_g, wa_sink, s5_a_re, s5_a_im,
                    s5_log_dt, s5_b_re, s5_b_im, s5_c_re, s5_c_im, s5_d, s5_glu_w, s5_glu_b, tm=512, tf=512)
```

```python
import functools
import math

import numpy as np
import jax
import jax.numpy as jnp
from jax import lax
from jax.experimental import pallas as pl
from jax.experimental.pallas import tpu as pltpu

F32 = jnp.float32
BF16 = jnp.bfloat16

D_MODEL = 2048
GRID_W = 64
HEAD_DIM = 64
GROUP_W = D_MODEL // 4
N_MOD = 9
NORM_EPS = 1e-6
ROPE_BASE = 10000.0
NEG_INF = -1e30

NA_HEADS = GROUP_W // HEAD_DIM
NA_WIN_R = 8
NA_WIN_C = 16
RW_HEADS = GROUP_W // HEAD_DIM
RW_DECAY_RANK = 32
RW_ICLR_RANK = 32
RW_GATE_RANK = 96
RW_GN_EPS = 64e-5
WA_HEADS = GROUP_W // HEAD_DIM
WA_KV_HEADS = 2
WA_GROUP = WA_HEADS // WA_KV_HEADS
WA_WINDOW = 128
S5_P = 16
S5_GROUPS = GROUP_W // S5_P
S5_N = 64

LANE = 128
QBLK = 2 * GRID_W
NA_KBLKS = 5
RW_CHUNK = 64
S5_CHUNK = 8
S5_OCTETS = 4
VMEM_LIMIT = 56 * 1024 * 1024

COL_AQ, COL_AK, COL_AV = 0, 512, 1024
COL_BR = 1536
COL_CQ = 3072
COL_DU = 3584
COL_BLR = 4096
COL_CK, COL_CV = 4352, 4480
D_IN_PAD = 4608

PB_R, PB_V, PB_KK, PB_G = 0, 1, 2, 9
PB_N = 10


def _cparams(sem):
    return pltpu.CompilerParams(dimension_semantics=sem, vmem_limit_bytes=VMEM_LIMIT)


def _dot(a, b):
    return jnp.dot(a, b, preferred_element_type=F32)


def _dot_nt(a, b):
    return lax.dot_general(a, b, (((1,), (1,)), ((), ())), preferred_element_type=F32)


def _dot_tn(a, b):
    return lax.dot_general(a, b, (((0,), (0,)), ((), ())), preferred_element_type=F32)


def _sigmoid(x):
    return 1.0 / (1.0 + jnp.exp(-x))


def _modulate(x, g, shift, scale):
    ms = jnp.mean(x * x, axis=-1, keepdims=True)
    return (x * lax.rsqrt(ms + NORM_EPS) * g) * (1.0 + scale) + shift


def _seg_sum(x, ones_bd):
    hi = x.astype(BF16)
    r1 = x - hi.astype(F32)
    mid = r1.astype(BF16)
    lo = (r1 - mid.astype(F32)).astype(BF16)
    return _dot(hi, ones_bd) + _dot(mid, ones_bd) + _dot(lo, ones_bd)


def _head_rmsnorm(x, ones_bd):
    ms = _seg_sum(x * x, ones_bd) * (1.0 / HEAD_DIM)
    return x * lax.rsqrt(ms + NORM_EPS)


def _ada_kernel(c_ref, w_ref, b_ref, o_ref):
    c = c_ref[...]
    s = (c * _sigmoid(c)).astype(BF16)
    o_ref[...] = _dot(s, w_ref[...].astype(BF16)) + b_ref[...]


def _ada_call(cc, w_ada, b_ada):
    L, D, N = w_ada.shape
    tn = 1024
    return pl.pallas_call(
        _ada_kernel,
        out_shape=jax.ShapeDtypeStruct((L, 8, N), F32),
        grid=(L, N // tn),
        in_specs=[
            pl.BlockSpec((8, D), lambda l, j: (0, 0)),
            pl.BlockSpec((None, D, tn), lambda l, j: (l, 0, j)),
            pl.BlockSpec((None, 1, tn), lambda l, j: (l, 0, j)),
        ],
        out_specs=pl.BlockSpec((None, 8, tn), lambda l, j: (l, 0, j)),
        compiler_params=_cparams(("arbitrary", "arbitrary")),
        name="adaln",
    )(cc, w_ada, b_ada.reshape(L, 1, N))


def _ffn_kernel(x_ref, mod_ref, g_ref, wg_ref, wu_ref, wd_ref, o_ref, h_sc, acc_sc, *, mi):
    j = pl.program_id(1)

    @pl.when(j == 0)
    def _():
        h = _modulate(x_ref[...], g_ref[...], mod_ref[mi:mi + 1, :], mod_ref[mi + 1:mi + 2, :])
        h_sc[...] = h.astype(BF16)
        acc_sc[...] = jnp.zeros_like(acc_sc)

    h = h_sc[...]
    gate = _dot(h, wg_ref[...])
    up = _dot(h, wu_ref[...])
    a = (gate * _sigmoid(gate) * up).astype(BF16)
    acc_sc[...] += _dot(a, wd_ref[...])

    @pl.when(j == pl.num_programs(1) - 1)
    def _():
        o_ref[...] = x_ref[...] + 0.5 * mod_ref[mi + 2:mi + 3, :] * acc_sc[...]


def _ffn_call(xs, mods, g, wg, wu, wd, l, mi, n_rows, rows_per_seq, n_batch, tm, tf):
    D = xs.shape[1]
    F = wg.shape[2]
    tps = rows_per_seq // tm
    return pl.pallas_call(
        functools.partial(_ffn_kernel, mi=mi),
        out_shape=jax.ShapeDtypeStruct((n_rows, D), F32),
        grid=(n_rows // tm, F // tf),
        in_specs=[
            pl.BlockSpec((tm, D), lambda i, j: (i, 0)),
            pl.BlockSpec((None, N_MOD, D), lambda i, j: (jnp.minimum(i // tps, n_batch), 0, 0)),
            pl.BlockSpec((1, D), lambda i, j: (0, 0)),
            pl.BlockSpec((None, D, tf), lambda i, j: (l, 0, j)),
            pl.BlockSpec((None, D, tf), lambda i, j: (l, 0, j)),
            pl.BlockSpec((None, tf, D), lambda i, j: (l, j, 0)),
        ],
        out_specs=pl.BlockSpec((tm, D), lambda i, j: (i, 0)),
        scratch_shapes=[pltpu.VMEM((tm, D), BF16), pltpu.VMEM((tm, D), F32)],
        compiler_params=_cparams(("arbitrary", "arbitrary")),
        name="ffn",
    )(xs, mods, g, wg, wu, wd)


def _win_kernel(x_ref, mod_ref, g_ref, w_ref, o_ref, h_sc):
    @pl.when(pl.program_id(1) == 0)
    def _():
        h = _modulate(x_ref[...], g_ref[...], mod_ref[3:4, :], mod_ref[4:5, :])
        h_sc[...] = h.astype(BF16)

    o_ref[...] = _dot(h_sc[...], w_ref[...])


def _win_call(xs, mods, g, w_in, l, rows_per_seq, n_batch, tm, tn):
    R, D = xs.shape
    N = w_in.shape[2]
    tps = rows_per_seq // tm
    return pl.pallas_call(
        _win_kernel,
        out_shape=jax.ShapeDtypeStruct((R, N), F32),
        grid=(R // tm, N // tn),
        in_specs=[
            pl.BlockSpec((tm, D), lambda i, j: (i, 0)),
            pl.BlockSpec((None, N_MOD, D), lambda i, j: (jnp.minimum(i // tps, n_batch), 0, 0)),
            pl.BlockSpec((1, D), lambda i, j: (0, 0)),
            pl.BlockSpec((None, D, tn), lambda i, j: (l, 0, j)),
        ],
        out_specs=pl.BlockSpec((tm, tn), lambda i, j: (i, j)),
        scratch_shapes=[pltpu.VMEM((tm, D), BF16)],
        compiler_params=_cparams(("arbitrary", "arbitrary")),
        name="in_proj",
    )(xs, mods, g, w_in)


def _wout_kernel(x_ref, oa_ref, ob_ref, oc_ref, od_ref, mod_ref, w_ref, o_ref):
    acc = _dot(oa_ref[...], w_ref[0 * GROUP_W:1 * GROUP_W, :])
    acc += _dot(ob_ref[...], w_ref[1 * GROUP_W:2 * GROUP_W, :])
    acc += _dot(oc_ref[...], w_ref[2 * GROUP_W:3 * GROUP_W, :])
    acc += _dot(od_ref[...], w_ref[3 * GROUP_W:4 * GROUP_W, :])
    o_ref[...] = x_ref[...] + mod_ref[5:6, :] * acc


def _wout_call(xs, outs, mods, w_out, l, n_rows, rows_per_seq, n_batch, tm):
    D = xs.shape[1]
    tps = rows_per_seq // tm
    ospec = pl.BlockSpec((tm, GROUP_W), lambda i: (i, 0))
    return pl.pallas_call(
        _wout_kernel,
        out_shape=jax.ShapeDtypeStruct((n_rows, D), F32),
        grid=(n_rows // tm,),
        in_specs=[
            pl.BlockSpec((tm, D), lambda i: (i, 0)),
            ospec, ospec, ospec, ospec,
            pl.BlockSpec((None, N_MOD, D), lambda i: (jnp.minimum(i // tps, n_batch), 0, 0)),
            pl.BlockSpec((None, D, D), lambda i: (l, 0, 0)),
        ],
        out_specs=pl.BlockSpec((tm, D), lambda i: (i, 0)),
        compiler_params=_cparams(("arbitrary",)),
        name="out_proj",
    )(xs, *outs, mods, w_out)


def _na_case_reps(n_qb):
    return (0, 1, 2, n_qb - 2, n_qb - 1)


def _na_start(p, n_qb):
    return jnp.clip(p - 2, 0, n_qb - NA_KBLKS)


def _na_bias_table(rpb, n_qb):
    H = rpb.shape[0]
    rows = 2 * n_qb
    qc = np.arange(GRID_W)[:, None]
    kc = np.arange(GRID_W)[None, :]
    cs = np.clip(qc - NA_WIN_C // 2, 0, GRID_W - NA_WIN_C)
    okc = (kc >= cs) & (kc < cs + NA_WIN_C)
    dc = np.where(okc, kc - qc + NA_WIN_C - 1, 0)
    rp = rpb.astype(F32).reshape(H, 2 * NA_WIN_R - 1, 2 * NA_WIN_C - 1)
    blocks = jnp.where(jnp.asarray(okc)[None, None], jnp.take(rp, jnp.asarray(dc), axis=2), NEG_INF)
    blocks = blocks.astype(BF16)
    neg = jnp.full((H, GRID_W, GRID_W), NEG_INF, BF16)
    cases = []
    for p in _na_case_reps(n_qb):
        start = min(max(p - 2, 0), n_qb - NA_KBLKS)
        qrows = []
        for qr in range(2):
            qa = 2 * p + qr
            rs = min(max(qa - NA_WIN_R // 2, 0), rows - NA_WIN_R)
            krow = []
            for kr in range(2 * NA_KBLKS):
                ka = 2 * start + kr
                krow.append(blocks[:, ka - qa + NA_WIN_R - 1] if rs <= ka < rs + NA_WIN_R else neg)
            qrows.append(jnp.concatenate(krow, axis=-1))
        cases.append(jnp.concatenate(qrows, axis=-2))
    return jnp.stack(cases, axis=0)


def _na_kernel(q_ref, k_ref, v_ref, kc_ref, vc_ref, bias_ref, qg_ref, kg_ref, ones_ref, o_ref,
               kn_sc, vn_sc, kcn_sc, vcn_sc, *, n_qb, seq, ctx):
    p = pl.program_id(2)
    ones_bd = ones_ref[...]
    scale = HEAD_DIM ** -0.5

    @pl.when(p == 0)
    def _():
        kg = kg_ref[...]
        rows = 512

        def body(i, carry):
            sl = pl.ds(pl.multiple_of(i * rows, rows), rows)
            kn_sc[sl, :] = (_head_rmsnorm(k_ref[sl, :], ones_bd) * kg).astype(BF16)
            vn_sc[sl, :] = v_ref[sl, :].astype(BF16)
            return carry

        lax.fori_loop(0, seq // rows, body, 0)
        kcn_sc[...] = (_head_rmsnorm(kc_ref[...], ones_bd) * kg).astype(BF16)
        vcn_sc[...] = vc_ref[...].astype(BF16)

    nq = q_ref.shape[0] // QBLK
    n_lat = n_qb // nq
    q = (_head_rmsnorm(q_ref[...], ones_bd) * (qg_ref[...] * scale)).astype(BF16)
    kcn = kcn_sc[...]
    vcn = vcn_sc[...]
    chains = [(qb, h) for qb in range(nq) for h in range(2)]
    hsl = lambda h: slice(h * HEAD_DIM, (h + 1) * HEAD_DIM)
    qh = [q[qb * QBLK:(qb + 1) * QBLK, hsl(h)] for qb, h in chains]
    s_c = [_dot_nt(qh[c], kcn[:, hsl(h)]) for c, (qb, h) in enumerate(chains)]
    m_c = [jnp.max(t, axis=-1, keepdims=True) for t in s_c]

    def finish(o, den):
        rows = [jnp.concatenate([o[qb * 2 + h] / den[qb * 2 + h] for h in range(2)], axis=-1) for qb in range(nq)]
        o_ref[...] = jnp.concatenate(rows, axis=0).astype(o_ref.dtype)

    @pl.when(p < n_lat)
    def _():
        starts = [_na_start(p * nq + qb, n_qb) for qb in range(nq)]
        cases = [p * nq + qb - starts[qb] for qb in range(nq)]
        sls = [pl.ds(pl.multiple_of(st * QBLK, QBLK), NA_KBLKS * QBLK) for st in starts]
        kw = [kn_sc[sl, :] for sl in sls]
        vw = [vn_sc[sl, :] for sl in sls]
        s_n = [_dot_nt(qh[c], kw[qb][:, hsl(h)]) + bias_ref[cases[qb], h].astype(F32)
               for c, (qb, h) in enumerate(chains)]
        m = [jnp.maximum(jnp.max(s_n[c], axis=-1, keepdims=True), m_c[c]) for c in range(len(chains))]
        p_n = [jnp.exp(s_n[c] - m[c]) for c in range(len(chains))]
        p_c = [jnp.exp(s_c[c] - m[c]) for c in range(len(chains))]
        den = [jnp.sum(p_n[c], axis=-1, keepdims=True) + jnp.sum(p_c[c], axis=-1, keepdims=True)
               for c in range(len(chains))]
        o = [_dot(p_n[c].astype(BF16), vw[qb][:, hsl(h)]) + _dot(p_c[c].astype(BF16), vcn[:, hsl(h)])
             for c, (qb, h) in enumerate(chains)]
        finish(o, den)

    @pl.when(p >= n_lat)
    def _():
        p_c = [jnp.exp(s_c[c] - m_c[c]) for c in range(len(chains))]
        den = [jnp.sum(t, axis=-1, keepdims=True) for t in p_c]
        o = [_dot(p_c[c].astype(BF16), vcn[:, hsl(h)]) for c, (qb, h) in enumerate(chains)]
        finish(o, den)


def _na_call(proj, bias, qg, kg, ones_bd, n_batch, seq, ctx, want_ctx):
    R = proj.shape[0]
    nq = 2
    n_qb = seq // QBLK
    n_lat, n_ctx = n_qb // nq, ctx // (nq * QBLK)
    steps = n_lat + (n_ctx if want_ctx else 0)
    ctx_kb = n_batch * seq // ctx

    def qrow(b, p):
        return jnp.where(p < n_lat, b * n_lat + p, n_batch * n_lat + b * n_ctx + (p - n_lat))

    cq, ck, cv = COL_AQ // LANE, COL_AK // LANE, COL_AV // LANE
    return pl.pallas_call(
        functools.partial(_na_kernel, n_qb=n_qb, seq=seq, ctx=ctx),
        out_shape=jax.ShapeDtypeStruct((R, GROUP_W), BF16),
        grid=(n_batch, NA_HEADS // 2, steps),
        in_specs=[
            pl.BlockSpec((nq * QBLK, LANE), lambda b, hp, p: (qrow(b, p), cq + hp)),
            pl.BlockSpec((seq, LANE), lambda b, hp, p: (b, ck + hp)),
            pl.BlockSpec((seq, LANE), lambda b, hp, p: (b, cv + hp)),
            pl.BlockSpec((ctx, LANE), lambda b, hp, p: (ctx_kb + b, ck + hp)),
            pl.BlockSpec((ctx, LANE), lambda b, hp, p: (ctx_kb + b, cv + hp)),
            pl.BlockSpec((5, 2, QBLK, NA_KBLKS * QBLK), lambda b, hp, p: (0, hp, 0, 0)),
            pl.BlockSpec((1, LANE), lambda b, hp, p: (0, 0)),
            pl.BlockSpec((1, LANE), lambda b, hp, p: (0, 0)),
            pl.BlockSpec((LANE, LANE), lambda b, hp, p: (0, 0)),
        ],
        out_specs=pl.BlockSpec((nq * QBLK, LANE), lambda b, hp, p: (qrow(b, p), hp)),
        scratch_shapes=[pltpu.VMEM((seq, LANE), BF16), pltpu.VMEM((seq, LANE), BF16),
                        pltpu.VMEM((ctx, LANE), BF16), pltpu.VMEM((ctx, LANE), BF16)],
        compiler_params=_cparams(("arbitrary", "arbitrary", "arbitrary")),
        name="mixer_a",
    )(proj, proj, proj, proj, proj, bias, qg, kg, ones_bd)


def _rope(x, cos, sin_signed, first_half):
    rot = jnp.where(first_half, pltpu.roll(x, LANE - 16, 1), pltpu.roll(x, 16, 1))
    return x * cos + rot * sin_signed


def _wa_kernel(sink_ref, q_ref, k_ref, v_ref, kc_ref, vc_ref, cosk_ref, sin_k_ref, cosq_ref, sinq_ref,
               qg_ref, kg_ref, ones_ref, wmask_ref, o_ref, kn_sc, vn_sc, kcn_sc, vcn_sc, *, n_qb, seq, ctx):
    p = pl.program_id(1)
    ones_bd = ones_ref[...]
    scale = HEAD_DIM ** -0.5
    lane = lax.broadcasted_iota(jnp.int32, (1, LANE), 1)
    first_half = (lane % 32) < 16

    @pl.when(p == 0)
    def _():
        kg = kg_ref[...]
        rows = 512

        def body(i, carry):
            sl = pl.ds(pl.multiple_of(i * rows, rows), rows)
            kn = _head_rmsnorm(k_ref[sl, :], ones_bd) * kg
            kn_sc[sl, :] = _rope(kn, cosk_ref[sl, :], sin_k_ref[sl, :], first_half).astype(BF16)
            vn_sc[sl, :] = v_ref[sl, :].astype(BF16)
            return carry

        lax.fori_loop(0, seq // rows, body, 0)
        kcn_sc[...] = (_head_rmsnorm(kc_ref[...], ones_bd) * kg).astype(BF16)
        vcn_sc[...] = vc_ref[...].astype(BF16)

    nq = q_ref.shape[0] // QBLK
    n_lat = n_qb // nq
    qg = qg_ref[...] * scale
    is_lat = p < n_lat
    cosq = jnp.where(is_lat, cosq_ref[...], 1.0)
    sinq = jnp.where(is_lat, sinq_ref[...], 0.0)
    qh = []
    for c in range(4):
        qn = _head_rmsnorm(q_ref[:, c * LANE:(c + 1) * LANE], ones_bd) * qg
        qn = _rope(qn, cosq, sinq, first_half).astype(BF16)
        qh += [qn[:, :HEAD_DIM], qn[:, HEAD_DIM:]]
    kcn = kcn_sc[...]
    vcn = vcn_sc[...]
    chains = [(qb, kh) for qb in range(nq) for kh in range(WA_KV_HEADS)]
    nch = range(len(chains))
    hsl = lambda kh: slice(kh * HEAD_DIM, (kh + 1) * HEAD_DIM)
    qs = [jnp.concatenate([qh[kh * WA_GROUP + g][qb * QBLK:(qb + 1) * QBLK, :] for g in range(WA_GROUP)], axis=0)
          for qb, kh in chains]
    sk = [jnp.concatenate([jnp.full((QBLK, 1), sink_ref[kh * WA_GROUP + g], F32) for g in range(WA_GROUP)],
                          axis=0) for qb, kh in chains]
    s_c = [_dot_nt(qs[c], kcn[:, hsl(chains[c][1])]) for c in nch]
    m_c = [jnp.maximum(jnp.max(s_c[c], axis=-1, keepdims=True), sk[c]) for c in nch]

    def finish(o, den):
        rows = []
        for qb in range(nq):
            heads = []
            for kh in range(WA_KV_HEADS):
                c = qb * WA_KV_HEADS + kh
                oc = o[c] / den[c]
                heads += [oc[g * QBLK:(g + 1) * QBLK, :] for g in range(WA_GROUP)]
            rows.append(jnp.concatenate(heads, axis=-1))
        o_ref[...] = jnp.concatenate(rows, axis=0).astype(o_ref.dtype)

    @pl.when(is_lat)
    def _():
        blk = [p * nq + qb for qb in range(nq)]
        ws = [jnp.clip((bq - 1) * QBLK, 0, seq - 3 * QBLK) for bq in blk]
        sls = [pl.ds(pl.multiple_of(w, QBLK), 3 * QBLK) for w in ws]
        kw = [kn_sc[sl, :] for sl in sls]
        vw = [vn_sc[sl, :] for sl in sls]
        case = [jnp.where(bq == 0, 0, jnp.where(bq == n_qb - 1, 2, 1)) for bq in blk]
        msk = [jnp.concatenate([wmask_ref[cs]] * WA_GROUP, axis=0) for cs in case]
        s_w = [_dot_nt(qs[c], kw[chains[c][0]][:, hsl(chains[c][1])]) + msk[chains[c][0]] for c in nch]
        m = [jnp.maximum(jnp.max(s_w[c], axis=-1, keepdims=True), m_c[c]) for c in nch]
        p_w = [jnp.exp(s_w[c] - m[c]) for c in nch]
        p_c = [jnp.exp(s_c[c] - m[c]) for c in nch]
        den = [jnp.sum(p_w[c], axis=-1, keepdims=True) + jnp.sum(p_c[c], axis=-1, keepdims=True)
               + jnp.exp(sk[c] - m[c]) for c in nch]
        o = [_dot(p_w[c].astype(BF16), vw[chains[c][0]][:, hsl(chains[c][1])])
             + _dot(p_c[c].astype(BF16), vcn[:, hsl(chains[c][1])]) for c in nch]
        finish(o, den)

    @pl.when(jnp.logical_not(is_lat))
    def _():
        p_c = [jnp.exp(s_c[c] - m_c[c]) for c in nch]
        den = [jnp.sum(p_c[c], axis=-1, keepdims=True) + jnp.exp(sk[c] - m_c[c]) for c in nch]
        o = [_dot(p_c[c].astype(BF16), vcn[:, hsl(chains[c][1])]) for c in nch]
        finish(o, den)


def _wa_mask_table():
    i = np.arange(QBLK)[:, None]
    j = np.arange(3 * QBLK)[None, :]
    shifts = (0, QBLK, 2 * QBLK)
    return jnp.asarray(np.stack([np.where(np.abs(j - sh - i) <= WA_WINDOW, 0.0, NEG_INF) for sh in shifts]), F32)


def _wa_call(proj, sink, cos_t, sin_t, qg, kg, ones_bd, n_batch, seq, ctx, want_ctx):
    R = proj.shape[0]
    nq = 2
    n_qb = seq // QBLK
    n_lat, n_ctx = n_qb // nq, ctx // (nq * QBLK)
    steps = n_lat + (n_ctx if want_ctx else 0)
    ctx_kb = n_batch * seq // ctx

    def qrow(b, p):
        return jnp.where(p < n_lat, b * n_lat + p, n_batch * n_lat + b * n_ctx + (p - n_lat))

    ck, cv = COL_CK // LANE, COL_CV // LANE
    return pl.pallas_call(
        functools.partial(_wa_kernel, n_qb=n_qb, seq=seq, ctx=ctx),
        out_shape=jax.ShapeDtypeStruct((R, GROUP_W), BF16),
        grid=(n_batch, steps),
        in_specs=[
            pl.BlockSpec(memory_space=pltpu.SMEM),
            pl.BlockSpec((nq * QBLK, GROUP_W), lambda b, p: (qrow(b, p), COL_CQ // GROUP_W)),
            pl.BlockSpec((seq, LANE), lambda b, p: (b, ck)),
            pl.BlockSpec((seq, LANE), lambda b, p: (b, cv)),
            pl.BlockSpec((ctx, LANE), lambda b, p: (ctx_kb + b, ck)),
            pl.BlockSpec((ctx, LANE), lambda b, p: (ctx_kb + b, cv)),
            pl.BlockSpec((seq, LANE), lambda b, p: (0, 0)),
            pl.BlockSpec((seq, LANE), lambda b, p: (0, 0)),
            pl.BlockSpec((nq * QBLK, LANE), lambda b, p: (jnp.minimum(p, n_lat - 1), 0)),
            pl.BlockSpec((nq * QBLK, LANE), lambda b, p: (jnp.minimum(p, n_lat - 1), 0)),
            pl.BlockSpec((1, LANE), lambda b, p: (0, 0)),
            pl.BlockSpec((1, LANE), lambda b, p: (0, 0)),
            pl.BlockSpec((LANE, LANE), lambda b, p: (0, 0)),
            pl.BlockSpec((3, QBLK, 3 * QBLK), lambda b, p: (0, 0, 0)),
        ],
        out_specs=pl.BlockSpec((nq * QBLK, GROUP_W), lambda b, p: (qrow(b, p), 0)),
        scratch_shapes=[pltpu.VMEM((seq, LANE), BF16), pltpu.VMEM((seq, LANE), BF16),
                        pltpu.VMEM((ctx, LANE), BF16), pltpu.VMEM((ctx, LANE), BF16)],
        compiler_params=_cparams(("arbitrary", "arbitrary")),
        name="mixer_c",
    )(sink, proj, proj, proj, proj, proj, cos_t, sin_t, cos_t, sin_t, qg, kg, ones_bd, _wa_mask_table())


def _rw_prep_kernel(x_ref, prev_ref, next_ref, lr_ref, conv_ref, wlr_ref, w0_ref, a0_ref, kk_ref, ka_ref,
                    ones_ref, o_ref, *, tiles_lat, tiles_ctx, n_lat_tiles):
    i = pl.program_id(0)
    tm = x_ref.shape[0]
    in_lat = i < n_lat_tiles
    first = jnp.where(in_lat, i % tiles_lat == 0, (i - n_lat_tiles) % tiles_ctx == 0)
    last = jnp.where(in_lat, i % tiles_lat == tiles_lat - 1, (i - n_lat_tiles) % tiles_ctx == tiles_ctx - 1)
    x = x_ref[...]
    prev_row = jnp.where(first, 0.0, prev_ref[7:8, :])
    next_row = jnp.where(last, 0.0, next_ref[0:1, :])
    row = lax.broadcasted_iota(jnp.int32, (tm, 1), 0)
    x_prev = jnp.where(row == 0, prev_row, pltpu.roll(x, 1, 0))
    x_next = jnp.where(row == tm - 1, next_row, pltpu.roll(x, tm - 1, 0))
    cw = conv_ref[...]
    y = x_prev * cw[0:1, :] + x * cw[1:2, :] + x_next * cw[2:3, :]
    r = y[:, 0:GROUP_W]
    k = y[:, GROUP_W:2 * GROUP_W]
    v = y[:, 2 * GROUP_W:3 * GROUP_W]

    lr = lr_ref[...]
    lane = lax.broadcasted_iota(jnp.int32, (1, lr.shape[1]), 1)
    c1 = RW_DECAY_RANK
    c2 = c1 + RW_ICLR_RANK
    c3 = c2 + RW_GATE_RANK
    act = jnp.where(lane < c1, jnp.tanh(lr),
                    jnp.where(lane < c2, lr, jnp.where(lane < c3, _sigmoid(lr), 0.0)))
    up = _dot(act.astype(BF16), wlr_ref[...])

    ones_bd = ones_ref[...]
    kk = k * kk_ref[...]
    nrm = jnp.sqrt(_seg_sum(kk * kk, ones_bd))
    kk = kk / jnp.maximum(nrm, 1e-12)

    o_ref[:, PB_R * GROUP_W:(PB_R + 1) * GROUP_W] = r
    o_ref[:, PB_V * GROUP_W:(PB_V + 1) * GROUP_W] = v
    o_ref[:, PB_KK * GROUP_W:(PB_KK + 1) * GROUP_W] = kk
    o_ref[:, PB_G * GROUP_W:(PB_G + 1) * GROUP_W] = up[:, 4 * GROUP_W:5 * GROUP_W]
    ka = ka_ref[...]
    for d in range(2):
        z = w0_ref[d:d + 1, :] + up[:, d * GROUP_W:(d + 1) * GROUP_W]
        sp = jnp.maximum(-z, 0.0) + jnp.log(1.0 + jnp.exp(-jnp.abs(z)))
        lw = -jnp.exp(-sp - 0.5)
        a = _sigmoid(a0_ref[d:d + 1, :] + up[:, (2 + d) * GROUP_W:(3 + d) * GROUP_W])
        kd = k * (1.0 + (a - 1.0) * ka)
        base = 3 + 3 * d
        o_ref[:, base * GROUP_W:(base + 1) * GROUP_W] = lw
        o_ref[:, (base + 1) * GROUP_W:(base + 2) * GROUP_W] = a
        o_ref[:, (base + 2) * GROUP_W:(base + 3) * GROUP_W] = kd


def _rw_prep_call(proj, conv_w, wlr, w0, a0, k_k, k_a, ones512, n_batch, seq, ctx, tm):
    R = proj.shape[0]
    n_lat_tiles = n_batch * seq // tm
    nb8 = R // 8
    t8 = tm // 8
    wide = 3 * GROUP_W
    cb = COL_BR // wide
    return pl.pallas_call(
        functools.partial(_rw_prep_kernel, tiles_lat=seq // tm, tiles_ctx=ctx // tm, n_lat_tiles=n_lat_tiles),
        out_shape=jax.ShapeDtypeStruct((R, PB_N * GROUP_W), F32),
        grid=(R // tm,),
        in_specs=[
            pl.BlockSpec((tm, wide), lambda i: (i, cb)),
            pl.BlockSpec((8, wide), lambda i: (jnp.maximum(i * t8 - 1, 0), cb)),
            pl.BlockSpec((8, wide), lambda i: (jnp.minimum((i + 1) * t8, nb8 - 1), cb)),
            pl.BlockSpec((tm, 256), lambda i: (i, COL_BLR // 256)),
            pl.BlockSpec((3, wide), lambda i: (0, 0)),
            pl.BlockSpec((256, 5 * GROUP_W), lambda i: (0, 0)),
            pl.BlockSpec((2, GROUP_W), lambda i: (0, 0)),
            pl.BlockSpec((2, GROUP_W), lambda i: (0, 0)),
            pl.BlockSpec((1, GROUP_W), lambda i: (0, 0)),
            pl.BlockSpec((1, GROUP_W), lambda i: (0, 0)),
            pl.BlockSpec((GROUP_W, GROUP_W), lambda i: (0, 0)),
        ],
        out_specs=pl.BlockSpec((tm, PB_N * GROUP_W), lambda i: (i, 0)),
        compiler_params=_cparams(("arbitrary",)),
        name="mixer_b_prep",
    )(proj, proj, proj, proj, conv_w, wlr, w0, a0, k_k, k_a, ones512)


RW_SPLIT = 1
_NN = ((1,), (0,))
_NT = ((1,), (1,))
_TN = ((0,), (0,))


def _split(x, n):
    parts = []
    for _ in range(n):
        p = x.astype(BF16)
        parts.append(p)
        x = x - p.astype(F32)
    return parts


def _sdot(a, b, dims):
    n = max(len(a), len(b))
    acc = None
    for i, ai in enumerate(a):
        for j, bj in enumerate(b):
            if i + j < n:
                t = lax.dot_general(ai, bj, (dims, ((), ())), preferred_element_type=F32)
                acc = t if acc is None else acc + t
    return acc


def _rw_chunks(at, bt, kt, rt, v, gam, s0, strict, incl):
    n = len(at)
    ids = range(n)
    C = at[0].shape[0]
    sp = lambda t: _split(t, RW_SPLIT)
    b1 = lambda t: [t.astype(BF16)]
    ar = [b1(jnp.concatenate([at[i], rt[i]], axis=0)) for i in ids]
    bk = [sp(jnp.concatenate([bt[i], kt[i]], axis=0)) for i in ids]
    g4 = [_sdot(ar[i], bk[i][:1], _NT) for i in ids]
    a_ab = [jnp.where(strict[i], g4[i][:C, :C], 0.0) for i in ids]
    a_ak = [jnp.where(strict[i], g4[i][:C, C:], 0.0) for i in ids]
    a_rb = [jnp.where(incl[i], g4[i][C:, :C], 0.0) for i in ids]
    a_rk = [jnp.where(incl[i], g4[i][C:, C:], 0.0) for i in ids]
    wv = [_sdot(b1(a_ak[i]), b1(v[i]), _NN) for i in ids]
    x = [jnp.concatenate([at[i], wv[i]], axis=1) for i in ids]

    row = lax.broadcasted_iota(jnp.int32, (C, C), 0)
    col = lax.broadcasted_iota(jnp.int32, (C, C), 1)
    eye = (row == col).astype(F32)
    blk = 16
    same = (row // blk) == (col // blk)
    p = [jnp.where(same, a_ab[i], 0.0) for i in ids]
    m = [eye + p[i] for i in ids]
    for _ in range(int(math.log2(blk)) - 1):
        pb_ = [b1(p[i]) for i in ids]
        p = [_sdot(pb_[i], pb_[i], _NN) for i in ids]
        m = [m[i] + _sdot(b1(m[i]), b1(p[i]), _NN) for i in ids]
    while blk < C:
        wider = (row // (2 * blk)) == (col // (2 * blk))
        join = wider & jnp.logical_not(same)
        mb = [b1(m[i]) for i in ids]
        t = [_sdot(b1(jnp.where(join, a_ab[i], 0.0)), mb[i], _NN) for i in ids]
        m = [m[i] + _sdot(mb[i], b1(t[i]), _NN) for i in ids]
        same, blk = wider, 2 * blk
    x = [_sdot(b1(m[i]), b1(x[i]), _NN) for i in ids]

    s0b = [sp(s0[i]) for i in ids]
    u = [x[i][:, HEAD_DIM:] + _sdot(sp(x[i][:, :HEAD_DIM]), s0b[i], _NT) for i in ids]
    uv = [sp(jnp.concatenate([u[i], v[i]], axis=0)) for i in ids]
    y = [_sdot(b1(jnp.concatenate([a_rb[i], a_rk[i]], axis=1)), uv[i][:1], _NN)
         + _sdot(sp(rt[i]), s0b[i], _NT) for i in ids]
    s1 = [(s0[i] + _sdot(uv[i], bk[i], _TN)) * gam[i] for i in ids]
    return y, s1


def _rw_scan_kernel(*refs, n_batch, nc):
    n_in = 12 * n_batch
    yfl_ref, ybl_ref, yfc_ref, ybc_ref, s_sc = refs[n_in:]
    s = pl.program_id(0)

    @pl.when(s == 0)
    def _():
        s_sc[...] = jnp.zeros_like(s_sc)

    C = refs[0].shape[0]
    row = lax.broadcasted_iota(jnp.int32, (C, C), 0)
    col = lax.broadcasted_iota(jnp.int32, (C, C), 1)
    chains = dict(at=[], bt=[], kt=[], rt=[], v=[], gam=[], s0=[], strict=[], incl=[])
    for b in range(n_batch):
        for d, reverse in enumerate((False, True)):
            r_ref, v_ref, kk_ref, lw_ref, a_ref, k_ref = refs[12 * b + 6 * d:12 * b + 6 * d + 6]
            incl, strict = (row <= col, row < col) if reverse else (row >= col, row > col)
            lw = lw_ref[...]
            kk = kk_ref[...]
            v = v_ref[...]
            cum = jnp.dot(incl.astype(F32), lw, preferred_element_type=F32, precision=lax.Precision.HIGHEST)
            e_l = jnp.exp(cum)
            e_n = jnp.exp(-cum)
            at = -(kk * jnp.exp(cum - lw))
            bt = kk * a_ref[...] * e_n
            kt = k_ref[...] * e_n
            rt = r_ref[...] * e_l
            gam = e_l[0:1, :] if reverse else e_l[C - 1:C, :]
            for h in range(RW_HEADS):
                hs = slice(h * HEAD_DIM, (h + 1) * HEAD_DIM)
                for name, val in (("at", at), ("bt", bt), ("kt", kt), ("rt", rt), ("v", v), ("gam", gam)):
                    chains[name].append(val[:, hs])
                chains["s0"].append(s_sc[b, d, h])
                chains["strict"].append(strict)
                chains["incl"].append(incl)
    y, s1 = _rw_chunks(**chains)
    ys = {}
    for b in range(n_batch):
        for d in range(2):
            base = (b * 2 + d) * RW_HEADS
            for h in range(RW_HEADS):
                s_sc[b, d, h] = s1[base + h]
            ys[b, d] = jnp.concatenate(y[base:base + RW_HEADS], axis=-1)

    @pl.when(s < nc)
    def _():
        for b in range(n_batch):
            yfc_ref[b] = ys[b, 0]
            ybc_ref[b] = ys[b, 1]

    @pl.when(s >= nc)
    def _():
        for b in range(n_batch):
            yfl_ref[b] = ys[b, 0]
            ybl_ref[b] = ys[b, 1]


def _rw_scan_call(prep, n_batch, seq, ctx):
    C = RW_CHUNK
    nc, nl = ctx // C, seq // C
    ctx_base = n_batch * nl

    def blk_f(b, s):
        return jnp.where(s < nc, ctx_base + b * nc + s, b * nl + (s - nc))

    def blk_b(b, s):
        return jnp.where(s < nc, ctx_base + b * nc + (nc - 1 - s), b * nl + (nl - 1 - (s - nc)))

    def col(blk, b, cb):
        return pl.BlockSpec((C, GROUP_W), lambda s: (blk(b, s), cb))

    in_specs = []
    for b in range(n_batch):
        in_specs += [col(blk_f, b, c) for c in (PB_R, PB_V, PB_KK, 3, 4, 5)]
        in_specs += [col(blk_b, b, c) for c in (PB_R, PB_V, PB_KK, 6, 7, 8)]
    lat = jax.ShapeDtypeStruct((n_batch, seq, GROUP_W), F32)
    cx = jax.ShapeDtypeStruct((n_batch, ctx, GROUP_W), F32)
    blk3 = (n_batch, C, GROUP_W)
    out_specs = (
        pl.BlockSpec(blk3, lambda s: (0, jnp.maximum(s - nc, 0), 0)),
        pl.BlockSpec(blk3, lambda s: (0, nl - 1 - jnp.maximum(s - nc, 0), 0)),
        pl.BlockSpec(blk3, lambda s: (0, jnp.minimum(s, nc - 1), 0)),
        pl.BlockSpec(blk3, lambda s: (0, nc - 1 - jnp.minimum(s, nc - 1), 0)),
    )
    yfl, ybl, yfc, ybc = pl.pallas_call(
        functools.partial(_rw_scan_kernel, n_batch=n_batch, nc=nc),
        out_shape=(lat, lat, cx, cx),
        grid=(nc + nl,),
        in_specs=in_specs,
        out_specs=out_specs,
        scratch_shapes=[pltpu.VMEM((n_batch, 2, RW_HEADS, HEAD_DIM, HEAD_DIM), F32)],
        compiler_params=_cparams(("arbitrary",)),
        name="mixer_b_scan",
    )(*([prep] * (12 * n_batch)))
    flat = lambda a, c: jnp.concatenate([a.reshape(n_batch * seq, GROUP_W), c.reshape(n_batch * ctx, GROUP_W)], 0)
    return flat(yfl, yfc), flat(ybl, ybc)


def _rw_readout_kernel(r_ref, v_ref, k0_ref, k1_ref, g_ref, yf_ref, yb_ref, rk_ref, gw_ref, gb_ref, ones_ref,
                       o_ref):
    ones_bd = ones_ref[...]
    v = v_ref[...]
    bonus = _seg_sum(r_ref[...] * (k0_ref[...] + k1_ref[...]) * rk_ref[...], ones_bd)
    y = yf_ref[...] + yb_ref[...] + bonus * v
    mu = _seg_sum(y, ones_bd) * (1.0 / HEAD_DIM)
    yc = y - mu
    var = _seg_sum(yc * yc, ones_bd) * (1.0 / HEAD_DIM)
    yn = yc * lax.rsqrt(var + RW_GN_EPS)
    o_ref[...] = ((yn * gw_ref[...] + gb_ref[...]) * g_ref[...]).astype(o_ref.dtype)


def _rw_readout_call(prep, yf, yb, r_k, gn_w, gn_b, ones512, tm):
    R = prep.shape[0]
    col = lambda cb: pl.BlockSpec((tm, GROUP_W), lambda i: (i, cb))
    vec = pl.BlockSpec((1, GROUP_W), lambda i: (0, 0))
    return pl.pallas_call(
        _rw_readout_kernel,
        out_shape=jax.ShapeDtypeStruct((R, GROUP_W), BF16),
        grid=(R // tm,),
        in_specs=[col(PB_R), col(PB_V), col(5), col(8), col(PB_G), col(0), col(0), vec, vec, vec,
                  pl.BlockSpec((GROUP_W, GROUP_W), lambda i: (0, 0))],
        out_specs=col(0),
        compiler_params=_cparams(("arbitrary",)),
        name="mixer_b_readout",
    )(prep, prep, prep, prep, prep, yf, yb, r_k, gn_w, gn_b, ones512)


def _s5_tables(a_re, a_im, log_dt, b_re, b_im, c_re, c_im):
    Lc, G, N, P = S5_CHUNK, S5_GROUPS, S5_N, S5_P
    dt = jnp.exp(log_dt)[..., None]
    lam_re, lam_im = dt * a_re, dt * a_im
    tau = jnp.arange(Lc + 1, dtype=F32)[:, None, None, None]
    mag = jnp.exp(tau * lam_re)
    pw_re, pw_im = mag * jnp.cos(tau * lam_im), mag * jnp.sin(tau * lam_im)
    ab_re, ab_im = pw_re[1], pw_im[1]
    den = a_re * a_re + a_im * a_im
    nr = ab_re - 1.0
    cf_re, cf_im = (nr * a_re + ab_im * a_im) / den, (ab_im * a_re - nr * a_im) / den
    bp_re = cf_re[..., None] * b_re[None] - cf_im[..., None] * b_im[None]
    bp_im = cf_re[..., None] * b_im[None] + cf_im[..., None] * b_re[None]
    pb_re = pw_re[..., None] * bp_re[None] - pw_im[..., None] * bp_im[None]
    pb_im = pw_re[..., None] * bp_im[None] + pw_im[..., None] * bp_re[None]
    kk = (jnp.einsum('gqn,tdgnp->tdgqp', c_re, pb_re[:Lc]) - jnp.einsum('gqn,tdgnp->tdgqp', c_im, pb_im[:Lc]))
    jj = np.arange(Lc)[:, None]
    ii = np.arange(Lc)[None, :]
    dist = np.abs(ii - jj)
    kf = kk[dist, 0] * jnp.asarray(ii >= jj, F32)[..., None, None, None]
    kb = kk[dist, 1] * jnp.asarray(ii <= jj, F32)[..., None, None, None]
    tz = jnp.transpose(kf + kb, (2, 0, 4, 1, 3))
    jr = np.arange(Lc)
    emap = lambda pbx, order, d: jnp.transpose(pbx[order, d], (1, 0, 3, 2))
    em = jnp.stack([emap(pb_re, Lc - 1 - jr, 0), emap(pb_im, Lc - 1 - jr, 0),
                    emap(pb_re, jr, 1), emap(pb_im, jr, 1)], axis=3)

    def gmap(order, d):
        pr, pi = pw_re[order, d], pw_im[order, d]
        cp_re = c_re[None] * pr[:, :, None, :] - c_im[None] * pi[:, :, None, :]
        cp_im = c_re[None] * pi[:, :, None, :] + c_im[None] * pr[:, :, None, :]
        to = lambda t: jnp.transpose(t, (1, 3, 0, 2))
        return to(cp_re), to(-cp_im)

    gk = jnp.stack(gmap(jr + 1, 0) + gmap(Lc - jr, 1), axis=1)
    NO, NQ, NG = S5_OCTETS, 2, 4
    tz, em, gk = tz.astype(BF16), em.astype(BF16), gk.astype(BF16)

    def bdiag(blocks):
        n, c = blocks.shape[1], blocks.shape[3]
        return jnp.concatenate(
            [jnp.pad(blocks[:, g], ((0, 0), (0, 0), (g * c, (n - 1 - g) * c))) for g in range(n)], axis=1)

    def perm_rows(m, dims, order):
        no, r, c = m.shape
        m = jnp.transpose(m.reshape((no,) + dims + (c,)), (0,) + tuple(1 + o for o in order) + (len(dims) + 1,))
        return m.reshape(no, r, c)

    tr = lambda m: jnp.swapaxes(m, 1, 2)
    gjp, rgkn = (8, Lc, P), (NQ, NG, 4, N)
    wy = perm_rows(bdiag(tz.reshape(NO, 8, Lc * P, Lc * P)), gjp, (1, 0, 2))
    wy = tr(perm_rows(tr(wy), gjp, (1, 0, 2)))
    we = perm_rows(bdiag(em.reshape(NO, 8, Lc * P, 4 * N)), gjp, (1, 0, 2))
    we = tr(perm_rows(tr(we), rgkn, (0, 2, 1, 3)))
    wz = jnp.concatenate([wy, we], axis=-1)
    gm = perm_rows(bdiag(gk.reshape(NO, 8, 4 * N, Lc * P)), rgkn, (0, 2, 1, 3))
    gm = tr(perm_rows(tr(gm), gjp, (1, 0, 2)))
    apow = jnp.stack([pw_re[Lc, 0], pw_im[Lc, 0], pw_re[Lc, 1], pw_im[Lc, 1]], axis=1)
    apow = jnp.transpose(apow.reshape(G // NG, NG, 4, N), (0, 2, 1, 3)).reshape(G // NG, 4, NG * N)
    return wz, gm, apow


def _s5_local_kernel(u_ref, wz_ref, y_ref, e_ref):
    Lc = u_ref.shape[1]
    lhs = jnp.concatenate([u_ref[:, j, :] for j in range(Lc)], axis=-1).astype(BF16)
    z = _dot(lhs, wz_ref[...])
    for i in range(Lc):
        y_ref[:, i, :] = z[:, i * LANE:(i + 1) * LANE]
    e_ref[...] = z[:, Lc * LANE:]


def _s5_local_call(proj3, wz):
    NR, Lc, _ = proj3.shape
    NO, K, N = wz.shape
    NE = N - Lc * LANE
    tr = NR // 8
    cu = COL_DU // LANE
    return pl.pallas_call(
        _s5_local_kernel,
        out_shape=(jax.ShapeDtypeStruct((NR, Lc, GROUP_W), F32), jax.ShapeDtypeStruct((NR, NO * NE), F32)),
        grid=(NO, NR // tr),
        in_specs=[
            pl.BlockSpec((tr, Lc, LANE), lambda o, i: (i, 0, cu + o)),
            pl.BlockSpec((None, K, N), lambda o, i: (o, 0, 0)),
        ],
        out_specs=(pl.BlockSpec((tr, Lc, LANE), lambda o, i: (i, 0, o)),
                   pl.BlockSpec((tr, NE), lambda o, i: (i, o))),
        compiler_params=_cparams(("arbitrary", "arbitrary")),
        name="mixer_d_local",
    )(proj3, wz)


def _s5_scan_kernel(e_ref, ap_ref, x_ref, *, n_batch, nlc, ncc):
    NS = e_ref.shape[1] // 4
    af_re, af_im = ap_ref[0:1, :], ap_ref[1:2, :]
    ab_re, ab_im = ap_ref[2:3, :], ap_ref[3:4, :]

    def step(a_re, a_im, x_re, x_im, e):
        return a_re * x_re - a_im * x_im + e[:, :NS], a_re * x_im + a_im * x_re + e[:, NS:]

    def phase(row0, n_chunks, stride, carry):
        n_tiles = n_chunks // 8

        def body(t, carry):
            out = []
            for b in range(n_batch):
                xf_re, xf_im, xb_re, xb_im = carry[4 * b:4 * b + 4]
                rf = pl.ds(pl.multiple_of(row0 + b * stride + t * 8, 8), 8)
                rb = pl.ds(pl.multiple_of(row0 + b * stride + (n_tiles - 1 - t) * 8, 8), 8)
                ef = e_ref[rf, 0:2 * NS]
                eb = e_ref[rb, 2 * NS:4 * NS]
                xf_in, xb_in = [], [None] * 8
                for j in range(8):
                    xf_in.append(jnp.concatenate([xf_re, xf_im], axis=-1))
                    xf_re, xf_im = step(af_re, af_im, xf_re, xf_im, ef[j:j + 1, :])
                for j in reversed(range(8)):
                    xb_in[j] = jnp.concatenate([xb_re, xb_im], axis=-1)
                    xb_re, xb_im = step(ab_re, ab_im, xb_re, xb_im, eb[j:j + 1, :])
                x_ref[rf, 0:2 * NS] = jnp.concatenate(xf_in, axis=0)
                x_ref[rb, 2 * NS:4 * NS] = jnp.concatenate(xb_in, axis=0)
                out += [xf_re, xf_im, xb_re, xb_im]
            return tuple(out)

        return lax.fori_loop(0, n_tiles, body, carry)

    zero = jnp.zeros((1, NS), F32)
    carry = phase(n_batch * nlc, ncc, ncc, (zero,) * (4 * n_batch))
    phase(0, nlc, nlc, carry)


def _s5_scan_call(e, apow, n_batch, seq, ctx):
    NR, NEall = e.shape
    NQ8, _, NS = apow.shape
    return pl.pallas_call(
        functools.partial(_s5_scan_kernel, n_batch=n_batch, nlc=seq // S5_CHUNK, ncc=ctx // S5_CHUNK),
        out_shape=jax.ShapeDtypeStruct((NR, NEall), F32),
        grid=(NQ8,),
        in_specs=[
            pl.BlockSpec((NR, 4 * NS), lambda q: (0, q)),
            pl.BlockSpec((None, 4, NS), lambda q: (q, 0, 0)),
        ],
        out_specs=pl.BlockSpec((NR, 4 * NS), lambda q: (0, q)),
        compiler_params=_cparams(("arbitrary",)),
        name="mixer_d_scan",
    )(e, apow)


def _s5_carry_kernel(x_ref, gm_ref, yl_ref, y_ref):
    Lc = yl_ref.shape[1]
    y = _dot(x_ref[...].astype(BF16), gm_ref[...])
    for i in range(Lc):
        y_ref[:, i, :] = yl_ref[:, i, :] + y[:, i * LANE:(i + 1) * LANE]


def _s5_carry_call(xin, gm, yloc):
    NR, Lc, _ = yloc.shape
    NO, NE, N = gm.shape
    tr = NR // 8
    return pl.pallas_call(
        _s5_carry_kernel,
        out_shape=jax.ShapeDtypeStruct((NR, Lc, GROUP_W), F32),
        grid=(NO, NR // tr),
        in_specs=[
            pl.BlockSpec((tr, NE), lambda o, i: (i, o)),
            pl.BlockSpec((None, NE, N), lambda o, i: (o, 0, 0)),
            pl.BlockSpec((tr, Lc, LANE), lambda o, i: (i, 0, o)),
        ],
        out_specs=pl.BlockSpec((tr, Lc, LANE), lambda o, i: (i, 0, o)),
        compiler_params=_cparams(("arbitrary", "arbitrary")),
        name="mixer_d_carry",
    )(xin, gm, yloc)


def _s5_out_kernel(y_ref, u_ref, d_ref, w_ref, b_ref, o_ref):
    y = y_ref[...] + d_ref[...] * u_ref[...]
    c = math.sqrt(2.0 / math.pi)
    y = 0.5 * y * (1.0 + jnp.tanh(c * (y + 0.044715 * (y * y * y))))
    z = _dot(y.astype(BF16), w_ref[...]) + b_ref[...]
    o_ref[...] = (y * _sigmoid(z)).astype(o_ref.dtype)


def _s5_out_call(y_tok, proj, d_skip, glu_w, glu_b, tm):
    R = proj.shape[0]
    vec = pl.BlockSpec((1, GROUP_W), lambda i: (0, 0))
    return pl.pallas_call(
        _s5_out_kernel,
        out_shape=jax.ShapeDtypeStruct((R, GROUP_W), BF16),
        grid=(R // tm,),
        in_specs=[
            pl.BlockSpec((tm, GROUP_W), lambda i: (i, 0)),
            pl.BlockSpec((tm, GROUP_W), lambda i: (i, COL_DU // GROUP_W)),
            vec,
            pl.BlockSpec((GROUP_W, GROUP_W), lambda i: (0, 0)),
            vec,
        ],
        out_specs=pl.BlockSpec((tm, GROUP_W), lambda i: (i, 0)),
        compiler_params=_cparams(("arbitrary",)),
        name="mixer_d_out",
    )(y_tok, proj, d_skip, glu_w, glu_b)


def _rope_tables(n_tokens):
    t = jnp.arange(n_tokens)
    nf = HEAD_DIM // 4
    inv = 1.0 / (ROPE_BASE ** (jnp.arange(nf, dtype=F32) / nf))

    def ang(pp):
        a = pp.astype(F32)[:, None] * inv[None, :]
        return jnp.concatenate([a, a], -1)

    a = jnp.concatenate([ang(t // GRID_W), ang(t % GRID_W)], -1)
    cos, sin = jnp.cos(a), jnp.sin(a)
    sign = np.where((np.arange(HEAD_DIM) % 32) < 16, -1.0, 1.0).astype(np.float32)
    cos2 = jnp.concatenate([cos, cos], -1)
    sin2 = jnp.concatenate([sin * sign, sin * sign], -1)
    return cos2, sin2


def _block_ones(n):
    return jnp.asarray(np.kron(np.eye(n // HEAD_DIM), np.ones((HEAD_DIM, HEAD_DIM))), BF16)


def _permute_w_in(w_in):
    cuts = np.cumsum([512, 512, 512, 512, 512, 512, 32, 32, 96, 512, 128, 128, 512])
    seg = lambda i: w_in[..., (0 if i == 0 else cuts[i - 1]):cuts[i]]
    L, D = w_in.shape[:2]
    pad = jnp.zeros((L, D, 96), w_in.dtype)
    parts = [seg(0), seg(1), seg(2), seg(3), seg(4), seg(5), seg(9), seg(12),
             seg(6), seg(7), seg(8), pad, seg(10), seg(11)]
    return jnp.concatenate(parts, axis=-1).astype(BF16)


def _forward(x, c, ctx, c_ctx, w_ada, b_ada, norm_g, ffn1_wg, ffn1_wu, ffn1_wd, ffn2_wg, ffn2_wu, ffn2_wd,
             w_in, w_out, na_q_g, na_k_g, na_rpb, rw_conv, rw_w0, rw_w_up, rw_a0, rw_a_up, rw_g_up,
             rw_k_k, rw_k_a, rw_r_k, rw_gn_w, rw_gn_b, wa_q_g, wa_k_g, wa_sink, s5_a_re, s5_a_im,
             s5_log_dt, s5_b_re, s5_b_im, s5_c_re, s5_c_im, s5_d, s5_glu_w, s5_glu_b, *, tm, tf):
    B, SEQ, D = x.shape
    CTX = ctx.shape[1]
    L = w_ada.shape[0]
    n_lat = B * SEQ
    n_qb = SEQ // QBLK

    bf = lambda t: t.astype(BF16)
    f1g, f1u, f1d, f2g, f2u, f2d = map(bf, (ffn1_wg, ffn1_wu, ffn1_wd, ffn2_wg, ffn2_wu, ffn2_wd))
    w_in_p = _permute_w_in(w_in)
    w_out_b = bf(w_out)
    glu_w_b = bf(s5_glu_w)
    ones128, ones512 = _block_ones(LANE), _block_ones(GROUP_W)
    cos_t, sin_t = _rope_tables(SEQ)
    na_bias = [_na_bias_table(na_rpb[l], n_qb) for l in range(L)]
    zr = lambda r, cdim: jnp.zeros((L, r, cdim), F32)
    wlr = jnp.concatenate([
        jnp.concatenate([rw_w_up[:, 0], rw_w_up[:, 1], zr(32, 3 * GROUP_W)], axis=-1),
        jnp.concatenate([zr(32, 2 * GROUP_W), rw_a_up[:, 0], rw_a_up[:, 1], zr(32, GROUP_W)], axis=-1),
        jnp.concatenate([zr(96, 4 * GROUP_W), rw_g_up], axis=-1),
        zr(96, 5 * GROUP_W)], axis=1).astype(BF16)
    s5_tabs = jax.vmap(_s5_tables)(s5_a_re, s5_a_im, s5_log_dt, s5_b_re, s5_b_im, s5_c_re, s5_c_im)
    tile2 = lambda t: jnp.concatenate([t, t], axis=-1)

    cc = jnp.concatenate([c, c_ctx[None], jnp.zeros((8 - B - 1, D), F32)], axis=0)
    mods_all = _ada_call(cc, w_ada, b_ada)[:, :B + 1].reshape(L, B + 1, N_MOD, D)

    xs = jnp.concatenate([x.reshape(n_lat, D), ctx.reshape(B * CTX, D)], axis=0)
    R = xs.shape[0]
    for l in range(L):
        want_ctx = l < L - 1
        mods = mods_all[l]
        ng = norm_g[l]
        xs = _ffn_call(xs, mods, ng[0:1], f1g, f1u, f1d, l, 0, R, SEQ, B, tm, tf)
        proj = _win_call(xs, mods, ng[1:2], w_in_p, l, SEQ, B, tm, D_IN_PAD // 3)
        o_a = _na_call(proj, na_bias[l], tile2(na_q_g[l][None]), tile2(na_k_g[l][None]), ones128,
                       B, SEQ, CTX, want_ctx)
        prep = _rw_prep_call(proj, rw_conv[l], wlr[l], rw_w0[l], rw_a0[l], rw_k_k[l][None], rw_k_a[l][None],
                             ones512, B, SEQ, CTX, min(tm, 256))
        vec = lambda t: t.reshape(1, GROUP_W)
        yf, yb = _rw_scan_call(prep, B, SEQ, CTX)
        o_b = _rw_readout_call(prep, yf, yb, vec(rw_r_k[l]), vec(rw_gn_w[l]), vec(rw_gn_b[l]), ones512, tm)
        o_c = _wa_call(proj, wa_sink[l], cos_t, sin_t, tile2(wa_q_g[l][None]), tile2(wa_k_g[l][None]), ones128,
                       B, SEQ, CTX, want_ctx)
        wz, gm, apow = (t[l] for t in s5_tabs)
        yloc, e_loc = _s5_local_call(proj.reshape(R // S5_CHUNK, S5_CHUNK, D_IN_PAD), bf(wz))
        xin = _s5_scan_call(e_loc, apow, B, SEQ, CTX)
        y_tok = _s5_carry_call(xin, bf(gm), yloc).reshape(R, GROUP_W)
        o_d = _s5_out_call(y_tok, proj, vec(s5_d[l]), glu_w_b[l], vec(s5_glu_b[l]), tm)
        n_rows = R if want_ctx else n_lat
        xs = _wout_call(xs, (o_a, o_b, o_c, o_d), mods, w_out_b, l, n_rows, SEQ, B, tm)
        xs = _ffn_call(xs, mods, ng[2:3], f2g, f2u, f2d, l, 6, n_rows, SEQ, B, tm, tf)
    return xs[:n_lat].reshape(B, SEQ, D)


def kernel(x, c, ctx, c_ctx, w_ada, b_ada, norm_g, ffn1_wg, ffn1_wu, ffn1_wd, ffn2_wg, ffn2_wu, ffn2_wd, w_in, w_out, na_q_g, na_k_g, na_rpb, rw_conv, rw_w0, rw_w_up, rw_a0, rw_a_up, rw_g_up, rw_k_k, rw_k_a, rw_r_k, rw_gn_w, rw_gn_b, wa_q_g, wa_k_g, wa_sink, s5_a_re, s5_a_im, s5_log_dt, s5_b_re, s5_b_im, s5_c_re, s5_c_im, s5_d, s5_glu_w, s5_glu_b):
    return _forward(x, c, ctx, c_ctx, w_ada, b_ada, norm_g, ffn1_wg, ffn1_wu, ffn1_wd, ffn2_wg, ffn2_wu, ffn2_wd,
                    w_in, w_out, na_q_g, na_k_g, na_rpb, rw_conv, rw_w0, rw_w_up, rw_a0, rw_a_up, rw_g_up,
                    rw_k_k, rw_k_a, rw_r_k, rw_gn_w, rw_gn_b, wa_q_g, wa_k_g, wa_sink, s5_a_re, s5_a_im,
                    s5_log_dt, s5_b_re, s5_b_im, s5_c_re, s5_c_im, s5_d, s5_glu_w, s5_glu_b, tm=512, tf=512)
```

```python
import functools
import math

import numpy as np
import jax
import jax.numpy as jnp
from jax import lax
from jax.experimental import pallas as pl
from jax.experimental.pallas import tpu as pltpu

F32 = jnp.float32
BF16 = jnp.bfloat16

D_MODEL = 2048
GRID_W = 64
HEAD_DIM = 64
GROUP_W = D_MODEL // 4
N_MOD = 9
NORM_EPS = 1e-6
ROPE_BASE = 10000.0
NEG_INF = -1e30

NA_HEADS = GROUP_W // HEAD_DIM
NA_WIN_R = 8
NA_WIN_C = 16
RW_HEADS = GROUP_W // HEAD_DIM
RW_DECAY_RANK = 32
RW_ICLR_RANK = 32
RW_GATE_RANK = 96
RW_GN_EPS = 64e-5
WA_HEADS = GROUP_W // HEAD_DIM
WA_KV_HEADS = 2
WA_GROUP = WA_HEADS // WA_KV_HEADS
WA_WINDOW = 128
S5_P = 16
S5_GROUPS = GROUP_W // S5_P
S5_N = 64

LANE = 128
QBLK = 2 * GRID_W
NA_KBLKS = 5
RW_CHUNK = 64
S5_CHUNK = 8
S5_OCTETS = 4
VMEM_LIMIT = 56 * 1024 * 1024

COL_AQ, COL_AK, COL_AV = 0, 512, 1024
COL_BR = 1536
COL_CQ = 3072
COL_DU = 3584
COL_BLR = 4096
COL_CK, COL_CV = 4352, 4480
D_IN_PAD = 4608

PB_R, PB_V, PB_KK, PB_G = 0, 1, 2, 9
PB_N = 10


def _cparams(sem):
    return pltpu.CompilerParams(dimension_semantics=sem, vmem_limit_bytes=VMEM_LIMIT)


def _dot(a, b):
    return jnp.dot(a, b, preferred_element_type=F32)


def _dot_nt(a, b):
    return lax.dot_general(a, b, (((1,), (1,)), ((), ())), preferred_element_type=F32)


def _dot_tn(a, b):
    return lax.dot_general(a, b, (((0,), (0,)), ((), ())), preferred_element_type=F32)


def _sigmoid(x):
    return 1.0 / (1.0 + jnp.exp(-x))


def _modulate(x, g, shift, scale):
    ms = jnp.mean(x * x, axis=-1, keepdims=True)
    return (x * lax.rsqrt(ms + NORM_EPS) * g) * (1.0 + scale) + shift


def _seg_sum(x, ones_bd):
    hi = x.astype(BF16)
    lo = (x - hi.astype(F32)).astype(BF16)
    return _dot(hi, ones_bd) + _dot(lo, ones_bd)


def _head_rmsnorm(x, ones_bd):
    ms = _seg_sum(x * x, ones_bd) * (1.0 / HEAD_DIM)
    return x * lax.rsqrt(ms + NORM_EPS)


def _ada_kernel(c_ref, w_ref, b_ref, o_ref):
    c = c_ref[...]
    s = (c * _sigmoid(c)).astype(BF16)
    o_ref[...] = _dot(s, w_ref[...].astype(BF16)) + b_ref[...]


def _ada_call(cc, w_ada, b_ada):
    L, D, N = w_ada.shape
    tn = 1024
    return pl.pallas_call(
        _ada_kernel,
        out_shape=jax.ShapeDtypeStruct((L, 8, N), F32),
        grid=(L, N // tn),
        in_specs=[
            pl.BlockSpec((8, D), lambda l, j: (0, 0)),
            pl.BlockSpec((None, D, tn), lambda l, j: (l, 0, j)),
            pl.BlockSpec((None, 1, tn), lambda l, j: (l, 0, j)),
        ],
        out_specs=pl.BlockSpec((None, 8, tn), lambda l, j: (l, 0, j)),
        compiler_params=_cparams(("arbitrary", "arbitrary")),
        name="adaln",
    )(cc, w_ada, b_ada.reshape(L, 1, N))


def _ffn_kernel(x_ref, mod_ref, g_ref, wg_ref, wu_ref, wd_ref, o_ref, h_sc, acc_sc, *, mi):
    j = pl.program_id(1)

    @pl.when(j == 0)
    def _():
        h = _modulate(x_ref[...], g_ref[...], mod_ref[mi:mi + 1, :], mod_ref[mi + 1:mi + 2, :])
        h_sc[...] = h.astype(BF16)
        acc_sc[...] = jnp.zeros_like(acc_sc)

    h = h_sc[...]
    gate = _dot(h, wg_ref[...])
    up = _dot(h, wu_ref[...])
    a = (gate * _sigmoid(gate) * up).astype(BF16)
    acc_sc[...] += _dot(a, wd_ref[...])

    @pl.when(j == pl.num_programs(1) - 1)
    def _():
        o_ref[...] = x_ref[...] + 0.5 * mod_ref[mi + 2:mi + 3, :] * acc_sc[...]


def _ffn_call(xs, mods, g, wg, wu, wd, l, mi, n_rows, rows_per_seq, n_batch, tm, tf):
    D = xs.shape[1]
    F = wg.shape[2]
    tps = rows_per_seq // tm
    return pl.pallas_call(
        functools.partial(_ffn_kernel, mi=mi),
        out_shape=jax.ShapeDtypeStruct((n_rows, D), F32),
        grid=(n_rows // tm, F // tf),
        in_specs=[
            pl.BlockSpec((tm, D), lambda i, j: (i, 0)),
            pl.BlockSpec((None, N_MOD, D), lambda i, j: (jnp.minimum(i // tps, n_batch), 0, 0)),
            pl.BlockSpec((1, D), lambda i, j: (0, 0)),
            pl.BlockSpec((None, D, tf), lambda i, j: (l, 0, j)),
            pl.BlockSpec((None, D, tf), lambda i, j: (l, 0, j)),
            pl.BlockSpec((None, tf, D), lambda i, j: (l, j, 0)),
        ],
        out_specs=pl.BlockSpec((tm, D), lambda i, j: (i, 0)),
        scratch_shapes=[pltpu.VMEM((tm, D), BF16), pltpu.VMEM((tm, D), F32)],
        compiler_params=_cparams(("arbitrary", "arbitrary")),
        name="ffn",
    )(xs, mods, g, wg, wu, wd)


def _win_kernel(x_ref, mod_ref, g_ref, w_ref, o_ref, h_sc):
    @pl.when(pl.program_id(1) == 0)
    def _():
        h = _modulate(x_ref[...], g_ref[...], mod_ref[3:4, :], mod_ref[4:5, :])
        h_sc[...] = h.astype(BF16)

    o_ref[...] = _dot(h_sc[...], w_ref[...])


def _win_call(xs, mods, g, w_in, l, rows_per_seq, n_batch, tm, tn):
    R, D = xs.shape
    N = w_in.shape[2]
    tps = rows_per_seq // tm
    return pl.pallas_call(
        _win_kernel,
        out_shape=jax.ShapeDtypeStruct((R, N), F32),
        grid=(R // tm, N // tn),
        in_specs=[
            pl.BlockSpec((tm, D), lambda i, j: (i, 0)),
            pl.BlockSpec((None, N_MOD, D), lambda i, j: (jnp.minimum(i // tps, n_batch), 0, 0)),
            pl.BlockSpec((1, D), lambda i, j: (0, 0)),
            pl.BlockSpec((None, D, tn), lambda i, j: (l, 0, j)),
        ],
        out_specs=pl.BlockSpec((tm, tn), lambda i, j: (i, j)),
        scratch_shapes=[pltpu.VMEM((tm, D), BF16)],
        compiler_params=_cparams(("arbitrary", "arbitrary")),
        name="in_proj",
    )(xs, mods, g, w_in)


def _wout_kernel(x_ref, oa_ref, ob_ref, oc_ref, od_ref, mod_ref, w_ref, o_ref):
    acc = _dot(oa_ref[...], w_ref[0 * GROUP_W:1 * GROUP_W, :])
    acc += _dot(ob_ref[...], w_ref[1 * GROUP_W:2 * GROUP_W, :])
    acc += _dot(oc_ref[...], w_ref[2 * GROUP_W:3 * GROUP_W, :])
    acc += _dot(od_ref[...], w_ref[3 * GROUP_W:4 * GROUP_W, :])
    o_ref[...] = x_ref[...] + mod_ref[5:6, :] * acc


def _wout_call(xs, outs, mods, w_out, l, n_rows, rows_per_seq, n_batch, tm):
    D = xs.shape[1]
    tps = rows_per_seq // tm
    ospec = pl.BlockSpec((tm, GROUP_W), lambda i: (i, 0))
    return pl.pallas_call(
        _wout_kernel,
        out_shape=jax.ShapeDtypeStruct((n_rows, D), F32),
        grid=(n_rows // tm,),
        in_specs=[
            pl.BlockSpec((tm, D), lambda i: (i, 0)),
            ospec, ospec, ospec, ospec,
            pl.BlockSpec((None, N_MOD, D), lambda i: (jnp.minimum(i // tps, n_batch), 0, 0)),
            pl.BlockSpec((None, D, D), lambda i: (l, 0, 0)),
        ],
        out_specs=pl.BlockSpec((tm, D), lambda i: (i, 0)),
        compiler_params=_cparams(("arbitrary",)),
        name="out_proj",
    )(xs, *outs, mods, w_out)


def _na_case_reps(n_qb):
    return (0, 1, 2, n_qb - 2, n_qb - 1)


def _na_start(p, n_qb):
    return jnp.clip(p - 2, 0, n_qb - NA_KBLKS)


def _na_bias_table(rpb, n_qb):
    H = rpb.shape[0]
    rows = 2 * n_qb
    qc = np.arange(GRID_W)[:, None]
    kc = np.arange(GRID_W)[None, :]
    cs = np.clip(qc - NA_WIN_C // 2, 0, GRID_W - NA_WIN_C)
    okc = (kc >= cs) & (kc < cs + NA_WIN_C)
    dc = np.where(okc, kc - qc + NA_WIN_C - 1, 0)
    rp = rpb.astype(F32).reshape(H, 2 * NA_WIN_R - 1, 2 * NA_WIN_C - 1)
    blocks = jnp.where(jnp.asarray(okc)[None, None], jnp.take(rp, jnp.asarray(dc), axis=2), NEG_INF)
    blocks = blocks.astype(BF16)
    neg = jnp.full((H, GRID_W, GRID_W), NEG_INF, BF16)
    cases = []
    for p in _na_case_reps(n_qb):
        start = min(max(p - 2, 0), n_qb - NA_KBLKS)
        qrows = []
        for qr in range(2):
            qa = 2 * p + qr
            rs = min(max(qa - NA_WIN_R // 2, 0), rows - NA_WIN_R)
            krow = []
            for kr in range(2 * NA_KBLKS):
                ka = 2 * start + kr
                krow.append(blocks[:, ka - qa + NA_WIN_R - 1] if rs <= ka < rs + NA_WIN_R else neg)
            qrows.append(jnp.concatenate(krow, axis=-1))
        cases.append(jnp.concatenate(qrows, axis=-2))
    return jnp.stack(cases, axis=0)


def _na_kernel(q_ref, k_ref, v_ref, kc_ref, vc_ref, bias_ref, qg_ref, kg_ref, ones_ref, o_ref,
               kn_sc, vn_sc, kcn_sc, vcn_sc, *, n_qb, seq, ctx):
    p = pl.program_id(2)
    ones_bd = ones_ref[...]
    scale = HEAD_DIM ** -0.5

    @pl.when(p == 0)
    def _():
        kg = kg_ref[...]
        rows = 512

        def body(i, carry):
            sl = pl.ds(pl.multiple_of(i * rows, rows), rows)
            kn_sc[sl, :] = (_head_rmsnorm(k_ref[sl, :], ones_bd) * kg).astype(BF16)
            vn_sc[sl, :] = v_ref[sl, :].astype(BF16)
            return carry

        lax.fori_loop(0, seq // rows, body, 0)
        kcn_sc[...] = (_head_rmsnorm(kc_ref[...], ones_bd) * kg).astype(BF16)
        vcn_sc[...] = vc_ref[...].astype(BF16)

    nq = q_ref.shape[0] // QBLK
    n_lat = n_qb // nq
    q = (_head_rmsnorm(q_ref[...], ones_bd) * (qg_ref[...] * scale)).astype(BF16)
    kcn = kcn_sc[...]
    vcn = vcn_sc[...]
    chains = [(qb, h) for qb in range(nq) for h in range(2)]
    hsl = lambda h: slice(h * HEAD_DIM, (h + 1) * HEAD_DIM)
    qh = [q[qb * QBLK:(qb + 1) * QBLK, hsl(h)] for qb, h in chains]
    s_c = [_dot_nt(qh[c], kcn[:, hsl(h)]) for c, (qb, h) in enumerate(chains)]
    m_c = [jnp.max(t, axis=-1, keepdims=True) for t in s_c]

    def finish(o, den):
        rows = [jnp.concatenate([o[qb * 2 + h] / den[qb * 2 + h] for h in range(2)], axis=-1) for qb in range(nq)]
        o_ref[...] = jnp.concatenate(rows, axis=0).astype(o_ref.dtype)

    @pl.when(p < n_lat)
    def _():
        starts = [_na_start(p * nq + qb, n_qb) for qb in range(nq)]
        cases = [p * nq + qb - starts[qb] for qb in range(nq)]
        sls = [pl.ds(pl.multiple_of(st * QBLK, QBLK), NA_KBLKS * QBLK) for st in starts]
        kw = [kn_sc[sl, :] for sl in sls]
        vw = [vn_sc[sl, :] for sl in sls]
        s_n = [_dot_nt(qh[c], kw[qb][:, hsl(h)]) + bias_ref[cases[qb], h].astype(F32)
               for c, (qb, h) in enumerate(chains)]
        m = [jnp.maximum(jnp.max(s_n[c], axis=-1, keepdims=True), m_c[c]) for c in range(len(chains))]
        p_n = [jnp.exp(s_n[c] - m[c]) for c in range(len(chains))]
        p_c = [jnp.exp(s_c[c] - m[c]) for c in range(len(chains))]
        den = [jnp.sum(p_n[c], axis=-1, keepdims=True) + jnp.sum(p_c[c], axis=-1, keepdims=True)
               for c in range(len(chains))]
        o = [_dot(p_n[c].astype(BF16), vw[qb][:, hsl(h)]) + _dot(p_c[c].astype(BF16), vcn[:, hsl(h)])
             for c, (qb, h) in enumerate(chains)]
        finish(o, den)

    @pl.when(p >= n_lat)
    def _():
        p_c = [jnp.exp(s_c[c] - m_c[c]) for c in range(len(chains))]
        den = [jnp.sum(t, axis=-1, keepdims=True) for t in p_c]
        o = [_dot(p_c[c].astype(BF16), vcn[:, hsl(h)]) for c, (qb, h) in enumerate(chains)]
        finish(o, den)


def _na_call(proj, bias, qg, kg, ones_bd, n_batch, seq, ctx, want_ctx):
    R = proj.shape[0]
    nq = 2
    n_qb = seq // QBLK
    n_lat, n_ctx = n_qb // nq, ctx // (nq * QBLK)
    steps = n_lat + (n_ctx if want_ctx else 0)
    ctx_kb = n_batch * seq // ctx

    def qrow(b, p):
        return jnp.where(p < n_lat, b * n_lat + p, n_batch * n_lat + b * n_ctx + (p - n_lat))

    cq, ck, cv = COL_AQ // LANE, COL_AK // LANE, COL_AV // LANE
    return pl.pallas_call(
        functools.partial(_na_kernel, n_qb=n_qb, seq=seq, ctx=ctx),
        out_shape=jax.ShapeDtypeStruct((R, GROUP_W), BF16),
        grid=(n_batch, NA_HEADS // 2, steps),
        in_specs=[
            pl.BlockSpec((nq * QBLK, LANE), lambda b, hp, p: (qrow(b, p), cq + hp)),
            pl.BlockSpec((seq, LANE), lambda b, hp, p: (b, ck + hp)),
            pl.BlockSpec((seq, LANE), lambda b, hp, p: (b, cv + hp)),
            pl.BlockSpec((ctx, LANE), lambda b, hp, p: (ctx_kb + b, ck + hp)),
            pl.BlockSpec((ctx, LANE), lambda b, hp, p: (ctx_kb + b, cv + hp)),
            pl.BlockSpec((5, 2, QBLK, NA_KBLKS * QBLK), lambda b, hp, p: (0, hp, 0, 0)),
            pl.BlockSpec((1, LANE), lambda b, hp, p: (0, 0)),
            pl.BlockSpec((1, LANE), lambda b, hp, p: (0, 0)),
            pl.BlockSpec((LANE, LANE), lambda b, hp, p: (0, 0)),
        ],
        out_specs=pl.BlockSpec((nq * QBLK, LANE), lambda b, hp, p: (qrow(b, p), hp)),
        scratch_shapes=[pltpu.VMEM((seq, LANE), BF16), pltpu.VMEM((seq, LANE), BF16),
                        pltpu.VMEM((ctx, LANE), BF16), pltpu.VMEM((ctx, LANE), BF16)],
        compiler_params=_cparams(("arbitrary", "arbitrary", "arbitrary")),
        name="mixer_a",
    )(proj, proj, proj, proj, proj, bias, qg, kg, ones_bd)


def _rope(x, cos, sin_signed, first_half):
    rot = jnp.where(first_half, pltpu.roll(x, LANE - 16, 1), pltpu.roll(x, 16, 1))
    return x * cos + rot * sin_signed


def _wa_kernel(sink_ref, q_ref, k_ref, v_ref, kc_ref, vc_ref, cosk_ref, sin_k_ref, cosq_ref, sinq_ref,
               qg_ref, kg_ref, ones_ref, wmask_ref, o_ref, kn_sc, vn_sc, kcn_sc, vcn_sc, *, n_qb, seq, ctx):
    p = pl.program_id(1)
    ones_bd = ones_ref[...]
    scale = HEAD_DIM ** -0.5
    lane = lax.broadcasted_iota(jnp.int32, (1, LANE), 1)
    first_half = (lane % 32) < 16

    @pl.when(p == 0)
    def _():
        kg = kg_ref[...]
        rows = 512

        def body(i, carry):
            sl = pl.ds(pl.multiple_of(i * rows, rows), rows)
            kn = _head_rmsnorm(k_ref[sl, :], ones_bd) * kg
            kn_sc[sl, :] = _rope(kn, cosk_ref[sl, :], sin_k_ref[sl, :], first_half).astype(BF16)
            vn_sc[sl, :] = v_ref[sl, :].astype(BF16)
            return carry

        lax.fori_loop(0, seq // rows, body, 0)
        kcn_sc[...] = (_head_rmsnorm(kc_ref[...], ones_bd) * kg).astype(BF16)
        vcn_sc[...] = vc_ref[...].astype(BF16)

    nq = q_ref.shape[0] // QBLK
    n_lat = n_qb // nq
    qg = qg_ref[...] * scale
    is_lat = p < n_lat
    cosq = jnp.where(is_lat, cosq_ref[...], 1.0)
    sinq = jnp.where(is_lat, sinq_ref[...], 0.0)
    qh = []
    for c in range(4):
        qn = _head_rmsnorm(q_ref[:, c * LANE:(c + 1) * LANE], ones_bd) * qg
        qn = _rope(qn, cosq, sinq, first_half).astype(BF16)
        qh += [qn[:, :HEAD_DIM], qn[:, HEAD_DIM:]]
    kcn = kcn_sc[...]
    vcn = vcn_sc[...]
    chains = [(qb, kh) for qb in range(nq) for kh in range(WA_KV_HEADS)]
    nch = range(len(chains))
    hsl = lambda kh: slice(kh * HEAD_DIM, (kh + 1) * HEAD_DIM)
    qs = [jnp.concatenate([qh[kh * WA_GROUP + g][qb * QBLK:(qb + 1) * QBLK, :] for g in range(WA_GROUP)], axis=0)
          for qb, kh in chains]
    sk = [jnp.concatenate([jnp.full((QBLK, 1), sink_ref[kh * WA_GROUP + g], F32) for g in range(WA_GROUP)],
                          axis=0) for qb, kh in chains]
    s_c = [_dot_nt(qs[c], kcn[:, hsl(chains[c][1])]) for c in nch]
    m_c = [jnp.maximum(jnp.max(s_c[c], axis=-1, keepdims=True), sk[c]) for c in nch]

    def finish(o, den):
        rows = []
        for qb in range(nq):
            heads = []
            for kh in range(WA_KV_HEADS):
                c = qb * WA_KV_HEADS + kh
                oc = o[c] / den[c]
                heads += [oc[g * QBLK:(g + 1) * QBLK, :] for g in range(WA_GROUP)]
            rows.append(jnp.concatenate(heads, axis=-1))
        o_ref[...] = jnp.concatenate(rows, axis=0).astype(o_ref.dtype)

    @pl.when(is_lat)
    def _():
        blk = [p * nq + qb for qb in range(nq)]
        ws = [jnp.clip((bq - 1) * QBLK, 0, seq - 3 * QBLK) for bq in blk]
        sls = [pl.ds(pl.multiple_of(w, QBLK), 3 * QBLK) for w in ws]
        kw = [kn_sc[sl, :] for sl in sls]
        vw = [vn_sc[sl, :] for sl in sls]
        case = [jnp.where(bq == 0, 0, jnp.where(bq == n_qb - 1, 2, 1)) for bq in blk]
        msk = [jnp.concatenate([wmask_ref[cs]] * WA_GROUP, axis=0) for cs in case]
        s_w = [_dot_nt(qs[c], kw[chains[c][0]][:, hsl(chains[c][1])]) + msk[chains[c][0]] for c in nch]
        m = [jnp.maximum(jnp.max(s_w[c], axis=-1, keepdims=True), m_c[c]) for c in nch]
        p_w = [jnp.exp(s_w[c] - m[c]) for c in nch]
        p_c = [jnp.exp(s_c[c] - m[c]) for c in nch]
        den = [jnp.sum(p_w[c], axis=-1, keepdims=True) + jnp.sum(p_c[c], axis=-1, keepdims=True)
               + jnp.exp(sk[c] - m[c]) for c in nch]
        o = [_dot(p_w[c].astype(BF16), vw[chains[c][0]][:, hsl(chains[c][1])])
             + _dot(p_c[c].astype(BF16), vcn[:, hsl(chains[c][1])]) for c in nch]
        finish(o, den)

    @pl.when(jnp.logical_not(is_lat))
    def _():
        p_c = [jnp.exp(s_c[c] - m_c[c]) for c in nch]
        den = [jnp.sum(p_c[c], axis=-1, keepdims=True) + jnp.exp(sk[c] - m_c[c]) for c in nch]
        o = [_dot(p_c[c].astype(BF16), vcn[:, hsl(chains[c][1])]) for c in nch]
        finish(o, den)


def _wa_mask_table():
    i = np.arange(QBLK)[:, None]
    j = np.arange(3 * QBLK)[None, :]
    shifts = (0, QBLK, 2 * QBLK)
    return jnp.asarray(np.stack([np.where(np.abs(j - sh - i) <= WA_WINDOW, 0.0, NEG_INF) for sh in shifts]), F32)


def _wa_call(proj, sink, cos_t, sin_t, qg, kg, ones_bd, n_batch, seq, ctx, want_ctx):
    R = proj.shape[0]
    nq = 2
    n_qb = seq // QBLK
    n_lat, n_ctx = n_qb // nq, ctx // (nq * QBLK)
    steps = n_lat + (n_ctx if want_ctx else 0)
    ctx_kb = n_batch * seq // ctx

    def qrow(b, p):
        return jnp.where(p < n_lat, b * n_lat + p, n_batch * n_lat + b * n_ctx + (p - n_lat))

    ck, cv = COL_CK // LANE, COL_CV // LANE
    return pl.pallas_call(
        functools.partial(_wa_kernel, n_qb=n_qb, seq=seq, ctx=ctx),
        out_shape=jax.ShapeDtypeStruct((R, GROUP_W), BF16),
        grid=(n_batch, steps),
        in_specs=[
            pl.BlockSpec(memory_space=pltpu.SMEM),
            pl.BlockSpec((nq * QBLK, GROUP_W), lambda b, p: (qrow(b, p), COL_CQ // GROUP_W)),
            pl.BlockSpec((seq, LANE), lambda b, p: (b, ck)),
            pl.BlockSpec((seq, LANE), lambda b, p: (b, cv)),
            pl.BlockSpec((ctx, LANE), lambda b, p: (ctx_kb + b, ck)),
            pl.BlockSpec((ctx, LANE), lambda b, p: (ctx_kb + b, cv)),
            pl.BlockSpec((seq, LANE), lambda b, p: (0, 0)),
            pl.BlockSpec((seq, LANE), lambda b, p: (0, 0)),
            pl.BlockSpec((nq * QBLK, LANE), lambda b, p: (jnp.minimum(p, n_lat - 1), 0)),
            pl.BlockSpec((nq * QBLK, LANE), lambda b, p: (jnp.minimum(p, n_lat - 1), 0)),
            pl.BlockSpec((1, LANE), lambda b, p: (0, 0)),
            pl.BlockSpec((1, LANE), lambda b, p: (0, 0)),
            pl.BlockSpec((LANE, LANE), lambda b, p: (0, 0)),
            pl.BlockSpec((3, QBLK, 3 * QBLK), lambda b, p: (0, 0, 0)),
        ],
        out_specs=pl.BlockSpec((nq * QBLK, GROUP_W), lambda b, p: (qrow(b, p), 0)),
        scratch_shapes=[pltpu.VMEM((seq, LANE), BF16), pltpu.VMEM((seq, LANE), BF16),
                        pltpu.VMEM((ctx, LANE), BF16), pltpu.VMEM((ctx, LANE), BF16)],
        compiler_params=_cparams(("arbitrary", "arbitrary")),
        name="mixer_c",
    )(sink, proj, proj, proj, proj, proj, cos_t, sin_t, cos_t, sin_t, qg, kg, ones_bd, _wa_mask_table())


def _rw_prep_kernel(x_ref, prev_ref, next_ref, lr_ref, conv_ref, wlr_ref, w0_ref, a0_ref, kk_ref, ka_ref,
                    ones_ref, o_ref, *, tiles_lat, tiles_ctx, n_lat_tiles):
    i = pl.program_id(0)
    tm = x_ref.shape[0]
    in_lat = i < n_lat_tiles
    first = jnp.where(in_lat, i % tiles_lat == 0, (i - n_lat_tiles) % tiles_ctx == 0)
    last = jnp.where(in_lat, i % tiles_lat == tiles_lat - 1, (i - n_lat_tiles) % tiles_ctx == tiles_ctx - 1)
    x = x_ref[...]
    prev_row = jnp.where(first, 0.0, prev_ref[7:8, :])
    next_row = jnp.where(last, 0.0, next_ref[0:1, :])
    row = lax.broadcasted_iota(jnp.int32, (tm, 1), 0)
    x_prev = jnp.where(row == 0, prev_row, pltpu.roll(x, 1, 0))
    x_next = jnp.where(row == tm - 1, next_row, pltpu.roll(x, tm - 1, 0))
    cw = conv_ref[...]
    y = x_prev * cw[0:1, :] + x * cw[1:2, :] + x_next * cw[2:3, :]
    r = y[:, 0:GROUP_W]
    k = y[:, GROUP_W:2 * GROUP_W]
    v = y[:, 2 * GROUP_W:3 * GROUP_W]

    lr = lr_ref[...]
    lane = lax.broadcasted_iota(jnp.int32, (1, lr.shape[1]), 1)
    c1 = RW_DECAY_RANK
    c2 = c1 + RW_ICLR_RANK
    c3 = c2 + RW_GATE_RANK
    act = jnp.where(lane < c1, jnp.tanh(lr),
                    jnp.where(lane < c2, lr, jnp.where(lane < c3, _sigmoid(lr), 0.0)))
    up = _dot(act.astype(BF16), wlr_ref[...])

    ones_bd = ones_ref[...]
    kk = k * kk_ref[...]
    nrm = jnp.sqrt(_seg_sum(kk * kk, ones_bd))
    kk = kk / jnp.maximum(nrm, 1e-12)

    o_ref[:, PB_R * GROUP_W:(PB_R + 1) * GROUP_W] = r
    o_ref[:, PB_V * GROUP_W:(PB_V + 1) * GROUP_W] = v
    o_ref[:, PB_KK * GROUP_W:(PB_KK + 1) * GROUP_W] = kk
    o_ref[:, PB_G * GROUP_W:(PB_G + 1) * GROUP_W] = up[:, 4 * GROUP_W:5 * GROUP_W]
    ka = ka_ref[...]
    for d in range(2):
        z = w0_ref[d:d + 1, :] + up[:, d * GROUP_W:(d + 1) * GROUP_W]
        sp = jnp.maximum(-z, 0.0) + jnp.log(1.0 + jnp.exp(-jnp.abs(z)))
        lw = -jnp.exp(-sp - 0.5)
        a = _sigmoid(a0_ref[d:d + 1, :] + up[:, (2 + d) * GROUP_W:(3 + d) * GROUP_W])
        kd = k * (1.0 + (a - 1.0) * ka)
        base = 3 + 3 * d
        o_ref[:, base * GROUP_W:(base + 1) * GROUP_W] = lw
        o_ref[:, (base + 1) * GROUP_W:(base + 2) * GROUP_W] = a
        o_ref[:, (base + 2) * GROUP_W:(base + 3) * GROUP_W] = kd


def _rw_prep_call(proj, conv_w, wlr, w0, a0, k_k, k_a, ones512, n_batch, seq, ctx, tm):
    R = proj.shape[0]
    n_lat_tiles = n_batch * seq // tm
    nb8 = R // 8
    t8 = tm // 8
    wide = 3 * GROUP_W
    cb = COL_BR // wide
    return pl.pallas_call(
        functools.partial(_rw_prep_kernel, tiles_lat=seq // tm, tiles_ctx=ctx // tm, n_lat_tiles=n_lat_tiles),
        out_shape=jax.ShapeDtypeStruct((R, PB_N * GROUP_W), F32),
        grid=(R // tm,),
        in_specs=[
            pl.BlockSpec((tm, wide), lambda i: (i, cb)),
            pl.BlockSpec((8, wide), lambda i: (jnp.maximum(i * t8 - 1, 0), cb)),
            pl.BlockSpec((8, wide), lambda i: (jnp.minimum((i + 1) * t8, nb8 - 1), cb)),
            pl.BlockSpec((tm, 256), lambda i: (i, COL_BLR // 256)),
            pl.BlockSpec((3, wide), lambda i: (0, 0)),
            pl.BlockSpec((256, 5 * GROUP_W), lambda i: (0, 0)),
            pl.BlockSpec((2, GROUP_W), lambda i: (0, 0)),
            pl.BlockSpec((2, GROUP_W), lambda i: (0, 0)),
            pl.BlockSpec((1, GROUP_W), lambda i: (0, 0)),
            pl.BlockSpec((1, GROUP_W), lambda i: (0, 0)),
            pl.BlockSpec((GROUP_W, GROUP_W), lambda i: (0, 0)),
        ],
        out_specs=pl.BlockSpec((tm, PB_N * GROUP_W), lambda i: (i, 0)),
        compiler_params=_cparams(("arbitrary",)),
        name="mixer_b_prep",
    )(proj, proj, proj, proj, conv_w, wlr, w0, a0, k_k, k_a, ones512)


RW_SPLIT = 1
_NN = ((1,), (0,))
_NT = ((1,), (1,))
_TN = ((0,), (0,))


def _split(x, n):
    parts = []
    for _ in range(n):
        p = x.astype(BF16)
        parts.append(p)
        x = x - p.astype(F32)
    return parts


def _sdot(a, b, dims):
    n = max(len(a), len(b))
    acc = None
    for i, ai in enumerate(a):
        for j, bj in enumerate(b):
            if i + j < n:
                t = lax.dot_general(ai, bj, (dims, ((), ())), preferred_element_type=F32)
                acc = t if acc is None else acc + t
    return acc


def _rw_chunks(at, bt, kt, rt, v, gam, s0, strict, incl):
    n = len(at)
    ids = range(n)
    C = at[0].shape[0]
    sp = lambda t: _split(t, RW_SPLIT)
    b1 = lambda t: [t.astype(BF16)]
    ar = [b1(jnp.concatenate([at[i], rt[i]], axis=0)) for i in ids]
    bk = [sp(jnp.concatenate([bt[i], kt[i]], axis=0)) for i in ids]
    g4 = [_sdot(ar[i], bk[i][:1], _NT) for i in ids]
    a_ab = [jnp.where(strict[i], g4[i][:C, :C], 0.0) for i in ids]
    a_ak = [jnp.where(strict[i], g4[i][:C, C:], 0.0) for i in ids]
    a_rb = [jnp.where(incl[i], g4[i][C:, :C], 0.0) for i in ids]
    a_rk = [jnp.where(incl[i], g4[i][C:, C:], 0.0) for i in ids]
    wv = [_sdot(b1(a_ak[i]), b1(v[i]), _NN) for i in ids]
    x = [jnp.concatenate([at[i], wv[i]], axis=1) for i in ids]

    row = lax.broadcasted_iota(jnp.int32, (C, C), 0)
    col = lax.broadcasted_iota(jnp.int32, (C, C), 1)
    eye = (row == col).astype(F32)
    blk = 16
    same = (row // blk) == (col // blk)
    p = [jnp.where(same, a_ab[i], 0.0) for i in ids]
    m = [eye + p[i] for i in ids]
    for _ in range(int(math.log2(blk)) - 1):
        pb_ = [b1(p[i]) for i in ids]
        p = [_sdot(pb_[i], pb_[i], _NN) for i in ids]
        m = [m[i] + _sdot(b1(m[i]), b1(p[i]), _NN) for i in ids]
    while blk < C:
        wider = (row // (2 * blk)) == (col // (2 * blk))
        join = wider & jnp.logical_not(same)
        mb = [b1(m[i]) for i in ids]
        t = [_sdot(b1(jnp.where(join, a_ab[i], 0.0)), mb[i], _NN) for i in ids]
        m = [m[i] + _sdot(mb[i], b1(t[i]), _NN) for i in ids]
        same, blk = wider, 2 * blk
    x = [_sdot(b1(m[i]), b1(x[i]), _NN) for i in ids]

    s0b = [sp(s0[i]) for i in ids]
    u = [x[i][:, HEAD_DIM:] + _sdot(sp(x[i][:, :HEAD_DIM]), s0b[i], _NT) for i in ids]
    uv = [sp(jnp.concatenate([u[i], v[i]], axis=0)) for i in ids]
    y = [_sdot(b1(jnp.concatenate([a_rb[i], a_rk[i]], axis=1)), uv[i][:1], _NN)
         + _sdot(sp(rt[i]), s0b[i], _NT) for i in ids]
    s1 = [(s0[i] + _sdot(uv[i], bk[i], _TN)) * gam[i] for i in ids]
    return y, s1


def _rw_scan_kernel(*refs, n_batch, nc):
    n_in = 12 * n_batch
    yfl_ref, ybl_ref, yfc_ref, ybc_ref, s_sc = refs[n_in:]
    s = pl.program_id(0)

    @pl.when(s == 0)
    def _():
        s_sc[...] = jnp.zeros_like(s_sc)

    C = refs[0].shape[0]
    row = lax.broadcasted_iota(jnp.int32, (C, C), 0)
    col = lax.broadcasted_iota(jnp.int32, (C, C), 1)
    chains = dict(at=[], bt=[], kt=[], rt=[], v=[], gam=[], s0=[], strict=[], incl=[])
    for b in range(n_batch):
        for d, reverse in enumerate((False, True)):
            r_ref, v_ref, kk_ref, lw_ref, a_ref, k_ref = refs[12 * b + 6 * d:12 * b + 6 * d + 6]
            incl, strict = (row <= col, row < col) if reverse else (row >= col, row > col)
            lw = lw_ref[...]
            kk = kk_ref[...]
            v = v_ref[...]
            cum = _sdot([incl.astype(BF16)], _split(lw, 3), _NN)
            e_l = jnp.exp(cum)
            e_n = jnp.exp(-cum)
            at = -(kk * jnp.exp(cum - lw))
            bt = kk * a_ref[...] * e_n
            kt = k_ref[...] * e_n
            rt = r_ref[...] * e_l
            gam = e_l[0:1, :] if reverse else e_l[C - 1:C, :]
            for h in range(RW_HEADS):
                hs = slice(h * HEAD_DIM, (h + 1) * HEAD_DIM)
                for name, val in (("at", at), ("bt", bt), ("kt", kt), ("rt", rt), ("v", v), ("gam", gam)):
                    chains[name].append(val[:, hs])
                chains["s0"].append(s_sc[b, d, h])
                chains["strict"].append(strict)
                chains["incl"].append(incl)
    y, s1 = _rw_chunks(**chains)
    ys = {}
    for b in range(n_batch):
        for d in range(2):
            base = (b * 2 + d) * RW_HEADS
            for h in range(RW_HEADS):
                s_sc[b, d, h] = s1[base + h]
            ys[b, d] = jnp.concatenate(y[base:base + RW_HEADS], axis=-1)

    @pl.when(s < nc)
    def _():
        for b in range(n_batch):
            yfc_ref[b] = ys[b, 0]
            ybc_ref[b] = ys[b, 1]

    @pl.when(s >= nc)
    def _():
        for b in range(n_batch):
            yfl_ref[b] = ys[b, 0]
            ybl_ref[b] = ys[b, 1]


def _rw_scan_call(prep, n_batch, seq, ctx):
    C = RW_CHUNK
    nc, nl = ctx // C, seq // C
    ctx_base = n_batch * nl

    def blk_f(b, s):
        return jnp.where(s < nc, ctx_base + b * nc + s, b * nl + (s - nc))

    def blk_b(b, s):
        return jnp.where(s < nc, ctx_base + b * nc + (nc - 1 - s), b * nl + (nl - 1 - (s - nc)))

    def col(blk, b, cb):
        return pl.BlockSpec((C, GROUP_W), lambda s: (blk(b, s), cb))

    in_specs = []
    for b in range(n_batch):
        in_specs += [col(blk_f, b, c) for c in (PB_R, PB_V, PB_KK, 3, 4, 5)]
        in_specs += [col(blk_b, b, c) for c in (PB_R, PB_V, PB_KK, 6, 7, 8)]
    lat = jax.ShapeDtypeStruct((n_batch, seq, GROUP_W), F32)
    cx = jax.ShapeDtypeStruct((n_batch, ctx, GROUP_W), F32)
    blk3 = (n_batch, C, GROUP_W)
    out_specs = (
        pl.BlockSpec(blk3, lambda s: (0, jnp.maximum(s - nc, 0), 0)),
        pl.BlockSpec(blk3, lambda s: (0, nl - 1 - jnp.maximum(s - nc, 0), 0)),
        pl.BlockSpec(blk3, lambda s: (0, jnp.minimum(s, nc - 1), 0)),
        pl.BlockSpec(blk3, lambda s: (0, nc - 1 - jnp.minimum(s, nc - 1), 0)),
    )
    yfl, ybl, yfc, ybc = pl.pallas_call(
        functools.partial(_rw_scan_kernel, n_batch=n_batch, nc=nc),
        out_shape=(lat, lat, cx, cx),
        grid=(nc + nl,),
        in_specs=in_specs,
        out_specs=out_specs,
        scratch_shapes=[pltpu.VMEM((n_batch, 2, RW_HEADS, HEAD_DIM, HEAD_DIM), F32)],
        compiler_params=_cparams(("arbitrary",)),
        name="mixer_b_scan",
    )(*([prep] * (12 * n_batch)))
    flat = lambda a, c: jnp.concatenate([a.reshape(n_batch * seq, GROUP_W), c.reshape(n_batch * ctx, GROUP_W)], 0)
    return flat(yfl, yfc), flat(ybl, ybc)


def _rw_readout_kernel(r_ref, v_ref, k0_ref, k1_ref, g_ref, yf_ref, yb_ref, rk_ref, gw_ref, gb_ref, ones_ref,
                       o_ref):
    ones_bd = ones_ref[...]
    v = v_ref[...]
    bonus = _seg_sum(r_ref[...] * (k0_ref[...] + k1_ref[...]) * rk_ref[...], ones_bd)
    y = yf_ref[...] + yb_ref[...] + bonus * v
    mu = _seg_sum(y, ones_bd) * (1.0 / HEAD_DIM)
    yc = y - mu
    var = _seg_sum(yc * yc, ones_bd) * (1.0 / HEAD_DIM)
    yn = yc * lax.rsqrt(var + RW_GN_EPS)
    o_ref[...] = ((yn * gw_ref[...] + gb_ref[...]) * g_ref[...]).astype(o_ref.dtype)


def _rw_readout_call(prep, yf, yb, r_k, gn_w, gn_b, ones512, tm):
    R = prep.shape[0]
    col = lambda cb: pl.BlockSpec((tm, GROUP_W), lambda i: (i, cb))
    vec = pl.BlockSpec((1, GROUP_W), lambda i: (0, 0))
    return pl.pallas_call(
        _rw_readout_kernel,
        out_shape=jax.ShapeDtypeStruct((R, GROUP_W), BF16),
        grid=(R // tm,),
        in_specs=[col(PB_R), col(PB_V), col(5), col(8), col(PB_G), col(0), col(0), vec, vec, vec,
                  pl.BlockSpec((GROUP_W, GROUP_W), lambda i: (0, 0))],
        out_specs=col(0),
        compiler_params=_cparams(("arbitrary",)),
        name="mixer_b_readout",
    )(prep, prep, prep, prep, prep, yf, yb, r_k, gn_w, gn_b, ones512)


def _s5_tables(a_re, a_im, log_dt, b_re, b_im, c_re, c_im):
    Lc, G, N, P = S5_CHUNK, S5_GROUPS, S5_N, S5_P
    dt = jnp.exp(log_dt)[..., None]
    lam_re, lam_im = dt * a_re, dt * a_im
    tau = jnp.arange(Lc + 1, dtype=F32)[:, None, None, None]
    mag = jnp.exp(tau * lam_re)
    pw_re, pw_im = mag * jnp.cos(tau * lam_im), mag * jnp.sin(tau * lam_im)
    ab_re, ab_im = pw_re[1], pw_im[1]
    den = a_re * a_re + a_im * a_im
    nr = ab_re - 1.0
    cf_re, cf_im = (nr * a_re + ab_im * a_im) / den, (ab_im * a_re - nr * a_im) / den
    bp_re = cf_re[..., None] * b_re[None] - cf_im[..., None] * b_im[None]
    bp_im = cf_re[..., None] * b_im[None] + cf_im[..., None] * b_re[None]
    pb_re = pw_re[..., None] * bp_re[None] - pw_im[..., None] * bp_im[None]
    pb_im = pw_re[..., None] * bp_im[None] + pw_im[..., None] * bp_re[None]
    kk = (jnp.einsum('gqn,tdgnp->tdgqp', c_re, pb_re[:Lc]) - jnp.einsum('gqn,tdgnp->tdgqp', c_im, pb_im[:Lc]))
    jj = np.arange(Lc)[:, None]
    ii = np.arange(Lc)[None, :]
    dist = np.abs(ii - jj)
    kf = kk[dist, 0] * jnp.asarray(ii >= jj, F32)[..., None, None, None]
    kb = kk[dist, 1] * jnp.asarray(ii <= jj, F32)[..., None, None, None]
    tz = jnp.transpose(kf + kb, (2, 0, 4, 1, 3))
    jr = np.arange(Lc)
    emap = lambda pbx, order, d: jnp.transpose(pbx[order, d], (1, 0, 3, 2))
    em = jnp.stack([emap(pb_re, Lc - 1 - jr, 0), emap(pb_im, Lc - 1 - jr, 0),
                    emap(pb_re, jr, 1), emap(pb_im, jr, 1)], axis=3)

    def gmap(order, d):
        pr, pi = pw_re[order, d], pw_im[order, d]
        cp_re = c_re[None] * pr[:, :, None, :] - c_im[None] * pi[:, :, None, :]
        cp_im = c_re[None] * pi[:, :, None, :] + c_im[None] * pr[:, :, None, :]
        to = lambda t: jnp.transpose(t, (1, 3, 0, 2))
        return to(cp_re), to(-cp_im)

    gk = jnp.stack(gmap(jr + 1, 0) + gmap(Lc - jr, 1), axis=1)
    NO, NQ, NG = S5_OCTETS, 2, 4
    tz, em, gk = tz.astype(BF16), em.astype(BF16), gk.astype(BF16)

    def bdiag(blocks):
        n, c = blocks.shape[1], blocks.shape[3]
        return jnp.concatenate(
            [jnp.pad(blocks[:, g], ((0, 0), (0, 0), (g * c, (n - 1 - g) * c))) for g in range(n)], axis=1)

    def perm_rows(m, dims, order):
        no, r, c = m.shape
        m = jnp.transpose(m.reshape((no,) + dims + (c,)), (0,) + tuple(1 + o for o in order) + (len(dims) + 1,))
        return m.reshape(no, r, c)

    tr = lambda m: jnp.swapaxes(m, 1, 2)
    gjp, rgkn = (8, Lc, P), (NQ, NG, 4, N)
    wy = perm_rows(bdiag(tz.reshape(NO, 8, Lc * P, Lc * P)), gjp, (1, 0, 2))
    wy = tr(perm_rows(tr(wy), gjp, (1, 0, 2)))
    we = perm_rows(bdiag(em.reshape(NO, 8, Lc * P, 4 * N)), gjp, (1, 0, 2))
    we = tr(perm_rows(tr(we), rgkn, (0, 2, 1, 3)))
    wz = jnp.concatenate([wy, we], axis=-1)
    gm = perm_rows(bdiag(gk.reshape(NO, 8, 4 * N, Lc * P)), rgkn, (0, 2, 1, 3))
    gm = tr(perm_rows(tr(gm), gjp, (1, 0, 2)))
    apow = jnp.stack([pw_re[Lc, 0], pw_im[Lc, 0], pw_re[Lc, 1], pw_im[Lc, 1]], axis=1)
    apow = jnp.transpose(apow.reshape(G // NG, NG, 4, N), (0, 2, 1, 3)).reshape(G // NG, 4, NG * N)
    return wz, gm, apow


def _s5_local_kernel(u_ref, wz_ref, y_ref, e_ref):
    Lc = u_ref.shape[1]
    lhs = jnp.concatenate([u_ref[:, j, :] for j in range(Lc)], axis=-1).astype(BF16)
    z = _dot(lhs, wz_ref[...])
    for i in range(Lc):
        y_ref[:, i, :] = z[:, i * LANE:(i + 1) * LANE]
    e_ref[...] = z[:, Lc * LANE:]


def _s5_local_call(proj3, wz):
    NR, Lc, _ = proj3.shape
    NO, K, N = wz.shape
    NE = N - Lc * LANE
    tr = NR // 8
    cu = COL_DU // LANE
    return pl.pallas_call(
        _s5_local_kernel,
        out_shape=(jax.ShapeDtypeStruct((NR, Lc, GROUP_W), F32), jax.ShapeDtypeStruct((NR, NO * NE), F32)),
        grid=(NO, NR // tr),
        in_specs=[
            pl.BlockSpec((tr, Lc, LANE), lambda o, i: (i, 0, cu + o)),
            pl.BlockSpec((None, K, N), lambda o, i: (o, 0, 0)),
        ],
        out_specs=(pl.BlockSpec((tr, Lc, LANE), lambda o, i: (i, 0, o)),
                   pl.BlockSpec((tr, NE), lambda o, i: (i, o))),
        compiler_params=_cparams(("arbitrary", "arbitrary")),
        name="mixer_d_local",
    )(proj3, wz)


def _s5_scan_kernel(e_ref, ap_ref, x_ref, *, n_batch, nlc, ncc):
    NS = e_ref.shape[1] // 4
    af_re, af_im = ap_ref[0:1, :], ap_ref[1:2, :]
    ab_re, ab_im = ap_ref[2:3, :], ap_ref[3:4, :]

    def step(a_re, a_im, x_re, x_im, e):
        return a_re * x_re - a_im * x_im + e[:, :NS], a_re * x_im + a_im * x_re + e[:, NS:]

    def phase(row0, n_chunks, stride, carry):
        n_tiles = n_chunks // 8

        def body(t, carry):
            out = []
            for b in range(n_batch):
                xf_re, xf_im, xb_re, xb_im = carry[4 * b:4 * b + 4]
                rf = pl.ds(pl.multiple_of(row0 + b * stride + t * 8, 8), 8)
                rb = pl.ds(pl.multiple_of(row0 + b * stride + (n_tiles - 1 - t) * 8, 8), 8)
                ef = e_ref[rf, 0:2 * NS]
                eb = e_ref[rb, 2 * NS:4 * NS]
                xf_in, xb_in = [], [None] * 8
                for j in range(8):
                    xf_in.append(jnp.concatenate([xf_re, xf_im], axis=-1))
                    xf_re, xf_im = step(af_re, af_im, xf_re, xf_im, ef[j:j + 1, :])
                for j in reversed(range(8)):
                    xb_in[j] = jnp.concatenate([xb_re, xb_im], axis=-1)
                    xb_re, xb_im = step(ab_re, ab_im, xb_re, xb_im, eb[j:j + 1, :])
                x_ref[rf, 0:2 * NS] = jnp.concatenate(xf_in, axis=0)
                x_ref[rb, 2 * NS:4 * NS] = jnp.concatenate(xb_in, axis=0)
                out += [xf_re, xf_im, xb_re, xb_im]
            return tuple(out)

        return lax.fori_loop(0, n_tiles, body, carry)

    zero = jnp.zeros((1, NS), F32)
    carry = phase(n_batch * nlc, ncc, ncc, (zero,) * (4 * n_batch))
    phase(0, nlc, nlc, carry)


def _s5_scan_call(e, apow, n_batch, seq, ctx):
    NR, NEall = e.shape
    NQ8, _, NS = apow.shape
    return pl.pallas_call(
        functools.partial(_s5_scan_kernel, n_batch=n_batch, nlc=seq // S5_CHUNK, ncc=ctx // S5_CHUNK),
        out_shape=jax.ShapeDtypeStruct((NR, NEall), F32),
        grid=(NQ8,),
        in_specs=[
            pl.BlockSpec((NR, 4 * NS), lambda q: (0, q)),
            pl.BlockSpec((None, 4, NS), lambda q: (q, 0, 0)),
        ],
        out_specs=pl.BlockSpec((NR, 4 * NS), lambda q: (0, q)),
        compiler_params=_cparams(("arbitrary",)),
        name="mixer_d_scan",
    )(e, apow)


def _s5_carry_kernel(x_ref, gm_ref, yl_ref, y_ref):
    Lc = yl_ref.shape[1]
    y = _dot(x_ref[...].astype(BF16), gm_ref[...])
    for i in range(Lc):
        y_ref[:, i, :] = yl_ref[:, i, :] + y[:, i * LANE:(i + 1) * LANE]


def _s5_carry_call(xin, gm, yloc):
    NR, Lc, _ = yloc.shape
    NO, NE, N = gm.shape
    tr = NR // 8
    return pl.pallas_call(
        _s5_carry_kernel,
        out_shape=jax.ShapeDtypeStruct((NR, Lc, GROUP_W), F32),
        grid=(NO, NR // tr),
        in_specs=[
            pl.BlockSpec((tr, NE), lambda o, i: (i, o)),
            pl.BlockSpec((None, NE, N), lambda o, i: (o, 0, 0)),
            pl.BlockSpec((tr, Lc, LANE), lambda o, i: (i, 0, o)),
        ],
        out_specs=pl.BlockSpec((tr, Lc, LANE), lambda o, i: (i, 0, o)),
        compiler_params=_cparams(("arbitrary", "arbitrary")),
        name="mixer_d_carry",
    )(xin, gm, yloc)


def _s5_out_kernel(y_ref, u_ref, d_ref, w_ref, b_ref, o_ref):
    y = y_ref[...] + d_ref[...] * u_ref[...]
    c = math.sqrt(2.0 / math.pi)
    y = 0.5 * y * (1.0 + jnp.tanh(c * (y + 0.044715 * (y * y * y))))
    z = _dot(y.astype(BF16), w_ref[...]) + b_ref[...]
    o_ref[...] = (y * _sigmoid(z)).astype(o_ref.dtype)


def _s5_out_call(y_tok, proj, d_skip, glu_w, glu_b, tm):
    R = proj.shape[0]
    vec = pl.BlockSpec((1, GROUP_W), lambda i: (0, 0))
    return pl.pallas_call(
        _s5_out_kernel,
        out_shape=jax.ShapeDtypeStruct((R, GROUP_W), BF16),
        grid=(R // tm,),
        in_specs=[
            pl.BlockSpec((tm, GROUP_W), lambda i: (i, 0)),
            pl.BlockSpec((tm, GROUP_W), lambda i: (i, COL_DU // GROUP_W)),
            vec,
            pl.BlockSpec((GROUP_W, GROUP_W), lambda i: (0, 0)),
            vec,
        ],
        out_specs=pl.BlockSpec((tm, GROUP_W), lambda i: (i, 0)),
        compiler_params=_cparams(("arbitrary",)),
        name="mixer_d_out",
    )(y_tok, proj, d_skip, glu_w, glu_b)


def _rope_tables(n_tokens):
    t = jnp.arange(n_tokens)
    nf = HEAD_DIM // 4
    inv = 1.0 / (ROPE_BASE ** (jnp.arange(nf, dtype=F32) / nf))

    def ang(pp):
        a = pp.astype(F32)[:, None] * inv[None, :]
        return jnp.concatenate([a, a], -1)

    a = jnp.concatenate([ang(t // GRID_W), ang(t % GRID_W)], -1)
    cos, sin = jnp.cos(a), jnp.sin(a)
    sign = np.where((np.arange(HEAD_DIM) % 32) < 16, -1.0, 1.0).astype(np.float32)
    cos2 = jnp.concatenate([cos, cos], -1)
    sin2 = jnp.concatenate([sin * sign, sin * sign], -1)
    return cos2, sin2


def _block_ones(n):
    return jnp.asarray(np.kron(np.eye(n // HEAD_DIM), np.ones((HEAD_DIM, HEAD_DIM))), BF16)


def _permute_w_in(w_in):
    cuts = np.cumsum([512, 512, 512, 512, 512, 512, 32, 32, 96, 512, 128, 128, 512])
    seg = lambda i: w_in[..., (0 if i == 0 else cuts[i - 1]):cuts[i]]
    L, D = w_in.shape[:2]
    pad = jnp.zeros((L, D, 96), w_in.dtype)
    parts = [seg(0), seg(1), seg(2), seg(3), seg(4), seg(5), seg(9), seg(12),
             seg(6), seg(7), seg(8), pad, seg(10), seg(11)]
    return jnp.concatenate(parts, axis=-1).astype(BF16)


def _forward(x, c, ctx, c_ctx, w_ada, b_ada, norm_g, ffn1_wg, ffn1_wu, ffn1_wd, ffn2_wg, ffn2_wu, ffn2_wd,
             w_in, w_out, na_q_g, na_k_g, na_rpb, rw_conv, rw_w0, rw_w_up, rw_a0, rw_a_up, rw_g_up,
             rw_k_k, rw_k_a, rw_r_k, rw_gn_w, rw_gn_b, wa_q_g, wa_k_g, wa_sink, s5_a_re, s5_a_im,
             s5_log_dt, s5_b_re, s5_b_im, s5_c_re, s5_c_im, s5_d, s5_glu_w, s5_glu_b, *, tm, tf):
    B, SEQ, D = x.shape
    CTX = ctx.shape[1]
    L = w_ada.shape[0]
    n_lat = B * SEQ
    n_qb = SEQ // QBLK

    bf = lambda t: t.astype(BF16)
    f1g, f1u, f1d, f2g, f2u, f2d = map(bf, (ffn1_wg, ffn1_wu, ffn1_wd, ffn2_wg, ffn2_wu, ffn2_wd))
    w_in_p = _permute_w_in(w_in)
    w_out_b = bf(w_out)
    glu_w_b = bf(s5_glu_w)
    ones128, ones512 = _block_ones(LANE), _block_ones(GROUP_W)
    cos_t, sin_t = _rope_tables(SEQ)
    na_bias = [_na_bias_table(na_rpb[l], n_qb) for l in range(L)]
    zr = lambda r, cdim: jnp.zeros((L, r, cdim), F32)
    wlr = jnp.concatenate([
        jnp.concatenate([rw_w_up[:, 0], rw_w_up[:, 1], zr(32, 3 * GROUP_W)], axis=-1),
        jnp.concatenate([zr(32, 2 * GROUP_W), rw_a_up[:, 0], rw_a_up[:, 1], zr(32, GROUP_W)], axis=-1),
        jnp.concatenate([zr(96, 4 * GROUP_W), rw_g_up], axis=-1),
        zr(96, 5 * GROUP_W)], axis=1).astype(BF16)
    s5_tabs = jax.vmap(_s5_tables)(s5_a_re, s5_a_im, s5_log_dt, s5_b_re, s5_b_im, s5_c_re, s5_c_im)
    tile2 = lambda t: jnp.concatenate([t, t], axis=-1)

    cc = jnp.concatenate([c, c_ctx[None], jnp.zeros((8 - B - 1, D), F32)], axis=0)
    mods_all = _ada_call(cc, w_ada, b_ada)[:, :B + 1].reshape(L, B + 1, N_MOD, D)

    xs = jnp.concatenate([x.reshape(n_lat, D), ctx.reshape(B * CTX, D)], axis=0)
    R = xs.shape[0]
    for l in range(L):
        want_ctx = l < L - 1
        mods = mods_all[l]
        ng = norm_g[l]
        xs = _ffn_call(xs, mods, ng[0:1], f1g, f1u, f1d, l, 0, R, SEQ, B, tm, tf)
        proj = _win_call(xs, mods, ng[1:2], w_in_p, l, SEQ, B, tm, D_IN_PAD // 3)
        o_a = _na_call(proj, na_bias[l], tile2(na_q_g[l][None]), tile2(na_k_g[l][None]), ones128,
                       B, SEQ, CTX, want_ctx)
        prep = _rw_prep_call(proj, rw_conv[l], wlr[l], rw_w0[l], rw_a0[l], rw_k_k[l][None], rw_k_a[l][None],
                             ones512, B, SEQ, CTX, min(tm, 256))
        vec = lambda t: t.reshape(1, GROUP_W)
        yf, yb = _rw_scan_call(prep, B, SEQ, CTX)
        o_b = _rw_readout_call(prep, yf, yb, vec(rw_r_k[l]), vec(rw_gn_w[l]), vec(rw_gn_b[l]), ones512, tm)
        o_c = _wa_call(proj, wa_sink[l], cos_t, sin_t, tile2(wa_q_g[l][None]), tile2(wa_k_g[l][None]), ones128,
                       B, SEQ, CTX, want_ctx)
        wz, gm, apow = (t[l] for t in s5_tabs)
        yloc, e_loc = _s5_local_call(proj.reshape(R // S5_CHUNK, S5_CHUNK, D_IN_PAD), bf(wz))
        xin = _s5_scan_call(e_loc, apow, B, SEQ, CTX)
        y_tok = _s5_carry_call(xin, bf(gm), yloc).reshape(R, GROUP_W)
        o_d = _s5_out_call(y_tok, proj, vec(s5_d[l]), glu_w_b[l], vec(s5_glu_b[l]), tm)
        n_rows = R if want_ctx else n_lat
        xs = _wout_call(xs, (o_a, o_b, o_c, o_d), mods, w_out_b, l, n_rows, SEQ, B, tm)
        xs = _ffn_call(xs, mods, ng[2:3], f2g, f2u, f2d, l, 6, n_rows, SEQ, B, tm, tf)
    return xs[:n_lat].reshape(B, SEQ, D)


def kernel(x, c, ctx, c_ctx, w_ada, b_ada, norm_g, ffn1_wg, ffn1_wu, ffn1_wd, ffn2_wg, ffn2_wu, ffn2_wd, w_in, w_out, na_q_g, na_k_g, na_rpb, rw_conv, rw_w0, rw_w_up, rw_a0, rw_a_up, rw_g_up, rw_k_k, rw_k_a, rw_r_k, rw_gn_w, rw_gn_b, wa_q_g, wa_k_g, wa_sink, s5_a_re, s5_a_im, s5_log_dt, s5_b_re, s5_b_im, s5_c_re, s5_c_im, s5_d, s5_glu_w, s5_glu_b):
    return _forward(x, c, ctx, c_ctx, w_ada, b_ada, norm_g, ffn1_wg, ffn1_wu, ffn1_wd, ffn2_wg, ffn2_wu, ffn2_wd,
                    w_in, w_out, na_q_g, na_k_g, na_rpb, rw_conv, rw_w0, rw_w_up, rw_a0, rw_a_up, rw_g_up,
                    rw_k_k, rw_k_a, rw_r_k, rw_gn_w, rw_gn_b, wa_q_g, wa_k_g, wa_sink, s5_a_re, s5_a_im,
                    s5_log_dt, s5_b_re, s5_b_im, s5_c_re, s5_c_im, s5_d, s5_glu_w, s5_glu_b, tm=512, tf=512)
```

```python
import functools
import math

import numpy as np
import jax
import jax.numpy as jnp
from jax import lax
from jax.experimental import pallas as pl
from jax.experimental.pallas import tpu as pltpu

F32 = jnp.float32
BF16 = jnp.bfloat16

D_MODEL = 2048
GRID_W = 64
HEAD_DIM = 64
GROUP_W = D_MODEL // 4
N_MOD = 9
NORM_EPS = 1e-6
ROPE_BASE = 10000.0
NEG_INF = -1e30

NA_HEADS = GROUP_W // HEAD_DIM
NA_WIN_R = 8
NA_WIN_C = 16
RW_HEADS = GROUP_W // HEAD_DIM
RW_DECAY_RANK = 32
RW_ICLR_RANK = 32
RW_GATE_RANK = 96
RW_GN_EPS = 64e-5
WA_HEADS = GROUP_W // HEAD_DIM
WA_KV_HEADS = 2
WA_GROUP = WA_HEADS // WA_KV_HEADS
WA_WINDOW = 128
S5_P = 16
S5_GROUPS = GROUP_W // S5_P
S5_N = 64

LANE = 128
QBLK = 2 * GRID_W
NA_KBLKS = 5
RW_CHUNK = 64
S5_CHUNK = 8
S5_OCTETS = 4
VMEM_LIMIT = 56 * 1024 * 1024

COL_AQ, COL_AK, COL_AV = 0, 512, 1024
COL_BR = 1536
COL_CQ = 3072
COL_DU = 3584
COL_BLR = 4096
COL_CK, COL_CV = 4352, 4480
D_IN_PAD = 4608

PB_R, PB_V, PB_KK, PB_G = 0, 1, 2, 9
PB_N = 10


def _cparams(sem):
    return pltpu.CompilerParams(dimension_semantics=sem, vmem_limit_bytes=VMEM_LIMIT)


def _dot(a, b):
    return jnp.dot(a, b, preferred_element_type=F32)


def _dot_nt(a, b):
    return lax.dot_general(a, b, (((1,), (1,)), ((), ())), preferred_element_type=F32)


def _dot_tn(a, b):
    return lax.dot_general(a, b, (((0,), (0,)), ((), ())), preferred_element_type=F32)


def _sigmoid(x):
    return 1.0 / (1.0 + jnp.exp(-x))


def _modulate(x, g, shift, scale):
    ms = jnp.mean(x * x, axis=-1, keepdims=True)
    return (x * lax.rsqrt(ms + NORM_EPS) * g) * (1.0 + scale) + shift


def _seg_sum(x, ones_bd):
    hi = x.astype(BF16)
    lo = (x - hi.astype(F32)).astype(BF16)
    return _dot(hi, ones_bd) + _dot(lo, ones_bd)


def _with_ones(v):
    one = jnp.ones((v.shape[0], HEAD_DIM), F32)
    return jnp.concatenate([v[:, :HEAD_DIM], one, v[:, HEAD_DIM:], one], axis=-1).astype(BF16)


def _prob(s, m):
    return jnp.exp((s - m).astype(BF16))


def _head_rmsnorm(x, ones_bd):
    ms = _seg_sum(x * x, ones_bd) * (1.0 / HEAD_DIM)
    return x * lax.rsqrt(ms + NORM_EPS)


def _ada_kernel(c_ref, w_ref, b_ref, o_ref):
    c = c_ref[...]
    s = (c * _sigmoid(c)).astype(BF16)
    o_ref[...] = _dot(s, w_ref[...].astype(BF16)) + b_ref[...]


def _ada_call(cc, w_ada, b_ada):
    L, D, N = w_ada.shape
    tn = 1024
    return pl.pallas_call(
        _ada_kernel,
        out_shape=jax.ShapeDtypeStruct((L, 8, N), F32),
        grid=(L, N // tn),
        in_specs=[
            pl.BlockSpec((8, D), lambda l, j: (0, 0)),
            pl.BlockSpec((None, D, tn), lambda l, j: (l, 0, j)),
            pl.BlockSpec((None, 1, tn), lambda l, j: (l, 0, j)),
        ],
        out_specs=pl.BlockSpec((None, 8, tn), lambda l, j: (l, 0, j)),
        compiler_params=_cparams(("arbitrary", "arbitrary")),
        name="adaln",
    )(cc, w_ada, b_ada.reshape(L, 1, N))


def _ffn_kernel(x_ref, mod_ref, g_ref, wg_ref, wu_ref, wd_ref, o_ref, h_sc, acc_sc, *, mi):
    j = pl.program_id(1)

    @pl.when(j == 0)
    def _():
        h = _modulate(x_ref[...], g_ref[...], mod_ref[mi:mi + 1, :], mod_ref[mi + 1:mi + 2, :])
        h_sc[...] = h.astype(BF16)
        acc_sc[...] = jnp.zeros_like(acc_sc)

    h = h_sc[...]
    gate = _dot(h, wg_ref[...])
    up = _dot(h, wu_ref[...])
    a = (gate * _sigmoid(gate) * up).astype(BF16)
    acc_sc[...] += _dot(a, wd_ref[...])

    @pl.when(j == pl.num_programs(1) - 1)
    def _():
        o_ref[...] = x_ref[...] + 0.5 * mod_ref[mi + 2:mi + 3, :] * acc_sc[...]


def _ffn_call(xs, mods, g, wg, wu, wd, l, mi, n_rows, rows_per_seq, n_batch, tm, tf):
    D = xs.shape[1]
    F = wg.shape[2]
    tps = rows_per_seq // tm
    return pl.pallas_call(
        functools.partial(_ffn_kernel, mi=mi),
        out_shape=jax.ShapeDtypeStruct((n_rows, D), F32),
        grid=(n_rows // tm, F // tf),
        in_specs=[
            pl.BlockSpec((tm, D), lambda i, j: (i, 0)),
            pl.BlockSpec((None, N_MOD, D), lambda i, j: (jnp.minimum(i // tps, n_batch), 0, 0)),
            pl.BlockSpec((1, D), lambda i, j: (0, 0)),
            pl.BlockSpec((None, D, tf), lambda i, j: (l, 0, j)),
            pl.BlockSpec((None, D, tf), lambda i, j: (l, 0, j)),
            pl.BlockSpec((None, tf, D), lambda i, j: (l, j, 0)),
        ],
        out_specs=pl.BlockSpec((tm, D), lambda i, j: (i, 0)),
        scratch_shapes=[pltpu.VMEM((tm, D), BF16), pltpu.VMEM((tm, D), F32)],
        compiler_params=_cparams(("arbitrary", "arbitrary")),
        name="ffn",
    )(xs, mods, g, wg, wu, wd)


def _win_kernel(x_ref, mod_ref, g_ref, w_ref, o_ref, h_sc):
    @pl.when(pl.program_id(1) == 0)
    def _():
        h = _modulate(x_ref[...], g_ref[...], mod_ref[3:4, :], mod_ref[4:5, :])
        h_sc[...] = h.astype(BF16)

    o_ref[...] = _dot(h_sc[...], w_ref[...])


def _win_call(xs, mods, g, w_in, l, rows_per_seq, n_batch, tm, tn):
    R, D = xs.shape
    N = w_in.shape[2]
    tps = rows_per_seq // tm
    return pl.pallas_call(
        _win_kernel,
        out_shape=jax.ShapeDtypeStruct((R, N), F32),
        grid=(R // tm, N // tn),
        in_specs=[
            pl.BlockSpec((tm, D), lambda i, j: (i, 0)),
            pl.BlockSpec((None, N_MOD, D), lambda i, j: (jnp.minimum(i // tps, n_batch), 0, 0)),
            pl.BlockSpec((1, D), lambda i, j: (0, 0)),
            pl.BlockSpec((None, D, tn), lambda i, j: (l, 0, j)),
        ],
        out_specs=pl.BlockSpec((tm, tn), lambda i, j: (i, j)),
        scratch_shapes=[pltpu.VMEM((tm, D), BF16)],
        compiler_params=_cparams(("arbitrary", "arbitrary")),
        name="in_proj",
    )(xs, mods, g, w_in)


def _wout_kernel(x_ref, oa_ref, ob_ref, oc_ref, od_ref, mod_ref, w_ref, o_ref):
    acc = _dot(oa_ref[...], w_ref[0 * GROUP_W:1 * GROUP_W, :])
    acc += _dot(ob_ref[...], w_ref[1 * GROUP_W:2 * GROUP_W, :])
    acc += _dot(oc_ref[...], w_ref[2 * GROUP_W:3 * GROUP_W, :])
    acc += _dot(od_ref[...], w_ref[3 * GROUP_W:4 * GROUP_W, :])
    o_ref[...] = x_ref[...] + mod_ref[5:6, :] * acc


def _wout_call(xs, outs, mods, w_out, l, n_rows, rows_per_seq, n_batch, tm):
    D = xs.shape[1]
    tps = rows_per_seq // tm
    ospec = pl.BlockSpec((tm, GROUP_W), lambda i: (i, 0))
    return pl.pallas_call(
        _wout_kernel,
        out_shape=jax.ShapeDtypeStruct((n_rows, D), F32),
        grid=(n_rows // tm,),
        in_specs=[
            pl.BlockSpec((tm, D), lambda i: (i, 0)),
            ospec, ospec, ospec, ospec,
            pl.BlockSpec((None, N_MOD, D), lambda i: (jnp.minimum(i // tps, n_batch), 0, 0)),
            pl.BlockSpec((None, D, D), lambda i: (l, 0, 0)),
        ],
        out_specs=pl.BlockSpec((tm, D), lambda i: (i, 0)),
        compiler_params=_cparams(("arbitrary",)),
        name="out_proj",
    )(xs, *outs, mods, w_out)


def _na_case_reps(n_qb):
    return (0, 1, 2, n_qb - 2, n_qb - 1)


def _na_start(p, n_qb):
    return jnp.clip(p - 2, 0, n_qb - NA_KBLKS)


def _na_bias_table(rpb, n_qb):
    H = rpb.shape[0]
    rows = 2 * n_qb
    qc = np.arange(GRID_W)[:, None]
    kc = np.arange(GRID_W)[None, :]
    cs = np.clip(qc - NA_WIN_C // 2, 0, GRID_W - NA_WIN_C)
    okc = (kc >= cs) & (kc < cs + NA_WIN_C)
    dc = np.where(okc, kc - qc + NA_WIN_C - 1, 0)
    rp = rpb.astype(F32).reshape(H, 2 * NA_WIN_R - 1, 2 * NA_WIN_C - 1)
    blocks = jnp.where(jnp.asarray(okc)[None, None], jnp.take(rp, jnp.asarray(dc), axis=2), NEG_INF)
    blocks = blocks.astype(BF16)
    neg = jnp.full((H, GRID_W, GRID_W), NEG_INF, BF16)
    cases = []
    for p in _na_case_reps(n_qb):
        start = min(max(p - 2, 0), n_qb - NA_KBLKS)
        qrows = []
        for qr in range(2):
            qa = 2 * p + qr
            rs = min(max(qa - NA_WIN_R // 2, 0), rows - NA_WIN_R)
            krow = []
            for kr in range(2 * NA_KBLKS):
                ka = 2 * start + kr
                krow.append(blocks[:, ka - qa + NA_WIN_R - 1] if rs <= ka < rs + NA_WIN_R else neg)
            qrows.append(jnp.concatenate(krow, axis=-1))
        cases.append(jnp.concatenate(qrows, axis=-2))
    return jnp.stack(cases, axis=0)


def _na_kernel(q_ref, k_ref, v_ref, kc_ref, vc_ref, bias_ref, qg_ref, kg_ref, ones_ref, o_ref,
               kn_sc, vn_sc, kcn_sc, vcn_sc, *, n_qb, seq, ctx):
    p = pl.program_id(2)
    ones_bd = ones_ref[...]
    scale = HEAD_DIM ** -0.5

    @pl.when(p == 0)
    def _():
        kg = kg_ref[...]
        rows = 512

        def body(i, carry):
            sl = pl.ds(pl.multiple_of(i * rows, rows), rows)
            kn_sc[sl, :] = (_head_rmsnorm(k_ref[sl, :], ones_bd) * kg).astype(BF16)
            vn_sc[sl, :] = _with_ones(v_ref[sl, :])
            return carry

        lax.fori_loop(0, seq // rows, body, 0)
        kcn_sc[...] = (_head_rmsnorm(kc_ref[...], ones_bd) * kg).astype(BF16)
        vcn_sc[...] = _with_ones(vc_ref[...])

    nq = q_ref.shape[0] // QBLK
    n_lat = n_qb // nq
    q = (_head_rmsnorm(q_ref[...], ones_bd) * (qg_ref[...] * scale)).astype(BF16)
    kcn = kcn_sc[...]
    vcn = vcn_sc[...]
    chains = [(qb, h) for qb in range(nq) for h in range(2)]
    hsl = lambda h: slice(h * HEAD_DIM, (h + 1) * HEAD_DIM)
    qh = [q[qb * QBLK:(qb + 1) * QBLK, hsl(h)] for qb, h in chains]
    s_c = [_dot_nt(qh[c], kcn[:, hsl(h)]) for c, (qb, h) in enumerate(chains)]
    m_c = [jnp.max(t, axis=-1, keepdims=True) for t in s_c]

    vsl = lambda h: slice(h * LANE, (h + 1) * LANE)
    vca = [vcn[:, vsl(h)] for h in range(2)]

    def finish(oa):
        rows = [jnp.concatenate([oa[qb * 2 + h][:, :HEAD_DIM] / oa[qb * 2 + h][:, HEAD_DIM:] for h in range(2)],
                                axis=-1) for qb in range(nq)]
        o_ref[...] = jnp.concatenate(rows, axis=0).astype(o_ref.dtype)

    @pl.when(p < n_lat)
    def _():
        starts = [_na_start(p * nq + qb, n_qb) for qb in range(nq)]
        cases = [p * nq + qb - starts[qb] for qb in range(nq)]
        sls = [pl.ds(pl.multiple_of(st * QBLK, QBLK), NA_KBLKS * QBLK) for st in starts]
        kw = [kn_sc[sl, :] for sl in sls]
        vw = [vn_sc[sl, :] for sl in sls]
        s_n = [_dot_nt(qh[c], kw[qb][:, hsl(h)]) + bias_ref[cases[qb], h].astype(F32)
               for c, (qb, h) in enumerate(chains)]
        m = [jnp.maximum(jnp.max(s_n[c], axis=-1, keepdims=True), m_c[c]) for c in range(len(chains))]
        p_n = [_prob(s_n[c], m[c]) for c in range(len(chains))]
        p_c = [_prob(s_c[c], m[c]) for c in range(len(chains))]
        finish([_dot(p_n[c], vw[qb][:, vsl(h)]) + _dot(p_c[c], vca[h])
                for c, (qb, h) in enumerate(chains)])

    @pl.when(p >= n_lat)
    def _():
        finish([_dot(_prob(s_c[c], m_c[c]), vca[h]) for c, (qb, h) in enumerate(chains)])


def _na_call(proj, bias, qg, kg, ones_bd, n_batch, seq, ctx, want_ctx):
    R = proj.shape[0]
    nq = 2
    n_qb = seq // QBLK
    n_lat, n_ctx = n_qb // nq, ctx // (nq * QBLK)
    steps = n_lat + (n_ctx if want_ctx else 0)
    ctx_kb = n_batch * seq // ctx

    def qrow(b, p):
        return jnp.where(p < n_lat, b * n_lat + p, n_batch * n_lat + b * n_ctx + (p - n_lat))

    cq, ck, cv = COL_AQ // LANE, COL_AK // LANE, COL_AV // LANE
    return pl.pallas_call(
        functools.partial(_na_kernel, n_qb=n_qb, seq=seq, ctx=ctx),
        out_shape=jax.ShapeDtypeStruct((R, GROUP_W), BF16),
        grid=(n_batch, NA_HEADS // 2, steps),
        in_specs=[
            pl.BlockSpec((nq * QBLK, LANE), lambda b, hp, p: (qrow(b, p), cq + hp)),
            pl.BlockSpec((seq, LANE), lambda b, hp, p: (b, ck + hp)),
            pl.BlockSpec((seq, LANE), lambda b, hp, p: (b, cv + hp)),
            pl.BlockSpec((ctx, LANE), lambda b, hp, p: (ctx_kb + b, ck + hp)),
            pl.BlockSpec((ctx, LANE), lambda b, hp, p: (ctx_kb + b, cv + hp)),
            pl.BlockSpec((5, 2, QBLK, NA_KBLKS * QBLK), lambda b, hp, p: (0, hp, 0, 0)),
            pl.BlockSpec((1, LANE), lambda b, hp, p: (0, 0)),
            pl.BlockSpec((1, LANE), lambda b, hp, p: (0, 0)),
            pl.BlockSpec((LANE, LANE), lambda b, hp, p: (0, 0)),
        ],
        out_specs=pl.BlockSpec((nq * QBLK, LANE), lambda b, hp, p: (qrow(b, p), hp)),
        scratch_shapes=[pltpu.VMEM((seq, LANE), BF16), pltpu.VMEM((seq, 2 * LANE), BF16),
                        pltpu.VMEM((ctx, LANE), BF16), pltpu.VMEM((ctx, 2 * LANE), BF16)],
        compiler_params=_cparams(("arbitrary", "arbitrary", "arbitrary")),
        name="mixer_a",
    )(proj, proj, proj, proj, proj, bias, qg, kg, ones_bd)


def _rope(x, cos, sin_signed, first_half):
    rot = jnp.where(first_half, pltpu.roll(x, LANE - 16, 1), pltpu.roll(x, 16, 1))
    return x * cos + rot * sin_signed


def _wa_kernel(sink_ref, q_ref, k_ref, v_ref, kc_ref, vc_ref, cosk_ref, sin_k_ref, cosq_ref, sinq_ref,
               qg_ref, kg_ref, ones_ref, wmask_ref, o_ref, kn_sc, vn_sc, kcn_sc, vcn_sc, *, n_qb, seq, ctx):
    p = pl.program_id(1)
    ones_bd = ones_ref[...]
    scale = HEAD_DIM ** -0.5
    lane = lax.broadcasted_iota(jnp.int32, (1, LANE), 1)
    first_half = (lane % 32) < 16

    @pl.when(p == 0)
    def _():
        kg = kg_ref[...]
        rows = 512

        def body(i, carry):
            sl = pl.ds(pl.multiple_of(i * rows, rows), rows)
            kn = _head_rmsnorm(k_ref[sl, :], ones_bd) * kg
            kn_sc[sl, :] = _rope(kn, cosk_ref[sl, :], sin_k_ref[sl, :], first_half).astype(BF16)
            vn_sc[sl, :] = _with_ones(v_ref[sl, :])
            return carry

        lax.fori_loop(0, seq // rows, body, 0)
        kcn_sc[...] = (_head_rmsnorm(kc_ref[...], ones_bd) * kg).astype(BF16)
        vcn_sc[...] = _with_ones(vc_ref[...])

    nq = q_ref.shape[0] // QBLK
    n_lat = n_qb // nq
    qg = qg_ref[...] * scale
    is_lat = p < n_lat
    cosq = jnp.where(is_lat, cosq_ref[...], 1.0)
    sinq = jnp.where(is_lat, sinq_ref[...], 0.0)
    qh = []
    for c in range(4):
        qn = _head_rmsnorm(q_ref[:, c * LANE:(c + 1) * LANE], ones_bd) * qg
        qn = _rope(qn, cosq, sinq, first_half).astype(BF16)
        qh += [qn[:, :HEAD_DIM], qn[:, HEAD_DIM:]]
    kcn = kcn_sc[...]
    vcn = vcn_sc[...]
    chains = [(qb, kh) for qb in range(nq) for kh in range(WA_KV_HEADS)]
    nch = range(len(chains))
    hsl = lambda kh: slice(kh * HEAD_DIM, (kh + 1) * HEAD_DIM)
    qs = [jnp.concatenate([qh[kh * WA_GROUP + g][qb * QBLK:(qb + 1) * QBLK, :] for g in range(WA_GROUP)], axis=0)
          for qb, kh in chains]
    sk = [jnp.concatenate([jnp.full((QBLK, 1), sink_ref[kh * WA_GROUP + g], F32) for g in range(WA_GROUP)],
                          axis=0) for qb, kh in chains]
    s_c = [_dot_nt(qs[c], kcn[:, hsl(chains[c][1])]) for c in nch]
    m_c = [jnp.maximum(jnp.max(s_c[c], axis=-1, keepdims=True), sk[c]) for c in nch]

    vsl = lambda kh: slice(kh * LANE, (kh + 1) * LANE)
    vca = [vcn[:, vsl(kh)] for kh in range(WA_KV_HEADS)]

    def finish(oa, sink_w):
        rows = []
        for qb in range(nq):
            heads = []
            for kh in range(WA_KV_HEADS):
                c = qb * WA_KV_HEADS + kh
                oc = oa[c][:, :HEAD_DIM] / (oa[c][:, HEAD_DIM:] + sink_w[c])
                heads += [oc[g * QBLK:(g + 1) * QBLK, :] for g in range(WA_GROUP)]
            rows.append(jnp.concatenate(heads, axis=-1))
        o_ref[...] = jnp.concatenate(rows, axis=0).astype(o_ref.dtype)

    @pl.when(is_lat)
    def _():
        blk = [p * nq + qb for qb in range(nq)]
        ws = [jnp.clip((bq - 1) * QBLK, 0, seq - 3 * QBLK) for bq in blk]
        sls = [pl.ds(pl.multiple_of(w, QBLK), 3 * QBLK) for w in ws]
        kw = [kn_sc[sl, :] for sl in sls]
        vw = [vn_sc[sl, :] for sl in sls]
        case = [jnp.where(bq == 0, 0, jnp.where(bq == n_qb - 1, 2, 1)) for bq in blk]
        msk = [jnp.concatenate([wmask_ref[cs]] * WA_GROUP, axis=0) for cs in case]
        s_w = [_dot_nt(qs[c], kw[chains[c][0]][:, hsl(chains[c][1])]) + msk[chains[c][0]] for c in nch]
        m = [jnp.maximum(jnp.max(s_w[c], axis=-1, keepdims=True), m_c[c]) for c in nch]
        p_w = [_prob(s_w[c], m[c]) for c in nch]
        p_c = [_prob(s_c[c], m[c]) for c in nch]
        finish([_dot(p_w[c], vw[chains[c][0]][:, vsl(chains[c][1])]) + _dot(p_c[c], vca[chains[c][1]])
                for c in nch], [jnp.exp(sk[c] - m[c]) for c in nch])

    @pl.when(jnp.logical_not(is_lat))
    def _():
        finish([_dot(_prob(s_c[c], m_c[c]), vca[chains[c][1]]) for c in nch],
               [jnp.exp(sk[c] - m_c[c]) for c in nch])


def _wa_mask_table():
    i = np.arange(QBLK)[:, None]
    j = np.arange(3 * QBLK)[None, :]
    shifts = (0, QBLK, 2 * QBLK)
    return jnp.asarray(np.stack([np.where(np.abs(j - sh - i) <= WA_WINDOW, 0.0, NEG_INF) for sh in shifts]), F32)


def _wa_call(proj, sink, cos_t, sin_t, qg, kg, ones_bd, n_batch, seq, ctx, want_ctx):
    R = proj.shape[0]
    nq = 2
    n_qb = seq // QBLK
    n_lat, n_ctx = n_qb // nq, ctx // (nq * QBLK)
    steps = n_lat + (n_ctx if want_ctx else 0)
    ctx_kb = n_batch * seq // ctx

    def qrow(b, p):
        return jnp.where(p < n_lat, b * n_lat + p, n_batch * n_lat + b * n_ctx + (p - n_lat))

    ck, cv = COL_CK // LANE, COL_CV // LANE
    return pl.pallas_call(
        functools.partial(_wa_kernel, n_qb=n_qb, seq=seq, ctx=ctx),
        out_shape=jax.ShapeDtypeStruct((R, GROUP_W), BF16),
        grid=(n_batch, steps),
        in_specs=[
            pl.BlockSpec(memory_space=pltpu.SMEM),
            pl.BlockSpec((nq * QBLK, GROUP_W), lambda b, p: (qrow(b, p), COL_CQ // GROUP_W)),
            pl.BlockSpec((seq, LANE), lambda b, p: (b, ck)),
            pl.BlockSpec((seq, LANE), lambda b, p: (b, cv)),
            pl.BlockSpec((ctx, LANE), lambda b, p: (ctx_kb + b, ck)),
            pl.BlockSpec((ctx, LANE), lambda b, p: (ctx_kb + b, cv)),
            pl.BlockSpec((seq, LANE), lambda b, p: (0, 0)),
            pl.BlockSpec((seq, LANE), lambda b, p: (0, 0)),
            pl.BlockSpec((nq * QBLK, LANE), lambda b, p: (jnp.minimum(p, n_lat - 1), 0)),
            pl.BlockSpec((nq * QBLK, LANE), lambda b, p: (jnp.minimum(p, n_lat - 1), 0)),
            pl.BlockSpec((1, LANE), lambda b, p: (0, 0)),
            pl.BlockSpec((1, LANE), lambda b, p: (0, 0)),
            pl.BlockSpec((LANE, LANE), lambda b, p: (0, 0)),
            pl.BlockSpec((3, QBLK, 3 * QBLK), lambda b, p: (0, 0, 0)),
        ],
        out_specs=pl.BlockSpec((nq * QBLK, GROUP_W), lambda b, p: (qrow(b, p), 0)),
        scratch_shapes=[pltpu.VMEM((seq, LANE), BF16), pltpu.VMEM((seq, 2 * LANE), BF16),
                        pltpu.VMEM((ctx, LANE), BF16), pltpu.VMEM((ctx, 2 * LANE), BF16)],
        compiler_params=_cparams(("arbitrary", "arbitrary")),
        name="mixer_c",
    )(sink, proj, proj, proj, proj, proj, cos_t, sin_t, cos_t, sin_t, qg, kg, ones_bd, _wa_mask_table())


def _rw_prep_kernel(x_ref, prev_ref, next_ref, lr_ref, conv_ref, wlr_ref, w0_ref, a0_ref, kk_ref, ka_ref,
                    ones_ref, o_ref, *, tiles_lat, tiles_ctx, n_lat_tiles):
    i = pl.program_id(0)
    tm = x_ref.shape[0]
    in_lat = i < n_lat_tiles
    first = jnp.where(in_lat, i % tiles_lat == 0, (i - n_lat_tiles) % tiles_ctx == 0)
    last = jnp.where(in_lat, i % tiles_lat == tiles_lat - 1, (i - n_lat_tiles) % tiles_ctx == tiles_ctx - 1)
    x = x_ref[...]
    prev_row = jnp.where(first, 0.0, prev_ref[7:8, :])
    next_row = jnp.where(last, 0.0, next_ref[0:1, :])
    row = lax.broadcasted_iota(jnp.int32, (tm, 1), 0)
    x_prev = jnp.where(row == 0, prev_row, pltpu.roll(x, 1, 0))
    x_next = jnp.where(row == tm - 1, next_row, pltpu.roll(x, tm - 1, 0))
    cw = conv_ref[...]
    y = x_prev * cw[0:1, :] + x * cw[1:2, :] + x_next * cw[2:3, :]
    r = y[:, 0:GROUP_W]
    k = y[:, GROUP_W:2 * GROUP_W]
    v = y[:, 2 * GROUP_W:3 * GROUP_W]

    lr = lr_ref[...]
    lane = lax.broadcasted_iota(jnp.int32, (1, lr.shape[1]), 1)
    c1 = RW_DECAY_RANK
    c2 = c1 + RW_ICLR_RANK
    c3 = c2 + RW_GATE_RANK
    act = jnp.where(lane < c1, jnp.tanh(lr),
                    jnp.where(lane < c2, lr, jnp.where(lane < c3, _sigmoid(lr), 0.0)))
    up = _dot(act.astype(BF16), wlr_ref[...])

    ones_bd = ones_ref[...]
    kk = k * kk_ref[...]
    nrm = jnp.sqrt(_seg_sum(kk * kk, ones_bd))
    kk = kk / jnp.maximum(nrm, 1e-12)

    o_ref[:, PB_R * GROUP_W:(PB_R + 1) * GROUP_W] = r
    o_ref[:, PB_V * GROUP_W:(PB_V + 1) * GROUP_W] = v
    o_ref[:, PB_KK * GROUP_W:(PB_KK + 1) * GROUP_W] = kk
    o_ref[:, PB_G * GROUP_W:(PB_G + 1) * GROUP_W] = up[:, 4 * GROUP_W:5 * GROUP_W]
    ka = ka_ref[...]
    for d in range(2):
        z = w0_ref[d:d + 1, :] + up[:, d * GROUP_W:(d + 1) * GROUP_W]
        sp = jnp.maximum(-z, 0.0) + jnp.log(1.0 + jnp.exp(-jnp.abs(z)))
        lw = -jnp.exp(-sp - 0.5)
        a = _sigmoid(a0_ref[d:d + 1, :] + up[:, (2 + d) * GROUP_W:(3 + d) * GROUP_W])
        kd = k * (1.0 + (a - 1.0) * ka)
        base = 3 + 3 * d
        o_ref[:, base * GROUP_W:(base + 1) * GROUP_W] = lw
        o_ref[:, (base + 1) * GROUP_W:(base + 2) * GROUP_W] = a
        o_ref[:, (base + 2) * GROUP_W:(base + 3) * GROUP_W] = kd


def _rw_prep_call(proj, conv_w, wlr, w0, a0, k_k, k_a, ones512, n_batch, seq, ctx, tm):
    R = proj.shape[0]
    n_lat_tiles = n_batch * seq // tm
    nb8 = R // 8
    t8 = tm // 8
    wide = 3 * GROUP_W
    cb = COL_BR // wide
    return pl.pallas_call(
        functools.partial(_rw_prep_kernel, tiles_lat=seq // tm, tiles_ctx=ctx // tm, n_lat_tiles=n_lat_tiles),
        out_shape=jax.ShapeDtypeStruct((R, PB_N * GROUP_W), F32),
        grid=(R // tm,),
        in_specs=[
            pl.BlockSpec((tm, wide), lambda i: (i, cb)),
            pl.BlockSpec((8, wide), lambda i: (jnp.maximum(i * t8 - 1, 0), cb)),
            pl.BlockSpec((8, wide), lambda i: (jnp.minimum((i + 1) * t8, nb8 - 1), cb)),
            pl.BlockSpec((tm, 256), lambda i: (i, COL_BLR // 256)),
            pl.BlockSpec((3, wide), lambda i: (0, 0)),
            pl.BlockSpec((256, 5 * GROUP_W), lambda i: (0, 0)),
            pl.BlockSpec((2, GROUP_W), lambda i: (0, 0)),
            pl.BlockSpec((2, GROUP_W), lambda i: (0, 0)),
            pl.BlockSpec((1, GROUP_W), lambda i: (0, 0)),
            pl.BlockSpec((1, GROUP_W), lambda i: (0, 0)),
            pl.BlockSpec((GROUP_W, GROUP_W), lambda i: (0, 0)),
        ],
        out_specs=pl.BlockSpec((tm, PB_N * GROUP_W), lambda i: (i, 0)),
        compiler_params=_cparams(("arbitrary",)),
        name="mixer_b_prep",
    )(proj, proj, proj, proj, conv_w, wlr, w0, a0, k_k, k_a, ones512)


RW_SPLIT = 1
_NN = ((1,), (0,))
_NT = ((1,), (1,))
_TN = ((0,), (0,))


def _split(x, n):
    parts = []
    for _ in range(n):
        p = x.astype(BF16)
        parts.append(p)
        x = x - p.astype(F32)
    return parts


def _sdot(a, b, dims):
    n = max(len(a), len(b))
    acc = None
    for i, ai in enumerate(a):
        for j, bj in enumerate(b):
            if i + j < n:
                t = lax.dot_general(ai, bj, (dims, ((), ())), preferred_element_type=F32)
                acc = t if acc is None else acc + t
    return acc


def _rw_chunks(at, bt, kt, rt, v, gam, s0, strict, incl):
    n = len(at)
    ids = range(n)
    C = at[0].shape[0]
    sp = lambda t: _split(t, RW_SPLIT)
    b1 = lambda t: [t.astype(BF16)]
    ar = [b1(jnp.concatenate([at[i], rt[i]], axis=0)) for i in ids]
    bk = [sp(jnp.concatenate([bt[i], kt[i]], axis=0)) for i in ids]
    g4 = [_sdot(ar[i], bk[i][:1], _NT) for i in ids]
    a_ab = [jnp.where(strict[i], g4[i][:C, :C], 0.0) for i in ids]
    a_ak = [jnp.where(strict[i], g4[i][:C, C:], 0.0) for i in ids]
    a_rb = [jnp.where(incl[i], g4[i][C:, :C], 0.0) for i in ids]
    a_rk = [jnp.where(incl[i], g4[i][C:, C:], 0.0) for i in ids]
    wv = [_sdot(b1(a_ak[i]), b1(v[i]), _NN) for i in ids]
    x = [jnp.concatenate([at[i], wv[i]], axis=1) for i in ids]

    row = lax.broadcasted_iota(jnp.int32, (C, C), 0)
    col = lax.broadcasted_iota(jnp.int32, (C, C), 1)
    eye = (row == col).astype(F32)
    blk = 16
    same = (row // blk) == (col // blk)
    p = [jnp.where(same, a_ab[i], 0.0) for i in ids]
    m = [eye + p[i] for i in ids]
    for _ in range(int(math.log2(blk)) - 1):
        pb_ = [b1(p[i]) for i in ids]
        p = [_sdot(pb_[i], pb_[i], _NN) for i in ids]
        m = [m[i] + _sdot(b1(m[i]), b1(p[i]), _NN) for i in ids]
    while blk < C:
        wider = (row // (2 * blk)) == (col // (2 * blk))
        join = wider & jnp.logical_not(same)
        mb = [b1(m[i]) for i in ids]
        t = [_sdot(b1(jnp.where(join, a_ab[i], 0.0)), mb[i], _NN) for i in ids]
        m = [m[i] + _sdot(mb[i], b1(t[i]), _NN) for i in ids]
        same, blk = wider, 2 * blk
    x = [_sdot(b1(m[i]), b1(x[i]), _NN) for i in ids]

    s0b = [sp(s0[i]) for i in ids]
    u = [x[i][:, HEAD_DIM:] + _sdot(sp(x[i][:, :HEAD_DIM]), s0b[i], _NT) for i in ids]
    uv = [sp(jnp.concatenate([u[i], v[i]], axis=0)) for i in ids]
    y = [_sdot(b1(jnp.concatenate([a_rb[i], a_rk[i]], axis=1)), uv[i][:1], _NN)
         + _sdot(sp(rt[i]), s0b[i], _NT) for i in ids]
    s1 = [(s0[i] + _sdot(uv[i], bk[i], _TN)) * gam[i] for i in ids]
    return y, s1


def _rw_scan_kernel(*refs, n_batch, nc):
    n_in = 12 * n_batch
    yfl_ref, ybl_ref, yfc_ref, ybc_ref, s_sc = refs[n_in:]
    s = pl.program_id(0)

    @pl.when(s == 0)
    def _():
        s_sc[...] = jnp.zeros_like(s_sc)

    C = refs[0].shape[0]
    row = lax.broadcasted_iota(jnp.int32, (C, C), 0)
    col = lax.broadcasted_iota(jnp.int32, (C, C), 1)
    chains = dict(at=[], bt=[], kt=[], rt=[], v=[], gam=[], s0=[], strict=[], incl=[])
    for b in range(n_batch):
        for d, reverse in enumerate((False, True)):
            r_ref, v_ref, kk_ref, lw_ref, a_ref, k_ref = refs[12 * b + 6 * d:12 * b + 6 * d + 6]
            incl, strict = (row <= col, row < col) if reverse else (row >= col, row > col)
            lw = lw_ref[...]
            kk = kk_ref[...]
            v = v_ref[...]
            cum = _sdot([incl.astype(BF16)], _split(lw, 3), _NN)
            e_l = jnp.exp(cum)
            e_n = jnp.exp(-cum)
            at = -(kk * jnp.exp(cum - lw))
            bt = kk * a_ref[...] * e_n
            kt = k_ref[...] * e_n
            rt = r_ref[...] * e_l
            gam = e_l[0:1, :] if reverse else e_l[C - 1:C, :]
            for h in range(RW_HEADS):
                hs = slice(h * HEAD_DIM, (h + 1) * HEAD_DIM)
                for name, val in (("at", at), ("bt", bt), ("kt", kt), ("rt", rt), ("v", v), ("gam", gam)):
                    chains[name].append(val[:, hs])
                chains["s0"].append(s_sc[b, d, h])
                chains["strict"].append(strict)
                chains["incl"].append(incl)
    y, s1 = _rw_chunks(**chains)
    ys = {}
    for b in range(n_batch):
        for d in range(2):
            base = (b * 2 + d) * RW_HEADS
            for h in range(RW_HEADS):
                s_sc[b, d, h] = s1[base + h]
            ys[b, d] = jnp.concatenate(y[base:base + RW_HEADS], axis=-1)

    @pl.when(s < nc)
    def _():
        for b in range(n_batch):
            yfc_ref[b] = ys[b, 0]
            ybc_ref[b] = ys[b, 1]

    @pl.when(s >= nc)
    def _():
        for b in range(n_batch):
            yfl_ref[b] = ys[b, 0]
            ybl_ref[b] = ys[b, 1]


def _rw_scan_call(prep, n_batch, seq, ctx):
    C = RW_CHUNK
    nc, nl = ctx // C, seq // C
    ctx_base = n_batch * nl

    def blk_f(b, s):
        return jnp.where(s < nc, ctx_base + b * nc + s, b * nl + (s - nc))

    def blk_b(b, s):
        return jnp.where(s < nc, ctx_base + b * nc + (nc - 1 - s), b * nl + (nl - 1 - (s - nc)))

    def col(blk, b, cb):
        return pl.BlockSpec((C, GROUP_W), lambda s: (blk(b, s), cb))

    in_specs = []
    for b in range(n_batch):
        in_specs += [col(blk_f, b, c) for c in (PB_R, PB_V, PB_KK, 3, 4, 5)]
        in_specs += [col(blk_b, b, c) for c in (PB_R, PB_V, PB_KK, 6, 7, 8)]
    lat = jax.ShapeDtypeStruct((n_batch, seq, GROUP_W), F32)
    cx = jax.ShapeDtypeStruct((n_batch, ctx, GROUP_W), F32)
    blk3 = (n_batch, C, GROUP_W)
    out_specs = (
        pl.BlockSpec(blk3, lambda s: (0, jnp.maximum(s - nc, 0), 0)),
        pl.BlockSpec(blk3, lambda s: (0, nl - 1 - jnp.maximum(s - nc, 0), 0)),
        pl.BlockSpec(blk3, lambda s: (0, jnp.minimum(s, nc - 1), 0)),
        pl.BlockSpec(blk3, lambda s: (0, nc - 1 - jnp.minimum(s, nc - 1), 0)),
    )
    yfl, ybl, yfc, ybc = pl.pallas_call(
        functools.partial(_rw_scan_kernel, n_batch=n_batch, nc=nc),
        out_shape=(lat, lat, cx, cx),
        grid=(nc + nl,),
        in_specs=in_specs,
        out_specs=out_specs,
        scratch_shapes=[pltpu.VMEM((n_batch, 2, RW_HEADS, HEAD_DIM, HEAD_DIM), F32)],
        compiler_params=_cparams(("arbitrary",)),
        name="mixer_b_scan",
    )(*([prep] * (12 * n_batch)))
    flat = lambda a, c: jnp.concatenate([a.reshape(n_batch * seq, GROUP_W), c.reshape(n_batch * ctx, GROUP_W)], 0)
    return flat(yfl, yfc), flat(ybl, ybc)


def _rw_readout_kernel(r_ref, v_ref, k0_ref, k1_ref, g_ref, yf_ref, yb_ref, rk_ref, gw_ref, gb_ref, ones_ref,
                       o_ref):
    ones_bd = ones_ref[...]
    v = v_ref[...]
    bonus = _seg_sum(r_ref[...] * (k0_ref[...] + k1_ref[...]) * rk_ref[...], ones_bd)
    y = yf_ref[...] + yb_ref[...] + bonus * v
    mu = _seg_sum(y, ones_bd) * (1.0 / HEAD_DIM)
    yc = y - mu
    var = _seg_sum(yc * yc, ones_bd) * (1.0 / HEAD_DIM)
    yn = yc * lax.rsqrt(var + RW_GN_EPS)
    o_ref[...] = ((yn * gw_ref[...] + gb_ref[...]) * g_ref[...]).astype(o_ref.dtype)


def _rw_readout_call(prep, yf, yb, r_k, gn_w, gn_b, ones512, tm):
    R = prep.shape[0]
    col = lambda cb: pl.BlockSpec((tm, GROUP_W), lambda i: (i, cb))
    vec = pl.BlockSpec((1, GROUP_W), lambda i: (0, 0))
    return pl.pallas_call(
        _rw_readout_kernel,
        out_shape=jax.ShapeDtypeStruct((R, GROUP_W), BF16),
        grid=(R // tm,),
        in_specs=[col(PB_R), col(PB_V), col(5), col(8), col(PB_G), col(0), col(0), vec, vec, vec,
                  pl.BlockSpec((GROUP_W, GROUP_W), lambda i: (0, 0))],
        out_specs=col(0),
        compiler_params=_cparams(("arbitrary",)),
        name="mixer_b_readout",
    )(prep, prep, prep, prep, prep, yf, yb, r_k, gn_w, gn_b, ones512)


def _s5_tables(a_re, a_im, log_dt, b_re, b_im, c_re, c_im):
    Lc, G, N, P = S5_CHUNK, S5_GROUPS, S5_N, S5_P
    dt = jnp.exp(log_dt)[..., None]
    lam_re, lam_im = dt * a_re, dt * a_im
    tau = jnp.arange(Lc + 1, dtype=F32)[:, None, None, None]
    mag = jnp.exp(tau * lam_re)
    pw_re, pw_im = mag * jnp.cos(tau * lam_im), mag * jnp.sin(tau * lam_im)
    ab_re, ab_im = pw_re[1], pw_im[1]
    den = a_re * a_re + a_im * a_im
    nr = ab_re - 1.0
    cf_re, cf_im = (nr * a_re + ab_im * a_im) / den, (ab_im * a_re - nr * a_im) / den
    bp_re = cf_re[..., None] * b_re[None] - cf_im[..., None] * b_im[None]
    bp_im = cf_re[..., None] * b_im[None] + cf_im[..., None] * b_re[None]
    pb_re = pw_re[..., None] * bp_re[None] - pw_im[..., None] * bp_im[None]
    pb_im = pw_re[..., None] * bp_im[None] + pw_im[..., None] * bp_re[None]
    kk = (jnp.einsum('gqn,tdgnp->tdgqp', c_re, pb_re[:Lc]) - jnp.einsum('gqn,tdgnp->tdgqp', c_im, pb_im[:Lc]))
    jj = np.arange(Lc)[:, None]
    ii = np.arange(Lc)[None, :]
    dist = np.abs(ii - jj)
    kf = kk[dist, 0] * jnp.asarray(ii >= jj, F32)[..., None, None, None]
    kb = kk[dist, 1] * jnp.asarray(ii <= jj, F32)[..., None, None, None]
    tz = jnp.transpose(kf + kb, (2, 0, 4, 1, 3))
    jr = np.arange(Lc)
    emap = lambda pbx, order, d: jnp.transpose(pbx[order, d], (1, 0, 3, 2))
    em = jnp.stack([emap(pb_re, Lc - 1 - jr, 0), emap(pb_im, Lc - 1 - jr, 0),
                    emap(pb_re, jr, 1), emap(pb_im, jr, 1)], axis=3)

    def gmap(order, d):
        pr, pi = pw_re[order, d], pw_im[order, d]
        cp_re = c_re[None] * pr[:, :, None, :] - c_im[None] * pi[:, :, None, :]
        cp_im = c_re[None] * pi[:, :, None, :] + c_im[None] * pr[:, :, None, :]
        to = lambda t: jnp.transpose(t, (1, 3, 0, 2))
        return to(cp_re), to(-cp_im)

    gk = jnp.stack(gmap(jr + 1, 0) + gmap(Lc - jr, 1), axis=1)
    NO, NQ, NG = S5_OCTETS, 2, 4
    tz, em, gk = tz.astype(BF16), em.astype(BF16), gk.astype(BF16)

    def bdiag(blocks):
        n, c = blocks.shape[1], blocks.shape[3]
        return jnp.concatenate(
            [jnp.pad(blocks[:, g], ((0, 0), (0, 0), (g * c, (n - 1 - g) * c))) for g in range(n)], axis=1)

    def perm_rows(m, dims, order):
        no, r, c = m.shape
        m = jnp.transpose(m.reshape((no,) + dims + (c,)), (0,) + tuple(1 + o for o in order) + (len(dims) + 1,))
        return m.reshape(no, r, c)

    tr = lambda m: jnp.swapaxes(m, 1, 2)
    gjp, rgkn = (8, Lc, P), (NQ, NG, 4, N)
    wy = perm_rows(bdiag(tz.reshape(NO, 8, Lc * P, Lc * P)), gjp, (1, 0, 2))
    wy = tr(perm_rows(tr(wy), gjp, (1, 0, 2)))
    we = perm_rows(bdiag(em.reshape(NO, 8, Lc * P, 4 * N)), gjp, (1, 0, 2))
    we = tr(perm_rows(tr(we), rgkn, (0, 2, 1, 3)))
    wz = jnp.concatenate([wy, we], axis=-1)
    gm = perm_rows(bdiag(gk.reshape(NO, 8, 4 * N, Lc * P)), rgkn, (0, 2, 1, 3))
    gm = tr(perm_rows(tr(gm), gjp, (1, 0, 2)))
    apow = jnp.stack([pw_re[Lc, 0], pw_im[Lc, 0], pw_re[Lc, 1], pw_im[Lc, 1]], axis=1)
    apow = jnp.transpose(apow.reshape(G // NG, NG, 4, N), (0, 2, 1, 3)).reshape(G // NG, 4, NG * N)
    return wz, gm, apow


def _s5_local_kernel(u_ref, wz_ref, y_ref, e_ref):
    Lc = u_ref.shape[1]
    lhs = jnp.concatenate([u_ref[:, j, :] for j in range(Lc)], axis=-1).astype(BF16)
    z = _dot(lhs, wz_ref[...])
    for i in range(Lc):
        y_ref[:, i, :] = z[:, i * LANE:(i + 1) * LANE]
    e_ref[...] = z[:, Lc * LANE:]


def _s5_local_call(proj3, wz):
    NR, Lc, _ = proj3.shape
    NO, K, N = wz.shape
    NE = N - Lc * LANE
    tr = NR // 8
    cu = COL_DU // LANE
    return pl.pallas_call(
        _s5_local_kernel,
        out_shape=(jax.ShapeDtypeStruct((NR, Lc, GROUP_W), F32), jax.ShapeDtypeStruct((NR, NO * NE), F32)),
        grid=(NO, NR // tr),
        in_specs=[
            pl.BlockSpec((tr, Lc, LANE), lambda o, i: (i, 0, cu + o)),
            pl.BlockSpec((None, K, N), lambda o, i: (o, 0, 0)),
        ],
        out_specs=(pl.BlockSpec((tr, Lc, LANE), lambda o, i: (i, 0, o)),
                   pl.BlockSpec((tr, NE), lambda o, i: (i, o))),
        compiler_params=_cparams(("arbitrary", "arbitrary")),
        name="mixer_d_local",
    )(proj3, wz)


def _s5_scan_kernel(e_ref, ap_ref, x_ref, *, n_batch, nlc, ncc):
    NS = e_ref.shape[1] // 4
    af_re, af_im = ap_ref[0:1, :], ap_ref[1:2, :]
    ab_re, ab_im = ap_ref[2:3, :], ap_ref[3:4, :]

    def step(a_re, a_im, x_re, x_im, e):
        return a_re * x_re - a_im * x_im + e[:, :NS], a_re * x_im + a_im * x_re + e[:, NS:]

    def phase(row0, n_chunks, stride, carry):
        n_tiles = n_chunks // 8

        def body(t, carry):
            out = []
            for b in range(n_batch):
                xf_re, xf_im, xb_re, xb_im = carry[4 * b:4 * b + 4]
                rf = pl.ds(pl.multiple_of(row0 + b * stride + t * 8, 8), 8)
                rb = pl.ds(pl.multiple_of(row0 + b * stride + (n_tiles - 1 - t) * 8, 8), 8)
                ef = e_ref[rf, 0:2 * NS]
                eb = e_ref[rb, 2 * NS:4 * NS]
                xf_in, xb_in = [], [None] * 8
                for j in range(8):
                    xf_in.append(jnp.concatenate([xf_re, xf_im], axis=-1))
                    xf_re, xf_im = step(af_re, af_im, xf_re, xf_im, ef[j:j + 1, :])
                for j in reversed(range(8)):
                    xb_in[j] = jnp.concatenate([xb_re, xb_im], axis=-1)
                    xb_re, xb_im = step(ab_re, ab_im, xb_re, xb_im, eb[j:j + 1, :])
                x_ref[rf, 0:2 * NS] = jnp.concatenate(xf_in, axis=0)
                x_ref[rb, 2 * NS:4 * NS] = jnp.concatenate(xb_in, axis=0)
                out += [xf_re, xf_im, xb_re, xb_im]
            return tuple(out)

        return lax.fori_loop(0, n_tiles, body, carry)

    zero = jnp.zeros((1, NS), F32)
    carry = phase(n_batch * nlc, ncc, ncc, (zero,) * (4 * n_batch))
    phase(0, nlc, nlc, carry)


def _s5_scan_call(e, apow, n_batch, seq, ctx):
    NR, NEall = e.shape
    NQ8, _, NS = apow.shape
    return pl.pallas_call(
        functools.partial(_s5_scan_kernel, n_batch=n_batch, nlc=seq // S5_CHUNK, ncc=ctx // S5_CHUNK),
        out_shape=jax.ShapeDtypeStruct((NR, NEall), F32),
        grid=(NQ8,),
        in_specs=[
            pl.BlockSpec((NR, 4 * NS), lambda q: (0, q)),
            pl.BlockSpec((None, 4, NS), lambda q: (q, 0, 0)),
        ],
        out_specs=pl.BlockSpec((NR, 4 * NS), lambda q: (0, q)),
        compiler_params=_cparams(("arbitrary",)),
        name="mixer_d_scan",
    )(e, apow)


def _s5_carry_kernel(x_ref, gm_ref, yl_ref, y_ref):
    Lc = yl_ref.shape[1]
    y = _dot(x_ref[...].astype(BF16), gm_ref[...])
    for i in range(Lc):
        y_ref[:, i, :] = yl_ref[:, i, :] + y[:, i * LANE:(i + 1) * LANE]


def _s5_carry_call(xin, gm, yloc):
    NR, Lc, _ = yloc.shape
    NO, NE, N = gm.shape
    tr = NR // 8
    return pl.pallas_call(
        _s5_carry_kernel,
        out_shape=jax.ShapeDtypeStruct((NR, Lc, GROUP_W), F32),
        grid=(NO, NR // tr),
        in_specs=[
            pl.BlockSpec((tr, NE), lambda o, i: (i, o)),
            pl.BlockSpec((None, NE, N), lambda o, i: (o, 0, 0)),
            pl.BlockSpec((tr, Lc, LANE), lambda o, i: (i, 0, o)),
        ],
        out_specs=pl.BlockSpec((tr, Lc, LANE), lambda o, i: (i, 0, o)),
        compiler_params=_cparams(("arbitrary", "arbitrary")),
        name="mixer_d_carry",
    )(xin, gm, yloc)


def _s5_out_kernel(y_ref, u_ref, d_ref, w_ref, b_ref, o_ref):
    y = y_ref[...] + d_ref[...] * u_ref[...]
    c = math.sqrt(2.0 / math.pi)
    y = 0.5 * y * (1.0 + jnp.tanh(c * (y + 0.044715 * (y * y * y))))
    z = _dot(y.astype(BF16), w_ref[...]) + b_ref[...]
    o_ref[...] = (y * _sigmoid(z)).astype(o_ref.dtype)


def _s5_out_call(y_tok, proj, d_skip, glu_w, glu_b, tm):
    R = proj.shape[0]
    vec = pl.BlockSpec((1, GROUP_W), lambda i: (0, 0))
    return pl.pallas_call(
        _s5_out_kernel,
        out_shape=jax.ShapeDtypeStruct((R, GROUP_W), BF16),
        grid=(R // tm,),
        in_specs=[
            pl.BlockSpec((tm, GROUP_W), lambda i: (i, 0)),
            pl.BlockSpec((tm, GROUP_W), lambda i: (i, COL_DU // GROUP_W)),
            vec,
            pl.BlockSpec((GROUP_W, GROUP_W), lambda i: (0, 0)),
            vec,
        ],
        out_specs=pl.BlockSpec((tm, GROUP_W), lambda i: (i, 0)),
        compiler_params=_cparams(("arbitrary",)),
        name="mixer_d_out",
    )(y_tok, proj, d_skip, glu_w, glu_b)


def _rope_tables(n_tokens):
    t = jnp.arange(n_tokens)
    nf = HEAD_DIM // 4
    inv = 1.0 / (ROPE_BASE ** (jnp.arange(nf, dtype=F32) / nf))

    def ang(pp):
        a = pp.astype(F32)[:, None] * inv[None, :]
        return jnp.concatenate([a, a], -1)

    a = jnp.concatenate([ang(t // GRID_W), ang(t % GRID_W)], -1)
    cos, sin = jnp.cos(a), jnp.sin(a)
    sign = np.where((np.arange(HEAD_DIM) % 32) < 16, -1.0, 1.0).astype(np.float32)
    cos2 = jnp.concatenate([cos, cos], -1)
    sin2 = jnp.concatenate([sin * sign, sin * sign], -1)
    return cos2, sin2


def _block_ones(n):
    return jnp.asarray(np.kron(np.eye(n // HEAD_DIM), np.ones((HEAD_DIM, HEAD_DIM))), BF16)


def _permute_w_in(w_in):
    cuts = np.cumsum([512, 512, 512, 512, 512, 512, 32, 32, 96, 512, 128, 128, 512])
    seg = lambda i: w_in[..., (0 if i == 0 else cuts[i - 1]):cuts[i]]
    L, D = w_in.shape[:2]
    pad = jnp.zeros((L, D, 96), w_in.dtype)
    parts = [seg(0), seg(1), seg(2), seg(3), seg(4), seg(5), seg(9), seg(12),
             seg(6), seg(7), seg(8), pad, seg(10), seg(11)]
    return jnp.concatenate(parts, axis=-1).astype(BF16)


def _forward(x, c, ctx, c_ctx, w_ada, b_ada, norm_g, ffn1_wg, ffn1_wu, ffn1_wd, ffn2_wg, ffn2_wu, ffn2_wd,
             w_in, w_out, na_q_g, na_k_g, na_rpb, rw_conv, rw_w0, rw_w_up, rw_a0, rw_a_up, rw_g_up,
             rw_k_k, rw_k_a, rw_r_k, rw_gn_w, rw_gn_b, wa_q_g, wa_k_g, wa_sink, s5_a_re, s5_a_im,
             s5_log_dt, s5_b_re, s5_b_im, s5_c_re, s5_c_im, s5_d, s5_glu_w, s5_glu_b, *, tm, tf):
    B, SEQ, D = x.shape
    CTX = ctx.shape[1]
    L = w_ada.shape[0]
    n_lat = B * SEQ
    n_qb = SEQ // QBLK

    bf = lambda t: t.astype(BF16)
    f1g, f1u, f1d, f2g, f2u, f2d = map(bf, (ffn1_wg, ffn1_wu, ffn1_wd, ffn2_wg, ffn2_wu, ffn2_wd))
    w_in_p = _permute_w_in(w_in)
    w_out_b = bf(w_out)
    glu_w_b = bf(s5_glu_w)
    ones128, ones512 = _block_ones(LANE), _block_ones(GROUP_W)
    cos_t, sin_t = _rope_tables(SEQ)
    na_bias = [_na_bias_table(na_rpb[l], n_qb) for l in range(L)]
    zr = lambda r, cdim: jnp.zeros((L, r, cdim), F32)
    wlr = jnp.concatenate([
        jnp.concatenate([rw_w_up[:, 0], rw_w_up[:, 1], zr(32, 3 * GROUP_W)], axis=-1),
        jnp.concatenate([zr(32, 2 * GROUP_W), rw_a_up[:, 0], rw_a_up[:, 1], zr(32, GROUP_W)], axis=-1),
        jnp.concatenate([zr(96, 4 * GROUP_W), rw_g_up], axis=-1),
        zr(96, 5 * GROUP_W)], axis=1).astype(BF16)
    s5_tabs = jax.vmap(_s5_tables)(s5_a_re, s5_a_im, s5_log_dt, s5_b_re, s5_b_im, s5_c_re, s5_c_im)
    tile2 = lambda t: jnp.concatenate([t, t], axis=-1)

    cc = jnp.concatenate([c, c_ctx[None], jnp.zeros((8 - B - 1, D), F32)], axis=0)
    mods_all = _ada_call(cc, w_ada, b_ada)[:, :B + 1].reshape(L, B + 1, N_MOD, D)

    xs = jnp.concatenate([x.reshape(n_lat, D), ctx.reshape(B * CTX, D)], axis=0)
    R = xs.shape[0]
    for l in range(L):
        want_ctx = l < L - 1
        mods = mods_all[l]
        ng = norm_g[l]
        xs = _ffn_call(xs, mods, ng[0:1], f1g, f1u, f1d, l, 0, R, SEQ, B, tm, tf)
        proj = _win_call(xs, mods, ng[1:2], w_in_p, l, SEQ, B, tm, D_IN_PAD // 3)
        o_a = _na_call(proj, na_bias[l], tile2(na_q_g[l][None]), tile2(na_k_g[l][None]), ones128,
                       B, SEQ, CTX, want_ctx)
        prep = _rw_prep_call(proj, rw_conv[l], wlr[l], rw_w0[l], rw_a0[l], rw_k_k[l][None], rw_k_a[l][None],
                             ones512, B, SEQ, CTX, min(tm, 256))
        vec = lambda t: t.reshape(1, GROUP_W)
        yf, yb = _rw_scan_call(prep, B, SEQ, CTX)
        o_b = _rw_readout_call(prep, yf, yb, vec(rw_r_k[l]), vec(rw_gn_w[l]), vec(rw_gn_b[l]), ones512, tm)
        o_c = _wa_call(proj, wa_sink[l], cos_t, sin_t, tile2(wa_q_g[l][None]), tile2(wa_k_g[l][None]), ones128,
                       B, SEQ, CTX, want_ctx)
        wz, gm, apow = (t[l] for t in s5_tabs)
        yloc, e_loc = _s5_local_call(proj.reshape(R // S5_CHUNK, S5_CHUNK, D_IN_PAD), bf(wz))
        xin = _s5_scan_call(e_loc, apow, B, SEQ, CTX)
        y_tok = _s5_carry_call(xin, bf(gm), yloc).reshape(R, GROUP_W)
        o_d = _s5_out_call(y_tok, proj, vec(s5_d[l]), glu_w_b[l], vec(s5_glu_b[l]), tm)
        n_rows = R if want_ctx else n_lat
        xs = _wout_call(xs, (o_a, o_b, o_c, o_d), mods, w_out_b, l, n_rows, SEQ, B, tm)
        xs = _ffn_call(xs, mods, ng[2:3], f2g, f2u, f2d, l, 6, n_rows, SEQ, B, tm, tf)
    return xs[:n_lat].reshape(B, SEQ, D)


def kernel(x, c, ctx, c_ctx, w_ada, b_ada, norm_g, ffn1_wg, ffn1_wu, ffn1_wd, ffn2_wg, ffn2_wu, ffn2_wd, w_in, w_out, na_q_g, na_k_g, na_rpb, rw_conv, rw_w0, rw_w_up, rw_a0, rw_a_up, rw_g_up, rw_k_k, rw_k_a, rw_r_k, rw_gn_w, rw_gn_b, wa_q_g, wa_k_g, wa_sink, s5_a_re, s5_a_im, s5_log_dt, s5_b_re, s5_b_im, s5_c_re, s5_c_im, s5_d, s5_glu_w, s5_glu_b):
    return _forward(x, c, ctx, c_ctx, w_ada, b_ada, norm_g, ffn1_wg, ffn1_wu, ffn1_wd, ffn2_wg, ffn2_wu, ffn2_wd,
                    w_in, w_out, na_q_g, na_k_g, na_rpb, rw_conv, rw_w0, rw_w_up, rw_a0, rw_a_up, rw_g_up,
                    rw_k_k, rw_k_a, rw_r_k, rw_gn_w, rw_gn_b, wa_q_g, wa_k_g, wa_sink, s5_a_re, s5_a_im,
                    s5_log_dt, s5_b_re, s5_b_im, s5_c_re, s5_c_im, s5_d, s5_glu_w, s5_glu_b, tm=512, tf=512)
```

```python
import functools
import math

import numpy as np
import jax
import jax.numpy as jnp
from jax import lax
from jax.experimental import pallas as pl
from jax.experimental.pallas import tpu as pltpu

F32 = jnp.float32
BF16 = jnp.bfloat16

D_MODEL = 2048
GRID_W = 64
HEAD_DIM = 64
GROUP_W = D_MODEL // 4
N_MOD = 9
NORM_EPS = 1e-6
ROPE_BASE = 10000.0
NEG_INF = -1e30

NA_HEADS = GROUP_W // HEAD_DIM
NA_WIN_R = 8
NA_WIN_C = 16
RW_HEADS = GROUP_W // HEAD_DIM
RW_DECAY_RANK = 32
RW_ICLR_RANK = 32
RW_GATE_RANK = 96
RW_GN_EPS = 64e-5
WA_HEADS = GROUP_W // HEAD_DIM
WA_KV_HEADS = 2
WA_GROUP = WA_HEADS // WA_KV_HEADS
WA_WINDOW = 128
S5_P = 16
S5_GROUPS = GROUP_W // S5_P
S5_N = 64

LANE = 128
QBLK = 2 * GRID_W
NA_KBLKS = 5
RW_CHUNK = 64
S5_CHUNK = 8
S5_OCTETS = 4
VMEM_LIMIT = 56 * 1024 * 1024

COL_AQ, COL_AK, COL_AV = 0, 512, 1024
COL_BR = 1536
COL_CQ = 3072
COL_DU = 3584
COL_BLR = 4096
COL_CK, COL_CV = 4352, 4480
D_IN_PAD = 4608

PB_R, PB_V, PB_KK, PB_G = 0, 1, 2, 9
PB_N = 10


def _cparams(sem):
    return pltpu.CompilerParams(dimension_semantics=sem, vmem_limit_bytes=VMEM_LIMIT)


def _dot(a, b):
    return jnp.dot(a, b, preferred_element_type=F32)


def _dot_nt(a, b):
    return lax.dot_general(a, b, (((1,), (1,)), ((), ())), preferred_element_type=F32)


def _dot_tn(a, b):
    return lax.dot_general(a, b, (((0,), (0,)), ((), ())), preferred_element_type=F32)


def _sigmoid(x):
    return 1.0 / (1.0 + jnp.exp(-x))


def _modulate(x, g, shift, scale):
    ms = jnp.mean(x * x, axis=-1, keepdims=True)
    return (x * lax.rsqrt(ms + NORM_EPS) * g) * (1.0 + scale) + shift


def _seg_sum(x, ones_bd):
    hi = x.astype(BF16)
    lo = (x - hi.astype(F32)).astype(BF16)
    return _dot(hi, ones_bd) + _dot(lo, ones_bd)


def _with_ones(v):
    one = jnp.ones((v.shape[0], HEAD_DIM), F32)
    return jnp.concatenate([v[:, :HEAD_DIM], one, v[:, HEAD_DIM:], one], axis=-1).astype(BF16)


def _prob(s, m):
    return jnp.exp((s - m).astype(BF16))


def _head_rmsnorm(x, ones_bd):
    ms = _seg_sum(x * x, ones_bd) * (1.0 / HEAD_DIM)
    return x * lax.rsqrt(ms + NORM_EPS)


def _ada_kernel(c_ref, w_ref, b_ref, o_ref):
    c = c_ref[...]
    s = (c * _sigmoid(c)).astype(BF16)
    o_ref[...] = _dot(s, w_ref[...].astype(BF16)) + b_ref[...]


def _ada_call(cc, w_ada, b_ada):
    L, D, N = w_ada.shape
    tn = 1024
    return pl.pallas_call(
        _ada_kernel,
        out_shape=jax.ShapeDtypeStruct((L, 8, N), F32),
        grid=(L, N // tn),
        in_specs=[
            pl.BlockSpec((8, D), lambda l, j: (0, 0)),
            pl.BlockSpec((None, D, tn), lambda l, j: (l, 0, j)),
            pl.BlockSpec((None, 1, tn), lambda l, j: (l, 0, j)),
        ],
        out_specs=pl.BlockSpec((None, 8, tn), lambda l, j: (l, 0, j)),
        compiler_params=_cparams(("arbitrary", "arbitrary")),
        name="adaln",
    )(cc, w_ada, b_ada.reshape(L, 1, N))


def _next_tile_rows(j, n_steps, tm):
    rs = -(-tm // (n_steps * 16)) * 16
    return pl.ds(pl.multiple_of(jnp.minimum(j * rs, tm - rs), 16), rs)


def _ffn_kernel(x_ref, mod_ref, g_ref, wg_ref, wu_ref, wd_ref, o_ref, h_sc, acc_sc, *, mi):
    j = pl.program_id(1)

    @pl.when(j == 0)
    def _():
        h = _modulate(x_ref[...], g_ref[...], mod_ref[mi:mi + 1, :], mod_ref[mi + 1:mi + 2, :])
        h_sc[...] = h.astype(BF16)
        acc_sc[...] = jnp.zeros_like(acc_sc)

    h = h_sc[...]
    gate = _dot(h, wg_ref[...])
    up = _dot(h, wu_ref[...])
    a = (gate * _sigmoid(gate) * up).astype(BF16)
    acc_sc[...] += _dot(a, wd_ref[...])

    @pl.when(j == pl.num_programs(1) - 1)
    def _():
        o_ref[...] = x_ref[...] + 0.5 * mod_ref[mi + 2:mi + 3, :] * acc_sc[...]


def _ffn_call(xs, mods, g, wg, wu, wd, l, mi, n_rows, rows_per_seq, n_batch, tm, tf):
    D = xs.shape[1]
    F = wg.shape[2]
    tps = rows_per_seq // tm
    return pl.pallas_call(
        functools.partial(_ffn_kernel, mi=mi),
        out_shape=jax.ShapeDtypeStruct((n_rows, D), F32),
        grid=(n_rows // tm, F // tf),
        in_specs=[
            pl.BlockSpec((tm, D), lambda i, j: (i, 0)),
            pl.BlockSpec((None, N_MOD, D), lambda i, j: (jnp.minimum(i // tps, n_batch), 0, 0)),
            pl.BlockSpec((1, D), lambda i, j: (0, 0)),
            pl.BlockSpec((None, D, tf), lambda i, j: (l, 0, j)),
            pl.BlockSpec((None, D, tf), lambda i, j: (l, 0, j)),
            pl.BlockSpec((None, tf, D), lambda i, j: (l, j, 0)),
        ],
        out_specs=pl.BlockSpec((tm, D), lambda i, j: (i, 0)),
        scratch_shapes=[pltpu.VMEM((tm, D), BF16), pltpu.VMEM((tm, D), F32)],
        compiler_params=_cparams(("arbitrary", "arbitrary")),
        name="ffn",
    )(xs, mods, g, wg, wu, wd)


def _win_kernel(x_ref, xn_ref, mod_ref, modn_ref, g_ref, w_ref, o_ref, h_sc):
    i, j = pl.program_id(0), pl.program_id(1)
    cur = i % 2

    @pl.when((i == 0) & (j == 0))
    def _():
        h = _modulate(x_ref[...], g_ref[...], mod_ref[3:4, :], mod_ref[4:5, :])
        h_sc[0] = h.astype(BF16)

    o_ref[...] = _dot(h_sc[cur], w_ref[...])

    rows = _next_tile_rows(j, pl.num_programs(1), x_ref.shape[0])
    hn = _modulate(xn_ref[rows, :], g_ref[...], modn_ref[3:4, :], modn_ref[4:5, :])
    h_sc[1 - cur, rows, :] = hn.astype(BF16)


def _win_call(xs, mods, g, w_in, l, rows_per_seq, n_batch, tm, tn):
    R, D = xs.shape
    N = w_in.shape[2]
    tps = rows_per_seq // tm
    n_i = R // tm
    nxt = lambda i: jnp.minimum(i + 1, n_i - 1)
    mod_id = lambda i: jnp.minimum(i // tps, n_batch)
    return pl.pallas_call(
        _win_kernel,
        out_shape=jax.ShapeDtypeStruct((R, N), F32),
        grid=(n_i, N // tn),
        in_specs=[
            pl.BlockSpec((tm, D), lambda i, j: (i, 0)),
            pl.BlockSpec((tm, D), lambda i, j: (nxt(i), 0)),
            pl.BlockSpec((None, N_MOD, D), lambda i, j: (mod_id(i), 0, 0)),
            pl.BlockSpec((None, N_MOD, D), lambda i, j: (mod_id(nxt(i)), 0, 0)),
            pl.BlockSpec((1, D), lambda i, j: (0, 0)),
            pl.BlockSpec((None, D, tn), lambda i, j: (l, 0, j)),
        ],
        out_specs=pl.BlockSpec((tm, tn), lambda i, j: (i, j)),
        scratch_shapes=[pltpu.VMEM((2, tm, D), BF16)],
        compiler_params=_cparams(("arbitrary", "arbitrary")),
        name="in_proj",
    )(xs, xs, mods, mods, g, w_in)


def _wout_kernel(x_ref, oa_ref, ob_ref, oc_ref, od_ref, mod_ref, w_ref, o_ref):
    acc = _dot(oa_ref[...], w_ref[0 * GROUP_W:1 * GROUP_W, :])
    acc += _dot(ob_ref[...], w_ref[1 * GROUP_W:2 * GROUP_W, :])
    acc += _dot(oc_ref[...], w_ref[2 * GROUP_W:3 * GROUP_W, :])
    acc += _dot(od_ref[...], w_ref[3 * GROUP_W:4 * GROUP_W, :])
    o_ref[...] = x_ref[...] + mod_ref[5:6, :] * acc


def _wout_call(xs, outs, mods, w_out, l, n_rows, rows_per_seq, n_batch, tm):
    D = xs.shape[1]
    tps = rows_per_seq // tm
    ospec = pl.BlockSpec((tm, GROUP_W), lambda i: (i, 0))
    return pl.pallas_call(
        _wout_kernel,
        out_shape=jax.ShapeDtypeStruct((n_rows, D), F32),
        grid=(n_rows // tm,),
        in_specs=[
            pl.BlockSpec((tm, D), lambda i: (i, 0)),
            ospec, ospec, ospec, ospec,
            pl.BlockSpec((None, N_MOD, D), lambda i: (jnp.minimum(i // tps, n_batch), 0, 0)),
            pl.BlockSpec((None, D, D), lambda i: (l, 0, 0)),
        ],
        out_specs=pl.BlockSpec((tm, D), lambda i: (i, 0)),
        compiler_params=_cparams(("arbitrary",)),
        name="out_proj",
    )(xs, *outs, mods, w_out)


def _na_case_reps(n_qb):
    return (0, 1, 2, n_qb - 2, n_qb - 1)


def _na_start(p, n_qb):
    return jnp.clip(p - 2, 0, n_qb - NA_KBLKS)


def _na_bias_table(rpb, n_qb):
    H = rpb.shape[0]
    rows = 2 * n_qb
    qc = np.arange(GRID_W)[:, None]
    kc = np.arange(GRID_W)[None, :]
    cs = np.clip(qc - NA_WIN_C // 2, 0, GRID_W - NA_WIN_C)
    okc = (kc >= cs) & (kc < cs + NA_WIN_C)
    dc = np.where(okc, kc - qc + NA_WIN_C - 1, 0)
    rp = rpb.astype(F32).reshape(H, 2 * NA_WIN_R - 1, 2 * NA_WIN_C - 1)
    blocks = jnp.where(jnp.asarray(okc)[None, None], jnp.take(rp, jnp.asarray(dc), axis=2), NEG_INF)
    blocks = blocks.astype(BF16)
    neg = jnp.full((H, GRID_W, GRID_W), NEG_INF, BF16)
    cases = []
    for p in _na_case_reps(n_qb):
        start = min(max(p - 2, 0), n_qb - NA_KBLKS)
        qrows = []
        for qr in range(2):
            qa = 2 * p + qr
            rs = min(max(qa - NA_WIN_R // 2, 0), rows - NA_WIN_R)
            krow = []
            for kr in range(2 * NA_KBLKS):
                ka = 2 * start + kr
                krow.append(blocks[:, ka - qa + NA_WIN_R - 1] if rs <= ka < rs + NA_WIN_R else neg)
            qrows.append(jnp.concatenate(krow, axis=-1))
        cases.append(jnp.concatenate(qrows, axis=-2))
    return jnp.stack(cases, axis=0)


def _na_kernel(q_ref, k_ref, v_ref, kc_ref, vc_ref, bias_ref, qg_ref, kg_ref, ones_ref, o_ref,
               kn_sc, vn_sc, kcn_sc, vcn_sc, *, n_qb, seq, ctx):
    p = pl.program_id(2)
    ones_bd = ones_ref[...]
    scale = HEAD_DIM ** -0.5

    @pl.when(p == 0)
    def _():
        kg = kg_ref[...]
        rows = 512

        def body(i, carry):
            sl = pl.ds(pl.multiple_of(i * rows, rows), rows)
            kn_sc[sl, :] = (_head_rmsnorm(k_ref[sl, :], ones_bd) * kg).astype(BF16)
            vn_sc[sl, :] = _with_ones(v_ref[sl, :])
            return carry

        lax.fori_loop(0, seq // rows, body, 0)
        kcn_sc[...] = (_head_rmsnorm(kc_ref[...], ones_bd) * kg).astype(BF16)
        vcn_sc[...] = _with_ones(vc_ref[...])

    nq = q_ref.shape[0] // QBLK
    n_lat = n_qb // nq
    q = (_head_rmsnorm(q_ref[...], ones_bd) * (qg_ref[...] * scale)).astype(BF16)
    kcn = kcn_sc[...]
    vcn = vcn_sc[...]
    chains = [(qb, h) for qb in range(nq) for h in range(2)]
    hsl = lambda h: slice(h * HEAD_DIM, (h + 1) * HEAD_DIM)
    qh = [q[qb * QBLK:(qb + 1) * QBLK, hsl(h)] for qb, h in chains]
    s_c = [_dot_nt(qh[c], kcn[:, hsl(h)]) for c, (qb, h) in enumerate(chains)]
    m_c = [jnp.max(t, axis=-1, keepdims=True) for t in s_c]

    vsl = lambda h: slice(h * LANE, (h + 1) * LANE)
    vca = [vcn[:, vsl(h)] for h in range(2)]

    def finish(oa):
        rows = [jnp.concatenate([oa[qb * 2 + h][:, :HEAD_DIM] / oa[qb * 2 + h][:, HEAD_DIM:] for h in range(2)],
                                axis=-1) for qb in range(nq)]
        o_ref[...] = jnp.concatenate(rows, axis=0).astype(o_ref.dtype)

    @pl.when(p < n_lat)
    def _():
        starts = [_na_start(p * nq + qb, n_qb) for qb in range(nq)]
        cases = [p * nq + qb - starts[qb] for qb in range(nq)]
        sls = [pl.ds(pl.multiple_of(st * QBLK, QBLK), NA_KBLKS * QBLK) for st in starts]
        kw = [kn_sc[sl, :] for sl in sls]
        vw = [vn_sc[sl, :] for sl in sls]
        s_n = [_dot_nt(qh[c], kw[qb][:, hsl(h)]) + bias_ref[cases[qb], h].astype(F32)
               for c, (qb, h) in enumerate(chains)]
        m = [jnp.maximum(jnp.max(s_n[c], axis=-1, keepdims=True), m_c[c]) for c in range(len(chains))]
        p_n = [_prob(s_n[c], m[c]) for c in range(len(chains))]
        p_c = [_prob(s_c[c], m[c]) for c in range(len(chains))]
        finish([_dot(p_n[c], vw[qb][:, vsl(h)]) + _dot(p_c[c], vca[h])
                for c, (qb, h) in enumerate(chains)])

    @pl.when(p >= n_lat)
    def _():
        finish([_dot(_prob(s_c[c], m_c[c]), vca[h]) for c, (qb, h) in enumerate(chains)])


def _na_call(proj, bias, qg, kg, ones_bd, n_batch, seq, ctx, want_ctx):
    R = proj.shape[0]
    nq = 2
    n_qb = seq // QBLK
    n_lat, n_ctx = n_qb // nq, ctx // (nq * QBLK)
    steps = n_lat + (n_ctx if want_ctx else 0)
    ctx_kb = n_batch * seq // ctx

    def qrow(b, p):
        return jnp.where(p < n_lat, b * n_lat + p, n_batch * n_lat + b * n_ctx + (p - n_lat))

    cq, ck, cv = COL_AQ // LANE, COL_AK // LANE, COL_AV // LANE
    return pl.pallas_call(
        functools.partial(_na_kernel, n_qb=n_qb, seq=seq, ctx=ctx),
        out_shape=jax.ShapeDtypeStruct((R, GROUP_W), BF16),
        grid=(n_batch, NA_HEADS // 2, steps),
        in_specs=[
            pl.BlockSpec((nq * QBLK, LANE), lambda b, hp, p: (qrow(b, p), cq + hp)),
            pl.BlockSpec((seq, LANE), lambda b, hp, p: (b, ck + hp)),
            pl.BlockSpec((seq, LANE), lambda b, hp, p: (b, cv + hp)),
            pl.BlockSpec((ctx, LANE), lambda b, hp, p: (ctx_kb + b, ck + hp)),
            pl.BlockSpec((ctx, LANE), lambda b, hp, p: (ctx_kb + b, cv + hp)),
            pl.BlockSpec((5, 2, QBLK, NA_KBLKS * QBLK), lambda b, hp, p: (0, hp, 0, 0)),
            pl.BlockSpec((1, LANE), lambda b, hp, p: (0, 0)),
            pl.BlockSpec((1, LANE), lambda b, hp, p: (0, 0)),
            pl.BlockSpec((LANE, LANE), lambda b, hp, p: (0, 0)),
        ],
        out_specs=pl.BlockSpec((nq * QBLK, LANE), lambda b, hp, p: (qrow(b, p), hp)),
        scratch_shapes=[pltpu.VMEM((seq, LANE), BF16), pltpu.VMEM((seq, 2 * LANE), BF16),
                        pltpu.VMEM((ctx, LANE), BF16), pltpu.VMEM((ctx, 2 * LANE), BF16)],
        compiler_params=_cparams(("arbitrary", "arbitrary", "arbitrary")),
        name="mixer_a",
    )(proj, proj, proj, proj, proj, bias, qg, kg, ones_bd)


def _rope(x, cos, sin_signed, first_half):
    rot = jnp.where(first_half, pltpu.roll(x, LANE - 16, 1), pltpu.roll(x, 16, 1))
    return x * cos + rot * sin_signed


def _wa_kernel(sink_ref, q_ref, k_ref, v_ref, kc_ref, vc_ref, cosk_ref, sin_k_ref, cosq_ref, sinq_ref,
               qg_ref, kg_ref, ones_ref, wmask_ref, o_ref, kn_sc, vn_sc, kcn_sc, vcn_sc, *, n_qb, seq, ctx):
    p = pl.program_id(1)
    ones_bd = ones_ref[...]
    scale = HEAD_DIM ** -0.5
    lane = lax.broadcasted_iota(jnp.int32, (1, LANE), 1)
    first_half = (lane % 32) < 16

    @pl.when(p == 0)
    def _():
        kg = kg_ref[...]
        rows = 512

        def body(i, carry):
            sl = pl.ds(pl.multiple_of(i * rows, rows), rows)
            kn = _head_rmsnorm(k_ref[sl, :], ones_bd) * kg
            kn_sc[sl, :] = _rope(kn, cosk_ref[sl, :], sin_k_ref[sl, :], first_half).astype(BF16)
            vn_sc[sl, :] = _with_ones(v_ref[sl, :])
            return carry

        lax.fori_loop(0, seq // rows, body, 0)
        kcn_sc[...] = (_head_rmsnorm(kc_ref[...], ones_bd) * kg).astype(BF16)
        vcn_sc[...] = _with_ones(vc_ref[...])

    nq = q_ref.shape[0] // QBLK
    n_lat = n_qb // nq
    qg = qg_ref[...] * scale
    is_lat = p < n_lat
    cosq = jnp.where(is_lat, cosq_ref[...], 1.0)
    sinq = jnp.where(is_lat, sinq_ref[...], 0.0)
    qh = []
    for c in range(4):
        qn = _head_rmsnorm(q_ref[:, c * LANE:(c + 1) * LANE], ones_bd) * qg
        qn = _rope(qn, cosq, sinq, first_half).astype(BF16)
        qh += [qn[:, :HEAD_DIM], qn[:, HEAD_DIM:]]
    kcn = kcn_sc[...]
    vcn = vcn_sc[...]
    chains = [(qb, kh) for qb in range(nq) for kh in range(WA_KV_HEADS)]
    nch = range(len(chains))
    hsl = lambda kh: slice(kh * HEAD_DIM, (kh + 1) * HEAD_DIM)
    qs = [jnp.concatenate([qh[kh * WA_GROUP + g][qb * QBLK:(qb + 1) * QBLK, :] for g in range(WA_GROUP)], axis=0)
          for qb, kh in chains]
    sk = [jnp.concatenate([jnp.full((QBLK, 1), sink_ref[kh * WA_GROUP + g], F32) for g in range(WA_GROUP)],
                          axis=0) for qb, kh in chains]
    s_c = [_dot_nt(qs[c], kcn[:, hsl(chains[c][1])]) for c in nch]
    m_c = [jnp.maximum(jnp.max(s_c[c], axis=-1, keepdims=True), sk[c]) for c in nch]

    vsl = lambda kh: slice(kh * LANE, (kh + 1) * LANE)
    vca = [vcn[:, vsl(kh)] for kh in range(WA_KV_HEADS)]

    def finish(oa, sink_w):
        rows = []
        for qb in range(nq):
            heads = []
            for kh in range(WA_KV_HEADS):
                c = qb * WA_KV_HEADS + kh
                oc = oa[c][:, :HEAD_DIM] / (oa[c][:, HEAD_DIM:] + sink_w[c])
                heads += [oc[g * QBLK:(g + 1) * QBLK, :] for g in range(WA_GROUP)]
            rows.append(jnp.concatenate(heads, axis=-1))
        o_ref[...] = jnp.concatenate(rows, axis=0).astype(o_ref.dtype)

    @pl.when(is_lat)
    def _():
        blk = [p * nq + qb for qb in range(nq)]
        ws = [jnp.clip((bq - 1) * QBLK, 0, seq - 3 * QBLK) for bq in blk]
        sls = [pl.ds(pl.multiple_of(w, QBLK), 3 * QBLK) for w in ws]
        kw = [kn_sc[sl, :] for sl in sls]
        vw = [vn_sc[sl, :] for sl in sls]
        case = [jnp.where(bq == 0, 0, jnp.where(bq == n_qb - 1, 2, 1)) for bq in blk]
        msk = [jnp.concatenate([wmask_ref[cs]] * WA_GROUP, axis=0) for cs in case]
        s_w = [_dot_nt(qs[c], kw[chains[c][0]][:, hsl(chains[c][1])]) + msk[chains[c][0]] for c in nch]
        m = [jnp.maximum(jnp.max(s_w[c], axis=-1, keepdims=True), m_c[c]) for c in nch]
        p_w = [_prob(s_w[c], m[c]) for c in nch]
        p_c = [_prob(s_c[c], m[c]) for c in nch]
        finish([_dot(p_w[c], vw[chains[c][0]][:, vsl(chains[c][1])]) + _dot(p_c[c], vca[chains[c][1]])
                for c in nch], [jnp.exp(sk[c] - m[c]) for c in nch])

    @pl.when(jnp.logical_not(is_lat))
    def _():
        finish([_dot(_prob(s_c[c], m_c[c]), vca[chains[c][1]]) for c in nch],
               [jnp.exp(sk[c] - m_c[c]) for c in nch])


def _wa_mask_table():
    i = np.arange(QBLK)[:, None]
    j = np.arange(3 * QBLK)[None, :]
    shifts = (0, QBLK, 2 * QBLK)
    return jnp.asarray(np.stack([np.where(np.abs(j - sh - i) <= WA_WINDOW, 0.0, NEG_INF) for sh in shifts]), F32)


def _wa_call(proj, sink, cos_t, sin_t, qg, kg, ones_bd, n_batch, seq, ctx, want_ctx):
    R = proj.shape[0]
    nq = 2
    n_qb = seq // QBLK
    n_lat, n_ctx = n_qb // nq, ctx // (nq * QBLK)
    steps = n_lat + (n_ctx if want_ctx else 0)
    ctx_kb = n_batch * seq // ctx

    def qrow(b, p):
        return jnp.where(p < n_lat, b * n_lat + p, n_batch * n_lat + b * n_ctx + (p - n_lat))

    ck, cv = COL_CK // LANE, COL_CV // LANE
    return pl.pallas_call(
        functools.partial(_wa_kernel, n_qb=n_qb, seq=seq, ctx=ctx),
        out_shape=jax.ShapeDtypeStruct((R, GROUP_W), BF16),
        grid=(n_batch, steps),
        in_specs=[
            pl.BlockSpec(memory_space=pltpu.SMEM),
            pl.BlockSpec((nq * QBLK, GROUP_W), lambda b, p: (qrow(b, p), COL_CQ // GROUP_W)),
            pl.BlockSpec((seq, LANE), lambda b, p: (b, ck)),
            pl.BlockSpec((seq, LANE), lambda b, p: (b, cv)),
            pl.BlockSpec((ctx, LANE), lambda b, p: (ctx_kb + b, ck)),
            pl.BlockSpec((ctx, LANE), lambda b, p: (ctx_kb + b, cv)),
            pl.BlockSpec((seq, LANE), lambda b, p: (0, 0)),
            pl.BlockSpec((seq, LANE), lambda b, p: (0, 0)),
            pl.BlockSpec((nq * QBLK, LANE), lambda b, p: (jnp.minimum(p, n_lat - 1), 0)),
            pl.BlockSpec((nq * QBLK, LANE), lambda b, p: (jnp.minimum(p, n_lat - 1), 0)),
            pl.BlockSpec((1, LANE), lambda b, p: (0, 0)),
            pl.BlockSpec((1, LANE), lambda b, p: (0, 0)),
            pl.BlockSpec((LANE, LANE), lambda b, p: (0, 0)),
            pl.BlockSpec((3, QBLK, 3 * QBLK), lambda b, p: (0, 0, 0)),
        ],
        out_specs=pl.BlockSpec((nq * QBLK, GROUP_W), lambda b, p: (qrow(b, p), 0)),
        scratch_shapes=[pltpu.VMEM((seq, LANE), BF16), pltpu.VMEM((seq, 2 * LANE), BF16),
                        pltpu.VMEM((ctx, LANE), BF16), pltpu.VMEM((ctx, 2 * LANE), BF16)],
        compiler_params=_cparams(("arbitrary", "arbitrary")),
        name="mixer_c",
    )(sink, proj, proj, proj, proj, proj, cos_t, sin_t, cos_t, sin_t, qg, kg, ones_bd, _wa_mask_table())


def _rw_prep_kernel(x_ref, prev_ref, next_ref, lr_ref, conv_ref, wlr_ref, w0_ref, a0_ref, kk_ref, ka_ref,
                    ones_ref, o_ref, *, tiles_lat, tiles_ctx, n_lat_tiles):
    i = pl.program_id(0)
    tm = x_ref.shape[0]
    in_lat = i < n_lat_tiles
    first = jnp.where(in_lat, i % tiles_lat == 0, (i - n_lat_tiles) % tiles_ctx == 0)
    last = jnp.where(in_lat, i % tiles_lat == tiles_lat - 1, (i - n_lat_tiles) % tiles_ctx == tiles_ctx - 1)
    x = x_ref[...]
    prev_row = jnp.where(first, 0.0, prev_ref[7:8, :])
    next_row = jnp.where(last, 0.0, next_ref[0:1, :])
    row = lax.broadcasted_iota(jnp.int32, (tm, 1), 0)
    x_prev = jnp.where(row == 0, prev_row, pltpu.roll(x, 1, 0))
    x_next = jnp.where(row == tm - 1, next_row, pltpu.roll(x, tm - 1, 0))
    cw = conv_ref[...]
    y = x_prev * cw[0:1, :] + x * cw[1:2, :] + x_next * cw[2:3, :]
    r = y[:, 0:GROUP_W]
    k = y[:, GROUP_W:2 * GROUP_W]
    v = y[:, 2 * GROUP_W:3 * GROUP_W]

    lr = lr_ref[...]
    lane = lax.broadcasted_iota(jnp.int32, (1, lr.shape[1]), 1)
    c1 = RW_DECAY_RANK
    c2 = c1 + RW_ICLR_RANK
    c3 = c2 + RW_GATE_RANK
    act = jnp.where(lane < c1, jnp.tanh(lr),
                    jnp.where(lane < c2, lr, jnp.where(lane < c3, _sigmoid(lr), 0.0)))
    up = _dot(act.astype(BF16), wlr_ref[...])

    ones_bd = ones_ref[...]
    kk = k * kk_ref[...]
    nrm = jnp.sqrt(_seg_sum(kk * kk, ones_bd))
    kk = kk / jnp.maximum(nrm, 1e-12)

    o_ref[:, PB_R * GROUP_W:(PB_R + 1) * GROUP_W] = r
    o_ref[:, PB_V * GROUP_W:(PB_V + 1) * GROUP_W] = v
    o_ref[:, PB_KK * GROUP_W:(PB_KK + 1) * GROUP_W] = kk
    o_ref[:, PB_G * GROUP_W:(PB_G + 1) * GROUP_W] = up[:, 4 * GROUP_W:5 * GROUP_W]
    ka = ka_ref[...]
    for d in range(2):
        z = w0_ref[d:d + 1, :] + up[:, d * GROUP_W:(d + 1) * GROUP_W]
        sp = jnp.maximum(-z, 0.0) + jnp.log(1.0 + jnp.exp(-jnp.abs(z)))
        lw = -jnp.exp(-sp - 0.5)
        a = _sigmoid(a0_ref[d:d + 1, :] + up[:, (2 + d) * GROUP_W:(3 + d) * GROUP_W])
        kd = k * (1.0 + (a - 1.0) * ka)
        base = 3 + 3 * d
        o_ref[:, base * GROUP_W:(base + 1) * GROUP_W] = lw
        o_ref[:, (base + 1) * GROUP_W:(base + 2) * GROUP_W] = a
        o_ref[:, (base + 2) * GROUP_W:(base + 3) * GROUP_W] = kd


def _rw_prep_call(proj, conv_w, wlr, w0, a0, k_k, k_a, ones512, n_batch, seq, ctx, tm):
    R = proj.shape[0]
    n_lat_tiles = n_batch * seq // tm
    nb8 = R // 8
    t8 = tm // 8
    wide = 3 * GROUP_W
    cb = COL_BR // wide
    return pl.pallas_call(
        functools.partial(_rw_prep_kernel, tiles_lat=seq // tm, tiles_ctx=ctx // tm, n_lat_tiles=n_lat_tiles),
        out_shape=jax.ShapeDtypeStruct((R, PB_N * GROUP_W), F32),
        grid=(R // tm,),
        in_specs=[
            pl.BlockSpec((tm, wide), lambda i: (i, cb)),
            pl.BlockSpec((8, wide), lambda i: (jnp.maximum(i * t8 - 1, 0), cb)),
            pl.BlockSpec((8, wide), lambda i: (jnp.minimum((i + 1) * t8, nb8 - 1), cb)),
            pl.BlockSpec((tm, 256), lambda i: (i, COL_BLR // 256)),
            pl.BlockSpec((3, wide), lambda i: (0, 0)),
            pl.BlockSpec((256, 5 * GROUP_W), lambda i: (0, 0)),
            pl.BlockSpec((2, GROUP_W), lambda i: (0, 0)),
            pl.BlockSpec((2, GROUP_W), lambda i: (0, 0)),
            pl.BlockSpec((1, GROUP_W), lambda i: (0, 0)),
            pl.BlockSpec((1, GROUP_W), lambda i: (0, 0)),
            pl.BlockSpec((GROUP_W, GROUP_W), lambda i: (0, 0)),
        ],
        out_specs=pl.BlockSpec((tm, PB_N * GROUP_W), lambda i: (i, 0)),
        compiler_params=_cparams(("arbitrary",)),
        name="mixer_b_prep",
    )(proj, proj, proj, proj, conv_w, wlr, w0, a0, k_k, k_a, ones512)


RW_SPLIT = 1
_NN = ((1,), (0,))
_NT = ((1,), (1,))
_TN = ((0,), (0,))


def _split(x, n):
    parts = []
    for _ in range(n):
        p = x.astype(BF16)
        parts.append(p)
        x = x - p.astype(F32)
    return parts


def _sdot(a, b, dims):
    n = max(len(a), len(b))
    acc = None
    for i, ai in enumerate(a):
        for j, bj in enumerate(b):
            if i + j < n:
                t = lax.dot_general(ai, bj, (dims, ((), ())), preferred_element_type=F32)
                acc = t if acc is None else acc + t
    return acc


def _rw_chunks(at, bt, kt, rt, v, gam, s0, strict, incl):
    n = len(at)
    ids = range(n)
    C = at[0].shape[0]
    sp = lambda t: _split(t, RW_SPLIT)
    b1 = lambda t: [t.astype(BF16)]
    ar = [b1(jnp.concatenate([at[i], rt[i]], axis=0)) for i in ids]
    bk = [sp(jnp.concatenate([bt[i], kt[i]], axis=0)) for i in ids]
    g4 = [_sdot(ar[i], bk[i][:1], _NT) for i in ids]
    a_ab = [jnp.where(strict[i], g4[i][:C, :C], 0.0) for i in ids]
    a_ak = [jnp.where(strict[i], g4[i][:C, C:], 0.0) for i in ids]
    a_rb = [jnp.where(incl[i], g4[i][C:, :C], 0.0) for i in ids]
    a_rk = [jnp.where(incl[i], g4[i][C:, C:], 0.0) for i in ids]
    wv = [_sdot(b1(a_ak[i]), b1(v[i]), _NN) for i in ids]
    x = [jnp.concatenate([at[i], wv[i]], axis=1) for i in ids]

    row = lax.broadcasted_iota(jnp.int32, (C, C), 0)
    col = lax.broadcasted_iota(jnp.int32, (C, C), 1)
    eye = (row == col).astype(F32)
    blk = 16
    same = (row // blk) == (col // blk)
    p = [jnp.where(same, a_ab[i], 0.0) for i in ids]
    m = [eye + p[i] for i in ids]
    for _ in range(int(math.log2(blk)) - 1):
        pb_ = [b1(p[i]) for i in ids]
        p = [_sdot(pb_[i], pb_[i], _NN) for i in ids]
        m = [m[i] + _sdot(b1(m[i]), b1(p[i]), _NN) for i in ids]
    while blk < C:
        wider = (row // (2 * blk)) == (col // (2 * blk))
        join = wider & jnp.logical_not(same)
        mb = [b1(m[i]) for i in ids]
        t = [_sdot(b1(jnp.where(join, a_ab[i], 0.0)), mb[i], _NN) for i in ids]
        m = [m[i] + _sdot(mb[i], b1(t[i]), _NN) for i in ids]
        same, blk = wider, 2 * blk
    x = [_sdot(b1(m[i]), b1(x[i]), _NN) for i in ids]

    s0b = [sp(s0[i]) for i in ids]
    u = [x[i][:, HEAD_DIM:] + _sdot(sp(x[i][:, :HEAD_DIM]), s0b[i], _NT) for i in ids]
    uv = [sp(jnp.concatenate([u[i], v[i]], axis=0)) for i in ids]
    y = [_sdot(b1(jnp.concatenate([a_rb[i], a_rk[i]], axis=1)), uv[i][:1], _NN)
         + _sdot(sp(rt[i]), s0b[i], _NT) for i in ids]
    s1 = [(s0[i] + _sdot(uv[i], bk[i], _TN)) * gam[i] for i in ids]
    return y, s1


def _rw_scan_kernel(*refs, n_batch, nc):
    n_in = 12 * n_batch
    yfl_ref, ybl_ref, yfc_ref, ybc_ref, s_sc = refs[n_in:]
    s = pl.program_id(0)

    @pl.when(s == 0)
    def _():
        s_sc[...] = jnp.zeros_like(s_sc)

    C = refs[0].shape[0]
    row = lax.broadcasted_iota(jnp.int32, (C, C), 0)
    col = lax.broadcasted_iota(jnp.int32, (C, C), 1)
    chains = dict(at=[], bt=[], kt=[], rt=[], v=[], gam=[], s0=[], strict=[], incl=[])
    for b in range(n_batch):
        for d, reverse in enumerate((False, True)):
            r_ref, v_ref, kk_ref, lw_ref, a_ref, k_ref = refs[12 * b + 6 * d:12 * b + 6 * d + 6]
            incl, strict = (row <= col, row < col) if reverse else (row >= col, row > col)
            lw = lw_ref[...]
            kk = kk_ref[...]
            v = v_ref[...]
            cum = _sdot([incl.astype(BF16)], _split(lw, 3), _NN)
            e_l = jnp.exp(cum)
            e_n = jnp.exp(-cum)
            at = -(kk * jnp.exp(cum - lw))
            bt = kk * a_ref[...] * e_n
            kt = k_ref[...] * e_n
            rt = r_ref[...] * e_l
            gam = e_l[0:1, :] if reverse else e_l[C - 1:C, :]
            for h in range(RW_HEADS):
                hs = slice(h * HEAD_DIM, (h + 1) * HEAD_DIM)
                for name, val in (("at", at), ("bt", bt), ("kt", kt), ("rt", rt), ("v", v), ("gam", gam)):
                    chains[name].append(val[:, hs])
                chains["s0"].append(s_sc[b, d, h])
                chains["strict"].append(strict)
                chains["incl"].append(incl)
    y, s1 = _rw_chunks(**chains)
    ys = {}
    for b in range(n_batch):
        for d in range(2):
            base = (b * 2 + d) * RW_HEADS
            for h in range(RW_HEADS):
                s_sc[b, d, h] = s1[base + h]
            ys[b, d] = jnp.concatenate(y[base:base + RW_HEADS], axis=-1)

    @pl.when(s < nc)
    def _():
        for b in range(n_batch):
            yfc_ref[b] = ys[b, 0]
            ybc_ref[b] = ys[b, 1]

    @pl.when(s >= nc)
    def _():
        for b in range(n_batch):
            yfl_ref[b] = ys[b, 0]
            ybl_ref[b] = ys[b, 1]


def _rw_scan_call(prep, n_batch, seq, ctx):
    C = RW_CHUNK
    nc, nl = ctx // C, seq // C
    ctx_base = n_batch * nl

    def blk_f(b, s):
        return jnp.where(s < nc, ctx_base + b * nc + s, b * nl + (s - nc))

    def blk_b(b, s):
        return jnp.where(s < nc, ctx_base + b * nc + (nc - 1 - s), b * nl + (nl - 1 - (s - nc)))

    def col(blk, b, cb):
        return pl.BlockSpec((C, GROUP_W), lambda s: (blk(b, s), cb))

    in_specs = []
    for b in range(n_batch):
        in_specs += [col(blk_f, b, c) for c in (PB_R, PB_V, PB_KK, 3, 4, 5)]
        in_specs += [col(blk_b, b, c) for c in (PB_R, PB_V, PB_KK, 6, 7, 8)]
    lat = jax.ShapeDtypeStruct((n_batch, seq, GROUP_W), F32)
    cx = jax.ShapeDtypeStruct((n_batch, ctx, GROUP_W), F32)
    blk3 = (n_batch, C, GROUP_W)
    out_specs = (
        pl.BlockSpec(blk3, lambda s: (0, jnp.maximum(s - nc, 0), 0)),
        pl.BlockSpec(blk3, lambda s: (0, nl - 1 - jnp.maximum(s - nc, 0), 0)),
        pl.BlockSpec(blk3, lambda s: (0, jnp.minimum(s, nc - 1), 0)),
        pl.BlockSpec(blk3, lambda s: (0, nc - 1 - jnp.minimum(s, nc - 1), 0)),
    )
    yfl, ybl, yfc, ybc = pl.pallas_call(
        functools.partial(_rw_scan_kernel, n_batch=n_batch, nc=nc),
        out_shape=(lat, lat, cx, cx),
        grid=(nc + nl,),
        in_specs=in_specs,
        out_specs=out_specs,
        scratch_shapes=[pltpu.VMEM((n_batch, 2, RW_HEADS, HEAD_DIM, HEAD_DIM), F32)],
        compiler_params=_cparams(("arbitrary",)),
        name="mixer_b_scan",
    )(*([prep] * (12 * n_batch)))
    flat = lambda a, c: jnp.concatenate([a.reshape(n_batch * seq, GROUP_W), c.reshape(n_batch * ctx, GROUP_W)], 0)
    return flat(yfl, yfc), flat(ybl, ybc)


def _rw_readout_kernel(r_ref, v_ref, k0_ref, k1_ref, g_ref, yf_ref, yb_ref, rk_ref, gw_ref, gb_ref, ones_ref,
                       o_ref):
    ones_bd = ones_ref[...]
    v = v_ref[...]
    bonus = _seg_sum(r_ref[...] * (k0_ref[...] + k1_ref[...]) * rk_ref[...], ones_bd)
    y = yf_ref[...] + yb_ref[...] + bonus * v
    mu = _seg_sum(y, ones_bd) * (1.0 / HEAD_DIM)
    yc = y - mu
    var = _seg_sum(yc * yc, ones_bd) * (1.0 / HEAD_DIM)
    yn = yc * lax.rsqrt(var + RW_GN_EPS)
    o_ref[...] = ((yn * gw_ref[...] + gb_ref[...]) * g_ref[...]).astype(o_ref.dtype)


def _rw_readout_call(prep, yf, yb, r_k, gn_w, gn_b, ones512, tm):
    R = prep.shape[0]
    col = lambda cb: pl.BlockSpec((tm, GROUP_W), lambda i: (i, cb))
    vec = pl.BlockSpec((1, GROUP_W), lambda i: (0, 0))
    return pl.pallas_call(
        _rw_readout_kernel,
        out_shape=jax.ShapeDtypeStruct((R, GROUP_W), BF16),
        grid=(R // tm,),
        in_specs=[col(PB_R), col(PB_V), col(5), col(8), col(PB_G), col(0), col(0), vec, vec, vec,
                  pl.BlockSpec((GROUP_W, GROUP_W), lambda i: (0, 0))],
        out_specs=col(0),
        compiler_params=_cparams(("arbitrary",)),
        name="mixer_b_readout",
    )(prep, prep, prep, prep, prep, yf, yb, r_k, gn_w, gn_b, ones512)


def _s5_tables(a_re, a_im, log_dt, b_re, b_im, c_re, c_im):
    Lc, G, N, P = S5_CHUNK, S5_GROUPS, S5_N, S5_P
    dt = jnp.exp(log_dt)[..., None]
    lam_re, lam_im = dt * a_re, dt * a_im
    tau = jnp.arange(Lc + 1, dtype=F32)[:, None, None, None]
    mag = jnp.exp(tau * lam_re)
    pw_re, pw_im = mag * jnp.cos(tau * lam_im), mag * jnp.sin(tau * lam_im)
    ab_re, ab_im = pw_re[1], pw_im[1]
    den = a_re * a_re + a_im * a_im
    nr = ab_re - 1.0
    cf_re, cf_im = (nr * a_re + ab_im * a_im) / den, (ab_im * a_re - nr * a_im) / den
    bp_re = cf_re[..., None] * b_re[None] - cf_im[..., None] * b_im[None]
    bp_im = cf_re[..., None] * b_im[None] + cf_im[..., None] * b_re[None]
    pb_re = pw_re[..., None] * bp_re[None] - pw_im[..., None] * bp_im[None]
    pb_im = pw_re[..., None] * bp_im[None] + pw_im[..., None] * bp_re[None]
    kk = (jnp.einsum('gqn,tdgnp->tdgqp', c_re, pb_re[:Lc]) - jnp.einsum('gqn,tdgnp->tdgqp', c_im, pb_im[:Lc]))
    jj = np.arange(Lc)[:, None]
    ii = np.arange(Lc)[None, :]
    dist = np.abs(ii - jj)
    kf = kk[dist, 0] * jnp.asarray(ii >= jj, F32)[..., None, None, None]
    kb = kk[dist, 1] * jnp.asarray(ii <= jj, F32)[..., None, None, None]
    tz = jnp.transpose(kf + kb, (2, 0, 4, 1, 3))
    jr = np.arange(Lc)
    emap = lambda pbx, order, d: jnp.transpose(pbx[order, d], (1, 0, 3, 2))
    em = jnp.stack([emap(pb_re, Lc - 1 - jr, 0), emap(pb_im, Lc - 1 - jr, 0),
                    emap(pb_re, jr, 1), emap(pb_im, jr, 1)], axis=3)

    def gmap(order, d):
        pr, pi = pw_re[order, d], pw_im[order, d]
        cp_re = c_re[None] * pr[:, :, None, :] - c_im[None] * pi[:, :, None, :]
        cp_im = c_re[None] * pi[:, :, None, :] + c_im[None] * pr[:, :, None, :]
        to = lambda t: jnp.transpose(t, (1, 3, 0, 2))
        return to(cp_re), to(-cp_im)

    gk = jnp.stack(gmap(jr + 1, 0) + gmap(Lc - jr, 1), axis=1)
    NO, NQ, NG = S5_OCTETS, 2, 4
    tz, em, gk = tz.astype(BF16), em.astype(BF16), gk.astype(BF16)

    def bdiag(blocks):
        n, c = blocks.shape[1], blocks.shape[3]
        return jnp.concatenate(
            [jnp.pad(blocks[:, g], ((0, 0), (0, 0), (g * c, (n - 1 - g) * c))) for g in range(n)], axis=1)

    def perm_rows(m, dims, order):
        no, r, c = m.shape
        m = jnp.transpose(m.reshape((no,) + dims + (c,)), (0,) + tuple(1 + o for o in order) + (len(dims) + 1,))
        return m.reshape(no, r, c)

    tr = lambda m: jnp.swapaxes(m, 1, 2)
    gjp, rgkn = (8, Lc, P), (NQ, NG, 4, N)
    wy = perm_rows(bdiag(tz.reshape(NO, 8, Lc * P, Lc * P)), gjp, (1, 0, 2))
    wy = tr(perm_rows(tr(wy), gjp, (1, 0, 2)))
    we = perm_rows(bdiag(em.reshape(NO, 8, Lc * P, 4 * N)), gjp, (1, 0, 2))
    we = tr(perm_rows(tr(we), rgkn, (0, 2, 1, 3)))
    wz = jnp.concatenate([wy, we], axis=-1)
    gm = perm_rows(bdiag(gk.reshape(NO, 8, 4 * N, Lc * P)), rgkn, (0, 2, 1, 3))
    gm = tr(perm_rows(tr(gm), gjp, (1, 0, 2)))
    apow = jnp.stack([pw_re[Lc, 0], pw_im[Lc, 0], pw_re[Lc, 1], pw_im[Lc, 1]], axis=1)
    apow = jnp.transpose(apow.reshape(G // NG, NG, 4, N), (0, 2, 1, 3)).reshape(G // NG, 4, NG * N)
    return wz, gm, apow


def _s5_local_kernel(u_ref, wz_ref, y_ref, e_ref):
    Lc = u_ref.shape[1]
    lhs = jnp.concatenate([u_ref[:, j, :] for j in range(Lc)], axis=-1).astype(BF16)
    z = _dot(lhs, wz_ref[...])
    for i in range(Lc):
        y_ref[:, i, :] = z[:, i * LANE:(i + 1) * LANE]
    e_ref[...] = z[:, Lc * LANE:]


def _s5_local_call(proj3, wz):
    NR, Lc, _ = proj3.shape
    NO, K, N = wz.shape
    NE = N - Lc * LANE
    tr = NR // 8
    cu = COL_DU // LANE
    return pl.pallas_call(
        _s5_local_kernel,
        out_shape=(jax.ShapeDtypeStruct((NR, Lc, GROUP_W), F32), jax.ShapeDtypeStruct((NR, NO * NE), F32)),
        grid=(NO, NR // tr),
        in_specs=[
            pl.BlockSpec((tr, Lc, LANE), lambda o, i: (i, 0, cu + o)),
            pl.BlockSpec((None, K, N), lambda o, i: (o, 0, 0)),
        ],
        out_specs=(pl.BlockSpec((tr, Lc, LANE), lambda o, i: (i, 0, o)),
                   pl.BlockSpec((tr, NE), lambda o, i: (i, o))),
        compiler_params=_cparams(("arbitrary", "arbitrary")),
        name="mixer_d_local",
    )(proj3, wz)


def _s5_scan_kernel(e_ref, ap_ref, x_ref, *, n_batch, nlc, ncc):
    NS = e_ref.shape[1] // 4
    af_re, af_im = ap_ref[0:1, :], ap_ref[1:2, :]
    ab_re, ab_im = ap_ref[2:3, :], ap_ref[3:4, :]

    def step(a_re, a_im, x_re, x_im, e):
        return a_re * x_re - a_im * x_im + e[:, :NS], a_re * x_im + a_im * x_re + e[:, NS:]

    def phase(row0, n_chunks, stride, carry):
        n_tiles = n_chunks // 8

        def body(t, carry):
            out = []
            for b in range(n_batch):
                xf_re, xf_im, xb_re, xb_im = carry[4 * b:4 * b + 4]
                rf = pl.ds(pl.multiple_of(row0 + b * stride + t * 8, 8), 8)
                rb = pl.ds(pl.multiple_of(row0 + b * stride + (n_tiles - 1 - t) * 8, 8), 8)
                ef = e_ref[rf, 0:2 * NS]
                eb = e_ref[rb, 2 * NS:4 * NS]
                xf_in, xb_in = [], [None] * 8
                for j in range(8):
                    xf_in.append(jnp.concatenate([xf_re, xf_im], axis=-1))
                    xf_re, xf_im = step(af_re, af_im, xf_re, xf_im, ef[j:j + 1, :])
                for j in reversed(range(8)):
                    xb_in[j] = jnp.concatenate([xb_re, xb_im], axis=-1)
                    xb_re, xb_im = step(ab_re, ab_im, xb_re, xb_im, eb[j:j + 1, :])
                x_ref[rf, 0:2 * NS] = jnp.concatenate(xf_in, axis=0)
                x_ref[rb, 2 * NS:4 * NS] = jnp.concatenate(xb_in, axis=0)
                out += [xf_re, xf_im, xb_re, xb_im]
            return tuple(out)

        return lax.fori_loop(0, n_tiles, body, carry)

    zero = jnp.zeros((1, NS), F32)
    carry = phase(n_batch * nlc, ncc, ncc, (zero,) * (4 * n_batch))
    phase(0, nlc, nlc, carry)


def _s5_scan_call(e, apow, n_batch, seq, ctx):
    NR, NEall = e.shape
    NQ8, _, NS = apow.shape
    return pl.pallas_call(
        functools.partial(_s5_scan_kernel, n_batch=n_batch, nlc=seq // S5_CHUNK, ncc=ctx // S5_CHUNK),
        out_shape=jax.ShapeDtypeStruct((NR, NEall), F32),
        grid=(NQ8,),
        in_specs=[
            pl.BlockSpec((NR, 4 * NS), lambda q: (0, q)),
            pl.BlockSpec((None, 4, NS), lambda q: (q, 0, 0)),
        ],
        out_specs=pl.BlockSpec((NR, 4 * NS), lambda q: (0, q)),
        compiler_params=_cparams(("arbitrary",)),
        name="mixer_d_scan",
    )(e, apow)


def _s5_carry_kernel(x_ref, gm_ref, yl_ref, y_ref):
    Lc = yl_ref.shape[1]
    y = _dot(x_ref[...].astype(BF16), gm_ref[...])
    for i in range(Lc):
        y_ref[:, i, :] = yl_ref[:, i, :] + y[:, i * LANE:(i + 1) * LANE]


def _s5_carry_call(xin, gm, yloc):
    NR, Lc, _ = yloc.shape
    NO, NE, N = gm.shape
    tr = NR // 8
    return pl.pallas_call(
        _s5_carry_kernel,
        out_shape=jax.ShapeDtypeStruct((NR, Lc, GROUP_W), F32),
        grid=(NO, NR // tr),
        in_specs=[
            pl.BlockSpec((tr, NE), lambda o, i: (i, o)),
            pl.BlockSpec((None, NE, N), lambda o, i: (o, 0, 0)),
            pl.BlockSpec((tr, Lc, LANE), lambda o, i: (i, 0, o)),
        ],
        out_specs=pl.BlockSpec((tr, Lc, LANE), lambda o, i: (i, 0, o)),
        compiler_params=_cparams(("arbitrary", "arbitrary")),
        name="mixer_d_carry",
    )(xin, gm, yloc)


def _s5_out_kernel(y_ref, u_ref, d_ref, w_ref, b_ref, o_ref):
    y = y_ref[...] + d_ref[...] * u_ref[...]
    c = math.sqrt(2.0 / math.pi)
    y = 0.5 * y * (1.0 + jnp.tanh(c * (y + 0.044715 * (y * y * y))))
    z = _dot(y.astype(BF16), w_ref[...]) + b_ref[...]
    o_ref[...] = (y * _sigmoid(z)).astype(o_ref.dtype)


def _s5_out_call(y_tok, proj, d_skip, glu_w, glu_b, tm):
    R = proj.shape[0]
    vec = pl.BlockSpec((1, GROUP_W), lambda i: (0, 0))
    return pl.pallas_call(
        _s5_out_kernel,
        out_shape=jax.ShapeDtypeStruct((R, GROUP_W), BF16),
        grid=(R // tm,),
        in_specs=[
            pl.BlockSpec((tm, GROUP_W), lambda i: (i, 0)),
            pl.BlockSpec((tm, GROUP_W), lambda i: (i, COL_DU // GROUP_W)),
            vec,
            pl.BlockSpec((GROUP_W, GROUP_W), lambda i: (0, 0)),
            vec,
        ],
        out_specs=pl.BlockSpec((tm, GROUP_W), lambda i: (i, 0)),
        compiler_params=_cparams(("arbitrary",)),
        name="mixer_d_out",
    )(y_tok, proj, d_skip, glu_w, glu_b)


def _rope_tables(n_tokens):
    t = jnp.arange(n_tokens)
    nf = HEAD_DIM // 4
    inv = 1.0 / (ROPE_BASE ** (jnp.arange(nf, dtype=F32) / nf))

    def ang(pp):
        a = pp.astype(F32)[:, None] * inv[None, :]
        return jnp.concatenate([a, a], -1)

    a = jnp.concatenate([ang(t // GRID_W), ang(t % GRID_W)], -1)
    cos, sin = jnp.cos(a), jnp.sin(a)
    sign = np.where((np.arange(HEAD_DIM) % 32) < 16, -1.0, 1.0).astype(np.float32)
    cos2 = jnp.concatenate([cos, cos], -1)
    sin2 = jnp.concatenate([sin * sign, sin * sign], -1)
    return cos2, sin2


def _block_ones(n):
    return jnp.asarray(np.kron(np.eye(n // HEAD_DIM), np.ones((HEAD_DIM, HEAD_DIM))), BF16)


def _permute_w_in(w_in):
    cuts = np.cumsum([512, 512, 512, 512, 512, 512, 32, 32, 96, 512, 128, 128, 512])
    seg = lambda i: w_in[..., (0 if i == 0 else cuts[i - 1]):cuts[i]]
    L, D = w_in.shape[:2]
    pad = jnp.zeros((L, D, 96), w_in.dtype)
    parts = [seg(0), seg(1), seg(2), seg(3), seg(4), seg(5), seg(9), seg(12),
             seg(6), seg(7), seg(8), pad, seg(10), seg(11)]
    return jnp.concatenate(parts, axis=-1).astype(BF16)


def _forward(x, c, ctx, c_ctx, w_ada, b_ada, norm_g, ffn1_wg, ffn1_wu, ffn1_wd, ffn2_wg, ffn2_wu, ffn2_wd,
             w_in, w_out, na_q_g, na_k_g, na_rpb, rw_conv, rw_w0, rw_w_up, rw_a0, rw_a_up, rw_g_up,
             rw_k_k, rw_k_a, rw_r_k, rw_gn_w, rw_gn_b, wa_q_g, wa_k_g, wa_sink, s5_a_re, s5_a_im,
             s5_log_dt, s5_b_re, s5_b_im, s5_c_re, s5_c_im, s5_d, s5_glu_w, s5_glu_b, *, tm, tf):
    B, SEQ, D = x.shape
    CTX = ctx.shape[1]
    L = w_ada.shape[0]
    n_lat = B * SEQ
    n_qb = SEQ // QBLK

    bf = lambda t: t.astype(BF16)
    f1g, f1u, f1d, f2g, f2u, f2d = map(bf, (ffn1_wg, ffn1_wu, ffn1_wd, ffn2_wg, ffn2_wu, ffn2_wd))
    w_in_p = _permute_w_in(w_in)
    w_out_b = bf(w_out)
    glu_w_b = bf(s5_glu_w)
    ones128, ones512 = _block_ones(LANE), _block_ones(GROUP_W)
    cos_t, sin_t = _rope_tables(SEQ)
    na_bias = [_na_bias_table(na_rpb[l], n_qb) for l in range(L)]
    zr = lambda r, cdim: jnp.zeros((L, r, cdim), F32)
    wlr = jnp.concatenate([
        jnp.concatenate([rw_w_up[:, 0], rw_w_up[:, 1], zr(32, 3 * GROUP_W)], axis=-1),
        jnp.concatenate([zr(32, 2 * GROUP_W), rw_a_up[:, 0], rw_a_up[:, 1], zr(32, GROUP_W)], axis=-1),
        jnp.concatenate([zr(96, 4 * GROUP_W), rw_g_up], axis=-1),
        zr(96, 5 * GROUP_W)], axis=1).astype(BF16)
    s5_tabs = jax.vmap(_s5_tables)(s5_a_re, s5_a_im, s5_log_dt, s5_b_re, s5_b_im, s5_c_re, s5_c_im)
    tile2 = lambda t: jnp.concatenate([t, t], axis=-1)

    cc = jnp.concatenate([c, c_ctx[None], jnp.zeros((8 - B - 1, D), F32)], axis=0)
    mods_all = _ada_call(cc, w_ada, b_ada)[:, :B + 1].reshape(L, B + 1, N_MOD, D)

    xs = jnp.concatenate([x.reshape(n_lat, D), ctx.reshape(B * CTX, D)], axis=0)
    R = xs.shape[0]
    for l in range(L):
        want_ctx = l < L - 1
        mods = mods_all[l]
        ng = norm_g[l]
        xs = _ffn_call(xs, mods, ng[0:1], f1g, f1u, f1d, l, 0, R, SEQ, B, tm, tf)
        proj = _win_call(xs, mods, ng[1:2], w_in_p, l, SEQ, B, tm, D_IN_PAD // 3)
        o_a = _na_call(proj, na_bias[l], tile2(na_q_g[l][None]), tile2(na_k_g[l][None]), ones128,
                       B, SEQ, CTX, want_ctx)
        prep = _rw_prep_call(proj, rw_conv[l], wlr[l], rw_w0[l], rw_a0[l], rw_k_k[l][None], rw_k_a[l][None],
                             ones512, B, SEQ, CTX, min(tm, 256))
        vec = lambda t: t.reshape(1, GROUP_W)
        yf, yb = _rw_scan_call(prep, B, SEQ, CTX)
        o_b = _rw_readout_call(prep, yf, yb, vec(rw_r_k[l]), vec(rw_gn_w[l]), vec(rw_gn_b[l]), ones512, tm)
        o_c = _wa_call(proj, wa_sink[l], cos_t, sin_t, tile2(wa_q_g[l][None]), tile2(wa_k_g[l][None]), ones128,
                       B, SEQ, CTX, want_ctx)
        wz, gm, apow = (t[l] for t in s5_tabs)
        yloc, e_loc = _s5_local_call(proj.reshape(R // S5_CHUNK, S5_CHUNK, D_IN_PAD), bf(wz))
        xin = _s5_scan_call(e_loc, apow, B, SEQ, CTX)
        y_tok = _s5_carry_call(xin, bf(gm), yloc).reshape(R, GROUP_W)
        o_d = _s5_out_call(y_tok, proj, vec(s5_d[l]), glu_w_b[l], vec(s5_glu_b[l]), tm)
        n_rows = R if want_ctx else n_lat
        xs = _wout_call(xs, (o_a, o_b, o_c, o_d), mods, w_out_b, l, n_rows, SEQ, B, tm)
        xs = _ffn_call(xs, mods, ng[2:3], f2g, f2u, f2d, l, 6, n_rows, SEQ, B, tm, tf)
    return xs[:n_lat].reshape(B, SEQ, D)


def kernel(x, c, ctx, c_ctx, w_ada, b_ada, norm_g, ffn1_wg, ffn1_wu, ffn1_wd, ffn2_wg, ffn2_wu, ffn2_wd, w_in, w_out, na_q_g, na_k_g, na_rpb, rw_conv, rw_w0, rw_w_up, rw_a0, rw_a_up, rw_g_up, rw_k_k, rw_k_a, rw_r_k, rw_gn_w, rw_gn_b, wa_q_g, wa_k_g, wa_sink, s5_a_re, s5_a_im, s5_log_dt, s5_b_re, s5_b_im, s5_c_re, s5_c_im, s5_d, s5_glu_w, s5_glu_b):
    return _forward(x, c, ctx, c_ctx, w_ada, b_ada, norm_g, ffn1_wg, ffn1_wu, ffn1_wd, ffn2_wg, ffn2_wu, ffn2_wd,
                    w_in, w_out, na_q_g, na_k_g, na_rpb, rw_conv, rw_w0, rw_w_up, rw_a0, rw_a_up, rw_g_up,
                    rw_k_k, rw_k_a, rw_r_k, rw_gn_w, rw_gn_b, wa_q_g, wa_k_g, wa_sink, s5_a_re, s5_a_im,
                    s5_log_dt, s5_b_re, s5_b_im, s5_c_re, s5_c_im, s5_d, s5_glu_w, s5_glu_b, tm=512, tf=512)
```

```python
import functools
import math

import numpy as np
import jax
import jax.numpy as jnp
from jax import lax
from jax.experimental import pallas as pl
from jax.experimental.pallas import tpu as pltpu

F32 = jnp.float32
BF16 = jnp.bfloat16

D_MODEL = 2048
GRID_W = 64
HEAD_DIM = 64
GROUP_W = D_MODEL // 4
N_MOD = 9
NORM_EPS = 1e-6
ROPE_BASE = 10000.0
NEG_INF = -1e30

NA_HEADS = GROUP_W // HEAD_DIM
NA_WIN_R = 8
NA_WIN_C = 16
RW_HEADS = GROUP_W // HEAD_DIM
RW_DECAY_RANK = 32
RW_ICLR_RANK = 32
RW_GATE_RANK = 96
RW_GN_EPS = 64e-5
WA_HEADS = GROUP_W // HEAD_DIM
WA_KV_HEADS = 2
WA_GROUP = WA_HEADS // WA_KV_HEADS
WA_WINDOW = 128
S5_P = 16
S5_GROUPS = GROUP_W // S5_P
S5_N = 64

LANE = 128
QBLK = 2 * GRID_W
NA_KBLKS = 5
RW_CHUNK = 64
S5_CHUNK = 8
S5_OCTETS = 4
VMEM_LIMIT = 56 * 1024 * 1024

COL_AQ, COL_AK, COL_AV = 0, 512, 1024
COL_BR = 1536
COL_CQ = 3072
COL_DU = 3584
COL_BLR = 4096
COL_CK, COL_CV = 4352, 4480
D_IN_PAD = 4608

PB_R, PB_V, PB_KK, PB_G = 0, 1, 2, 9
PB_N = 10


def _cparams(sem):
    return pltpu.CompilerParams(dimension_semantics=sem, vmem_limit_bytes=VMEM_LIMIT)


def _dot(a, b):
    return jnp.dot(a, b, preferred_element_type=F32)


def _dot_nt(a, b):
    return lax.dot_general(a, b, (((1,), (1,)), ((), ())), preferred_element_type=F32)


def _dot_tn(a, b):
    return lax.dot_general(a, b, (((0,), (0,)), ((), ())), preferred_element_type=F32)


def _sigmoid(x):
    return 1.0 / (1.0 + jnp.exp(-x))


def _modulate(x, g, shift, scale):
    ms = jnp.mean(x * x, axis=-1, keepdims=True)
    return (x * lax.rsqrt(ms + NORM_EPS) * g) * (1.0 + scale) + shift


def _seg_sum(x, ones_bd):
    hi = x.astype(BF16)
    lo = (x - hi.astype(F32)).astype(BF16)
    return _dot(hi, ones_bd) + _dot(lo, ones_bd)


def _with_ones(v):
    one = jnp.ones((v.shape[0], HEAD_DIM), F32)
    return jnp.concatenate([v[:, :HEAD_DIM], one, v[:, HEAD_DIM:], one], axis=-1).astype(BF16)


def _prob(s, m):
    return jnp.exp((s - m).astype(BF16))


def _head_rmsnorm(x, ones_bd):
    ms = _seg_sum(x * x, ones_bd) * (1.0 / HEAD_DIM)
    return x * lax.rsqrt(ms + NORM_EPS)


def _ada_kernel(c_ref, w_ref, b_ref, o_ref):
    c = c_ref[...]
    s = (c * _sigmoid(c)).astype(BF16)
    o_ref[...] = _dot(s, w_ref[...].astype(BF16)) + b_ref[...]


def _ada_call(cc, w_ada, b_ada):
    L, D, N = w_ada.shape
    tn = 1024
    return pl.pallas_call(
        _ada_kernel,
        out_shape=jax.ShapeDtypeStruct((L, 8, N), F32),
        grid=(L, N // tn),
        in_specs=[
            pl.BlockSpec((8, D), lambda l, j: (0, 0)),
            pl.BlockSpec((None, D, tn), lambda l, j: (l, 0, j)),
            pl.BlockSpec((None, 1, tn), lambda l, j: (l, 0, j)),
        ],
        out_specs=pl.BlockSpec((None, 8, tn), lambda l, j: (l, 0, j)),
        compiler_params=_cparams(("arbitrary", "arbitrary")),
        name="adaln",
    )(cc, w_ada, b_ada.reshape(L, 1, N))


def _ffn_kernel(x_ref, mod_ref, g_ref, wg_ref, wu_ref, wd_ref, o_ref, h_sc, acc_sc, *, mi):
    j = pl.program_id(1)

    @pl.when(j == 0)
    def _():
        h = _modulate(x_ref[...], g_ref[...], mod_ref[mi:mi + 1, :], mod_ref[mi + 1:mi + 2, :])
        h_sc[...] = h.astype(BF16)
        acc_sc[...] = jnp.zeros_like(acc_sc)

    h = h_sc[...]
    gate = _dot(h, wg_ref[...])
    up = _dot(h, wu_ref[...])
    a = (gate * _sigmoid(gate) * up).astype(BF16)
    acc_sc[...] += _dot(a, wd_ref[...])

    @pl.when(j == pl.num_programs(1) - 1)
    def _():
        o_ref[...] = x_ref[...] + 0.5 * mod_ref[mi + 2:mi + 3, :] * acc_sc[...]


def _ffn_call(xs, mods, g, wg, wu, wd, l, mi, n_rows, rows_per_seq, n_batch, tm, tf):
    D = xs.shape[1]
    F = wg.shape[2]
    tps = rows_per_seq // tm
    return pl.pallas_call(
        functools.partial(_ffn_kernel, mi=mi),
        out_shape=jax.ShapeDtypeStruct((n_rows, D), F32),
        grid=(n_rows // tm, F // tf),
        in_specs=[
            pl.BlockSpec((tm, D), lambda i, j: (i, 0)),
            pl.BlockSpec((None, N_MOD, D), lambda i, j: (jnp.minimum(i // tps, n_batch), 0, 0)),
            pl.BlockSpec((1, D), lambda i, j: (0, 0)),
            pl.BlockSpec((None, D, tf), lambda i, j: (l, 0, j)),
            pl.BlockSpec((None, D, tf), lambda i, j: (l, 0, j)),
            pl.BlockSpec((None, tf, D), lambda i, j: (l, j, 0)),
        ],
        out_specs=pl.BlockSpec((tm, D), lambda i, j: (i, 0)),
        scratch_shapes=[pltpu.VMEM((tm, D), BF16), pltpu.VMEM((tm, D), F32)],
        compiler_params=_cparams(("arbitrary", "arbitrary")),
        name="ffn",
    )(xs, mods, g, wg, wu, wd)


def _win_kernel(x_ref, mod_ref, g_ref, w_ref, o_ref, h_sc):
    @pl.when(pl.program_id(1) == 0)
    def _():
        h = _modulate(x_ref[...], g_ref[...], mod_ref[3:4, :], mod_ref[4:5, :])
        h_sc[...] = h.astype(BF16)

    o_ref[...] = _dot(h_sc[...], w_ref[...])


def _win_call(xs, mods, g, w_in, l, rows_per_seq, n_batch, tm, tn):
    R, D = xs.shape
    N = w_in.shape[2]
    tps = rows_per_seq // tm
    return pl.pallas_call(
        _win_kernel,
        out_shape=jax.ShapeDtypeStruct((R, N), F32),
        grid=(R // tm, N // tn),
        in_specs=[
            pl.BlockSpec((tm, D), lambda i, j: (i, 0)),
            pl.BlockSpec((None, N_MOD, D), lambda i, j: (jnp.minimum(i // tps, n_batch), 0, 0)),
            pl.BlockSpec((1, D), lambda i, j: (0, 0)),
            pl.BlockSpec((None, D, tn), lambda i, j: (l, 0, j)),
        ],
        out_specs=pl.BlockSpec((tm, tn), lambda i, j: (i, j)),
        scratch_shapes=[pltpu.VMEM((tm, D), BF16)],
        compiler_params=_cparams(("arbitrary", "arbitrary")),
        name="in_proj",
    )(xs, mods, g, w_in)


def _wout_kernel(x_ref, oa_ref, ob_ref, oc_ref, od_ref, mod_ref, w_ref, o_ref):
    acc = _dot(oa_ref[...], w_ref[0 * GROUP_W:1 * GROUP_W, :])
    acc += _dot(ob_ref[...], w_ref[1 * GROUP_W:2 * GROUP_W, :])
    acc += _dot(oc_ref[...], w_ref[2 * GROUP_W:3 * GROUP_W, :])
    acc += _dot(od_ref[...], w_ref[3 * GROUP_W:4 * GROUP_W, :])
    o_ref[...] = x_ref[...] + mod_ref[5:6, :] * acc


def _wout_call(xs, outs, mods, w_out, l, n_rows, rows_per_seq, n_batch, tm):
    D = xs.shape[1]
    tps = rows_per_seq // tm
    ospec = pl.BlockSpec((tm, GROUP_W), lambda i: (i, 0))
    return pl.pallas_call(
        _wout_kernel,
        out_shape=jax.ShapeDtypeStruct((n_rows, D), F32),
        grid=(n_rows // tm,),
        in_specs=[
            pl.BlockSpec((tm, D), lambda i: (i, 0)),
            ospec, ospec, ospec, ospec,
            pl.BlockSpec((None, N_MOD, D), lambda i: (jnp.minimum(i // tps, n_batch), 0, 0)),
            pl.BlockSpec((None, D, D), lambda i: (l, 0, 0)),
        ],
        out_specs=pl.BlockSpec((tm, D), lambda i: (i, 0)),
        compiler_params=_cparams(("arbitrary",)),
        name="out_proj",
    )(xs, *outs, mods, w_out)


def _na_case_reps(n_qb):
    return (0, 1, 2, n_qb - 2, n_qb - 1)


def _na_start(p, n_qb):
    return jnp.clip(p - 2, 0, n_qb - NA_KBLKS)


def _na_bias_table(rpb, n_qb):
    H = rpb.shape[0]
    rows = 2 * n_qb
    qc = np.arange(GRID_W)[:, None]
    kc = np.arange(GRID_W)[None, :]
    cs = np.clip(qc - NA_WIN_C // 2, 0, GRID_W - NA_WIN_C)
    okc = (kc >= cs) & (kc < cs + NA_WIN_C)
    dc = np.where(okc, kc - qc + NA_WIN_C - 1, 0)
    rp = rpb.astype(F32).reshape(H, 2 * NA_WIN_R - 1, 2 * NA_WIN_C - 1)
    blocks = jnp.where(jnp.asarray(okc)[None, None], jnp.take(rp, jnp.asarray(dc), axis=2), NEG_INF)
    blocks = blocks.astype(BF16)
    neg = jnp.full((H, GRID_W, GRID_W), NEG_INF, BF16)
    cases = []
    for p in _na_case_reps(n_qb):
        start = min(max(p - 2, 0), n_qb - NA_KBLKS)
        qrows = []
        for qr in range(2):
            qa = 2 * p + qr
            rs = min(max(qa - NA_WIN_R // 2, 0), rows - NA_WIN_R)
            krow = []
            for kr in range(2 * NA_KBLKS):
                ka = 2 * start + kr
                krow.append(blocks[:, ka - qa + NA_WIN_R - 1] if rs <= ka < rs + NA_WIN_R else neg)
            qrows.append(jnp.concatenate(krow, axis=-1))
        cases.append(jnp.concatenate(qrows, axis=-2))
    return jnp.stack(cases, axis=0)


def _na_kernel(q_ref, k_ref, v_ref, kc_ref, vc_ref, bias_ref, qg_ref, kg_ref, ones_ref, o_ref,
               kn_sc, vn_sc, kcn_sc, vcn_sc, *, n_qb, seq, ctx):
    p = pl.program_id(2)
    ones_bd = ones_ref[...]
    scale = HEAD_DIM ** -0.5

    @pl.when(p == 0)
    def _():
        kg = kg_ref[...]
        rows = 512

        def body(i, carry):
            sl = pl.ds(pl.multiple_of(i * rows, rows), rows)
            kn_sc[sl, :] = (_head_rmsnorm(k_ref[sl, :], ones_bd) * kg).astype(BF16)
            vn_sc[sl, :] = _with_ones(v_ref[sl, :])
            return carry

        lax.fori_loop(0, seq // rows, body, 0)
        kcn_sc[...] = (_head_rmsnorm(kc_ref[...], ones_bd) * kg).astype(BF16)
        vcn_sc[...] = _with_ones(vc_ref[...])

    nq = q_ref.shape[0] // QBLK
    n_lat = n_qb // nq
    q = (_head_rmsnorm(q_ref[...], ones_bd) * (qg_ref[...] * scale)).astype(BF16)
    kcn = kcn_sc[...]
    vcn = vcn_sc[...]
    chains = [(qb, h) for qb in range(nq) for h in range(2)]
    hsl = lambda h: slice(h * HEAD_DIM, (h + 1) * HEAD_DIM)
    qh = [q[qb * QBLK:(qb + 1) * QBLK, hsl(h)] for qb, h in chains]
    s_c = [_dot_nt(qh[c], kcn[:, hsl(h)]) for c, (qb, h) in enumerate(chains)]
    m_c = [jnp.max(t, axis=-1, keepdims=True) for t in s_c]

    vsl = lambda h: slice(h * LANE, (h + 1) * LANE)
    vca = [vcn[:, vsl(h)] for h in range(2)]

    def finish(oa):
        rows = [jnp.concatenate([oa[qb * 2 + h][:, :HEAD_DIM] / oa[qb * 2 + h][:, HEAD_DIM:] for h in range(2)],
                                axis=-1) for qb in range(nq)]
        o_ref[...] = jnp.concatenate(rows, axis=0).astype(o_ref.dtype)

    @pl.when(p < n_lat)
    def _():
        starts = [_na_start(p * nq + qb, n_qb) for qb in range(nq)]
        cases = [p * nq + qb - starts[qb] for qb in range(nq)]
        sls = [pl.ds(pl.multiple_of(st * QBLK, QBLK), NA_KBLKS * QBLK) for st in starts]
        kw = [kn_sc[sl, :] for sl in sls]
        vw = [vn_sc[sl, :] for sl in sls]
        s_n = [_dot_nt(qh[c], kw[qb][:, hsl(h)]) + bias_ref[cases[qb], h].astype(F32)
               for c, (qb, h) in enumerate(chains)]
        m = [jnp.maximum(jnp.max(s_n[c], axis=-1, keepdims=True), m_c[c]) for c in range(len(chains))]
        p_n = [_prob(s_n[c], m[c]) for c in range(len(chains))]
        p_c = [_prob(s_c[c], m[c]) for c in range(len(chains))]
        finish([_dot(p_n[c], vw[qb][:, vsl(h)]) + _dot(p_c[c], vca[h])
                for c, (qb, h) in enumerate(chains)])

    @pl.when(p >= n_lat)
    def _():
        finish([_dot(_prob(s_c[c], m_c[c]), vca[h]) for c, (qb, h) in enumerate(chains)])


def _na_call(proj, bias, qg, kg, ones_bd, n_batch, seq, ctx, want_ctx):
    R = proj.shape[0]
    nq = 2
    n_qb = seq // QBLK
    n_lat, n_ctx = n_qb // nq, ctx // (nq * QBLK)
    steps = n_lat + (n_ctx if want_ctx else 0)
    ctx_kb = n_batch * seq // ctx

    def qrow(b, p):
        return jnp.where(p < n_lat, b * n_lat + p, n_batch * n_lat + b * n_ctx + (p - n_lat))

    cq, ck, cv = COL_AQ // LANE, COL_AK // LANE, COL_AV // LANE
    return pl.pallas_call(
        functools.partial(_na_kernel, n_qb=n_qb, seq=seq, ctx=ctx),
        out_shape=jax.ShapeDtypeStruct((R, GROUP_W), BF16),
        grid=(n_batch, NA_HEADS // 2, steps),
        in_specs=[
            pl.BlockSpec((nq * QBLK, LANE), lambda b, hp, p: (qrow(b, p), cq + hp)),
            pl.BlockSpec((seq, LANE), lambda b, hp, p: (b, ck + hp)),
            pl.BlockSpec((seq, LANE), lambda b, hp, p: (b, cv + hp)),
            pl.BlockSpec((ctx, LANE), lambda b, hp, p: (ctx_kb + b, ck + hp)),
            pl.BlockSpec((ctx, LANE), lambda b, hp, p: (ctx_kb + b, cv + hp)),
            pl.BlockSpec((5, 2, QBLK, NA_KBLKS * QBLK), lambda b, hp, p: (0, hp, 0, 0)),
            pl.BlockSpec((1, LANE), lambda b, hp, p: (0, 0)),
            pl.BlockSpec((1, LANE), lambda b, hp, p: (0, 0)),
            pl.BlockSpec((LANE, LANE), lambda b, hp, p: (0, 0)),
        ],
        out_specs=pl.BlockSpec((nq * QBLK, LANE), lambda b, hp, p: (qrow(b, p), hp)),
        scratch_shapes=[pltpu.VMEM((seq, LANE), BF16), pltpu.VMEM((seq, 2 * LANE), BF16),
                        pltpu.VMEM((ctx, LANE), BF16), pltpu.VMEM((ctx, 2 * LANE), BF16)],
        compiler_params=_cparams(("arbitrary", "arbitrary", "arbitrary")),
        name="mixer_a",
    )(proj, proj, proj, proj, proj, bias, qg, kg, ones_bd)


def _rope(x, cos, sin_signed, first_half):
    rot = jnp.where(first_half, pltpu.roll(x, LANE - 16, 1), pltpu.roll(x, 16, 1))
    return x * cos + rot * sin_signed


def _wa_kernel(sink_ref, q_ref, k_ref, v_ref, kc_ref, vc_ref, cosk_ref, sin_k_ref, cosq_ref, sinq_ref,
               qg_ref, kg_ref, ones_ref, wmask_ref, o_ref, kn_sc, vn_sc, kcn_sc, vcn_sc, *, n_qb, seq, ctx):
    p = pl.program_id(1)
    ones_bd = ones_ref[...]
    scale = HEAD_DIM ** -0.5
    lane = lax.broadcasted_iota(jnp.int32, (1, LANE), 1)
    first_half = (lane % 32) < 16

    @pl.when(p == 0)
    def _():
        kg = kg_ref[...]
        rows = 512

        def body(i, carry):
            sl = pl.ds(pl.multiple_of(i * rows, rows), rows)
            kn = _head_rmsnorm(k_ref[sl, :], ones_bd) * kg
            kn_sc[sl, :] = _rope(kn, cosk_ref[sl, :], sin_k_ref[sl, :], first_half).astype(BF16)
            vn_sc[sl, :] = _with_ones(v_ref[sl, :])
            return carry

        lax.fori_loop(0, seq // rows, body, 0)
        kcn_sc[...] = (_head_rmsnorm(kc_ref[...], ones_bd) * kg).astype(BF16)
        vcn_sc[...] = _with_ones(vc_ref[...])

    nq = q_ref.shape[0] // QBLK
    n_lat = n_qb // nq
    qg = qg_ref[...] * scale
    is_lat = p < n_lat
    cosq = jnp.where(is_lat, cosq_ref[...], 1.0)
    sinq = jnp.where(is_lat, sinq_ref[...], 0.0)
    qh = []
    for c in range(4):
        qn = _head_rmsnorm(q_ref[:, c * LANE:(c + 1) * LANE], ones_bd) * qg
        qn = _rope(qn, cosq, sinq, first_half).astype(BF16)
        qh += [qn[:, :HEAD_DIM], qn[:, HEAD_DIM:]]
    kcn = kcn_sc[...]
    vcn = vcn_sc[...]
    chains = [(qb, kh) for qb in range(nq) for kh in range(WA_KV_HEADS)]
    nch = range(len(chains))
    hsl = lambda kh: slice(kh * HEAD_DIM, (kh + 1) * HEAD_DIM)
    qs = [jnp.concatenate([qh[kh * WA_GROUP + g][qb * QBLK:(qb + 1) * QBLK, :] for g in range(WA_GROUP)], axis=0)
          for qb, kh in chains]
    sk = [jnp.concatenate([jnp.full((QBLK, 1), sink_ref[kh * WA_GROUP + g], F32) for g in range(WA_GROUP)],
                          axis=0) for qb, kh in chains]
    s_c = [_dot_nt(qs[c], kcn[:, hsl(chains[c][1])]) for c in nch]
    m_c = [jnp.maximum(jnp.max(s_c[c], axis=-1, keepdims=True), sk[c]) for c in nch]

    vsl = lambda kh: slice(kh * LANE, (kh + 1) * LANE)
    vca = [vcn[:, vsl(kh)] for kh in range(WA_KV_HEADS)]

    def finish(oa, sink_w):
        rows = []
        for qb in range(nq):
            heads = []
            for kh in range(WA_KV_HEADS):
                c = qb * WA_KV_HEADS + kh
                oc = oa[c][:, :HEAD_DIM] / (oa[c][:, HEAD_DIM:] + sink_w[c])
                heads += [oc[g * QBLK:(g + 1) * QBLK, :] for g in range(WA_GROUP)]
            rows.append(jnp.concatenate(heads, axis=-1))
        o_ref[...] = jnp.concatenate(rows, axis=0).astype(o_ref.dtype)

    @pl.when(is_lat)
    def _():
        blk = [p * nq + qb for qb in range(nq)]
        ws = [jnp.clip((bq - 1) * QBLK, 0, seq - 3 * QBLK) for bq in blk]
        sls = [pl.ds(pl.multiple_of(w, QBLK), 3 * QBLK) for w in ws]
        kw = [kn_sc[sl, :] for sl in sls]
        vw = [vn_sc[sl, :] for sl in sls]
        case = [jnp.where(bq == 0, 0, jnp.where(bq == n_qb - 1, 2, 1)) for bq in blk]
        msk = [jnp.concatenate([wmask_ref[cs]] * WA_GROUP, axis=0) for cs in case]
        s_w = [_dot_nt(qs[c], kw[chains[c][0]][:, hsl(chains[c][1])]) + msk[chains[c][0]] for c in nch]
        m = [jnp.maximum(jnp.max(s_w[c], axis=-1, keepdims=True), m_c[c]) for c in nch]
        p_w = [_prob(s_w[c], m[c]) for c in nch]
        p_c = [_prob(s_c[c], m[c]) for c in nch]
        finish([_dot(p_w[c], vw[chains[c][0]][:, vsl(chains[c][1])]) + _dot(p_c[c], vca[chains[c][1]])
                for c in nch], [jnp.exp(sk[c] - m[c]) for c in nch])

    @pl.when(jnp.logical_not(is_lat))
    def _():
        finish([_dot(_prob(s_c[c], m_c[c]), vca[chains[c][1]]) for c in nch],
               [jnp.exp(sk[c] - m_c[c]) for c in nch])


def _wa_mask_table():
    i = np.arange(QBLK)[:, None]
    j = np.arange(3 * QBLK)[None, :]
    shifts = (0, QBLK, 2 * QBLK)
    return jnp.asarray(np.stack([np.where(np.abs(j - sh - i) <= WA_WINDOW, 0.0, NEG_INF) for sh in shifts]), F32)


def _wa_call(proj, sink, cos_t, sin_t, qg, kg, ones_bd, n_batch, seq, ctx, want_ctx):
    R = proj.shape[0]
    nq = 2
    n_qb = seq // QBLK
    n_lat, n_ctx = n_qb // nq, ctx // (nq * QBLK)
    steps = n_lat + (n_ctx if want_ctx else 0)
    ctx_kb = n_batch * seq // ctx

    def qrow(b, p):
        return jnp.where(p < n_lat, b * n_lat + p, n_batch * n_lat + b * n_ctx + (p - n_lat))

    ck, cv = COL_CK // LANE, COL_CV // LANE
    return pl.pallas_call(
        functools.partial(_wa_kernel, n_qb=n_qb, seq=seq, ctx=ctx),
        out_shape=jax.ShapeDtypeStruct((R, GROUP_W), BF16),
        grid=(n_batch, steps),
        in_specs=[
            pl.BlockSpec(memory_space=pltpu.SMEM),
            pl.BlockSpec((nq * QBLK, GROUP_W), lambda b, p: (qrow(b, p), COL_CQ // GROUP_W)),
            pl.BlockSpec((seq, LANE), lambda b, p: (b, ck)),
            pl.BlockSpec((seq, LANE), lambda b, p: (b, cv)),
            pl.BlockSpec((ctx, LANE), lambda b, p: (ctx_kb + b, ck)),
            pl.BlockSpec((ctx, LANE), lambda b, p: (ctx_kb + b, cv)),
            pl.BlockSpec((seq, LANE), lambda b, p: (0, 0)),
            pl.BlockSpec((seq, LANE), lambda b, p: (0, 0)),
            pl.BlockSpec((nq * QBLK, LANE), lambda b, p: (jnp.minimum(p, n_lat - 1), 0)),
            pl.BlockSpec((nq * QBLK, LANE), lambda b, p: (jnp.minimum(p, n_lat - 1), 0)),
            pl.BlockSpec((1, LANE), lambda b, p: (0, 0)),
            pl.BlockSpec((1, LANE), lambda b, p: (0, 0)),
            pl.BlockSpec((LANE, LANE), lambda b, p: (0, 0)),
            pl.BlockSpec((3, QBLK, 3 * QBLK), lambda b, p: (0, 0, 0)),
        ],
        out_specs=pl.BlockSpec((nq * QBLK, GROUP_W), lambda b, p: (qrow(b, p), 0)),
        scratch_shapes=[pltpu.VMEM((seq, LANE), BF16), pltpu.VMEM((seq, 2 * LANE), BF16),
                        pltpu.VMEM((ctx, LANE), BF16), pltpu.VMEM((ctx, 2 * LANE), BF16)],
        compiler_params=_cparams(("arbitrary", "arbitrary")),
        name="mixer_c",
    )(sink, proj, proj, proj, proj, proj, cos_t, sin_t, cos_t, sin_t, qg, kg, ones_bd, _wa_mask_table())


def _rw_prep_kernel(x_ref, prev_ref, next_ref, lr_ref, conv_ref, wlr_ref, w0_ref, a0_ref, kk_ref, ka_ref,
                    ones_ref, o_ref, *, tiles_lat, tiles_ctx, n_lat_tiles):
    i = pl.program_id(0)
    tm = x_ref.shape[0]
    in_lat = i < n_lat_tiles
    first = jnp.where(in_lat, i % tiles_lat == 0, (i - n_lat_tiles) % tiles_ctx == 0)
    last = jnp.where(in_lat, i % tiles_lat == tiles_lat - 1, (i - n_lat_tiles) % tiles_ctx == tiles_ctx - 1)
    x = x_ref[...]
    prev_row = jnp.where(first, 0.0, prev_ref[7:8, :])
    next_row = jnp.where(last, 0.0, next_ref[0:1, :])
    row = lax.broadcasted_iota(jnp.int32, (tm, 1), 0)
    x_prev = jnp.where(row == 0, prev_row, pltpu.roll(x, 1, 0))
    x_next = jnp.where(row == tm - 1, next_row, pltpu.roll(x, tm - 1, 0))
    cw = conv_ref[...]
    y = x_prev * cw[0:1, :] + x * cw[1:2, :] + x_next * cw[2:3, :]
    r = y[:, 0:GROUP_W]
    k = y[:, GROUP_W:2 * GROUP_W]
    v = y[:, 2 * GROUP_W:3 * GROUP_W]

    lr = lr_ref[...]
    lane = lax.broadcasted_iota(jnp.int32, (1, lr.shape[1]), 1)
    c1 = RW_DECAY_RANK
    c2 = c1 + RW_ICLR_RANK
    c3 = c2 + RW_GATE_RANK
    act = jnp.where(lane < c1, jnp.tanh(lr),
                    jnp.where(lane < c2, lr, jnp.where(lane < c3, _sigmoid(lr), 0.0)))
    up = _dot(act.astype(BF16), wlr_ref[...])

    ones_bd = ones_ref[...]
    kk = k * kk_ref[...]
    nrm = jnp.sqrt(_seg_sum(kk * kk, ones_bd))
    kk = kk / jnp.maximum(nrm, 1e-12)

    o_ref[:, PB_R * GROUP_W:(PB_R + 1) * GROUP_W] = r
    o_ref[:, PB_V * GROUP_W:(PB_V + 1) * GROUP_W] = v
    o_ref[:, PB_KK * GROUP_W:(PB_KK + 1) * GROUP_W] = kk
    o_ref[:, PB_G * GROUP_W:(PB_G + 1) * GROUP_W] = up[:, 4 * GROUP_W:5 * GROUP_W]
    ka = ka_ref[...]
    for d in range(2):
        z = w0_ref[d:d + 1, :] + up[:, d * GROUP_W:(d + 1) * GROUP_W]
        sp = jnp.maximum(-z, 0.0) + jnp.log(1.0 + jnp.exp(-jnp.abs(z)))
        lw = -jnp.exp(-sp - 0.5)
        a = _sigmoid(a0_ref[d:d + 1, :] + up[:, (2 + d) * GROUP_W:(3 + d) * GROUP_W])
        kd = k * (1.0 + (a - 1.0) * ka)
        base = 3 + 3 * d
        o_ref[:, base * GROUP_W:(base + 1) * GROUP_W] = lw
        o_ref[:, (base + 1) * GROUP_W:(base + 2) * GROUP_W] = a
        o_ref[:, (base + 2) * GROUP_W:(base + 3) * GROUP_W] = kd


def _rw_prep_call(proj, conv_w, wlr, w0, a0, k_k, k_a, ones512, n_batch, seq, ctx, tm):
    R = proj.shape[0]
    n_lat_tiles = n_batch * seq // tm
    nb8 = R // 8
    t8 = tm // 8
    wide = 3 * GROUP_W
    cb = COL_BR // wide
    return pl.pallas_call(
        functools.partial(_rw_prep_kernel, tiles_lat=seq // tm, tiles_ctx=ctx // tm, n_lat_tiles=n_lat_tiles),
        out_shape=jax.ShapeDtypeStruct((R, PB_N * GROUP_W), F32),
        grid=(R // tm,),
        in_specs=[
            pl.BlockSpec((tm, wide), lambda i: (i, cb)),
            pl.BlockSpec((8, wide), lambda i: (jnp.maximum(i * t8 - 1, 0), cb)),
            pl.BlockSpec((8, wide), lambda i: (jnp.minimum((i + 1) * t8, nb8 - 1), cb)),
            pl.BlockSpec((tm, 256), lambda i: (i, COL_BLR // 256)),
            pl.BlockSpec((3, wide), lambda i: (0, 0)),
            pl.BlockSpec((256, 5 * GROUP_W), lambda i: (0, 0)),
            pl.BlockSpec((2, GROUP_W), lambda i: (0, 0)),
            pl.BlockSpec((2, GROUP_W), lambda i: (0, 0)),
            pl.BlockSpec((1, GROUP_W), lambda i: (0, 0)),
            pl.BlockSpec((1, GROUP_W), lambda i: (0, 0)),
            pl.BlockSpec((GROUP_W, GROUP_W), lambda i: (0, 0)),
        ],
        out_specs=pl.BlockSpec((tm, PB_N * GROUP_W), lambda i: (i, 0)),
        compiler_params=_cparams(("arbitrary",)),
        name="mixer_b_prep",
    )(proj, proj, proj, proj, conv_w, wlr, w0, a0, k_k, k_a, ones512)


RW_SPLIT = 1
_NN = ((1,), (0,))
_NT = ((1,), (1,))
_TN = ((0,), (0,))


def _split(x, n):
    parts = []
    for _ in range(n):
        p = x.astype(BF16)
        parts.append(p)
        x = x - p.astype(F32)
    return parts


def _sdot(a, b, dims):
    n = max(len(a), len(b))
    acc = None
    for i, ai in enumerate(a):
        for j, bj in enumerate(b):
            if i + j < n:
                t = lax.dot_general(ai, bj, (dims, ((), ())), preferred_element_type=F32)
                acc = t if acc is None else acc + t
    return acc


def _rw_chunks(at, bt, kt, rt, v, gam, s0, strict, incl):
    n = len(at)
    ids = range(n)
    C = at[0].shape[0]
    sp = lambda t: _split(t, RW_SPLIT)
    b1 = lambda t: [t.astype(BF16)]
    ar = [b1(jnp.concatenate([at[i], rt[i]], axis=0)) for i in ids]
    bk = [sp(jnp.concatenate([bt[i], kt[i]], axis=0)) for i in ids]
    g4 = [_sdot(ar[i], bk[i][:1], _NT) for i in ids]
    a_ab = [jnp.where(strict[i], g4[i][:C, :C], 0.0) for i in ids]
    a_ak = [jnp.where(strict[i], g4[i][:C, C:], 0.0) for i in ids]
    a_rb = [jnp.where(incl[i], g4[i][C:, :C], 0.0) for i in ids]
    a_rk = [jnp.where(incl[i], g4[i][C:, C:], 0.0) for i in ids]
    wv = [_sdot(b1(a_ak[i]), b1(v[i]), _NN) for i in ids]
    x = [jnp.concatenate([at[i], wv[i]], axis=1) for i in ids]

    row = lax.broadcasted_iota(jnp.int32, (C, C), 0)
    col = lax.broadcasted_iota(jnp.int32, (C, C), 1)
    eye = (row == col).astype(F32)
    blk = 16
    same = (row // blk) == (col // blk)
    p = [jnp.where(same, a_ab[i], 0.0) for i in ids]
    m = [eye + p[i] for i in ids]
    for _ in range(int(math.log2(blk)) - 1):
        pb_ = [b1(p[i]) for i in ids]
        p = [_sdot(pb_[i], pb_[i], _NN) for i in ids]
        m = [m[i] + _sdot(b1(m[i]), b1(p[i]), _NN) for i in ids]
    while blk < C:
        wider = (row // (2 * blk)) == (col // (2 * blk))
        join = wider & jnp.logical_not(same)
        mb = [b1(m[i]) for i in ids]
        t = [_sdot(b1(jnp.where(join, a_ab[i], 0.0)), mb[i], _NN) for i in ids]
        m = [m[i] + _sdot(mb[i], b1(t[i]), _NN) for i in ids]
        same, blk = wider, 2 * blk
    x = [_sdot(b1(m[i]), b1(x[i]), _NN) for i in ids]

    s0b = [sp(s0[i]) for i in ids]
    u = [x[i][:, HEAD_DIM:] + _sdot(sp(x[i][:, :HEAD_DIM]), s0b[i], _NT) for i in ids]
    uv = [sp(jnp.concatenate([u[i], v[i]], axis=0)) for i in ids]
    y = [_sdot(b1(jnp.concatenate([a_rb[i], a_rk[i]], axis=1)), uv[i][:1], _NN)
         + _sdot(sp(rt[i]), s0b[i], _NT) for i in ids]
    s1 = [(s0[i] + _sdot(uv[i], bk[i], _TN)) * gam[i] for i in ids]
    return y, s1


def _rw_scan_kernel(*refs, n_batch, nc):
    n_in = 12 * n_batch
    yfl_ref, ybl_ref, yfc_ref, ybc_ref, s_sc = refs[n_in:]
    s = pl.program_id(0)

    @pl.when(s == 0)
    def _():
        s_sc[...] = jnp.zeros_like(s_sc)

    C = refs[0].shape[0]
    row = lax.broadcasted_iota(jnp.int32, (C, C), 0)
    col = lax.broadcasted_iota(jnp.int32, (C, C), 1)
    chains = dict(at=[], bt=[], kt=[], rt=[], v=[], gam=[], s0=[], strict=[], incl=[])
    for b in range(n_batch):
        for d, reverse in enumerate((False, True)):
            r_ref, v_ref, kk_ref, lw_ref, a_ref, k_ref = refs[12 * b + 6 * d:12 * b + 6 * d + 6]
            incl, strict = (row <= col, row < col) if reverse else (row >= col, row > col)
            lw = lw_ref[...]
            kk = kk_ref[...]
            v = v_ref[...]
            cum = _sdot([incl.astype(BF16)], _split(lw, 3), _NN)
            e_l = jnp.exp(cum)
            e_n = jnp.exp(-cum)
            at = -(kk * jnp.exp(cum - lw))
            bt = kk * a_ref[...] * e_n
            kt = k_ref[...] * e_n
            rt = r_ref[...] * e_l
            gam = e_l[0:1, :] if reverse else e_l[C - 1:C, :]
            for h in range(RW_HEADS):
                hs = slice(h * HEAD_DIM, (h + 1) * HEAD_DIM)
                for name, val in (("at", at), ("bt", bt), ("kt", kt), ("rt", rt), ("v", v), ("gam", gam)):
                    chains[name].append(val[:, hs])
                chains["s0"].append(s_sc[b, d, h])
                chains["strict"].append(strict)
                chains["incl"].append(incl)
    y, s1 = _rw_chunks(**chains)
    ys = {}
    for b in range(n_batch):
        for d in range(2):
            base = (b * 2 + d) * RW_HEADS
            for h in range(RW_HEADS):
                s_sc[b, d, h] = s1[base + h]
            ys[b, d] = jnp.concatenate(y[base:base + RW_HEADS], axis=-1)

    @pl.when(s < nc)
    def _():
        for b in range(n_batch):
            yfc_ref[b] = ys[b, 0]
            ybc_ref[b] = ys[b, 1]

    @pl.when(s >= nc)
    def _():
        for b in range(n_batch):
            yfl_ref[b] = ys[b, 0]
            ybl_ref[b] = ys[b, 1]


def _rw_scan_call(prep, n_batch, seq, ctx):
    C = RW_CHUNK
    nc, nl = ctx // C, seq // C
    ctx_base = n_batch * nl

    def blk_f(b, s):
        return jnp.where(s < nc, ctx_base + b * nc + s, b * nl + (s - nc))

    def blk_b(b, s):
        return jnp.where(s < nc, ctx_base + b * nc + (nc - 1 - s), b * nl + (nl - 1 - (s - nc)))

    def col(blk, b, cb):
        return pl.BlockSpec((C, GROUP_W), lambda s: (blk(b, s), cb))

    in_specs = []
    for b in range(n_batch):
        in_specs += [col(blk_f, b, c) for c in (PB_R, PB_V, PB_KK, 3, 4, 5)]
        in_specs += [col(blk_b, b, c) for c in (PB_R, PB_V, PB_KK, 6, 7, 8)]
    lat = jax.ShapeDtypeStruct((n_batch, seq, GROUP_W), F32)
    cx = jax.ShapeDtypeStruct((n_batch, ctx, GROUP_W), F32)
    blk3 = (n_batch, C, GROUP_W)
    out_specs = (
        pl.BlockSpec(blk3, lambda s: (0, jnp.maximum(s - nc, 0), 0)),
        pl.BlockSpec(blk3, lambda s: (0, nl - 1 - jnp.maximum(s - nc, 0), 0)),
        pl.BlockSpec(blk3, lambda s: (0, jnp.minimum(s, nc - 1), 0)),
        pl.BlockSpec(blk3, lambda s: (0, nc - 1 - jnp.minimum(s, nc - 1), 0)),
    )
    yfl, ybl, yfc, ybc = pl.pallas_call(
        functools.partial(_rw_scan_kernel, n_batch=n_batch, nc=nc),
        out_shape=(lat, lat, cx, cx),
        grid=(nc + nl,),
        in_specs=in_specs,
        out_specs=out_specs,
        scratch_shapes=[pltpu.VMEM((n_batch, 2, RW_HEADS, HEAD_DIM, HEAD_DIM), F32)],
        compiler_params=_cparams(("arbitrary",)),
        name="mixer_b_scan",
    )(*([prep] * (12 * n_batch)))
    flat = lambda a, c: jnp.concatenate([a.reshape(n_batch * seq, GROUP_W), c.reshape(n_batch * ctx, GROUP_W)], 0)
    return flat(yfl, yfc), flat(ybl, ybc)


def _rw_readout_kernel(r_ref, v_ref, k0_ref, k1_ref, g_ref, yf_ref, yb_ref, rk_ref, gw_ref, gb_ref, ones_ref,
                       o_ref):
    ones_bd = ones_ref[...]
    v = v_ref[...]
    bonus = _seg_sum(r_ref[...] * (k0_ref[...] + k1_ref[...]) * rk_ref[...], ones_bd)
    y = yf_ref[...] + yb_ref[...] + bonus * v
    mu = _seg_sum(y, ones_bd) * (1.0 / HEAD_DIM)
    yc = y - mu
    var = _seg_sum(yc * yc, ones_bd) * (1.0 / HEAD_DIM)
    yn = yc * lax.rsqrt(var + RW_GN_EPS)
    o_ref[...] = ((yn * gw_ref[...] + gb_ref[...]) * g_ref[...]).astype(o_ref.dtype)


def _rw_readout_call(prep, yf, yb, r_k, gn_w, gn_b, ones512, tm):
    R = prep.shape[0]
    col = lambda cb: pl.BlockSpec((tm, GROUP_W), lambda i: (i, cb))
    vec = pl.BlockSpec((1, GROUP_W), lambda i: (0, 0))
    return pl.pallas_call(
        _rw_readout_kernel,
        out_shape=jax.ShapeDtypeStruct((R, GROUP_W), BF16),
        grid=(R // tm,),
        in_specs=[col(PB_R), col(PB_V), col(5), col(8), col(PB_G), col(0), col(0), vec, vec, vec,
                  pl.BlockSpec((GROUP_W, GROUP_W), lambda i: (0, 0))],
        out_specs=col(0),
        compiler_params=_cparams(("arbitrary",)),
        name="mixer_b_readout",
    )(prep, prep, prep, prep, prep, yf, yb, r_k, gn_w, gn_b, ones512)


def _s5_tables(a_re, a_im, log_dt, b_re, b_im, c_re, c_im):
    Lc, G, N, P = S5_CHUNK, S5_GROUPS, S5_N, S5_P
    dt = jnp.exp(log_dt)[..., None]
    lam_re, lam_im = dt * a_re, dt * a_im
    tau = jnp.arange(Lc + 1, dtype=F32)[:, None, None, None]
    mag = jnp.exp(tau * lam_re)
    pw_re, pw_im = mag * jnp.cos(tau * lam_im), mag * jnp.sin(tau * lam_im)
    ab_re, ab_im = pw_re[1], pw_im[1]
    den = a_re * a_re + a_im * a_im
    nr = ab_re - 1.0
    cf_re, cf_im = (nr * a_re + ab_im * a_im) / den, (ab_im * a_re - nr * a_im) / den
    bp_re = cf_re[..., None] * b_re[None] - cf_im[..., None] * b_im[None]
    bp_im = cf_re[..., None] * b_im[None] + cf_im[..., None] * b_re[None]
    pb_re = pw_re[..., None] * bp_re[None] - pw_im[..., None] * bp_im[None]
    pb_im = pw_re[..., None] * bp_im[None] + pw_im[..., None] * bp_re[None]
    kk = (jnp.einsum('gqn,tdgnp->tdgqp', c_re, pb_re[:Lc]) - jnp.einsum('gqn,tdgnp->tdgqp', c_im, pb_im[:Lc]))
    jj = np.arange(Lc)[:, None]
    ii = np.arange(Lc)[None, :]
    dist = np.abs(ii - jj)
    kf = kk[dist, 0] * jnp.asarray(ii >= jj, F32)[..., None, None, None]
    kb = kk[dist, 1] * jnp.asarray(ii <= jj, F32)[..., None, None, None]
    tz = jnp.transpose(kf + kb, (2, 0, 4, 1, 3))
    jr = np.arange(Lc)
    emap = lambda pbx, order, d: jnp.transpose(pbx[order, d], (1, 0, 3, 2))
    em = jnp.stack([emap(pb_re, Lc - 1 - jr, 0), emap(pb_im, Lc - 1 - jr, 0),
                    emap(pb_re, jr, 1), emap(pb_im, jr, 1)], axis=3)

    def gmap(order, d):
        pr, pi = pw_re[order, d], pw_im[order, d]
        cp_re = c_re[None] * pr[:, :, None, :] - c_im[None] * pi[:, :, None, :]
        cp_im = c_re[None] * pi[:, :, None, :] + c_im[None] * pr[:, :, None, :]
        to = lambda t: jnp.transpose(t, (1, 3, 0, 2))
        return to(cp_re), to(-cp_im)

    gk = jnp.stack(gmap(jr + 1, 0) + gmap(Lc - jr, 1), axis=1)
    NO, NQ, NG = S5_OCTETS, 2, 4
    tz, em, gk = tz.astype(BF16), em.astype(BF16), gk.astype(BF16)

    def bdiag(blocks):
        n, c = blocks.shape[1], blocks.shape[3]
        return jnp.concatenate(
            [jnp.pad(blocks[:, g], ((0, 0), (0, 0), (g * c, (n - 1 - g) * c))) for g in range(n)], axis=1)

    def perm_rows(m, dims, order):
        no, r, c = m.shape
        m = jnp.transpose(m.reshape((no,) + dims + (c,)), (0,) + tuple(1 + o for o in order) + (len(dims) + 1,))
        return m.reshape(no, r, c)

    tr = lambda m: jnp.swapaxes(m, 1, 2)
    gjp, rgkn = (8, Lc, P), (NQ, NG, 4, N)
    wy = perm_rows(bdiag(tz.reshape(NO, 8, Lc * P, Lc * P)), gjp, (1, 0, 2))
    wy = tr(perm_rows(tr(wy), gjp, (1, 0, 2)))
    we = perm_rows(bdiag(em.reshape(NO, 8, Lc * P, 4 * N)), gjp, (1, 0, 2))
    we = tr(perm_rows(tr(we), rgkn, (0, 2, 1, 3)))
    wz = jnp.concatenate([wy, we], axis=-1)
    gm = perm_rows(bdiag(gk.reshape(NO, 8, 4 * N, Lc * P)), rgkn, (0, 2, 1, 3))
    gm = tr(perm_rows(tr(gm), gjp, (1, 0, 2)))
    mp = Lc * jnp.arange(1, 9, dtype=F32)[:, None, None, None]
    cm = jnp.exp(mp * lam_re)
    cp_re, cp_im = cm * jnp.cos(mp * lam_im), cm * jnp.sin(mp * lam_im)
    apow = jnp.stack([cp_re[:, 0], cp_im[:, 0], cp_re[::-1, 1], cp_im[::-1, 1]], axis=0)
    apow = jnp.transpose(apow.reshape(4, 8, G // NG, NG * N), (2, 0, 1, 3))
    return wz, gm, apow


def _s5_local_kernel(u_ref, wz_ref, y_ref, e_ref):
    Lc = u_ref.shape[1]
    lhs = jnp.concatenate([u_ref[:, j, :] for j in range(Lc)], axis=-1).astype(BF16)
    z = _dot(lhs, wz_ref[...])
    for i in range(Lc):
        y_ref[:, i, :] = z[:, i * LANE:(i + 1) * LANE]
    e_ref[...] = z[:, Lc * LANE:]


def _s5_local_call(proj3, wz):
    NR, Lc, _ = proj3.shape
    NO, K, N = wz.shape
    NE = N - Lc * LANE
    tr = NR // 8
    cu = COL_DU // LANE
    return pl.pallas_call(
        _s5_local_kernel,
        out_shape=(jax.ShapeDtypeStruct((NR, Lc, GROUP_W), F32), jax.ShapeDtypeStruct((NR, NO * NE), F32)),
        grid=(NO, NR // tr),
        in_specs=[
            pl.BlockSpec((tr, Lc, LANE), lambda o, i: (i, 0, cu + o)),
            pl.BlockSpec((None, K, N), lambda o, i: (o, 0, 0)),
        ],
        out_specs=(pl.BlockSpec((tr, Lc, LANE), lambda o, i: (i, 0, o)),
                   pl.BlockSpec((tr, NE), lambda o, i: (i, o))),
        compiler_params=_cparams(("arbitrary", "arbitrary")),
        name="mixer_d_local",
    )(proj3, wz)


def _s5_scan_kernel(e_ref, ap_ref, x_ref, *, n_batch, nlc, ncc):
    NS = e_ref.shape[1] // 4
    row = lax.broadcasted_iota(jnp.int32, (8, NS), 0)
    pw = [ap_ref[k] for k in range(4)]

    def cmul(a_re, a_im, x_re, x_im):
        return a_re * x_re - a_im * x_im, a_re * x_im + a_im * x_re

    def tile_scan(x_re, x_im, p_re, p_im, c_re, c_im, reverse):
        for d in (1, 2, 4):
            if reverse:
                keep, sh, pr = row < 8 - d, 8 - d, 8 - d
            else:
                keep, sh, pr = row >= d, d, d - 1
            s_re = jnp.where(keep, pltpu.roll(x_re, sh, 0), 0.0)
            s_im = jnp.where(keep, pltpu.roll(x_im, sh, 0), 0.0)
            t_re, t_im = cmul(p_re[pr:pr + 1, :], p_im[pr:pr + 1, :], s_re, s_im)
            x_re, x_im = x_re + t_re, x_im + t_im
        t_re, t_im = cmul(p_re, p_im, c_re, c_im)
        x_re, x_im = x_re + t_re, x_im + t_im
        edge, sh, last = (7, 7, 0) if reverse else (0, 1, 7)
        in_re = jnp.where(row == edge, c_re, pltpu.roll(x_re, sh, 0))
        in_im = jnp.where(row == edge, c_im, pltpu.roll(x_im, sh, 0))
        return in_re, in_im, x_re[last:last + 1, :], x_im[last:last + 1, :]

    def phase(row0, n_chunks, stride, carry):
        n_tiles = n_chunks // 8

        def body(t, carry):
            out = []
            for b in range(n_batch):
                xf_re, xf_im, xb_re, xb_im = carry[4 * b:4 * b + 4]
                rf = pl.ds(pl.multiple_of(row0 + b * stride + t * 8, 8), 8)
                rb = pl.ds(pl.multiple_of(row0 + b * stride + (n_tiles - 1 - t) * 8, 8), 8)
                f_re, f_im, xf_re, xf_im = tile_scan(e_ref[rf, 0:NS], e_ref[rf, NS:2 * NS], pw[0], pw[1],
                                                     xf_re, xf_im, False)
                b_re, b_im, xb_re, xb_im = tile_scan(e_ref[rb, 2 * NS:3 * NS], e_ref[rb, 3 * NS:4 * NS],
                                                     pw[2], pw[3], xb_re, xb_im, True)
                x_ref[rf, 0:NS] = f_re
                x_ref[rf, NS:2 * NS] = f_im
                x_ref[rb, 2 * NS:3 * NS] = b_re
                x_ref[rb, 3 * NS:4 * NS] = b_im
                out += [xf_re, xf_im, xb_re, xb_im]
            return tuple(out)

        return lax.fori_loop(0, n_tiles, body, carry)

    zero = jnp.zeros((1, NS), F32)
    carry = phase(n_batch * nlc, ncc, ncc, (zero,) * (4 * n_batch))
    phase(0, nlc, nlc, carry)


def _s5_scan_call(e, apow, n_batch, seq, ctx):
    NR, NEall = e.shape
    NQ8, _, _, NS = apow.shape
    return pl.pallas_call(
        functools.partial(_s5_scan_kernel, n_batch=n_batch, nlc=seq // S5_CHUNK, ncc=ctx // S5_CHUNK),
        out_shape=jax.ShapeDtypeStruct((NR, NEall), F32),
        grid=(NQ8,),
        in_specs=[
            pl.BlockSpec((NR, 4 * NS), lambda q: (0, q)),
            pl.BlockSpec((None, 4, 8, NS), lambda q: (q, 0, 0, 0)),
        ],
        out_specs=pl.BlockSpec((NR, 4 * NS), lambda q: (0, q)),
        compiler_params=_cparams(("arbitrary",)),
        name="mixer_d_scan",
    )(e, apow)


def _s5_carry_kernel(x_ref, gm_ref, yl_ref, y_ref):
    Lc = yl_ref.shape[1]
    y = _dot(x_ref[...].astype(BF16), gm_ref[...])
    for i in range(Lc):
        y_ref[:, i, :] = yl_ref[:, i, :] + y[:, i * LANE:(i + 1) * LANE]


def _s5_carry_call(xin, gm, yloc):
    NR, Lc, _ = yloc.shape
    NO, NE, N = gm.shape
    tr = NR // 8
    return pl.pallas_call(
        _s5_carry_kernel,
        out_shape=jax.ShapeDtypeStruct((NR, Lc, GROUP_W), F32),
        grid=(NO, NR // tr),
        in_specs=[
            pl.BlockSpec((tr, NE), lambda o, i: (i, o)),
            pl.BlockSpec((None, NE, N), lambda o, i: (o, 0, 0)),
            pl.BlockSpec((tr, Lc, LANE), lambda o, i: (i, 0, o)),
        ],
        out_specs=pl.BlockSpec((tr, Lc, LANE), lambda o, i: (i, 0, o)),
        compiler_params=_cparams(("arbitrary", "arbitrary")),
        name="mixer_d_carry",
    )(xin, gm, yloc)


def _s5_out_kernel(y_ref, u_ref, d_ref, w_ref, b_ref, o_ref):
    y = y_ref[...] + d_ref[...] * u_ref[...]
    c = math.sqrt(2.0 / math.pi)
    y = 0.5 * y * (1.0 + jnp.tanh(c * (y + 0.044715 * (y * y * y))))
    z = _dot(y.astype(BF16), w_ref[...]) + b_ref[...]
    o_ref[...] = (y * _sigmoid(z)).astype(o_ref.dtype)


def _s5_out_call(y_tok, proj, d_skip, glu_w, glu_b, tm):
    R = proj.shape[0]
    vec = pl.BlockSpec((1, GROUP_W), lambda i: (0, 0))
    return pl.pallas_call(
        _s5_out_kernel,
        out_shape=jax.ShapeDtypeStruct((R, GROUP_W), BF16),
        grid=(R // tm,),
        in_specs=[
            pl.BlockSpec((tm, GROUP_W), lambda i: (i, 0)),
            pl.BlockSpec((tm, GROUP_W), lambda i: (i, COL_DU // GROUP_W)),
            vec,
            pl.BlockSpec((GROUP_W, GROUP_W), lambda i: (0, 0)),
            vec,
        ],
        out_specs=pl.BlockSpec((tm, GROUP_W), lambda i: (i, 0)),
        compiler_params=_cparams(("arbitrary",)),
        name="mixer_d_out",
    )(y_tok, proj, d_skip, glu_w, glu_b)


def _rope_tables(n_tokens):
    t = jnp.arange(n_tokens)
    nf = HEAD_DIM // 4
    inv = 1.0 / (ROPE_BASE ** (jnp.arange(nf, dtype=F32) / nf))

    def ang(pp):
        a = pp.astype(F32)[:, None] * inv[None, :]
        return jnp.concatenate([a, a], -1)

    a = jnp.concatenate([ang(t // GRID_W), ang(t % GRID_W)], -1)
    cos, sin = jnp.cos(a), jnp.sin(a)
    sign = np.where((np.arange(HEAD_DIM) % 32) < 16, -1.0, 1.0).astype(np.float32)
    cos2 = jnp.concatenate([cos, cos], -1)
    sin2 = jnp.concatenate([sin * sign, sin * sign], -1)
    return cos2, sin2


def _block_ones(n):
    return jnp.asarray(np.kron(np.eye(n // HEAD_DIM), np.ones((HEAD_DIM, HEAD_DIM))), BF16)


def _permute_w_in(w_in):
    cuts = np.cumsum([512, 512, 512, 512, 512, 512, 32, 32, 96, 512, 128, 128, 512])
    seg = lambda i: w_in[..., (0 if i == 0 else cuts[i - 1]):cuts[i]]
    L, D = w_in.shape[:2]
    pad = jnp.zeros((L, D, 96), w_in.dtype)
    parts = [seg(0), seg(1), seg(2), seg(3), seg(4), seg(5), seg(9), seg(12),
             seg(6), seg(7), seg(8), pad, seg(10), seg(11)]
    return jnp.concatenate(parts, axis=-1).astype(BF16)


def _forward(x, c, ctx, c_ctx, w_ada, b_ada, norm_g, ffn1_wg, ffn1_wu, ffn1_wd, ffn2_wg, ffn2_wu, ffn2_wd,
             w_in, w_out, na_q_g, na_k_g, na_rpb, rw_conv, rw_w0, rw_w_up, rw_a0, rw_a_up, rw_g_up,
             rw_k_k, rw_k_a, rw_r_k, rw_gn_w, rw_gn_b, wa_q_g, wa_k_g, wa_sink, s5_a_re, s5_a_im,
             s5_log_dt, s5_b_re, s5_b_im, s5_c_re, s5_c_im, s5_d, s5_glu_w, s5_glu_b, *, tm, tf):
    B, SEQ, D = x.shape
    CTX = ctx.shape[1]
    L = w_ada.shape[0]
    n_lat = B * SEQ
    n_qb = SEQ // QBLK

    bf = lambda t: t.astype(BF16)
    f1g, f1u, f1d, f2g, f2u, f2d = map(bf, (ffn1_wg, ffn1_wu, ffn1_wd, ffn2_wg, ffn2_wu, ffn2_wd))
    w_in_p = _permute_w_in(w_in)
    w_out_b = bf(w_out)
    glu_w_b = bf(s5_glu_w)
    ones128, ones512 = _block_ones(LANE), _block_ones(GROUP_W)
    cos_t, sin_t = _rope_tables(SEQ)
    na_bias = [_na_bias_table(na_rpb[l], n_qb) for l in range(L)]
    zr = lambda r, cdim: jnp.zeros((L, r, cdim), F32)
    wlr = jnp.concatenate([
        jnp.concatenate([rw_w_up[:, 0], rw_w_up[:, 1], zr(32, 3 * GROUP_W)], axis=-1),
        jnp.concatenate([zr(32, 2 * GROUP_W), rw_a_up[:, 0], rw_a_up[:, 1], zr(32, GROUP_W)], axis=-1),
        jnp.concatenate([zr(96, 4 * GROUP_W), rw_g_up], axis=-1),
        zr(96, 5 * GROUP_W)], axis=1).astype(BF16)
    s5_tabs = jax.vmap(_s5_tables)(s5_a_re, s5_a_im, s5_log_dt, s5_b_re, s5_b_im, s5_c_re, s5_c_im)
    tile2 = lambda t: jnp.concatenate([t, t], axis=-1)

    cc = jnp.concatenate([c, c_ctx[None], jnp.zeros((8 - B - 1, D), F32)], axis=0)
    mods_all = _ada_call(cc, w_ada, b_ada)[:, :B + 1].reshape(L, B + 1, N_MOD, D)

    xs = jnp.concatenate([x.reshape(n_lat, D), ctx.reshape(B * CTX, D)], axis=0)
    R = xs.shape[0]
    for l in range(L):
        want_ctx = l < L - 1
        mods = mods_all[l]
        ng = norm_g[l]
        xs = _ffn_call(xs, mods, ng[0:1], f1g, f1u, f1d, l, 0, R, SEQ, B, tm, tf)
        proj = _win_call(xs, mods, ng[1:2], w_in_p, l, SEQ, B, tm, D_IN_PAD // 3)
        o_a = _na_call(proj, na_bias[l], tile2(na_q_g[l][None]), tile2(na_k_g[l][None]), ones128,
                       B, SEQ, CTX, want_ctx)
        prep = _rw_prep_call(proj, rw_conv[l], wlr[l], rw_w0[l], rw_a0[l], rw_k_k[l][None], rw_k_a[l][None],
                             ones512, B, SEQ, CTX, min(tm, 256))
        vec = lambda t: t.reshape(1, GROUP_W)
        yf, yb = _rw_scan_call(prep, B, SEQ, CTX)
        o_b = _rw_readout_call(prep, yf, yb, vec(rw_r_k[l]), vec(rw_gn_w[l]), vec(rw_gn_b[l]), ones512, tm)
        o_c = _wa_call(proj, wa_sink[l], cos_t, sin_t, tile2(wa_q_g[l][None]), tile2(wa_k_g[l][None]), ones128,
                       B, SEQ, CTX, want_ctx)
        wz, gm, apow = (t[l] for t in s5_tabs)
        yloc, e_loc = _s5_local_call(proj.reshape(R // S5_CHUNK, S5_CHUNK, D_IN_PAD), bf(wz))
        xin = _s5_scan_call(e_loc, apow, B, SEQ, CTX)
        y_tok = _s5_carry_call(xin, bf(gm), yloc).reshape(R, GROUP_W)
        o_d = _s5_out_call(y_tok, proj, vec(s5_d[l]), glu_w_b[l], vec(s5_glu_b[l]), tm)
        n_rows = R if want_ctx else n_lat
        xs = _wout_call(xs, (o_a, o_b, o_c, o_d), mods, w_out_b, l, n_rows, SEQ, B, tm)
        xs = _ffn_call(xs, mods, ng[2:3], f2g, f2u, f2d, l, 6, n_rows, SEQ, B, tm, tf)
    return xs[:n_lat].reshape(B, SEQ, D)


def kernel(x, c, ctx, c_ctx, w_ada, b_ada, norm_g, ffn1_wg, ffn1_wu, ffn1_wd, ffn2_wg, ffn2_wu, ffn2_wd, w_in, w_out, na_q_g, na_k_g, na_rpb, rw_conv, rw_w0, rw_w_up, rw_a0, rw_a_up, rw_g_up, rw_k_k, rw_k_a, rw_r_k, rw_gn_w, rw_gn_b, wa_q_g, wa_k_g, wa_sink, s5_a_re, s5_a_im, s5_log_dt, s5_b_re, s5_b_im, s5_c_re, s5_c_im, s5_d, s5_glu_w, s5_glu_b):
    return _forward(x, c, ctx, c_ctx, w_ada, b_ada, norm_g, ffn1_wg, ffn1_wu, ffn1_wd, ffn2_wg, ffn2_wu, ffn2_wd,
                    w_in, w_out, na_q_g, na_k_g, na_rpb, rw_conv, rw_w0, rw_w_up, rw_a0, rw_a_up, rw_g_up,
                    rw_k_k, rw_k_a, rw_r_k, rw_gn_w, rw_gn_b, wa_q_g, wa_k_g, wa_sink, s5_a_re, s5_a_im,
                    s5_log_dt, s5_b_re, s5_b_im, s5_c_re, s5_c_im, s5_d, s5_glu_w, s5_glu_b, tm=512, tf=512)
```

```python
import functools
import math

import numpy as np
import jax
import jax.numpy as jnp
from jax import lax
from jax.experimental import pallas as pl
from jax.experimental.pallas import tpu as pltpu

F32 = jnp.float32
BF16 = jnp.bfloat16

D_MODEL = 2048
GRID_W = 64
HEAD_DIM = 64
GROUP_W = D_MODEL // 4
N_MOD = 9
NORM_EPS = 1e-6
ROPE_BASE = 10000.0
NEG_INF = -1e30

NA_HEADS = GROUP_W // HEAD_DIM
NA_WIN_R = 8
NA_WIN_C = 16
RW_HEADS = GROUP_W // HEAD_DIM
RW_DECAY_RANK = 32
RW_ICLR_RANK = 32
RW_GATE_RANK = 96
RW_GN_EPS = 64e-5
WA_HEADS = GROUP_W // HEAD_DIM
WA_KV_HEADS = 2
WA_GROUP = WA_HEADS // WA_KV_HEADS
WA_WINDOW = 128
S5_P = 16
S5_GROUPS = GROUP_W // S5_P
S5_N = 64

LANE = 128
QBLK = 2 * GRID_W
NA_KBLKS = 5
RW_CHUNK = 64
S5_CHUNK = 8
S5_OCTETS = 4
VMEM_LIMIT = 56 * 1024 * 1024

COL_AQ, COL_AK, COL_AV = 0, 512, 1024
COL_BR = 1536
COL_CQ = 3072
COL_DU = 3584
COL_BLR = 4096
COL_CK, COL_CV = 4352, 4480
D_IN_PAD = 4608

PB_R, PB_V, PB_KK, PB_G = 0, 1, 2, 9
PB_N = 10


def _cparams(sem):
    return pltpu.CompilerParams(dimension_semantics=sem, vmem_limit_bytes=VMEM_LIMIT)


def _dot(a, b):
    return jnp.dot(a, b, preferred_element_type=F32)


def _dot_nt(a, b):
    return lax.dot_general(a, b, (((1,), (1,)), ((), ())), preferred_element_type=F32)


def _dot_tn(a, b):
    return lax.dot_general(a, b, (((0,), (0,)), ((), ())), preferred_element_type=F32)


def _sigmoid(x):
    return 1.0 / (1.0 + jnp.exp(-x))


def _modulate(x, g, shift, scale):
    ms = jnp.mean(x * x, axis=-1, keepdims=True)
    return (x * lax.rsqrt(ms + NORM_EPS) * g) * (1.0 + scale) + shift


def _seg_sum(x, ones_bd):
    hi = x.astype(BF16)
    lo = (x - hi.astype(F32)).astype(BF16)
    return _dot(hi, ones_bd) + _dot(lo, ones_bd)


def _with_ones(v):
    one = jnp.ones((v.shape[0], HEAD_DIM), F32)
    return jnp.concatenate([v[:, :HEAD_DIM], one, v[:, HEAD_DIM:], one], axis=-1).astype(BF16)


def _prob(s, m):
    return jnp.exp((s - m).astype(BF16))


def _head_rmsnorm(x, ones_bd):
    ms = _seg_sum(x * x, ones_bd) * (1.0 / HEAD_DIM)
    return x * lax.rsqrt(ms + NORM_EPS)


def _ada_kernel(c_ref, w_ref, b_ref, o_ref):
    c = c_ref[...]
    s = (c * _sigmoid(c)).astype(BF16)
    o_ref[...] = _dot(s, w_ref[...].astype(BF16)) + b_ref[...]


def _ada_call(cc, w_ada, b_ada):
    L, D, N = w_ada.shape
    tn = 1024
    return pl.pallas_call(
        _ada_kernel,
        out_shape=jax.ShapeDtypeStruct((L, 8, N), F32),
        grid=(L, N // tn),
        in_specs=[
            pl.BlockSpec((8, D), lambda l, j: (0, 0)),
            pl.BlockSpec((None, D, tn), lambda l, j: (l, 0, j)),
            pl.BlockSpec((None, 1, tn), lambda l, j: (l, 0, j)),
        ],
        out_specs=pl.BlockSpec((None, 8, tn), lambda l, j: (l, 0, j)),
        compiler_params=_cparams(("arbitrary", "arbitrary")),
        name="adaln",
    )(cc, w_ada, b_ada.reshape(L, 1, N))


def _ffn_kernel(x_ref, mod_ref, g_ref, wg_ref, wu_ref, wd_ref, o_ref, h_sc, acc_sc, *, mi):
    j = pl.program_id(1)

    @pl.when(j == 0)
    def _():
        h = _modulate(x_ref[...], g_ref[...], mod_ref[mi:mi + 1, :], mod_ref[mi + 1:mi + 2, :])
        h_sc[...] = h.astype(BF16)
        acc_sc[...] = jnp.zeros_like(acc_sc)

    h = h_sc[...]
    gate = _dot(h, wg_ref[...])
    up = _dot(h, wu_ref[...])
    a = (gate * _sigmoid(gate) * up).astype(BF16)
    acc_sc[...] += _dot(a, wd_ref[...])

    @pl.when(j == pl.num_programs(1) - 1)
    def _():
        o_ref[...] = x_ref[...] + 0.5 * mod_ref[mi + 2:mi + 3, :] * acc_sc[...]


def _ffn_call(xs, mods, g, wg, wu, wd, l, mi, n_rows, rows_per_seq, n_batch, tm, tf):
    D = xs.shape[1]
    F = wg.shape[2]
    tps = rows_per_seq // tm
    return pl.pallas_call(
        functools.partial(_ffn_kernel, mi=mi),
        out_shape=jax.ShapeDtypeStruct((n_rows, D), F32),
        grid=(n_rows // tm, F // tf),
        in_specs=[
            pl.BlockSpec((tm, D), lambda i, j: (i, 0)),
            pl.BlockSpec((None, N_MOD, D), lambda i, j: (jnp.minimum(i // tps, n_batch), 0, 0)),
            pl.BlockSpec((1, D), lambda i, j: (0, 0)),
            pl.BlockSpec((None, D, tf), lambda i, j: (l, 0, j)),
            pl.BlockSpec((None, D, tf), lambda i, j: (l, 0, j)),
            pl.BlockSpec((None, tf, D), lambda i, j: (l, j, 0)),
        ],
        out_specs=pl.BlockSpec((tm, D), lambda i, j: (i, 0)),
        scratch_shapes=[pltpu.VMEM((tm, D), BF16), pltpu.VMEM((tm, D), F32)],
        compiler_params=_cparams(("arbitrary", "arbitrary")),
        name="ffn",
    )(xs, mods, g, wg, wu, wd)


def _win_kernel(x_ref, mod_ref, g_ref, w_ref, o_ref, h_sc):
    @pl.when(pl.program_id(1) == 0)
    def _():
        h = _modulate(x_ref[...], g_ref[...], mod_ref[3:4, :], mod_ref[4:5, :])
        h_sc[...] = h.astype(BF16)

    o_ref[...] = _dot(h_sc[...], w_ref[...])


def _win_call(xs, mods, g, w_in, l, rows_per_seq, n_batch, tm, tn):
    R, D = xs.shape
    N = w_in.shape[2]
    tps = rows_per_seq // tm
    return pl.pallas_call(
        _win_kernel,
        out_shape=jax.ShapeDtypeStruct((R, N), F32),
        grid=(R // tm, N // tn),
        in_specs=[
            pl.BlockSpec((tm, D), lambda i, j: (i, 0)),
            pl.BlockSpec((None, N_MOD, D), lambda i, j: (jnp.minimum(i // tps, n_batch), 0, 0)),
            pl.BlockSpec((1, D), lambda i, j: (0, 0)),
            pl.BlockSpec((None, D, tn), lambda i, j: (l, 0, j)),
        ],
        out_specs=pl.BlockSpec((tm, tn), lambda i, j: (i, j)),
        scratch_shapes=[pltpu.VMEM((tm, D), BF16)],
        compiler_params=_cparams(("arbitrary", "arbitrary")),
        name="in_proj",
    )(xs, mods, g, w_in)


def _wout_kernel(x_ref, oa_ref, ob_ref, oc_ref, od_ref, mod_ref, w_ref, o_ref):
    acc = _dot(oa_ref[...], w_ref[0 * GROUP_W:1 * GROUP_W, :])
    acc += _dot(ob_ref[...], w_ref[1 * GROUP_W:2 * GROUP_W, :])
    acc += _dot(oc_ref[...], w_ref[2 * GROUP_W:3 * GROUP_W, :])
    acc += _dot(od_ref[...], w_ref[3 * GROUP_W:4 * GROUP_W, :])
    o_ref[...] = x_ref[...] + mod_ref[5:6, :] * acc


def _wout_call(xs, outs, mods, w_out, l, n_rows, rows_per_seq, n_batch, tm):
    D = xs.shape[1]
    tps = rows_per_seq // tm
    ospec = pl.BlockSpec((tm, GROUP_W), lambda i: (i, 0))
    return pl.pallas_call(
        _wout_kernel,
        out_shape=jax.ShapeDtypeStruct((n_rows, D), F32),
        grid=(n_rows // tm,),
        in_specs=[
            pl.BlockSpec((tm, D), lambda i: (i, 0)),
            ospec, ospec, ospec, ospec,
            pl.BlockSpec((None, N_MOD, D), lambda i: (jnp.minimum(i // tps, n_batch), 0, 0)),
            pl.BlockSpec((None, D, D), lambda i: (l, 0, 0)),
        ],
        out_specs=pl.BlockSpec((tm, D), lambda i: (i, 0)),
        compiler_params=_cparams(("arbitrary",)),
        name="out_proj",
    )(xs, *outs, mods, w_out)


def _na_case_reps(n_qb):
    return (0, 1, 2, n_qb - 2, n_qb - 1)


def _na_start(p, n_qb):
    return jnp.clip(p - 2, 0, n_qb - NA_KBLKS)


def _na_bias_table(rpb, n_qb):
    H = rpb.shape[0]
    rows = 2 * n_qb
    qc = np.arange(GRID_W)[:, None]
    kc = np.arange(GRID_W)[None, :]
    cs = np.clip(qc - NA_WIN_C // 2, 0, GRID_W - NA_WIN_C)
    okc = (kc >= cs) & (kc < cs + NA_WIN_C)
    dc = np.where(okc, kc - qc + NA_WIN_C - 1, 0)
    rp = rpb.astype(F32).reshape(H, 2 * NA_WIN_R - 1, 2 * NA_WIN_C - 1)
    blocks = jnp.where(jnp.asarray(okc)[None, None], jnp.take(rp, jnp.asarray(dc), axis=2), NEG_INF)
    blocks = blocks.astype(BF16)
    neg = jnp.full((H, GRID_W, GRID_W), NEG_INF, BF16)
    cases = []
    for p in _na_case_reps(n_qb):
        start = min(max(p - 2, 0), n_qb - NA_KBLKS)
        qrows = []
        for qr in range(2):
            qa = 2 * p + qr
            rs = min(max(qa - NA_WIN_R // 2, 0), rows - NA_WIN_R)
            krow = []
            for kr in range(2 * NA_KBLKS):
                ka = 2 * start + kr
                krow.append(blocks[:, ka - qa + NA_WIN_R - 1] if rs <= ka < rs + NA_WIN_R else neg)
            qrows.append(jnp.concatenate(krow, axis=-1))
        cases.append(jnp.concatenate(qrows, axis=-2))
    return jnp.stack(cases, axis=0)


def _na_kernel(q_ref, k_ref, v_ref, kc_ref, vc_ref, bias_ref, qg_ref, kg_ref, ones_ref, o_ref,
               kn_sc, vn_sc, kcn_sc, vcn_sc, *, n_qb, seq, ctx):
    p = pl.program_id(2)
    ones_bd = ones_ref[...]
    scale = HEAD_DIM ** -0.5

    @pl.when(p == 0)
    def _():
        kg = kg_ref[...]
        rows = 512

        def body(i, carry):
            sl = pl.ds(pl.multiple_of(i * rows, rows), rows)
            kn_sc[sl, :] = (_head_rmsnorm(k_ref[sl, :], ones_bd) * kg).astype(BF16)
            vn_sc[sl, :] = _with_ones(v_ref[sl, :])
            return carry

        lax.fori_loop(0, seq // rows, body, 0)
        kcn_sc[...] = (_head_rmsnorm(kc_ref[...], ones_bd) * kg).astype(BF16)
        vcn_sc[...] = _with_ones(vc_ref[...])

    nq = q_ref.shape[0] // QBLK
    n_lat = n_qb // nq
    q = (_head_rmsnorm(q_ref[...], ones_bd) * (qg_ref[...] * scale)).astype(BF16)
    kcn = kcn_sc[...]
    vcn = vcn_sc[...]
    chains = [(qb, h) for qb in range(nq) for h in range(2)]
    hsl = lambda h: slice(h * HEAD_DIM, (h + 1) * HEAD_DIM)
    qh = [q[qb * QBLK:(qb + 1) * QBLK, hsl(h)] for qb, h in chains]
    s_c = [_dot_nt(qh[c], kcn[:, hsl(h)]) for c, (qb, h) in enumerate(chains)]
    m_c = [jnp.max(t, axis=-1, keepdims=True) for t in s_c]

    vsl = lambda h: slice(h * LANE, (h + 1) * LANE)
    vca = [vcn[:, vsl(h)] for h in range(2)]

    def finish(oa):
        rows = [jnp.concatenate([oa[qb * 2 + h][:, :HEAD_DIM] / oa[qb * 2 + h][:, HEAD_DIM:] for h in range(2)],
                                axis=-1) for qb in range(nq)]
        o_ref[...] = jnp.concatenate(rows, axis=0).astype(o_ref.dtype)

    @pl.when(p < n_lat)
    def _():
        starts = [_na_start(p * nq + qb, n_qb) for qb in range(nq)]
        cases = [p * nq + qb - starts[qb] for qb in range(nq)]
        sls = [pl.ds(pl.multiple_of(st * QBLK, QBLK), NA_KBLKS * QBLK) for st in starts]
        kw = [kn_sc[sl, :] for sl in sls]
        vw = [vn_sc[sl, :] for sl in sls]
        s_n = [_dot_nt(qh[c], kw[qb][:, hsl(h)]) + bias_ref[cases[qb], h].astype(F32)
               for c, (qb, h) in enumerate(chains)]
        m = [jnp.maximum(jnp.max(s_n[c], axis=-1, keepdims=True), m_c[c]) for c in range(len(chains))]
        p_n = [_prob(s_n[c], m[c]) for c in range(len(chains))]
        p_c = [_prob(s_c[c], m[c]) for c in range(len(chains))]
        finish([_dot(p_n[c], vw[qb][:, vsl(h)]) + _dot(p_c[c], vca[h])
                for c, (qb, h) in enumerate(chains)])

    @pl.when(p >= n_lat)
    def _():
        finish([_dot(_prob(s_c[c], m_c[c]), vca[h]) for c, (qb, h) in enumerate(chains)])


def _na_call(proj, bias, qg, kg, ones_bd, n_batch, seq, ctx, want_ctx):
    R = proj.shape[0]
    nq = 2
    n_qb = seq // QBLK
    n_lat, n_ctx = n_qb // nq, ctx // (nq * QBLK)
    steps = n_lat + (n_ctx if want_ctx else 0)
    ctx_kb = n_batch * seq // ctx

    def qrow(b, p):
        return jnp.where(p < n_lat, b * n_lat + p, n_batch * n_lat + b * n_ctx + (p - n_lat))

    cq, ck, cv = COL_AQ // LANE, COL_AK // LANE, COL_AV // LANE
    return pl.pallas_call(
        functools.partial(_na_kernel, n_qb=n_qb, seq=seq, ctx=ctx),
        out_shape=jax.ShapeDtypeStruct((R, GROUP_W), BF16),
        grid=(n_batch, NA_HEADS // 2, steps),
        in_specs=[
            pl.BlockSpec((nq * QBLK, LANE), lambda b, hp, p: (qrow(b, p), cq + hp)),
            pl.BlockSpec((seq, LANE), lambda b, hp, p: (b, ck + hp)),
            pl.BlockSpec((seq, LANE), lambda b, hp, p: (b, cv + hp)),
            pl.BlockSpec((ctx, LANE), lambda b, hp, p: (ctx_kb + b, ck + hp)),
            pl.BlockSpec((ctx, LANE), lambda b, hp, p: (ctx_kb + b, cv + hp)),
            pl.BlockSpec((5, 2, QBLK, NA_KBLKS * QBLK), lambda b, hp, p: (0, hp, 0, 0)),
            pl.BlockSpec((1, LANE), lambda b, hp, p: (0, 0)),
            pl.BlockSpec((1, LANE), lambda b, hp, p: (0, 0)),
            pl.BlockSpec((LANE, LANE), lambda b, hp, p: (0, 0)),
        ],
        out_specs=pl.BlockSpec((nq * QBLK, LANE), lambda b, hp, p: (qrow(b, p), hp)),
        scratch_shapes=[pltpu.VMEM((seq, LANE), BF16), pltpu.VMEM((seq, 2 * LANE), BF16),
                        pltpu.VMEM((ctx, LANE), BF16), pltpu.VMEM((ctx, 2 * LANE), BF16)],
        compiler_params=_cparams(("arbitrary", "arbitrary", "arbitrary")),
        name="mixer_a",
    )(proj, proj, proj, proj, proj, bias, qg, kg, ones_bd)


def _rope(x, cos, sin_signed, first_half):
    rot = jnp.where(first_half, pltpu.roll(x, LANE - 16, 1), pltpu.roll(x, 16, 1))
    return x * cos + rot * sin_signed


def _wa_kernel(sink_ref, q_ref, k_ref, v_ref, kc_ref, vc_ref, cosk_ref, sin_k_ref, cosq_ref, sinq_ref,
               qg_ref, kg_ref, ones_ref, wmask_ref, o_ref, kn_sc, vn_sc, kcn_sc, vcn_sc, *, n_qb, seq, ctx):
    p = pl.program_id(1)
    ones_bd = ones_ref[...]
    scale = HEAD_DIM ** -0.5
    lane = lax.broadcasted_iota(jnp.int32, (1, LANE), 1)
    first_half = (lane % 32) < 16

    @pl.when(p == 0)
    def _():
        kg = kg_ref[...]
        rows = 512

        def body(i, carry):
            sl = pl.ds(pl.multiple_of(i * rows, rows), rows)
            kn = _head_rmsnorm(k_ref[sl, :], ones_bd) * kg
            kn_sc[sl, :] = _rope(kn, cosk_ref[sl, :], sin_k_ref[sl, :], first_half).astype(BF16)
            vn_sc[sl, :] = _with_ones(v_ref[sl, :])
            return carry

        lax.fori_loop(0, seq // rows, body, 0)
        kcn_sc[...] = (_head_rmsnorm(kc_ref[...], ones_bd) * kg).astype(BF16)
        vcn_sc[...] = _with_ones(vc_ref[...])

    nq = q_ref.shape[0] // QBLK
    n_lat = n_qb // nq
    qg = qg_ref[...] * scale
    is_lat = p < n_lat
    cosq = jnp.where(is_lat, cosq_ref[...], 1.0)
    sinq = jnp.where(is_lat, sinq_ref[...], 0.0)
    qh = []
    for c in range(4):
        qn = _head_rmsnorm(q_ref[:, c * LANE:(c + 1) * LANE], ones_bd) * qg
        qn = _rope(qn, cosq, sinq, first_half).astype(BF16)
        qh += [qn[:, :HEAD_DIM], qn[:, HEAD_DIM:]]
    kcn = kcn_sc[...]
    vcn = vcn_sc[...]
    chains = [(qb, kh) for qb in range(nq) for kh in range(WA_KV_HEADS)]
    nch = range(len(chains))
    hsl = lambda kh: slice(kh * HEAD_DIM, (kh + 1) * HEAD_DIM)
    qs = [jnp.concatenate([qh[kh * WA_GROUP + g][qb * QBLK:(qb + 1) * QBLK, :] for g in range(WA_GROUP)], axis=0)
          for qb, kh in chains]
    sk = [jnp.concatenate([jnp.full((QBLK, 1), sink_ref[kh * WA_GROUP + g], F32) for g in range(WA_GROUP)],
                          axis=0) for qb, kh in chains]
    s_c = [_dot_nt(qs[c], kcn[:, hsl(chains[c][1])]) for c in nch]
    m_c = [jnp.maximum(jnp.max(s_c[c], axis=-1, keepdims=True), sk[c]) for c in nch]

    vsl = lambda kh: slice(kh * LANE, (kh + 1) * LANE)
    vca = [vcn[:, vsl(kh)] for kh in range(WA_KV_HEADS)]

    def finish(oa, sink_w):
        rows = []
        for qb in range(nq):
            heads = []
            for kh in range(WA_KV_HEADS):
                c = qb * WA_KV_HEADS + kh
                oc = oa[c][:, :HEAD_DIM] / (oa[c][:, HEAD_DIM:] + sink_w[c])
                heads += [oc[g * QBLK:(g + 1) * QBLK, :] for g in range(WA_GROUP)]
            rows.append(jnp.concatenate(heads, axis=-1))
        o_ref[...] = jnp.concatenate(rows, axis=0).astype(o_ref.dtype)

    @pl.when(is_lat)
    def _():
        blk = [p * nq + qb for qb in range(nq)]
        ws = [jnp.clip((bq - 1) * QBLK, 0, seq - 3 * QBLK) for bq in blk]
        sls = [pl.ds(pl.multiple_of(w, QBLK), 3 * QBLK) for w in ws]
        kw = [kn_sc[sl, :] for sl in sls]
        vw = [vn_sc[sl, :] for sl in sls]
        case = [jnp.where(bq == 0, 0, jnp.where(bq == n_qb - 1, 2, 1)) for bq in blk]
        msk = [jnp.concatenate([wmask_ref[cs]] * WA_GROUP, axis=0) for cs in case]
        s_w = [_dot_nt(qs[c], kw[chains[c][0]][:, hsl(chains[c][1])]) + msk[chains[c][0]] for c in nch]
        m = [jnp.maximum(jnp.max(s_w[c], axis=-1, keepdims=True), m_c[c]) for c in nch]
        p_w = [_prob(s_w[c], m[c]) for c in nch]
        p_c = [_prob(s_c[c], m[c]) for c in nch]
        finish([_dot(p_w[c], vw[chains[c][0]][:, vsl(chains[c][1])]) + _dot(p_c[c], vca[chains[c][1]])
                for c in nch], [jnp.exp(sk[c] - m[c]) for c in nch])

    @pl.when(jnp.logical_not(is_lat))
    def _():
        finish([_dot(_prob(s_c[c], m_c[c]), vca[chains[c][1]]) for c in nch],
               [jnp.exp(sk[c] - m_c[c]) for c in nch])


def _wa_mask_table():
    i = np.arange(QBLK)[:, None]
    j = np.arange(3 * QBLK)[None, :]
    shifts = (0, QBLK, 2 * QBLK)
    return jnp.asarray(np.stack([np.where(np.abs(j - sh - i) <= WA_WINDOW, 0.0, NEG_INF) for sh in shifts]), F32)


def _wa_call(proj, sink, cos_t, sin_t, qg, kg, ones_bd, n_batch, seq, ctx, want_ctx):
    R = proj.shape[0]
    nq = 2
    n_qb = seq // QBLK
    n_lat, n_ctx = n_qb // nq, ctx // (nq * QBLK)
    steps = n_lat + (n_ctx if want_ctx else 0)
    ctx_kb = n_batch * seq // ctx

    def qrow(b, p):
        return jnp.where(p < n_lat, b * n_lat + p, n_batch * n_lat + b * n_ctx + (p - n_lat))

    ck, cv = COL_CK // LANE, COL_CV // LANE
    return pl.pallas_call(
        functools.partial(_wa_kernel, n_qb=n_qb, seq=seq, ctx=ctx),
        out_shape=jax.ShapeDtypeStruct((R, GROUP_W), BF16),
        grid=(n_batch, steps),
        in_specs=[
            pl.BlockSpec(memory_space=pltpu.SMEM),
            pl.BlockSpec((nq * QBLK, GROUP_W), lambda b, p: (qrow(b, p), COL_CQ // GROUP_W)),
            pl.BlockSpec((seq, LANE), lambda b, p: (b, ck)),
            pl.BlockSpec((seq, LANE), lambda b, p: (b, cv)),
            pl.BlockSpec((ctx, LANE), lambda b, p: (ctx_kb + b, ck)),
            pl.BlockSpec((ctx, LANE), lambda b, p: (ctx_kb + b, cv)),
            pl.BlockSpec((seq, LANE), lambda b, p: (0, 0)),
            pl.BlockSpec((seq, LANE), lambda b, p: (0, 0)),
            pl.BlockSpec((nq * QBLK, LANE), lambda b, p: (jnp.minimum(p, n_lat - 1), 0)),
            pl.BlockSpec((nq * QBLK, LANE), lambda b, p: (jnp.minimum(p, n_lat - 1), 0)),
            pl.BlockSpec((1, LANE), lambda b, p: (0, 0)),
            pl.BlockSpec((1, LANE), lambda b, p: (0, 0)),
            pl.BlockSpec((LANE, LANE), lambda b, p: (0, 0)),
            pl.BlockSpec((3, QBLK, 3 * QBLK), lambda b, p: (0, 0, 0)),
        ],
        out_specs=pl.BlockSpec((nq * QBLK, GROUP_W), lambda b, p: (qrow(b, p), 0)),
        scratch_shapes=[pltpu.VMEM((seq, LANE), BF16), pltpu.VMEM((seq, 2 * LANE), BF16),
                        pltpu.VMEM((ctx, LANE), BF16), pltpu.VMEM((ctx, 2 * LANE), BF16)],
        compiler_params=_cparams(("arbitrary", "arbitrary")),
        name="mixer_c",
    )(sink, proj, proj, proj, proj, proj, cos_t, sin_t, cos_t, sin_t, qg, kg, ones_bd, _wa_mask_table())


def _rw_prep_kernel(x_ref, prev_ref, next_ref, lr_ref, conv_ref, wlr_ref, w0_ref, a0_ref, kk_ref, ka_ref,
                    ones_ref, o_ref, *, tiles_lat, tiles_ctx, n_lat_tiles):
    i = pl.program_id(0)
    tm = x_ref.shape[0]
    in_lat = i < n_lat_tiles
    first = jnp.where(in_lat, i % tiles_lat == 0, (i - n_lat_tiles) % tiles_ctx == 0)
    last = jnp.where(in_lat, i % tiles_lat == tiles_lat - 1, (i - n_lat_tiles) % tiles_ctx == tiles_ctx - 1)
    x = x_ref[...]
    prev_row = jnp.where(first, 0.0, prev_ref[7:8, :])
    next_row = jnp.where(last, 0.0, next_ref[0:1, :])
    row = lax.broadcasted_iota(jnp.int32, (tm, 1), 0)
    x_prev = jnp.where(row == 0, prev_row, pltpu.roll(x, 1, 0))
    x_next = jnp.where(row == tm - 1, next_row, pltpu.roll(x, tm - 1, 0))
    cw = conv_ref[...]
    y = x_prev * cw[0:1, :] + x * cw[1:2, :] + x_next * cw[2:3, :]
    r = y[:, 0:GROUP_W]
    k = y[:, GROUP_W:2 * GROUP_W]
    v = y[:, 2 * GROUP_W:3 * GROUP_W]

    lr = lr_ref[...]
    lane = lax.broadcasted_iota(jnp.int32, (1, lr.shape[1]), 1)
    c1 = RW_DECAY_RANK
    c2 = c1 + RW_ICLR_RANK
    c3 = c2 + RW_GATE_RANK
    act = jnp.where(lane < c1, jnp.tanh(lr),
                    jnp.where(lane < c2, lr, jnp.where(lane < c3, _sigmoid(lr), 0.0)))
    up = _dot(act.astype(BF16), wlr_ref[...])

    ones_bd = ones_ref[...]
    kk = k * kk_ref[...]
    nrm = jnp.sqrt(_seg_sum(kk * kk, ones_bd))
    kk = kk / jnp.maximum(nrm, 1e-12)

    o_ref[:, PB_R * GROUP_W:(PB_R + 1) * GROUP_W] = r
    o_ref[:, PB_V * GROUP_W:(PB_V + 1) * GROUP_W] = v
    o_ref[:, PB_KK * GROUP_W:(PB_KK + 1) * GROUP_W] = kk
    o_ref[:, PB_G * GROUP_W:(PB_G + 1) * GROUP_W] = up[:, 4 * GROUP_W:5 * GROUP_W]
    ka = ka_ref[...]
    for d in range(2):
        z = w0_ref[d:d + 1, :] + up[:, d * GROUP_W:(d + 1) * GROUP_W]
        sp = jnp.maximum(-z, 0.0) + jnp.log(1.0 + jnp.exp(-jnp.abs(z)))
        lw = -jnp.exp(-sp - 0.5)
        a = _sigmoid(a0_ref[d:d + 1, :] + up[:, (2 + d) * GROUP_W:(3 + d) * GROUP_W])
        kd = k * (1.0 + (a - 1.0) * ka)
        base = 3 + 3 * d
        o_ref[:, base * GROUP_W:(base + 1) * GROUP_W] = lw
        o_ref[:, (base + 1) * GROUP_W:(base + 2) * GROUP_W] = a
        o_ref[:, (base + 2) * GROUP_W:(base + 3) * GROUP_W] = kd


def _rw_prep_call(proj, conv_w, wlr, w0, a0, k_k, k_a, ones512, n_batch, seq, ctx, tm):
    R = proj.shape[0]
    n_lat_tiles = n_batch * seq // tm
    nb8 = R // 8
    t8 = tm // 8
    wide = 3 * GROUP_W
    cb = COL_BR // wide
    return pl.pallas_call(
        functools.partial(_rw_prep_kernel, tiles_lat=seq // tm, tiles_ctx=ctx // tm, n_lat_tiles=n_lat_tiles),
        out_shape=jax.ShapeDtypeStruct((R, PB_N * GROUP_W), F32),
        grid=(R // tm,),
        in_specs=[
            pl.BlockSpec((tm, wide), lambda i: (i, cb)),
            pl.BlockSpec((8, wide), lambda i: (jnp.maximum(i * t8 - 1, 0), cb)),
            pl.BlockSpec((8, wide), lambda i: (jnp.minimum((i + 1) * t8, nb8 - 1), cb)),
            pl.BlockSpec((tm, 256), lambda i: (i, COL_BLR // 256)),
            pl.BlockSpec((3, wide), lambda i: (0, 0)),
            pl.BlockSpec((256, 5 * GROUP_W), lambda i: (0, 0)),
            pl.BlockSpec((2, GROUP_W), lambda i: (0, 0)),
            pl.BlockSpec((2, GROUP_W), lambda i: (0, 0)),
            pl.BlockSpec((1, GROUP_W), lambda i: (0, 0)),
            pl.BlockSpec((1, GROUP_W), lambda i: (0, 0)),
            pl.BlockSpec((GROUP_W, GROUP_W), lambda i: (0, 0)),
        ],
        out_specs=pl.BlockSpec((tm, PB_N * GROUP_W), lambda i: (i, 0)),
        compiler_params=_cparams(("arbitrary",)),
        name="mixer_b_prep",
    )(proj, proj, proj, proj, conv_w, wlr, w0, a0, k_k, k_a, ones512)


RW_SPLIT = 1
_NN = ((1,), (0,))
_NT = ((1,), (1,))
_TN = ((0,), (0,))


def _split(x, n):
    parts = []
    for _ in range(n):
        p = x.astype(BF16)
        parts.append(p)
        x = x - p.astype(F32)
    return parts


def _sdot(a, b, dims):
    n = max(len(a), len(b))
    acc = None
    for i, ai in enumerate(a):
        for j, bj in enumerate(b):
            if i + j < n:
                t = lax.dot_general(ai, bj, (dims, ((), ())), preferred_element_type=F32)
                acc = t if acc is None else acc + t
    return acc


def _rw_chunks(at, bt, kt, rt, v, gam, s0, strict, incl):
    n = len(at)
    ids = range(n)
    C = at[0].shape[0]
    sp = lambda t: _split(t, RW_SPLIT)
    b1 = lambda t: [t.astype(BF16)]
    ar = [b1(jnp.concatenate([at[i], rt[i]], axis=0)) for i in ids]
    bk = [sp(jnp.concatenate([bt[i], kt[i]], axis=0)) for i in ids]
    g4 = [_sdot(ar[i], bk[i][:1], _NT) for i in ids]
    a_ab = [jnp.where(strict[i], g4[i][:C, :C], 0.0) for i in ids]
    a_ak = [jnp.where(strict[i], g4[i][:C, C:], 0.0) for i in ids]
    a_rb = [jnp.where(incl[i], g4[i][C:, :C], 0.0) for i in ids]
    a_rk = [jnp.where(incl[i], g4[i][C:, C:], 0.0) for i in ids]
    wv = [_sdot(b1(a_ak[i]), b1(v[i]), _NN) for i in ids]
    x = [jnp.concatenate([at[i], wv[i]], axis=1) for i in ids]

    row = lax.broadcasted_iota(jnp.int32, (C, C), 0)
    col = lax.broadcasted_iota(jnp.int32, (C, C), 1)
    eye = (row == col).astype(F32)
    blk = 16
    same = (row // blk) == (col // blk)
    p = [jnp.where(same, a_ab[i], 0.0) for i in ids]
    m = [eye + p[i] for i in ids]
    for _ in range(int(math.log2(blk)) - 1):
        pb_ = [b1(p[i]) for i in ids]
        p = [_sdot(pb_[i], pb_[i], _NN) for i in ids]
        m = [m[i] + _sdot(b1(m[i]), b1(p[i]), _NN) for i in ids]
    while blk < C:
        wider = (row // (2 * blk)) == (col // (2 * blk))
        join = wider & jnp.logical_not(same)
        mb = [b1(m[i]) for i in ids]
        t = [_sdot(b1(jnp.where(join, a_ab[i], 0.0)), mb[i], _NN) for i in ids]
        m = [m[i] + _sdot(mb[i], b1(t[i]), _NN) for i in ids]
        same, blk = wider, 2 * blk
    x = [_sdot(b1(m[i]), b1(x[i]), _NN) for i in ids]

    s0b = [sp(s0[i]) for i in ids]
    u = [x[i][:, HEAD_DIM:] + _sdot(sp(x[i][:, :HEAD_DIM]), s0b[i], _NT) for i in ids]
    uv = [sp(jnp.concatenate([u[i], v[i]], axis=0)) for i in ids]
    y = [_sdot(b1(jnp.concatenate([a_rb[i], a_rk[i]], axis=1)), uv[i][:1], _NN)
         + _sdot(sp(rt[i]), s0b[i], _NT) for i in ids]
    s1 = [(s0[i] + _sdot(uv[i], bk[i], _TN)) * gam[i] for i in ids]
    return y, s1


def _rw_scan_kernel(*refs, n_batch, nc):
    n_in = 12 * n_batch
    yfl_ref, ybl_ref, yfc_ref, ybc_ref, s_sc = refs[n_in:]
    s = pl.program_id(0)

    @pl.when(s == 0)
    def _():
        s_sc[...] = jnp.zeros_like(s_sc)

    C = refs[0].shape[0]
    row = lax.broadcasted_iota(jnp.int32, (C, C), 0)
    col = lax.broadcasted_iota(jnp.int32, (C, C), 1)
    chains = dict(at=[], bt=[], kt=[], rt=[], v=[], gam=[], s0=[], strict=[], incl=[])
    for b in range(n_batch):
        for d, reverse in enumerate((False, True)):
            r_ref, v_ref, kk_ref, lw_ref, a_ref, k_ref = refs[12 * b + 6 * d:12 * b + 6 * d + 6]
            incl, strict = (row <= col, row < col) if reverse else (row >= col, row > col)
            lw = lw_ref[...]
            kk = kk_ref[...]
            v = v_ref[...]
            cum = _sdot([incl.astype(BF16)], _split(lw, 3), _NN)
            e_l = jnp.exp(cum)
            e_n = jnp.exp(-cum)
            at = -(kk * jnp.exp(cum - lw))
            bt = kk * a_ref[...] * e_n
            kt = k_ref[...] * e_n
            rt = r_ref[...] * e_l
            gam = e_l[0:1, :] if reverse else e_l[C - 1:C, :]
            for h in range(RW_HEADS):
                hs = slice(h * HEAD_DIM, (h + 1) * HEAD_DIM)
                for name, val in (("at", at), ("bt", bt), ("kt", kt), ("rt", rt), ("v", v), ("gam", gam)):
                    chains[name].append(val[:, hs])
                chains["s0"].append(s_sc[b, d, h])
                chains["strict"].append(strict)
                chains["incl"].append(incl)
    y, s1 = _rw_chunks(**chains)
    ys = {}
    for b in range(n_batch):
        for d in range(2):
            base = (b * 2 + d) * RW_HEADS
            for h in range(RW_HEADS):
                s_sc[b, d, h] = s1[base + h]
            ys[b, d] = jnp.concatenate(y[base:base + RW_HEADS], axis=-1)

    @pl.when(s < nc)
    def _():
        for b in range(n_batch):
            yfc_ref[b] = ys[b, 0]
            ybc_ref[b] = ys[b, 1]

    @pl.when(s >= nc)
    def _():
        for b in range(n_batch):
            yfl_ref[b] = ys[b, 0]
            ybl_ref[b] = ys[b, 1]


def _rw_scan_call(prep, n_batch, seq, ctx):
    C = RW_CHUNK
    nc, nl = ctx // C, seq // C
    ctx_base = n_batch * nl

    def blk_f(b, s):
        return jnp.where(s < nc, ctx_base + b * nc + s, b * nl + (s - nc))

    def blk_b(b, s):
        return jnp.where(s < nc, ctx_base + b * nc + (nc - 1 - s), b * nl + (nl - 1 - (s - nc)))

    def col(blk, b, cb):
        return pl.BlockSpec((C, GROUP_W), lambda s: (blk(b, s), cb))

    in_specs = []
    for b in range(n_batch):
        in_specs += [col(blk_f, b, c) for c in (PB_R, PB_V, PB_KK, 3, 4, 5)]
        in_specs += [col(blk_b, b, c) for c in (PB_R, PB_V, PB_KK, 6, 7, 8)]
    lat = jax.ShapeDtypeStruct((n_batch, seq, GROUP_W), F32)
    cx = jax.ShapeDtypeStruct((n_batch, ctx, GROUP_W), F32)
    blk3 = (n_batch, C, GROUP_W)
    out_specs = (
        pl.BlockSpec(blk3, lambda s: (0, jnp.maximum(s - nc, 0), 0)),
        pl.BlockSpec(blk3, lambda s: (0, nl - 1 - jnp.maximum(s - nc, 0), 0)),
        pl.BlockSpec(blk3, lambda s: (0, jnp.minimum(s, nc - 1), 0)),
        pl.BlockSpec(blk3, lambda s: (0, nc - 1 - jnp.minimum(s, nc - 1), 0)),
    )
    yfl, ybl, yfc, ybc = pl.pallas_call(
        functools.partial(_rw_scan_kernel, n_batch=n_batch, nc=nc),
        out_shape=(lat, lat, cx, cx),
        grid=(nc + nl,),
        in_specs=in_specs,
        out_specs=out_specs,
        scratch_shapes=[pltpu.VMEM((n_batch, 2, RW_HEADS, HEAD_DIM, HEAD_DIM), F32)],
        compiler_params=_cparams(("arbitrary",)),
        name="mixer_b_scan",
    )(*([prep] * (12 * n_batch)))
    flat = lambda a, c: jnp.concatenate([a.reshape(n_batch * seq, GROUP_W), c.reshape(n_batch * ctx, GROUP_W)], 0)
    return flat(yfl, yfc), flat(ybl, ybc)


def _rw_readout_kernel(r_ref, v_ref, k0_ref, k1_ref, g_ref, yf_ref, yb_ref, rk_ref, gw_ref, gb_ref, ones_ref,
                       o_ref):
    ones_bd = ones_ref[...]
    v = v_ref[...]
    bonus = _seg_sum(r_ref[...] * (k0_ref[...] + k1_ref[...]) * rk_ref[...], ones_bd)
    y = yf_ref[...] + yb_ref[...] + bonus * v
    mu = _seg_sum(y, ones_bd) * (1.0 / HEAD_DIM)
    yc = y - mu
    var = _seg_sum(yc * yc, ones_bd) * (1.0 / HEAD_DIM)
    yn = yc * lax.rsqrt(var + RW_GN_EPS)
    o_ref[...] = ((yn * gw_ref[...] + gb_ref[...]) * g_ref[...]).astype(o_ref.dtype)


def _rw_readout_call(prep, yf, yb, r_k, gn_w, gn_b, ones512, tm):
    R = prep.shape[0]
    col = lambda cb: pl.BlockSpec((tm, GROUP_W), lambda i: (i, cb))
    vec = pl.BlockSpec((1, GROUP_W), lambda i: (0, 0))
    return pl.pallas_call(
        _rw_readout_kernel,
        out_shape=jax.ShapeDtypeStruct((R, GROUP_W), BF16),
        grid=(R // tm,),
        in_specs=[col(PB_R), col(PB_V), col(5), col(8), col(PB_G), col(0), col(0), vec, vec, vec,
                  pl.BlockSpec((GROUP_W, GROUP_W), lambda i: (0, 0))],
        out_specs=col(0),
        compiler_params=_cparams(("arbitrary",)),
        name="mixer_b_readout",
    )(prep, prep, prep, prep, prep, yf, yb, r_k, gn_w, gn_b, ones512)


def _s5_tables(a_re, a_im, log_dt, b_re, b_im, c_re, c_im):
    Lc, G, N, P = S5_CHUNK, S5_GROUPS, S5_N, S5_P
    dt = jnp.exp(log_dt)[..., None]
    lam_re, lam_im = dt * a_re, dt * a_im
    tau = jnp.arange(Lc + 1, dtype=F32)[:, None, None, None]
    mag = jnp.exp(tau * lam_re)
    pw_re, pw_im = mag * jnp.cos(tau * lam_im), mag * jnp.sin(tau * lam_im)
    ab_re, ab_im = pw_re[1], pw_im[1]
    den = a_re * a_re + a_im * a_im
    nr = ab_re - 1.0
    cf_re, cf_im = (nr * a_re + ab_im * a_im) / den, (ab_im * a_re - nr * a_im) / den
    bp_re = cf_re[..., None] * b_re[None] - cf_im[..., None] * b_im[None]
    bp_im = cf_re[..., None] * b_im[None] + cf_im[..., None] * b_re[None]
    pb_re = pw_re[..., None] * bp_re[None] - pw_im[..., None] * bp_im[None]
    pb_im = pw_re[..., None] * bp_im[None] + pw_im[..., None] * bp_re[None]
    kk = (jnp.einsum('gqn,tdgnp->tdgqp', c_re, pb_re[:Lc]) - jnp.einsum('gqn,tdgnp->tdgqp', c_im, pb_im[:Lc]))
    jj = np.arange(Lc)[:, None]
    ii = np.arange(Lc)[None, :]
    dist = np.abs(ii - jj)
    kf = kk[dist, 0] * jnp.asarray(ii >= jj, F32)[..., None, None, None]
    kb = kk[dist, 1] * jnp.asarray(ii <= jj, F32)[..., None, None, None]
    tz = jnp.transpose(kf + kb, (2, 0, 4, 1, 3))
    jr = np.arange(Lc)
    emap = lambda pbx, order, d: jnp.transpose(pbx[order, d], (1, 0, 3, 2))
    em = jnp.stack([emap(pb_re, Lc - 1 - jr, 0), emap(pb_im, Lc - 1 - jr, 0),
                    emap(pb_re, jr, 1), emap(pb_im, jr, 1)], axis=3)

    def gmap(order, d):
        pr, pi = pw_re[order, d], pw_im[order, d]
        cp_re = c_re[None] * pr[:, :, None, :] - c_im[None] * pi[:, :, None, :]
        cp_im = c_re[None] * pi[:, :, None, :] + c_im[None] * pr[:, :, None, :]
        to = lambda t: jnp.transpose(t, (1, 3, 0, 2))
        return to(cp_re), to(-cp_im)

    gk = jnp.stack(gmap(jr + 1, 0) + gmap(Lc - jr, 1), axis=1)
    NO, NQ, NG = S5_OCTETS, 2, 4
    tz, em, gk = tz.astype(BF16), em.astype(BF16), gk.astype(BF16)

    def bdiag(blocks):
        n, c = blocks.shape[1], blocks.shape[3]
        return jnp.concatenate(
            [jnp.pad(blocks[:, g], ((0, 0), (0, 0), (g * c, (n - 1 - g) * c))) for g in range(n)], axis=1)

    def perm_rows(m, dims, order):
        no, r, c = m.shape
        m = jnp.transpose(m.reshape((no,) + dims + (c,)), (0,) + tuple(1 + o for o in order) + (len(dims) + 1,))
        return m.reshape(no, r, c)

    tr = lambda m: jnp.swapaxes(m, 1, 2)
    gjp, rgkn = (8, Lc, P), (NQ, NG, 4, N)
    wy = perm_rows(bdiag(tz.reshape(NO, 8, Lc * P, Lc * P)), gjp, (1, 0, 2))
    wy = tr(perm_rows(tr(wy), gjp, (1, 0, 2)))
    we = perm_rows(bdiag(em.reshape(NO, 8, Lc * P, 4 * N)), gjp, (1, 0, 2))
    we = tr(perm_rows(tr(we), rgkn, (0, 2, 1, 3)))
    wz = jnp.concatenate([wy, we], axis=-1)
    gm = perm_rows(bdiag(gk.reshape(NO, 8, 4 * N, Lc * P)), rgkn, (0, 2, 1, 3))
    gm = tr(perm_rows(tr(gm), gjp, (1, 0, 2)))
    mp = Lc * jnp.arange(1, 9, dtype=F32)[:, None, None, None]
    cm = jnp.exp(mp * lam_re)
    cp_re, cp_im = cm * jnp.cos(mp * lam_im), cm * jnp.sin(mp * lam_im)
    apow = jnp.stack([cp_re[:, 0], cp_im[:, 0], cp_re[::-1, 1], cp_im[::-1, 1]], axis=0)
    apow = jnp.transpose(apow.reshape(4, 8, G // NG, NG * N), (2, 0, 1, 3))
    return wz, gm, apow


def _s5_local_kernel(u_ref, wz_ref, y_ref, e_ref):
    Lc = u_ref.shape[1]
    lhs = jnp.concatenate([u_ref[:, j, :] for j in range(Lc)], axis=-1).astype(BF16)
    z = _dot(lhs, wz_ref[...])
    for i in range(Lc):
        y_ref[:, i, :] = z[:, i * LANE:(i + 1) * LANE]
    e_ref[...] = z[:, Lc * LANE:]


def _s5_local_call(proj3, wz):
    NR, Lc, _ = proj3.shape
    NO, K, N = wz.shape
    NE = N - Lc * LANE
    tr = NR // 4
    cu = COL_DU // LANE
    return pl.pallas_call(
        _s5_local_kernel,
        out_shape=(jax.ShapeDtypeStruct((NR, Lc, GROUP_W), F32), jax.ShapeDtypeStruct((NR, NO * NE), F32)),
        grid=(NO, NR // tr),
        in_specs=[
            pl.BlockSpec((tr, Lc, LANE), lambda o, i: (i, 0, cu + o)),
            pl.BlockSpec((None, K, N), lambda o, i: (o, 0, 0)),
        ],
        out_specs=(pl.BlockSpec((tr, Lc, LANE), lambda o, i: (i, 0, o)),
                   pl.BlockSpec((tr, NE), lambda o, i: (i, o))),
        compiler_params=_cparams(("arbitrary", "arbitrary")),
        name="mixer_d_local",
    )(proj3, wz)


def _s5_scan_kernel(e_ref, ap_ref, x_ref, *, n_batch, nlc, ncc):
    NS = e_ref.shape[1] // 4
    row = lax.broadcasted_iota(jnp.int32, (8, NS), 0)
    pw = [ap_ref[k] for k in range(4)]

    def cmul(a_re, a_im, x_re, x_im):
        return a_re * x_re - a_im * x_im, a_re * x_im + a_im * x_re

    def tile_scan(x_re, x_im, p_re, p_im, c_re, c_im, reverse):
        for d in (1, 2, 4):
            if reverse:
                keep, sh, pr = row < 8 - d, 8 - d, 8 - d
            else:
                keep, sh, pr = row >= d, d, d - 1
            s_re = jnp.where(keep, pltpu.roll(x_re, sh, 0), 0.0)
            s_im = jnp.where(keep, pltpu.roll(x_im, sh, 0), 0.0)
            t_re, t_im = cmul(p_re[pr:pr + 1, :], p_im[pr:pr + 1, :], s_re, s_im)
            x_re, x_im = x_re + t_re, x_im + t_im
        t_re, t_im = cmul(p_re, p_im, c_re, c_im)
        x_re, x_im = x_re + t_re, x_im + t_im
        edge, sh, last = (7, 7, 0) if reverse else (0, 1, 7)
        in_re = jnp.where(row == edge, c_re, pltpu.roll(x_re, sh, 0))
        in_im = jnp.where(row == edge, c_im, pltpu.roll(x_im, sh, 0))
        return in_re, in_im, x_re[last:last + 1, :], x_im[last:last + 1, :]

    def phase(row0, n_chunks, stride, carry):
        n_tiles = n_chunks // 8

        def body(t, carry):
            out = []
            for b in range(n_batch):
                xf_re, xf_im, xb_re, xb_im = carry[4 * b:4 * b + 4]
                rf = pl.ds(pl.multiple_of(row0 + b * stride + t * 8, 8), 8)
                rb = pl.ds(pl.multiple_of(row0 + b * stride + (n_tiles - 1 - t) * 8, 8), 8)
                f_re, f_im, xf_re, xf_im = tile_scan(e_ref[rf, 0:NS], e_ref[rf, NS:2 * NS], pw[0], pw[1],
                                                     xf_re, xf_im, False)
                b_re, b_im, xb_re, xb_im = tile_scan(e_ref[rb, 2 * NS:3 * NS], e_ref[rb, 3 * NS:4 * NS],
                                                     pw[2], pw[3], xb_re, xb_im, True)
                x_ref[rf, 0:NS] = f_re
                x_ref[rf, NS:2 * NS] = f_im
                x_ref[rb, 2 * NS:3 * NS] = b_re
                x_ref[rb, 3 * NS:4 * NS] = b_im
                out += [xf_re, xf_im, xb_re, xb_im]
            return tuple(out)

        return lax.fori_loop(0, n_tiles, body, carry)

    zero = jnp.zeros((1, NS), F32)
    carry = phase(n_batch * nlc, ncc, ncc, (zero,) * (4 * n_batch))
    phase(0, nlc, nlc, carry)


def _s5_scan_call(e, apow, n_batch, seq, ctx):
    NR, NEall = e.shape
    NQ8, _, _, NS = apow.shape
    return pl.pallas_call(
        functools.partial(_s5_scan_kernel, n_batch=n_batch, nlc=seq // S5_CHUNK, ncc=ctx // S5_CHUNK),
        out_shape=jax.ShapeDtypeStruct((NR, NEall), F32),
        grid=(NQ8,),
        in_specs=[
            pl.BlockSpec((NR, 4 * NS), lambda q: (0, q)),
            pl.BlockSpec((None, 4, 8, NS), lambda q: (q, 0, 0, 0)),
        ],
        out_specs=pl.BlockSpec((NR, 4 * NS), lambda q: (0, q)),
        compiler_params=_cparams(("arbitrary",)),
        name="mixer_d_scan",
    )(e, apow)


def _s5_carry_kernel(x_ref, gm_ref, yl_ref, y_ref):
    Lc = yl_ref.shape[1]
    y = _dot(x_ref[...].astype(BF16), gm_ref[...])
    for i in range(Lc):
        y_ref[:, i, :] = yl_ref[:, i, :] + y[:, i * LANE:(i + 1) * LANE]


def _s5_carry_call(xin, gm, yloc):
    NR, Lc, _ = yloc.shape
    NO, NE, N = gm.shape
    tr = NR // 4
    return pl.pallas_call(
        _s5_carry_kernel,
        out_shape=jax.ShapeDtypeStruct((NR, Lc, GROUP_W), F32),
        grid=(NO, NR // tr),
        in_specs=[
            pl.BlockSpec((tr, NE), lambda o, i: (i, o)),
            pl.BlockSpec((None, NE, N), lambda o, i: (o, 0, 0)),
            pl.BlockSpec((tr, Lc, LANE), lambda o, i: (i, 0, o)),
        ],
        out_specs=pl.BlockSpec((tr, Lc, LANE), lambda o, i: (i, 0, o)),
        compiler_params=_cparams(("arbitrary", "arbitrary")),
        name="mixer_d_carry",
    )(xin, gm, yloc)


def _s5_out_kernel(y_ref, u_ref, d_ref, w_ref, b_ref, o_ref):
    y = y_ref[...] + d_ref[...] * u_ref[...]
    c = math.sqrt(2.0 / math.pi)
    y = 0.5 * y * (1.0 + jnp.tanh(c * (y + 0.044715 * (y * y * y))))
    z = _dot(y.astype(BF16), w_ref[...]) + b_ref[...]
    o_ref[...] = (y * _sigmoid(z)).astype(o_ref.dtype)


def _s5_out_call(y_tok, proj, d_skip, glu_w, glu_b, tm):
    R = proj.shape[0]
    vec = pl.BlockSpec((1, GROUP_W), lambda i: (0, 0))
    return pl.pallas_call(
        _s5_out_kernel,
        out_shape=jax.ShapeDtypeStruct((R, GROUP_W), BF16),
        grid=(R // tm,),
        in_specs=[
            pl.BlockSpec((tm, GROUP_W), lambda i: (i, 0)),
            pl.BlockSpec((tm, GROUP_W), lambda i: (i, COL_DU // GROUP_W)),
            vec,
            pl.BlockSpec((GROUP_W, GROUP_W), lambda i: (0, 0)),
            vec,
        ],
        out_specs=pl.BlockSpec((tm, GROUP_W), lambda i: (i, 0)),
        compiler_params=_cparams(("arbitrary",)),
        name="mixer_d_out",
    )(y_tok, proj, d_skip, glu_w, glu_b)


def _rope_tables(n_tokens):
    t = jnp.arange(n_tokens)
    nf = HEAD_DIM // 4
    inv = 1.0 / (ROPE_BASE ** (jnp.arange(nf, dtype=F32) / nf))

    def ang(pp):
        a = pp.astype(F32)[:, None] * inv[None, :]
        return jnp.concatenate([a, a], -1)

    a = jnp.concatenate([ang(t // GRID_W), ang(t % GRID_W)], -1)
    cos, sin = jnp.cos(a), jnp.sin(a)
    sign = np.where((np.arange(HEAD_DIM) % 32) < 16, -1.0, 1.0).astype(np.float32)
    cos2 = jnp.concatenate([cos, cos], -1)
    sin2 = jnp.concatenate([sin * sign, sin * sign], -1)
    return cos2, sin2


def _block_ones(n):
    return jnp.asarray(np.kron(np.eye(n // HEAD_DIM), np.ones((HEAD_DIM, HEAD_DIM))), BF16)


def _permute_w_in(w_in):
    w = w_in.astype(BF16)
    L, D = w.shape[:2]
    pad = jnp.zeros((L, D, COL_CK - COL_BLR - 160), BF16)
    parts = [w[..., :3072], w[..., 3232:3744], w[..., 4000:4512], w[..., 3072:3232], pad, w[..., 3744:4000]]
    return jnp.concatenate(parts, axis=-1)


def _forward(x, c, ctx, c_ctx, w_ada, b_ada, norm_g, ffn1_wg, ffn1_wu, ffn1_wd, ffn2_wg, ffn2_wu, ffn2_wd,
             w_in, w_out, na_q_g, na_k_g, na_rpb, rw_conv, rw_w0, rw_w_up, rw_a0, rw_a_up, rw_g_up,
             rw_k_k, rw_k_a, rw_r_k, rw_gn_w, rw_gn_b, wa_q_g, wa_k_g, wa_sink, s5_a_re, s5_a_im,
             s5_log_dt, s5_b_re, s5_b_im, s5_c_re, s5_c_im, s5_d, s5_glu_w, s5_glu_b, *, tm, tf):
    B, SEQ, D = x.shape
    CTX = ctx.shape[1]
    L = w_ada.shape[0]
    n_lat = B * SEQ
    n_qb = SEQ // QBLK

    bf = lambda t: t.astype(BF16)
    f1g, f1u, f1d, f2g, f2u, f2d = map(bf, (ffn1_wg, ffn1_wu, ffn1_wd, ffn2_wg, ffn2_wu, ffn2_wd))
    w_in_p = _permute_w_in(w_in)
    w_out_b = bf(w_out)
    glu_w_b = bf(s5_glu_w)
    ones128, ones512 = _block_ones(LANE), _block_ones(GROUP_W)
    cos_t, sin_t = _rope_tables(SEQ)
    na_bias = [_na_bias_table(na_rpb[l], n_qb) for l in range(L)]
    zr = lambda r, cdim: jnp.zeros((L, r, cdim), F32)
    wlr = jnp.concatenate([
        jnp.concatenate([rw_w_up[:, 0], rw_w_up[:, 1], zr(32, 3 * GROUP_W)], axis=-1),
        jnp.concatenate([zr(32, 2 * GROUP_W), rw_a_up[:, 0], rw_a_up[:, 1], zr(32, GROUP_W)], axis=-1),
        jnp.concatenate([zr(96, 4 * GROUP_W), rw_g_up], axis=-1),
        zr(96, 5 * GROUP_W)], axis=1).astype(BF16)
    s5_tabs = jax.vmap(_s5_tables)(s5_a_re, s5_a_im, s5_log_dt, s5_b_re, s5_b_im, s5_c_re, s5_c_im)
    tile2 = lambda t: jnp.concatenate([t, t], axis=-1)

    cc = jnp.concatenate([c, c_ctx[None], jnp.zeros((8 - B - 1, D), F32)], axis=0)
    mods_all = _ada_call(cc, w_ada, b_ada)[:, :B + 1].reshape(L, B + 1, N_MOD, D)

    xs = jnp.concatenate([x.reshape(n_lat, D), ctx.reshape(B * CTX, D)], axis=0)
    R = xs.shape[0]
    for l in range(L):
        want_ctx = l < L - 1
        mods = mods_all[l]
        ng = norm_g[l]
        xs = _ffn_call(xs, mods, ng[0:1], f1g, f1u, f1d, l, 0, R, SEQ, B, tm, tf)
        proj = _win_call(xs, mods, ng[1:2], w_in_p, l, SEQ, B, tm, D_IN_PAD // 3)
        o_a = _na_call(proj, na_bias[l], tile2(na_q_g[l][None]), tile2(na_k_g[l][None]), ones128,
                       B, SEQ, CTX, want_ctx)
        prep = _rw_prep_call(proj, rw_conv[l], wlr[l], rw_w0[l], rw_a0[l], rw_k_k[l][None], rw_k_a[l][None],
                             ones512, B, SEQ, CTX, min(tm, 256))
        vec = lambda t: t.reshape(1, GROUP_W)
        yf, yb = _rw_scan_call(prep, B, SEQ, CTX)
        o_b = _rw_readout_call(prep, yf, yb, vec(rw_r_k[l]), vec(rw_gn_w[l]), vec(rw_gn_b[l]), ones512, tm)
        o_c = _wa_call(proj, wa_sink[l], cos_t, sin_t, tile2(wa_q_g[l][None]), tile2(wa_k_g[l][None]), ones128,
                       B, SEQ, CTX, want_ctx)
        wz, gm, apow = (t[l] for t in s5_tabs)
        yloc, e_loc = _s5_local_call(proj.reshape(R // S5_CHUNK, S5_CHUNK, D_IN_PAD), bf(wz))
        xin = _s5_scan_call(e_loc, apow, B, SEQ, CTX)
        y_tok = _s5_carry_call(xin, bf(gm), yloc).reshape(R, GROUP_W)
        o_d = _s5_out_call(y_tok, proj, vec(s5_d[l]), glu_w_b[l], vec(s5_glu_b[l]), tm)
        n_rows = R if want_ctx else n_lat
        xs = _wout_call(xs, (o_a, o_b, o_c, o_d), mods, w_out_b, l, n_rows, SEQ, B, tm)
        xs = _ffn_call(xs, mods, ng[2:3], f2g, f2u, f2d, l, 6, n_rows, SEQ, B, tm, tf)
    return xs[:n_lat].reshape(B, SEQ, D)


def kernel(x, c, ctx, c_ctx, w_ada, b_ada, norm_g, ffn1_wg, ffn1_wu, ffn1_wd, ffn2_wg, ffn2_wu, ffn2_wd, w_in, w_out, na_q_g, na_k_g, na_rpb, rw_conv, rw_w0, rw_w_up, rw_a0, rw_a_up, rw_g_up, rw_k_k, rw_k_a, rw_r_k, rw_gn_w, rw_gn_b, wa_q_g, wa_k_g, wa_sink, s5_a_re, s5_a_im, s5_log_dt, s5_b_re, s5_b_im, s5_c_re, s5_c_im, s5_d, s5_glu_w, s5_glu_b):
    return _forward(x, c, ctx, c_ctx, w_ada, b_ada, norm_g, ffn1_wg, ffn1_wu, ffn1_wd, ffn2_wg, ffn2_wu, ffn2_wd,
                    w_in, w_out, na_q_g, na_k_g, na_rpb, rw_conv, rw_w0, rw_w_up, rw_a0, rw_a_up, rw_g_up,
                    rw_k_k, rw_k_a, rw_r_k, rw_gn_w, rw_gn_b, wa_q_g, wa_k_g, wa_sink, s5_a_re, s5_a_im,
                    s5_log_dt, s5_b_re, s5_b_im, s5_c_re, s5_c_im, s5_d, s5_glu_w, s5_glu_b, tm=512, tf=512)
```

```python
import functools
import math

import numpy as np
import jax
import jax.numpy as jnp
from jax import lax
from jax.experimental import pallas as pl
from jax.experimental.pallas import tpu as pltpu

F32 = jnp.float32
BF16 = jnp.bfloat16

D_MODEL = 2048
GRID_W = 64
HEAD_DIM = 64
GROUP_W = D_MODEL // 4
N_MOD = 9
NORM_EPS = 1e-6
ROPE_BASE = 10000.0
NEG_INF = -1e30

NA_HEADS = GROUP_W // HEAD_DIM
NA_WIN_R = 8
NA_WIN_C = 16
RW_HEADS = GROUP_W // HEAD_DIM
RW_DECAY_RANK = 32
RW_ICLR_RANK = 32
RW_GATE_RANK = 96
RW_GN_EPS = 64e-5
WA_HEADS = GROUP_W // HEAD_DIM
WA_KV_HEADS = 2
WA_GROUP = WA_HEADS // WA_KV_HEADS
WA_WINDOW = 128
S5_P = 16
S5_GROUPS = GROUP_W // S5_P
S5_N = 64

LANE = 128
QBLK = 2 * GRID_W
NA_KBLKS = 5
RW_CHUNK = 64
S5_CHUNK = 8
S5_OCTETS = 4
VMEM_LIMIT = 56 * 1024 * 1024

COL_AQ, COL_AK, COL_AV = 0, 512, 1024
COL_BR = 1536
COL_CQ = 3072
COL_DU = 3584
COL_BLR = 4096
COL_CK, COL_CV = 4352, 4480
D_IN_PAD = 4608

PB_R, PB_V, PB_KK, PB_G = 0, 1, 2, 9
PB_N = 10


def _cparams(sem):
    return pltpu.CompilerParams(dimension_semantics=sem, vmem_limit_bytes=VMEM_LIMIT)


def _dot(a, b):
    return jnp.dot(a, b, preferred_element_type=F32)


def _dot_nt(a, b):
    return lax.dot_general(a, b, (((1,), (1,)), ((), ())), preferred_element_type=F32)


def _dot_tn(a, b):
    return lax.dot_general(a, b, (((0,), (0,)), ((), ())), preferred_element_type=F32)


def _sigmoid(x):
    return 1.0 / (1.0 + jnp.exp(-x))


def _modulate(x, g, shift, scale):
    ms = jnp.mean(x * x, axis=-1, keepdims=True)
    return (x * lax.rsqrt(ms + NORM_EPS) * g) * (1.0 + scale) + shift


def _seg_sum(x, ones_bd):
    hi = x.astype(BF16)
    lo = (x - hi.astype(F32)).astype(BF16)
    return _dot(hi, ones_bd) + _dot(lo, ones_bd)


def _with_ones(v):
    one = jnp.ones((v.shape[0], HEAD_DIM), F32)
    return jnp.concatenate([v[:, :HEAD_DIM], one, v[:, HEAD_DIM:], one], axis=-1).astype(BF16)


def _prob(s, m):
    return jnp.exp((s - m).astype(BF16))


def _head_rmsnorm(x, ones_bd):
    ms = _seg_sum(x * x, ones_bd) * (1.0 / HEAD_DIM)
    return x * lax.rsqrt(ms + NORM_EPS)


def _ada_kernel(c_ref, w_ref, b_ref, o_ref):
    c = c_ref[...]
    s = (c * _sigmoid(c)).astype(BF16)
    o_ref[...] = _dot(s, w_ref[...].astype(BF16)) + b_ref[...]


def _ada_call(cc, w_ada, b_ada):
    L, D, N = w_ada.shape
    tn = 1024
    return pl.pallas_call(
        _ada_kernel,
        out_shape=jax.ShapeDtypeStruct((L, 8, N), F32),
        grid=(L, N // tn),
        in_specs=[
            pl.BlockSpec((8, D), lambda l, j: (0, 0)),
            pl.BlockSpec((None, D, tn), lambda l, j: (l, 0, j)),
            pl.BlockSpec((None, 1, tn), lambda l, j: (l, 0, j)),
        ],
        out_specs=pl.BlockSpec((None, 8, tn), lambda l, j: (l, 0, j)),
        compiler_params=_cparams(("arbitrary", "arbitrary")),
        name="adaln",
    )(cc, w_ada, b_ada.reshape(L, 1, N))


def _ffn_kernel(x_ref, mod_ref, g_ref, wg_ref, wu_ref, wd_ref, o_ref, h_sc, acc_sc, *, mi):
    j = pl.program_id(1)

    @pl.when(j == 0)
    def _():
        h = _modulate(x_ref[...], g_ref[...], mod_ref[mi:mi + 1, :], mod_ref[mi + 1:mi + 2, :])
        h_sc[...] = h.astype(BF16)
        acc_sc[...] = jnp.zeros_like(acc_sc)

    h = h_sc[...]
    gate = _dot(h, wg_ref[...])
    up = _dot(h, wu_ref[...])
    a = (gate * _sigmoid(gate) * up).astype(BF16)
    acc_sc[...] += _dot(a, wd_ref[...])

    @pl.when(j == pl.num_programs(1) - 1)
    def _():
        o_ref[...] = x_ref[...] + 0.5 * mod_ref[mi + 2:mi + 3, :] * acc_sc[...]


def _ffn_call(xs, mods, g, wg, wu, wd, l, mi, n_rows, rows_per_seq, n_batch, tm, tf):
    D = xs.shape[1]
    F = wg.shape[2]
    tps = rows_per_seq // tm
    return pl.pallas_call(
        functools.partial(_ffn_kernel, mi=mi),
        out_shape=jax.ShapeDtypeStruct((n_rows, D), F32),
        grid=(n_rows // tm, F // tf),
        in_specs=[
            pl.BlockSpec((tm, D), lambda i, j: (i, 0)),
            pl.BlockSpec((None, N_MOD, D), lambda i, j: (jnp.minimum(i // tps, n_batch), 0, 0)),
            pl.BlockSpec((1, D), lambda i, j: (0, 0)),
            pl.BlockSpec((None, D, tf), lambda i, j: (l, 0, j)),
            pl.BlockSpec((None, D, tf), lambda i, j: (l, 0, j)),
            pl.BlockSpec((None, tf, D), lambda i, j: (l, j, 0)),
        ],
        out_specs=pl.BlockSpec((tm, D), lambda i, j: (i, 0)),
        scratch_shapes=[pltpu.VMEM((tm, D), BF16), pltpu.VMEM((tm, D), F32)],
        compiler_params=_cparams(("arbitrary", "arbitrary")),
        name="ffn",
    )(xs, mods, g, wg, wu, wd)


def _win_kernel(x_ref, mod_ref, g_ref, w_ref, o_ref, h_sc):
    @pl.when(pl.program_id(1) == 0)
    def _():
        h = _modulate(x_ref[...], g_ref[...], mod_ref[3:4, :], mod_ref[4:5, :])
        h_sc[...] = h.astype(BF16)

    o_ref[...] = _dot(h_sc[...], w_ref[...])


def _win_call(xs, mods, g, w_in, l, rows_per_seq, n_batch, tm, tn):
    R, D = xs.shape
    N = w_in.shape[2]
    tps = rows_per_seq // tm
    return pl.pallas_call(
        _win_kernel,
        out_shape=jax.ShapeDtypeStruct((R, N), F32),
        grid=(R // tm, N // tn),
        in_specs=[
            pl.BlockSpec((tm, D), lambda i, j: (i, 0)),
            pl.BlockSpec((None, N_MOD, D), lambda i, j: (jnp.minimum(i // tps, n_batch), 0, 0)),
            pl.BlockSpec((1, D), lambda i, j: (0, 0)),
            pl.BlockSpec((None, D, tn), lambda i, j: (l, 0, j)),
        ],
        out_specs=pl.BlockSpec((tm, tn), lambda i, j: (i, j)),
        scratch_shapes=[pltpu.VMEM((tm, D), BF16)],
        compiler_params=_cparams(("arbitrary", "arbitrary")),
        name="in_proj",
    )(xs, mods, g, w_in)


def _wout_kernel(x_ref, oa_ref, ob_ref, oc_ref, od_ref, mod_ref, w_ref, o_ref):
    acc = _dot(oa_ref[...], w_ref[0 * GROUP_W:1 * GROUP_W, :])
    acc += _dot(ob_ref[...], w_ref[1 * GROUP_W:2 * GROUP_W, :])
    acc += _dot(oc_ref[...], w_ref[2 * GROUP_W:3 * GROUP_W, :])
    acc += _dot(od_ref[...], w_ref[3 * GROUP_W:4 * GROUP_W, :])
    o_ref[...] = x_ref[...] + mod_ref[5:6, :] * acc


def _wout_call(xs, outs, mods, w_out, l, n_rows, rows_per_seq, n_batch, tm):
    D = xs.shape[1]
    tps = rows_per_seq // tm
    ospec = pl.BlockSpec((tm, GROUP_W), lambda i: (i, 0))
    return pl.pallas_call(
        _wout_kernel,
        out_shape=jax.ShapeDtypeStruct((n_rows, D), F32),
        grid=(n_rows // tm,),
        in_specs=[
            pl.BlockSpec((tm, D), lambda i: (i, 0)),
            ospec, ospec, ospec, ospec,
            pl.BlockSpec((None, N_MOD, D), lambda i: (jnp.minimum(i // tps, n_batch), 0, 0)),
            pl.BlockSpec((None, D, D), lambda i: (l, 0, 0)),
        ],
        out_specs=pl.BlockSpec((tm, D), lambda i: (i, 0)),
        compiler_params=_cparams(("arbitrary",)),
        name="out_proj",
    )(xs, *outs, mods, w_out)


def _na_case_reps(n_qb):
    return (0, 1, 2, n_qb - 2, n_qb - 1)


def _na_start(p, n_qb):
    return jnp.clip(p - 2, 0, n_qb - NA_KBLKS)


def _na_bias_table(rpb, n_qb):
    H = rpb.shape[0]
    rows = 2 * n_qb
    qc = np.arange(GRID_W)[:, None]
    kc = np.arange(GRID_W)[None, :]
    cs = np.clip(qc - NA_WIN_C // 2, 0, GRID_W - NA_WIN_C)
    okc = (kc >= cs) & (kc < cs + NA_WIN_C)
    dc = np.where(okc, kc - qc + NA_WIN_C - 1, 0)
    rp = rpb.astype(F32).reshape(H, 2 * NA_WIN_R - 1, 2 * NA_WIN_C - 1)
    blocks = jnp.where(jnp.asarray(okc)[None, None], jnp.take(rp, jnp.asarray(dc), axis=2), NEG_INF)
    blocks = blocks.astype(BF16)
    neg = jnp.full((H, GRID_W, GRID_W), NEG_INF, BF16)
    cases = []
    for p in _na_case_reps(n_qb):
        start = min(max(p - 2, 0), n_qb - NA_KBLKS)
        qrows = []
        for qr in range(2):
            qa = 2 * p + qr
            rs = min(max(qa - NA_WIN_R // 2, 0), rows - NA_WIN_R)
            krow = []
            for kr in range(2 * NA_KBLKS):
                ka = 2 * start + kr
                krow.append(blocks[:, ka - qa + NA_WIN_R - 1] if rs <= ka < rs + NA_WIN_R else neg)
            qrows.append(jnp.concatenate(krow, axis=-1))
        cases.append(jnp.concatenate(qrows, axis=-2))
    return jnp.stack(cases, axis=0)


def _na_kernel(q_ref, k_ref, v_ref, kc_ref, vc_ref, bias_ref, qg_ref, kg_ref, ones_ref, o_ref,
               kn_sc, vn_sc, kcn_sc, vcn_sc, *, n_qb, seq, ctx):
    p = pl.program_id(2)
    ones_bd = ones_ref[...]
    scale = HEAD_DIM ** -0.5

    @pl.when(p == 0)
    def _():
        kg = kg_ref[...]
        rows = 512

        def body(i, carry):
            sl = pl.ds(pl.multiple_of(i * rows, rows), rows)
            kn_sc[sl, :] = (_head_rmsnorm(k_ref[sl, :], ones_bd) * kg).astype(BF16)
            vn_sc[sl, :] = _with_ones(v_ref[sl, :])
            return carry

        lax.fori_loop(0, seq // rows, body, 0)
        kcn_sc[...] = (_head_rmsnorm(kc_ref[...], ones_bd) * kg).astype(BF16)
        vcn_sc[...] = _with_ones(vc_ref[...])

    nq = q_ref.shape[0] // QBLK
    n_lat = n_qb // nq
    q = (_head_rmsnorm(q_ref[...], ones_bd) * (qg_ref[...] * scale)).astype(BF16)
    kcn = kcn_sc[...]
    vcn = vcn_sc[...]
    chains = [(qb, h) for qb in range(nq) for h in range(2)]
    hsl = lambda h: slice(h * HEAD_DIM, (h + 1) * HEAD_DIM)
    lane = lax.broadcasted_iota(jnp.int32, (1, LANE), 1)
    qh = [jnp.where((lane // HEAD_DIM) == h, q[qb * QBLK:(qb + 1) * QBLK, :], 0) for qb, h in chains]
    s_c = [_dot_nt(qh[c], kcn) for c in range(len(chains))]
    m_c = [jnp.max(t, axis=-1, keepdims=True) for t in s_c]

    vsl = lambda h: slice(h * LANE, (h + 1) * LANE)
    vca = [vcn[:, vsl(h)] for h in range(2)]

    def finish(oa):
        rows = [jnp.concatenate([oa[qb * 2 + h][:, :HEAD_DIM] / oa[qb * 2 + h][:, HEAD_DIM:] for h in range(2)],
                                axis=-1) for qb in range(nq)]
        o_ref[...] = jnp.concatenate(rows, axis=0).astype(o_ref.dtype)

    @pl.when(p < n_lat)
    def _():
        starts = [_na_start(p * nq + qb, n_qb) for qb in range(nq)]
        cases = [p * nq + qb - starts[qb] for qb in range(nq)]
        sls = [pl.ds(pl.multiple_of(st * QBLK, QBLK), NA_KBLKS * QBLK) for st in starts]
        kw = [kn_sc[sl, :] for sl in sls]
        vw = [vn_sc[sl, :] for sl in sls]
        s_n = [_dot_nt(qh[c], kw[qb]) + bias_ref[cases[qb], h].astype(F32)
               for c, (qb, h) in enumerate(chains)]
        m = [jnp.maximum(jnp.max(s_n[c], axis=-1, keepdims=True), m_c[c]) for c in range(len(chains))]
        p_n = [_prob(s_n[c], m[c]) for c in range(len(chains))]
        p_c = [_prob(s_c[c], m[c]) for c in range(len(chains))]
        finish([_dot(p_n[c], vw[qb][:, vsl(h)]) + _dot(p_c[c], vca[h])
                for c, (qb, h) in enumerate(chains)])

    @pl.when(p >= n_lat)
    def _():
        finish([_dot(_prob(s_c[c], m_c[c]), vca[h]) for c, (qb, h) in enumerate(chains)])


def _na_call(proj, bias, qg, kg, ones_bd, n_batch, seq, ctx, want_ctx):
    R = proj.shape[0]
    nq = 2
    n_qb = seq // QBLK
    n_lat, n_ctx = n_qb // nq, ctx // (nq * QBLK)
    steps = n_lat + (n_ctx if want_ctx else 0)
    ctx_kb = n_batch * seq // ctx

    def qrow(b, p):
        return jnp.where(p < n_lat, b * n_lat + p, n_batch * n_lat + b * n_ctx + (p - n_lat))

    cq, ck, cv = COL_AQ // LANE, COL_AK // LANE, COL_AV // LANE
    return pl.pallas_call(
        functools.partial(_na_kernel, n_qb=n_qb, seq=seq, ctx=ctx),
        out_shape=jax.ShapeDtypeStruct((R, GROUP_W), BF16),
        grid=(n_batch, NA_HEADS // 2, steps),
        in_specs=[
            pl.BlockSpec((nq * QBLK, LANE), lambda b, hp, p: (qrow(b, p), cq + hp)),
            pl.BlockSpec((seq, LANE), lambda b, hp, p: (b, ck + hp)),
            pl.BlockSpec((seq, LANE), lambda b, hp, p: (b, cv + hp)),
            pl.BlockSpec((ctx, LANE), lambda b, hp, p: (ctx_kb + b, ck + hp)),
            pl.BlockSpec((ctx, LANE), lambda b, hp, p: (ctx_kb + b, cv + hp)),
            pl.BlockSpec((5, 2, QBLK, NA_KBLKS * QBLK), lambda b, hp, p: (0, hp, 0, 0)),
            pl.BlockSpec((1, LANE), lambda b, hp, p: (0, 0)),
            pl.BlockSpec((1, LANE), lambda b, hp, p: (0, 0)),
            pl.BlockSpec((LANE, LANE), lambda b, hp, p: (0, 0)),
        ],
        out_specs=pl.BlockSpec((nq * QBLK, LANE), lambda b, hp, p: (qrow(b, p), hp)),
        scratch_shapes=[pltpu.VMEM((seq, LANE), BF16), pltpu.VMEM((seq, 2 * LANE), BF16),
                        pltpu.VMEM((ctx, LANE), BF16), pltpu.VMEM((ctx, 2 * LANE), BF16)],
        compiler_params=_cparams(("arbitrary", "arbitrary", "arbitrary")),
        name="mixer_a",
    )(proj, proj, proj, proj, proj, bias, qg, kg, ones_bd)


def _rope(x, cos, sin_signed, first_half):
    rot = jnp.where(first_half, pltpu.roll(x, LANE - 16, 1), pltpu.roll(x, 16, 1))
    return x * cos + rot * sin_signed


def _wa_kernel(sink_ref, q_ref, k_ref, v_ref, kc_ref, vc_ref, cosk_ref, sin_k_ref, cosq_ref, sinq_ref,
               qg_ref, kg_ref, ones_ref, wmask_ref, o_ref, kn_sc, vn_sc, kcn_sc, vcn_sc, *, n_qb, seq, ctx):
    p = pl.program_id(1)
    ones_bd = ones_ref[...]
    scale = HEAD_DIM ** -0.5
    lane = lax.broadcasted_iota(jnp.int32, (1, LANE), 1)
    first_half = (lane % 32) < 16

    @pl.when(p == 0)
    def _():
        kg = kg_ref[...]
        rows = 512

        def body(i, carry):
            sl = pl.ds(pl.multiple_of(i * rows, rows), rows)
            kn = _head_rmsnorm(k_ref[sl, :], ones_bd) * kg
            kn_sc[sl, :] = _rope(kn, cosk_ref[sl, :], sin_k_ref[sl, :], first_half).astype(BF16)
            vn_sc[sl, :] = _with_ones(v_ref[sl, :])
            return carry

        lax.fori_loop(0, seq // rows, body, 0)
        kcn_sc[...] = (_head_rmsnorm(kc_ref[...], ones_bd) * kg).astype(BF16)
        vcn_sc[...] = _with_ones(vc_ref[...])

    nq = q_ref.shape[0] // QBLK
    n_lat = n_qb // nq
    qg = qg_ref[...] * scale
    is_lat = p < n_lat
    cosq = jnp.where(is_lat, cosq_ref[...], 1.0)
    sinq = jnp.where(is_lat, sinq_ref[...], 0.0)
    qh = []
    for c in range(4):
        qn = _head_rmsnorm(q_ref[:, c * LANE:(c + 1) * LANE], ones_bd) * qg
        qn = _rope(qn, cosq, sinq, first_half).astype(BF16)
        qh += [qn[:, :HEAD_DIM], qn[:, HEAD_DIM:]]
    kcn = kcn_sc[...]
    vcn = vcn_sc[...]
    chains = [(qb, kh) for qb in range(nq) for kh in range(WA_KV_HEADS)]
    nch = range(len(chains))
    hsl = lambda kh: slice(kh * HEAD_DIM, (kh + 1) * HEAD_DIM)
    qs = [jnp.concatenate([qh[kh * WA_GROUP + g][qb * QBLK:(qb + 1) * QBLK, :] for g in range(WA_GROUP)], axis=0)
          for qb, kh in chains]
    sk = [jnp.concatenate([jnp.full((QBLK, 1), sink_ref[kh * WA_GROUP + g], F32) for g in range(WA_GROUP)],
                          axis=0) for qb, kh in chains]
    s_c = [_dot_nt(qs[c], kcn[:, hsl(chains[c][1])]) for c in nch]
    m_c = [jnp.maximum(jnp.max(s_c[c], axis=-1, keepdims=True), sk[c]) for c in nch]

    vsl = lambda kh: slice(kh * LANE, (kh + 1) * LANE)
    vca = [vcn[:, vsl(kh)] for kh in range(WA_KV_HEADS)]

    def finish(oa, sink_w):
        rows = []
        for qb in range(nq):
            heads = []
            for kh in range(WA_KV_HEADS):
                c = qb * WA_KV_HEADS + kh
                oc = oa[c][:, :HEAD_DIM] / (oa[c][:, HEAD_DIM:] + sink_w[c])
                heads += [oc[g * QBLK:(g + 1) * QBLK, :] for g in range(WA_GROUP)]
            rows.append(jnp.concatenate(heads, axis=-1))
        o_ref[...] = jnp.concatenate(rows, axis=0).astype(o_ref.dtype)

    @pl.when(is_lat)
    def _():
        blk = [p * nq + qb for qb in range(nq)]
        ws = [jnp.clip((bq - 1) * QBLK, 0, seq - 3 * QBLK) for bq in blk]
        sls = [pl.ds(pl.multiple_of(w, QBLK), 3 * QBLK) for w in ws]
        kw = [kn_sc[sl, :] for sl in sls]
        vw = [vn_sc[sl, :] for sl in sls]
        case = [jnp.where(bq == 0, 0, jnp.where(bq == n_qb - 1, 2, 1)) for bq in blk]
        msk = [jnp.concatenate([wmask_ref[cs]] * WA_GROUP, axis=0) for cs in case]
        s_w = [_dot_nt(qs[c], kw[chains[c][0]][:, hsl(chains[c][1])]) + msk[chains[c][0]] for c in nch]
        m = [jnp.maximum(jnp.max(s_w[c], axis=-1, keepdims=True), m_c[c]) for c in nch]
        p_w = [_prob(s_w[c], m[c]) for c in nch]
        p_c = [_prob(s_c[c], m[c]) for c in nch]
        finish([_dot(p_w[c], vw[chains[c][0]][:, vsl(chains[c][1])]) + _dot(p_c[c], vca[chains[c][1]])
                for c in nch], [jnp.exp(sk[c] - m[c]) for c in nch])

    @pl.when(jnp.logical_not(is_lat))
    def _():
        finish([_dot(_prob(s_c[c], m_c[c]), vca[chains[c][1]]) for c in nch],
               [jnp.exp(sk[c] - m_c[c]) for c in nch])


def _wa_mask_table():
    i = np.arange(QBLK)[:, None]
    j = np.arange(3 * QBLK)[None, :]
    shifts = (0, QBLK, 2 * QBLK)
    return jnp.asarray(np.stack([np.where(np.abs(j - sh - i) <= WA_WINDOW, 0.0, NEG_INF) for sh in shifts]), F32)


def _wa_call(proj, sink, cos_t, sin_t, qg, kg, ones_bd, n_batch, seq, ctx, want_ctx):
    R = proj.shape[0]
    nq = 2
    n_qb = seq // QBLK
    n_lat, n_ctx = n_qb // nq, ctx // (nq * QBLK)
    steps = n_lat + (n_ctx if want_ctx else 0)
    ctx_kb = n_batch * seq // ctx

    def qrow(b, p):
        return jnp.where(p < n_lat, b * n_lat + p, n_batch * n_lat + b * n_ctx + (p - n_lat))

    ck, cv = COL_CK // LANE, COL_CV // LANE
    return pl.pallas_call(
        functools.partial(_wa_kernel, n_qb=n_qb, seq=seq, ctx=ctx),
        out_shape=jax.ShapeDtypeStruct((R, GROUP_W), BF16),
        grid=(n_batch, steps),
        in_specs=[
            pl.BlockSpec(memory_space=pltpu.SMEM),
            pl.BlockSpec((nq * QBLK, GROUP_W), lambda b, p: (qrow(b, p), COL_CQ // GROUP_W)),
            pl.BlockSpec((seq, LANE), lambda b, p: (b, ck)),
            pl.BlockSpec((seq, LANE), lambda b, p: (b, cv)),
            pl.BlockSpec((ctx, LANE), lambda b, p: (ctx_kb + b, ck)),
            pl.BlockSpec((ctx, LANE), lambda b, p: (ctx_kb + b, cv)),
            pl.BlockSpec((seq, LANE), lambda b, p: (0, 0)),
            pl.BlockSpec((seq, LANE), lambda b, p: (0, 0)),
            pl.BlockSpec((nq * QBLK, LANE), lambda b, p: (jnp.minimum(p, n_lat - 1), 0)),
            pl.BlockSpec((nq * QBLK, LANE), lambda b, p: (jnp.minimum(p, n_lat - 1), 0)),
            pl.BlockSpec((1, LANE), lambda b, p: (0, 0)),
            pl.BlockSpec((1, LANE), lambda b, p: (0, 0)),
            pl.BlockSpec((LANE, LANE), lambda b, p: (0, 0)),
            pl.BlockSpec((3, QBLK, 3 * QBLK), lambda b, p: (0, 0, 0)),
        ],
        out_specs=pl.BlockSpec((nq * QBLK, GROUP_W), lambda b, p: (qrow(b, p), 0)),
        scratch_shapes=[pltpu.VMEM((seq, LANE), BF16), pltpu.VMEM((seq, 2 * LANE), BF16),
                        pltpu.VMEM((ctx, LANE), BF16), pltpu.VMEM((ctx, 2 * LANE), BF16)],
        compiler_params=_cparams(("arbitrary", "arbitrary")),
        name="mixer_c",
    )(sink, proj, proj, proj, proj, proj, cos_t, sin_t, cos_t, sin_t, qg, kg, ones_bd, _wa_mask_table())


def _rw_prep_kernel(x_ref, prev_ref, next_ref, lr_ref, conv_ref, wlr_ref, w0_ref, a0_ref, kk_ref, ka_ref,
                    ones_ref, o_ref, *, tiles_lat, tiles_ctx, n_lat_tiles):
    i = pl.program_id(0)
    tm = x_ref.shape[0]
    in_lat = i < n_lat_tiles
    first = jnp.where(in_lat, i % tiles_lat == 0, (i - n_lat_tiles) % tiles_ctx == 0)
    last = jnp.where(in_lat, i % tiles_lat == tiles_lat - 1, (i - n_lat_tiles) % tiles_ctx == tiles_ctx - 1)
    x = x_ref[...]
    prev_row = jnp.where(first, 0.0, prev_ref[7:8, :])
    next_row = jnp.where(last, 0.0, next_ref[0:1, :])
    row = lax.broadcasted_iota(jnp.int32, (tm, 1), 0)
    x_prev = jnp.where(row == 0, prev_row, pltpu.roll(x, 1, 0))
    x_next = jnp.where(row == tm - 1, next_row, pltpu.roll(x, tm - 1, 0))
    cw = conv_ref[...]
    y = x_prev * cw[0:1, :] + x * cw[1:2, :] + x_next * cw[2:3, :]
    r = y[:, 0:GROUP_W]
    k = y[:, GROUP_W:2 * GROUP_W]
    v = y[:, 2 * GROUP_W:3 * GROUP_W]

    lr = lr_ref[...]
    lane = lax.broadcasted_iota(jnp.int32, (1, lr.shape[1]), 1)
    c1 = RW_DECAY_RANK
    c2 = c1 + RW_ICLR_RANK
    c3 = c2 + RW_GATE_RANK
    act = jnp.where(lane < c1, jnp.tanh(lr),
                    jnp.where(lane < c2, lr, jnp.where(lane < c3, _sigmoid(lr), 0.0)))
    up = _dot(act.astype(BF16), wlr_ref[...])

    ones_bd = ones_ref[...]
    kk = k * kk_ref[...]
    nrm = jnp.sqrt(_seg_sum(kk * kk, ones_bd))
    kk = kk / jnp.maximum(nrm, 1e-12)

    o_ref[:, PB_R * GROUP_W:(PB_R + 1) * GROUP_W] = r
    o_ref[:, PB_V * GROUP_W:(PB_V + 1) * GROUP_W] = v
    o_ref[:, PB_KK * GROUP_W:(PB_KK + 1) * GROUP_W] = kk
    o_ref[:, PB_G * GROUP_W:(PB_G + 1) * GROUP_W] = up[:, 4 * GROUP_W:5 * GROUP_W]
    ka = ka_ref[...]
    for d in range(2):
        z = w0_ref[d:d + 1, :] + up[:, d * GROUP_W:(d + 1) * GROUP_W]
        sp = jnp.maximum(-z, 0.0) + jnp.log(1.0 + jnp.exp(-jnp.abs(z)))
        lw = -jnp.exp(-sp - 0.5)
        a = _sigmoid(a0_ref[d:d + 1, :] + up[:, (2 + d) * GROUP_W:(3 + d) * GROUP_W])
        kd = k * (1.0 + (a - 1.0) * ka)
        base = 3 + 3 * d
        o_ref[:, base * GROUP_W:(base + 1) * GROUP_W] = lw
        o_ref[:, (base + 1) * GROUP_W:(base + 2) * GROUP_W] = a
        o_ref[:, (base + 2) * GROUP_W:(base + 3) * GROUP_W] = kd


def _rw_prep_call(proj, conv_w, wlr, w0, a0, k_k, k_a, ones512, n_batch, seq, ctx, tm):
    R = proj.shape[0]
    n_lat_tiles = n_batch * seq // tm
    nb8 = R // 8
    t8 = tm // 8
    wide = 3 * GROUP_W
    cb = COL_BR // wide
    return pl.pallas_call(
        functools.partial(_rw_prep_kernel, tiles_lat=seq // tm, tiles_ctx=ctx // tm, n_lat_tiles=n_lat_tiles),
        out_shape=jax.ShapeDtypeStruct((R, PB_N * GROUP_W), F32),
        grid=(R // tm,),
        in_specs=[
            pl.BlockSpec((tm, wide), lambda i: (i, cb)),
            pl.BlockSpec((8, wide), lambda i: (jnp.maximum(i * t8 - 1, 0), cb)),
            pl.BlockSpec((8, wide), lambda i: (jnp.minimum((i + 1) * t8, nb8 - 1), cb)),
            pl.BlockSpec((tm, 256), lambda i: (i, COL_BLR // 256)),
            pl.BlockSpec((3, wide), lambda i: (0, 0)),
            pl.BlockSpec((256, 5 * GROUP_W), lambda i: (0, 0)),
            pl.BlockSpec((2, GROUP_W), lambda i: (0, 0)),
            pl.BlockSpec((2, GROUP_W), lambda i: (0, 0)),
            pl.BlockSpec((1, GROUP_W), lambda i: (0, 0)),
            pl.BlockSpec((1, GROUP_W), lambda i: (0, 0)),
            pl.BlockSpec((GROUP_W, GROUP_W), lambda i: (0, 0)),
        ],
        out_specs=pl.BlockSpec((tm, PB_N * GROUP_W), lambda i: (i, 0)),
        compiler_params=_cparams(("arbitrary",)),
        name="mixer_b_prep",
    )(proj, proj, proj, proj, conv_w, wlr, w0, a0, k_k, k_a, ones512)


RW_SPLIT = 1
_NN = ((1,), (0,))
_NT = ((1,), (1,))
_TN = ((0,), (0,))


def _split(x, n):
    parts = []
    for _ in range(n):
        p = x.astype(BF16)
        parts.append(p)
        x = x - p.astype(F32)
    return parts


def _sdot(a, b, dims):
    n = max(len(a), len(b))
    acc = None
    for i, ai in enumerate(a):
        for j, bj in enumerate(b):
            if i + j < n:
                t = lax.dot_general(ai, bj, (dims, ((), ())), preferred_element_type=F32)
                acc = t if acc is None else acc + t
    return acc


def _rw_chunks(at, bt, kt, rt, v, gam, s0, strict, incl):
    n = len(at)
    ids = range(n)
    C = at[0].shape[0]
    sp = lambda t: _split(t, RW_SPLIT)
    b1 = lambda t: [t.astype(BF16)]
    ar = [b1(jnp.concatenate([at[i], rt[i]], axis=0)) for i in ids]
    bk = [sp(jnp.concatenate([bt[i], kt[i]], axis=0)) for i in ids]
    g4 = [_sdot(ar[i], bk[i][:1], _NT) for i in ids]
    a_ab = [jnp.where(strict[i], g4[i][:C, :C], 0.0) for i in ids]
    a_ak = [jnp.where(strict[i], g4[i][:C, C:], 0.0) for i in ids]
    a_rb = [jnp.where(incl[i], g4[i][C:, :C], 0.0) for i in ids]
    a_rk = [jnp.where(incl[i], g4[i][C:, C:], 0.0) for i in ids]
    wv = [_sdot(b1(a_ak[i]), b1(v[i]), _NN) for i in ids]
    x = [jnp.concatenate([at[i], wv[i]], axis=1) for i in ids]

    row = lax.broadcasted_iota(jnp.int32, (C, C), 0)
    col = lax.broadcasted_iota(jnp.int32, (C, C), 1)
    eye = (row == col).astype(F32)
    blk = 16
    same = (row // blk) == (col // blk)
    p = [jnp.where(same, a_ab[i], 0.0) for i in ids]
    m = [eye + p[i] for i in ids]
    for _ in range(int(math.log2(blk)) - 1):
        pb_ = [b1(p[i]) for i in ids]
        p = [_sdot(pb_[i], pb_[i], _NN) for i in ids]
        m = [m[i] + _sdot(b1(m[i]), b1(p[i]), _NN) for i in ids]
    while blk < C:
        wider = (row // (2 * blk)) == (col // (2 * blk))
        join = wider & jnp.logical_not(same)
        mb = [b1(m[i]) for i in ids]
        t = [_sdot(b1(jnp.where(join, a_ab[i], 0.0)), mb[i], _NN) for i in ids]
        m = [m[i] + _sdot(mb[i], b1(t[i]), _NN) for i in ids]
        same, blk = wider, 2 * blk
    x = [_sdot(b1(m[i]), b1(x[i]), _NN) for i in ids]

    s0b = [sp(s0[i]) for i in ids]
    u = [x[i][:, HEAD_DIM:] + _sdot(sp(x[i][:, :HEAD_DIM]), s0b[i], _NT) for i in ids]
    uv = [sp(jnp.concatenate([u[i], v[i]], axis=0)) for i in ids]
    y = [_sdot(b1(jnp.concatenate([a_rb[i], a_rk[i]], axis=1)), uv[i][:1], _NN)
         + _sdot(sp(rt[i]), s0b[i], _NT) for i in ids]
    s1 = [(s0[i] + _sdot(uv[i], bk[i], _TN)) * gam[i] for i in ids]
    return y, s1


def _rw_scan_kernel(*refs, n_batch, nc):
    n_in = 12 * n_batch
    yfl_ref, ybl_ref, yfc_ref, ybc_ref, s_sc = refs[n_in:]
    s = pl.program_id(0)

    @pl.when(s == 0)
    def _():
        s_sc[...] = jnp.zeros_like(s_sc)

    C = refs[0].shape[0]
    row = lax.broadcasted_iota(jnp.int32, (C, C), 0)
    col = lax.broadcasted_iota(jnp.int32, (C, C), 1)
    chains = dict(at=[], bt=[], kt=[], rt=[], v=[], gam=[], s0=[], strict=[], incl=[])
    for b in range(n_batch):
        for d, reverse in enumerate((False, True)):
            r_ref, v_ref, kk_ref, lw_ref, a_ref, k_ref = refs[12 * b + 6 * d:12 * b + 6 * d + 6]
            incl, strict = (row <= col, row < col) if reverse else (row >= col, row > col)
            lw = lw_ref[...]
            kk = kk_ref[...]
            v = v_ref[...]
            cum = _sdot([incl.astype(BF16)], _split(lw, 3), _NN)
            e_l = jnp.exp(cum)
            e_n = jnp.exp(-cum)
            at = -(kk * jnp.exp(cum - lw))
            bt = kk * a_ref[...] * e_n
            kt = k_ref[...] * e_n
            rt = r_ref[...] * e_l
            gam = e_l[0:1, :] if reverse else e_l[C - 1:C, :]
            for h in range(RW_HEADS):
                hs = slice(h * HEAD_DIM, (h + 1) * HEAD_DIM)
                for name, val in (("at", at), ("bt", bt), ("kt", kt), ("rt", rt), ("v", v), ("gam", gam)):
                    chains[name].append(val[:, hs])
                chains["s0"].append(s_sc[b, d, h])
                chains["strict"].append(strict)
                chains["incl"].append(incl)
    y, s1 = _rw_chunks(**chains)
    ys = {}
    for b in range(n_batch):
        for d in range(2):
            base = (b * 2 + d) * RW_HEADS
            for h in range(RW_HEADS):
                s_sc[b, d, h] = s1[base + h]
            ys[b, d] = jnp.concatenate(y[base:base + RW_HEADS], axis=-1)

    @pl.when(s < nc)
    def _():
        for b in range(n_batch):
            yfc_ref[b] = ys[b, 0]
            ybc_ref[b] = ys[b, 1]

    @pl.when(s >= nc)
    def _():
        for b in range(n_batch):
            yfl_ref[b] = ys[b, 0]
            ybl_ref[b] = ys[b, 1]


def _rw_scan_call(prep, n_batch, seq, ctx):
    C = RW_CHUNK
    nc, nl = ctx // C, seq // C
    ctx_base = n_batch * nl

    def blk_f(b, s):
        return jnp.where(s < nc, ctx_base + b * nc + s, b * nl + (s - nc))

    def blk_b(b, s):
        return jnp.where(s < nc, ctx_base + b * nc + (nc - 1 - s), b * nl + (nl - 1 - (s - nc)))

    def col(blk, b, cb):
        return pl.BlockSpec((C, GROUP_W), lambda s: (blk(b, s), cb))

    in_specs = []
    for b in range(n_batch):
        in_specs += [col(blk_f, b, c) for c in (PB_R, PB_V, PB_KK, 3, 4, 5)]
        in_specs += [col(blk_b, b, c) for c in (PB_R, PB_V, PB_KK, 6, 7, 8)]
    lat = jax.ShapeDtypeStruct((n_batch, seq, GROUP_W), F32)
    cx = jax.ShapeDtypeStruct((n_batch, ctx, GROUP_W), F32)
    blk3 = (n_batch, C, GROUP_W)
    out_specs = (
        pl.BlockSpec(blk3, lambda s: (0, jnp.maximum(s - nc, 0), 0)),
        pl.BlockSpec(blk3, lambda s: (0, nl - 1 - jnp.maximum(s - nc, 0), 0)),
        pl.BlockSpec(blk3, lambda s: (0, jnp.minimum(s, nc - 1), 0)),
        pl.BlockSpec(blk3, lambda s: (0, nc - 1 - jnp.minimum(s, nc - 1), 0)),
    )
    yfl, ybl, yfc, ybc = pl.pallas_call(
        functools.partial(_rw_scan_kernel, n_batch=n_batch, nc=nc),
        out_shape=(lat, lat, cx, cx),
        grid=(nc + nl,),
        in_specs=in_specs,
        out_specs=out_specs,
        scratch_shapes=[pltpu.VMEM((n_batch, 2, RW_HEADS, HEAD_DIM, HEAD_DIM), F32)],
        compiler_params=_cparams(("arbitrary",)),
        name="mixer_b_scan",
    )(*([prep] * (12 * n_batch)))
    flat = lambda a, c: jnp.concatenate([a.reshape(n_batch * seq, GROUP_W), c.reshape(n_batch * ctx, GROUP_W)], 0)
    return flat(yfl, yfc), flat(ybl, ybc)


def _rw_readout_kernel(r_ref, v_ref, k0_ref, k1_ref, g_ref, yf_ref, yb_ref, rk_ref, gw_ref, gb_ref, ones_ref,
                       o_ref):
    ones_bd = ones_ref[...]
    v = v_ref[...]
    bonus = _seg_sum(r_ref[...] * (k0_ref[...] + k1_ref[...]) * rk_ref[...], ones_bd)
    y = yf_ref[...] + yb_ref[...] + bonus * v
    mu = _seg_sum(y, ones_bd) * (1.0 / HEAD_DIM)
    yc = y - mu
    var = _seg_sum(yc * yc, ones_bd) * (1.0 / HEAD_DIM)
    yn = yc * lax.rsqrt(var + RW_GN_EPS)
    o_ref[...] = ((yn * gw_ref[...] + gb_ref[...]) * g_ref[...]).astype(o_ref.dtype)


def _rw_readout_call(prep, yf, yb, r_k, gn_w, gn_b, ones512, tm):
    R = prep.shape[0]
    col = lambda cb: pl.BlockSpec((tm, GROUP_W), lambda i: (i, cb))
    vec = pl.BlockSpec((1, GROUP_W), lambda i: (0, 0))
    return pl.pallas_call(
        _rw_readout_kernel,
        out_shape=jax.ShapeDtypeStruct((R, GROUP_W), BF16),
        grid=(R // tm,),
        in_specs=[col(PB_R), col(PB_V), col(5), col(8), col(PB_G), col(0), col(0), vec, vec, vec,
                  pl.BlockSpec((GROUP_W, GROUP_W), lambda i: (0, 0))],
        out_specs=col(0),
        compiler_params=_cparams(("arbitrary",)),
        name="mixer_b_readout",
    )(prep, prep, prep, prep, prep, yf, yb, r_k, gn_w, gn_b, ones512)


def _s5_tables(a_re, a_im, log_dt, b_re, b_im, c_re, c_im):
    Lc, G, N, P = S5_CHUNK, S5_GROUPS, S5_N, S5_P
    dt = jnp.exp(log_dt)[..., None]
    lam_re, lam_im = dt * a_re, dt * a_im
    tau = jnp.arange(Lc + 1, dtype=F32)[:, None, None, None]
    mag = jnp.exp(tau * lam_re)
    pw_re, pw_im = mag * jnp.cos(tau * lam_im), mag * jnp.sin(tau * lam_im)
    ab_re, ab_im = pw_re[1], pw_im[1]
    den = a_re * a_re + a_im * a_im
    nr = ab_re - 1.0
    cf_re, cf_im = (nr * a_re + ab_im * a_im) / den, (ab_im * a_re - nr * a_im) / den
    bp_re = cf_re[..., None] * b_re[None] - cf_im[..., None] * b_im[None]
    bp_im = cf_re[..., None] * b_im[None] + cf_im[..., None] * b_re[None]
    pb_re = pw_re[..., None] * bp_re[None] - pw_im[..., None] * bp_im[None]
    pb_im = pw_re[..., None] * bp_im[None] + pw_im[..., None] * bp_re[None]
    kk = (jnp.einsum('gqn,tdgnp->tdgqp', c_re, pb_re[:Lc]) - jnp.einsum('gqn,tdgnp->tdgqp', c_im, pb_im[:Lc]))
    jj = np.arange(Lc)[:, None]
    ii = np.arange(Lc)[None, :]
    dist = np.abs(ii - jj)
    kf = kk[dist, 0] * jnp.asarray(ii >= jj, F32)[..., None, None, None]
    kb = kk[dist, 1] * jnp.asarray(ii <= jj, F32)[..., None, None, None]
    tz = jnp.transpose(kf + kb, (2, 0, 4, 1, 3))
    jr = np.arange(Lc)
    emap = lambda pbx, order, d: jnp.transpose(pbx[order, d], (1, 0, 3, 2))
    em = jnp.stack([emap(pb_re, Lc - 1 - jr, 0), emap(pb_im, Lc - 1 - jr, 0),
                    emap(pb_re, jr, 1), emap(pb_im, jr, 1)], axis=3)

    def gmap(order, d):
        pr, pi = pw_re[order, d], pw_im[order, d]
        cp_re = c_re[None] * pr[:, :, None, :] - c_im[None] * pi[:, :, None, :]
        cp_im = c_re[None] * pi[:, :, None, :] + c_im[None] * pr[:, :, None, :]
        to = lambda t: jnp.transpose(t, (1, 3, 0, 2))
        return to(cp_re), to(-cp_im)

    gk = jnp.stack(gmap(jr + 1, 0) + gmap(Lc - jr, 1), axis=1)
    NO, NQ, NG = S5_OCTETS, 2, 4
    tz, em, gk = tz.astype(BF16), em.astype(BF16), gk.astype(BF16)

    def bdiag(blocks):
        n, c = blocks.shape[1], blocks.shape[3]
        return jnp.concatenate(
            [jnp.pad(blocks[:, g], ((0, 0), (0, 0), (g * c, (n - 1 - g) * c))) for g in range(n)], axis=1)

    def perm_rows(m, dims, order):
        no, r, c = m.shape
        m = jnp.transpose(m.reshape((no,) + dims + (c,)), (0,) + tuple(1 + o for o in order) + (len(dims) + 1,))
        return m.reshape(no, r, c)

    tr = lambda m: jnp.swapaxes(m, 1, 2)
    gjp, rgkn = (8, Lc, P), (NQ, NG, 4, N)
    wy = perm_rows(bdiag(tz.reshape(NO, 8, Lc * P, Lc * P)), gjp, (1, 0, 2))
    wy = tr(perm_rows(tr(wy), gjp, (1, 0, 2)))
    we = perm_rows(bdiag(em.reshape(NO, 8, Lc * P, 4 * N)), gjp, (1, 0, 2))
    we = tr(perm_rows(tr(we), rgkn, (0, 2, 1, 3)))
    wz = jnp.concatenate([wy, we], axis=-1)
    gm = perm_rows(bdiag(gk.reshape(NO, 8, 4 * N, Lc * P)), rgkn, (0, 2, 1, 3))
    gm = tr(perm_rows(tr(gm), gjp, (1, 0, 2)))
    mp = Lc * jnp.arange(1, 9, dtype=F32)[:, None, None, None]
    cm = jnp.exp(mp * lam_re)
    cp_re, cp_im = cm * jnp.cos(mp * lam_im), cm * jnp.sin(mp * lam_im)
    apow = jnp.stack([cp_re[:, 0], cp_im[:, 0], cp_re[::-1, 1], cp_im[::-1, 1]], axis=0)
    apow = jnp.transpose(apow.reshape(4, 8, G // NG, NG * N), (2, 0, 1, 3))
    return wz, gm, apow


def _s5_local_kernel(u_ref, wz_ref, y_ref, e_ref):
    Lc = u_ref.shape[1]
    lhs = jnp.concatenate([u_ref[:, j, :] for j in range(Lc)], axis=-1).astype(BF16)
    z = _dot(lhs, wz_ref[...])
    for i in range(Lc):
        y_ref[:, i, :] = z[:, i * LANE:(i + 1) * LANE]
    e_ref[...] = z[:, Lc * LANE:]


def _s5_local_call(proj3, wz):
    NR, Lc, _ = proj3.shape
    NO, K, N = wz.shape
    NE = N - Lc * LANE
    tr = NR // 4
    cu = COL_DU // LANE
    return pl.pallas_call(
        _s5_local_kernel,
        out_shape=(jax.ShapeDtypeStruct((NR, Lc, GROUP_W), F32), jax.ShapeDtypeStruct((NR, NO * NE), F32)),
        grid=(NO, NR // tr),
        in_specs=[
            pl.BlockSpec((tr, Lc, LANE), lambda o, i: (i, 0, cu + o)),
            pl.BlockSpec((None, K, N), lambda o, i: (o, 0, 0)),
        ],
        out_specs=(pl.BlockSpec((tr, Lc, LANE), lambda o, i: (i, 0, o)),
                   pl.BlockSpec((tr, NE), lambda o, i: (i, o))),
        compiler_params=_cparams(("arbitrary", "arbitrary")),
        name="mixer_d_local",
    )(proj3, wz)


def _s5_scan_kernel(e_ref, ap_ref, x_ref, *, n_batch, nlc, ncc):
    NS = e_ref.shape[1] // 4
    row = lax.broadcasted_iota(jnp.int32, (8, NS), 0)
    pw = [ap_ref[k] for k in range(4)]

    def cmul(a_re, a_im, x_re, x_im):
        return a_re * x_re - a_im * x_im, a_re * x_im + a_im * x_re

    def tile_scan(x_re, x_im, p_re, p_im, c_re, c_im, reverse):
        for d in (1, 2, 4):
            if reverse:
                keep, sh, pr = row < 8 - d, 8 - d, 8 - d
            else:
                keep, sh, pr = row >= d, d, d - 1
            s_re = jnp.where(keep, pltpu.roll(x_re, sh, 0), 0.0)
            s_im = jnp.where(keep, pltpu.roll(x_im, sh, 0), 0.0)
            t_re, t_im = cmul(p_re[pr:pr + 1, :], p_im[pr:pr + 1, :], s_re, s_im)
            x_re, x_im = x_re + t_re, x_im + t_im
        t_re, t_im = cmul(p_re, p_im, c_re, c_im)
        x_re, x_im = x_re + t_re, x_im + t_im
        edge, sh, last = (7, 7, 0) if reverse else (0, 1, 7)
        in_re = jnp.where(row == edge, c_re, pltpu.roll(x_re, sh, 0))
        in_im = jnp.where(row == edge, c_im, pltpu.roll(x_im, sh, 0))
        return in_re, in_im, x_re[last:last + 1, :], x_im[last:last + 1, :]

    def phase(row0, n_chunks, stride, carry):
        n_tiles = n_chunks // 8

        def body(t, carry):
            out = []
            for b in range(n_batch):
                xf_re, xf_im, xb_re, xb_im = carry[4 * b:4 * b + 4]
                rf = pl.ds(pl.multiple_of(row0 + b * stride + t * 8, 8), 8)
                rb = pl.ds(pl.multiple_of(row0 + b * stride + (n_tiles - 1 - t) * 8, 8), 8)
                f_re, f_im, xf_re, xf_im = tile_scan(e_ref[rf, 0:NS], e_ref[rf, NS:2 * NS], pw[0], pw[1],
                                                     xf_re, xf_im, False)
                b_re, b_im, xb_re, xb_im = tile_scan(e_ref[rb, 2 * NS:3 * NS], e_ref[rb, 3 * NS:4 * NS],
                                                     pw[2], pw[3], xb_re, xb_im, True)
                x_ref[rf, 0:NS] = f_re
                x_ref[rf, NS:2 * NS] = f_im
                x_ref[rb, 2 * NS:3 * NS] = b_re
                x_ref[rb, 3 * NS:4 * NS] = b_im
                out += [xf_re, xf_im, xb_re, xb_im]
            return tuple(out)

        return lax.fori_loop(0, n_tiles, body, carry)

    zero = jnp.zeros((1, NS), F32)
    carry = phase(n_batch * nlc, ncc, ncc, (zero,) * (4 * n_batch))
    phase(0, nlc, nlc, carry)


def _s5_scan_call(e, apow, n_batch, seq, ctx):
    NR, NEall = e.shape
    NQ8, _, _, NS = apow.shape
    return pl.pallas_call(
        functools.partial(_s5_scan_kernel, n_batch=n_batch, nlc=seq // S5_CHUNK, ncc=ctx // S5_CHUNK),
        out_shape=jax.ShapeDtypeStruct((NR, NEall), F32),
        grid=(NQ8,),
        in_specs=[
            pl.BlockSpec((NR, 4 * NS), lambda q: (0, q)),
            pl.BlockSpec((None, 4, 8, NS), lambda q: (q, 0, 0, 0)),
        ],
        out_specs=pl.BlockSpec((NR, 4 * NS), lambda q: (0, q)),
        compiler_params=_cparams(("arbitrary",)),
        name="mixer_d_scan",
    )(e, apow)


def _s5_carry_kernel(x_ref, gm_ref, yl_ref, y_ref):
    Lc = yl_ref.shape[1]
    y = _dot(x_ref[...].astype(BF16), gm_ref[...])
    for i in range(Lc):
        y_ref[:, i, :] = yl_ref[:, i, :] + y[:, i * LANE:(i + 1) * LANE]


def _s5_carry_call(xin, gm, yloc):
    NR, Lc, _ = yloc.shape
    NO, NE, N = gm.shape
    tr = NR // 4
    return pl.pallas_call(
        _s5_carry_kernel,
        out_shape=jax.ShapeDtypeStruct((NR, Lc, GROUP_W), F32),
        grid=(NO, NR // tr),
        in_specs=[
            pl.BlockSpec((tr, NE), lambda o, i: (i, o)),
            pl.BlockSpec((None, NE, N), lambda o, i: (o, 0, 0)),
            pl.BlockSpec((tr, Lc, LANE), lambda o, i: (i, 0, o)),
        ],
        out_specs=pl.BlockSpec((tr, Lc, LANE), lambda o, i: (i, 0, o)),
        compiler_params=_cparams(("arbitrary", "arbitrary")),
        name="mixer_d_carry",
    )(xin, gm, yloc)


def _s5_out_kernel(y_ref, u_ref, d_ref, w_ref, b_ref, o_ref):
    y = y_ref[...] + d_ref[...] * u_ref[...]
    c = math.sqrt(2.0 / math.pi)
    y = 0.5 * y * (1.0 + jnp.tanh(c * (y + 0.044715 * (y * y * y))))
    z = _dot(y.astype(BF16), w_ref[...]) + b_ref[...]
    o_ref[...] = (y * _sigmoid(z)).astype(o_ref.dtype)


def _s5_out_call(y_tok, proj, d_skip, glu_w, glu_b, tm):
    R = proj.shape[0]
    vec = pl.BlockSpec((1, GROUP_W), lambda i: (0, 0))
    return pl.pallas_call(
        _s5_out_kernel,
        out_shape=jax.ShapeDtypeStruct((R, GROUP_W), BF16),
        grid=(R // tm,),
        in_specs=[
            pl.BlockSpec((tm, GROUP_W), lambda i: (i, 0)),
            pl.BlockSpec((tm, GROUP_W), lambda i: (i, COL_DU // GROUP_W)),
            vec,
            pl.BlockSpec((GROUP_W, GROUP_W), lambda i: (0, 0)),
            vec,
        ],
        out_specs=pl.BlockSpec((tm, GROUP_W), lambda i: (i, 0)),
        compiler_params=_cparams(("arbitrary",)),
        name="mixer_d_out",
    )(y_tok, proj, d_skip, glu_w, glu_b)


def _rope_tables(n_tokens):
    t = jnp.arange(n_tokens)
    nf = HEAD_DIM // 4
    inv = 1.0 / (ROPE_BASE ** (jnp.arange(nf, dtype=F32) / nf))

    def ang(pp):
        a = pp.astype(F32)[:, None] * inv[None, :]
        return jnp.concatenate([a, a], -1)

    a = jnp.concatenate([ang(t // GRID_W), ang(t % GRID_W)], -1)
    cos, sin = jnp.cos(a), jnp.sin(a)
    sign = np.where((np.arange(HEAD_DIM) % 32) < 16, -1.0, 1.0).astype(np.float32)
    cos2 = jnp.concatenate([cos, cos], -1)
    sin2 = jnp.concatenate([sin * sign, sin * sign], -1)
    return cos2, sin2


def _block_ones(n):
    return jnp.asarray(np.kron(np.eye(n // HEAD_DIM), np.ones((HEAD_DIM, HEAD_DIM))), BF16)


def _permute_w_in(w_in):
    w = w_in.astype(BF16)
    L, D = w.shape[:2]
    pad = jnp.zeros((L, D, COL_CK - COL_BLR - 160), BF16)
    parts = [w[..., :3072], w[..., 3232:3744], w[..., 4000:4512], w[..., 3072:3232], pad, w[..., 3744:4000]]
    return jnp.concatenate(parts, axis=-1)


def _forward(x, c, ctx, c_ctx, w_ada, b_ada, norm_g, ffn1_wg, ffn1_wu, ffn1_wd, ffn2_wg, ffn2_wu, ffn2_wd,
             w_in, w_out, na_q_g, na_k_g, na_rpb, rw_conv, rw_w0, rw_w_up, rw_a0, rw_a_up, rw_g_up,
             rw_k_k, rw_k_a, rw_r_k, rw_gn_w, rw_gn_b, wa_q_g, wa_k_g, wa_sink, s5_a_re, s5_a_im,
             s5_log_dt, s5_b_re, s5_b_im, s5_c_re, s5_c_im, s5_d, s5_glu_w, s5_glu_b, *, tm, tf):
    B, SEQ, D = x.shape
    CTX = ctx.shape[1]
    L = w_ada.shape[0]
    n_lat = B * SEQ
    n_qb = SEQ // QBLK

    bf = lambda t: t.astype(BF16)
    f1g, f1u, f1d, f2g, f2u, f2d = map(bf, (ffn1_wg, ffn1_wu, ffn1_wd, ffn2_wg, ffn2_wu, ffn2_wd))
    w_in_p = _permute_w_in(w_in)
    w_out_b = bf(w_out)
    glu_w_b = bf(s5_glu_w)
    ones128, ones512 = _block_ones(LANE), _block_ones(GROUP_W)
    cos_t, sin_t = _rope_tables(SEQ)
    na_bias = [_na_bias_table(na_rpb[l], n_qb) for l in range(L)]
    zr = lambda r, cdim: jnp.zeros((L, r, cdim), F32)
    wlr = jnp.concatenate([
        jnp.concatenate([rw_w_up[:, 0], rw_w_up[:, 1], zr(32, 3 * GROUP_W)], axis=-1),
        jnp.concatenate([zr(32, 2 * GROUP_W), rw_a_up[:, 0], rw_a_up[:, 1], zr(32, GROUP_W)], axis=-1),
        jnp.concatenate([zr(96, 4 * GROUP_W), rw_g_up], axis=-1),
        zr(96, 5 * GROUP_W)], axis=1).astype(BF16)
    s5_tabs = jax.vmap(_s5_tables)(s5_a_re, s5_a_im, s5_log_dt, s5_b_re, s5_b_im, s5_c_re, s5_c_im)
    tile2 = lambda t: jnp.concatenate([t, t], axis=-1)

    cc = jnp.concatenate([c, c_ctx[None], jnp.zeros((8 - B - 1, D), F32)], axis=0)
    mods_all = _ada_call(cc, w_ada, b_ada)[:, :B + 1].reshape(L, B + 1, N_MOD, D)

    xs = jnp.concatenate([x.reshape(n_lat, D), ctx.reshape(B * CTX, D)], axis=0)
    R = xs.shape[0]
    for l in range(L):
        want_ctx = l < L - 1
        mods = mods_all[l]
        ng = norm_g[l]
        xs = _ffn_call(xs, mods, ng[0:1], f1g, f1u, f1d, l, 0, R, SEQ, B, tm, tf)
        proj = _win_call(xs, mods, ng[1:2], w_in_p, l, SEQ, B, tm, D_IN_PAD // 3)
        o_a = _na_call(proj, na_bias[l], tile2(na_q_g[l][None]), tile2(na_k_g[l][None]), ones128,
                       B, SEQ, CTX, want_ctx)
        prep = _rw_prep_call(proj, rw_conv[l], wlr[l], rw_w0[l], rw_a0[l], rw_k_k[l][None], rw_k_a[l][None],
                             ones512, B, SEQ, CTX, min(tm, 256))
        vec = lambda t: t.reshape(1, GROUP_W)
        yf, yb = _rw_scan_call(prep, B, SEQ, CTX)
        o_b = _rw_readout_call(prep, yf, yb, vec(rw_r_k[l]), vec(rw_gn_w[l]), vec(rw_gn_b[l]), ones512, tm)
        o_c = _wa_call(proj, wa_sink[l], cos_t, sin_t, tile2(wa_q_g[l][None]), tile2(wa_k_g[l][None]), ones128,
                       B, SEQ, CTX, want_ctx)
        wz, gm, apow = (t[l] for t in s5_tabs)
        yloc, e_loc = _s5_local_call(proj.reshape(R // S5_CHUNK, S5_CHUNK, D_IN_PAD), bf(wz))
        xin = _s5_scan_call(e_loc, apow, B, SEQ, CTX)
        y_tok = _s5_carry_call(xin, bf(gm), yloc).reshape(R, GROUP_W)
        o_d = _s5_out_call(y_tok, proj, vec(s5_d[l]), glu_w_b[l], vec(s5_glu_b[l]), tm)
        n_rows = R if want_ctx else n_lat
        xs = _wout_call(xs, (o_a, o_b, o_c, o_d), mods, w_out_b, l, n_rows, SEQ, B, tm)
        xs = _ffn_call(xs, mods, ng[2:3], f2g, f2u, f2d, l, 6, n_rows, SEQ, B, tm, tf)
    return xs[:n_lat].reshape(B, SEQ, D)


def kernel(x, c, ctx, c_ctx, w_ada, b_ada, norm_g, ffn1_wg, ffn1_wu, ffn1_wd, ffn2_wg, ffn2_wu, ffn2_wd, w_in, w_out, na_q_g, na_k_g, na_rpb, rw_conv, rw_w0, rw_w_up, rw_a0, rw_a_up, rw_g_up, rw_k_k, rw_k_a, rw_r_k, rw_gn_w, rw_gn_b, wa_q_g, wa_k_g, wa_sink, s5_a_re, s5_a_im, s5_log_dt, s5_b_re, s5_b_im, s5_c_re, s5_c_im, s5_d, s5_glu_w, s5_glu_b):
    return _forward(x, c, ctx, c_ctx, w_ada, b_ada, norm_g, ffn1_wg, ffn1_wu, ffn1_wd, ffn2_wg, ffn2_wu, ffn2_wd,
                    w_in, w_out, na_q_g, na_k_g, na_rpb, rw_conv, rw_w0, rw_w_up, rw_a0, rw_a_up, rw_g_up,
                    rw_k_k, rw_k_a, rw_r_k, rw_gn_w, rw_gn_b, wa_q_g, wa_k_g, wa_sink, s5_a_re, s5_a_im,
                    s5_log_dt, s5_b_re, s5_b_im, s5_c_re, s5_c_im, s5_d, s5_glu_w, s5_glu_b, tm=512, tf=512)
```
